```python
import math
import jax, jax.numpy as jnp
from jax import lax
import numpy as np

D_MODEL = 2048
BATCH = 4
SEQ = 4096
DEPTH = 4

DN_ALPHA = (2 * DEPTH) ** 0.25
DN_BETA = (8 * DEPTH) ** -0.25
LN_EPS = 1e-5

A_HEADS = 8
A_HEAD_DIM = 128
A_WIDTH = A_HEADS * A_HEAD_DIM
MOBA_BLOCK = 256
MOBA_TOPK = 3
MOBA_QCHUNK = 16
ROPE_THETA = 500000.0
ROPE_DIM = A_HEAD_DIM // 4
NEG_INF = -1e30

B_HEADS = 4
B_HEAD_DIM = 256
B_WIDTH = B_HEADS * B_HEAD_DIM
MLSTM_CHUNK = 64
B_CONV = 4
HNORM_EPS = 1e-6

EVEN_IN = 3 * A_WIDTH + 4 * B_WIDTH + 2 * B_HEADS
EVEN_SPLITS = (A_WIDTH, 2 * A_WIDTH, 3 * A_WIDTH, 3 * A_WIDTH + 2 * B_WIDTH,
               3 * A_WIDTH + 3 * B_WIDTH, 3 * A_WIDTH + 4 * B_WIDTH)
EVEN_MIX = A_WIDTH + B_WIDTH

C_HEAD_DIM = 64
C_HEADS = D_MODEL // C_HEAD_DIM
LORA_W = 96
LORA_A = 96
LORA_V = 64
LORA_G = 256
LNX_EPS = 64e-5
KK_EPS = 1e-12

N_GROUPS = 4
EXPERTS_PER_GROUP = 8
N_EXPERTS = N_GROUPS * EXPERTS_PER_GROUP
MOE_TOPK = 2
EXPERT_FF = 512
MOE_BLOCK = 128

N_EVEN = (DEPTH + 1) // 2
N_ODD = DEPTH // 2
N_VRES = max(N_ODD - 1, 0)

kernel_name = "moba_mlstm_rwkv7_hmoe_deepnorm_trunk"

F32 = jnp.float32


def layer_norm(x, g, b):
    xf = x.astype(F32)
    mu = jnp.mean(xf, -1, keepdims=True)
    var = jnp.mean(jnp.square(xf - mu), -1, keepdims=True)
    return ((xf - mu) * lax.rsqrt(var + LN_EPS) * g.astype(F32) + b.astype(F32)).astype(x.dtype)


def split_heads(t, n_heads):
    bsz, s, w = t.shape
    return t.reshape(bsz, s, n_heads, w // n_heads).transpose(0, 2, 1, 3)


def merge_heads(t):
    bsz, h, s, d = t.shape
    return t.transpose(0, 2, 1, 3).reshape(bsz, s, h * d)


def partial_rotary(x, pos):
    half = ROPE_DIM // 2
    inv = jnp.power(jnp.float32(ROPE_THETA), -jnp.arange(half, dtype=F32) / half)
    ang = pos.astype(F32)[:, None] * inv[None, :]
    cos, sin = jnp.cos(ang), jnp.sin(ang)
    xr = x[..., :ROPE_DIM].astype(F32)
    x1, x2 = xr[..., :half], xr[..., half:]
    rot = jnp.concatenate([x1 * cos - x2 * sin, x2 * cos + x1 * sin], -1)
    return jnp.concatenate([rot.astype(x.dtype), x[..., ROPE_DIM:]], -1)


def moba_attention(q, k, v):
    bsz, h, s, dh = q.shape
    nblk = s // MOBA_BLOCK
    n_sel = min(MOBA_TOPK, nblk)
    kb = k.reshape(bsz, h, nblk, MOBA_BLOCK, dh)
    vb = v.reshape(bsz, h, nblk, MOBA_BLOCK, dh)
    kmean = jnp.mean(kb.astype(F32), axis=3)
    gate = jnp.einsum('bhsd,bhnd->bhsn', q.astype(F32), kmean)
    qblk = jnp.arange(s) // MOBA_BLOCK
    past = jnp.arange(nblk)[None, :] < qblk[:, None]
    gate = jnp.where(past, gate, -jnp.inf)
    _, sel = lax.top_k(gate, n_sel)
    sel_valid = sel < qblk[:, None]
    scale = A_HEAD_DIM ** -0.5
    nq = s // MOBA_QCHUNK

    def to_chunks(t):
        t = t.reshape(bsz, h, nq, MOBA_QCHUNK, *t.shape[3:])
        return jnp.moveaxis(t, 2, 0)

    bi = jnp.arange(bsz)[:, None, None, None]
    hi = jnp.arange(h)[None, :, None, None]

    def attend(args):
        qc, selc, validc, c = args
        q0 = c * MOBA_QCHUNK
        own = q0 // MOBA_BLOCK
        qpos = q0 + jnp.arange(MOBA_QCHUNK)
        kpos = own * MOBA_BLOCK + jnp.arange(MOBA_BLOCK)
        k_own = lax.dynamic_index_in_dim(kb, own, axis=2, keepdims=False)
        v_own = lax.dynamic_index_in_dim(vb, own, axis=2, keepdims=False)
        s_own = jnp.einsum('bhqd,bhkd->bhqk', qc, k_own).astype(F32) * scale
        s_own = jnp.where(kpos[None, :] <= qpos[:, None], s_own, NEG_INF)
        k_sel = kb[bi, hi, selc]
        v_sel = vb[bi, hi, selc]
        s_sel = jnp.einsum('bhqd,bhqnkd->bhqnk', qc, k_sel).astype(F32) * scale
        s_sel = jnp.where(validc[..., None], s_sel, NEG_INF)
        logits = jnp.concatenate(
            [s_own, s_sel.reshape(bsz, h, MOBA_QCHUNK, n_sel * MOBA_BLOCK)], -1)
        p = jax.nn.softmax(logits, axis=-1).astype(v.dtype)
        p_own = p[..., :MOBA_BLOCK]
        p_sel = p[..., MOBA_BLOCK:].reshape(bsz, h, MOBA_QCHUNK, n_sel, MOBA_BLOCK)
        return (jnp.einsum('bhqk,bhkd->bhqd', p_own, v_own)
                + jnp.einsum('bhqnk,bhqnkd->bhqd', p_sel, v_sel))

    out = lax.map(attend, (to_chunks(q), to_chunks(sel), to_chunks(sel_valid), jnp.arange(nq)))
    return jnp.moveaxis(out, 0, 2).reshape(bsz, h, s, dh)


def mlstm_chunkwise(q, k, v, i_pre, f_pre):
    bsz, h, s, dk = q.shape
    dv = v.shape[-1]
    L = MLSTM_CHUNK
    nc = s // L
    q = q.astype(F32)
    k = k.astype(F32) * (dk ** -0.5)
    v = v.astype(F32)
    log_f = jax.nn.log_sigmoid(f_pre.astype(F32))
    log_i = i_pre.astype(F32)

    def chunks(t):
        t = t.reshape(bsz, h, nc, L, *t.shape[3:])
        return jnp.moveaxis(t, 2, 0)

    tri = jnp.tril(jnp.ones((L, L), dtype=bool))

    def step(carry, xs):
        C, n, m = carry
        qc, kc, vc, ic, fc = xs
        b = jnp.cumsum(fc, axis=-1)
        log_inter = b + m[..., None]
        log_intra = b[..., :, None] - b[..., None, :] + ic[..., None, :]
        log_intra = jnp.where(tri, log_intra, -jnp.inf)
        m_t = jnp.maximum(log_inter, jnp.max(log_intra, -1))
        w_intra = jnp.exp(log_intra - m_t[..., None])
        w_inter = jnp.exp(log_inter - m_t)
        qk = jnp.einsum('bhtd,bhsd->bhts', qc, kc) * w_intra
        num = (w_inter[..., None] * jnp.einsum('bhtd,bhde->bhte', qc, C)
               + jnp.einsum('bhts,bhse->bhte', qk, vc))
        den = w_inter * jnp.einsum('bhtd,bhd->bht', qc, n) + jnp.sum(qk, -1)
        h_out = num / jnp.maximum(jnp.abs(den), jnp.exp(-m_t))[..., None]
        b_last = b[..., -1]
        log_s = b_last[..., None] - b + ic
        m_new = jnp.maximum(b_last + m, jnp.max(log_s, -1))
        carry_decay = jnp.exp(b_last + m - m_new)
        ws = jnp.exp(log_s - m_new[..., None])
        kw = kc * ws[..., None]
        C_new = carry_decay[..., None, None] * C + jnp.einsum('bhsd,bhse->bhde', kw, vc)
        n_new = carry_decay[..., None] * n + jnp.sum(kw, axis=2)
        return (C_new, n_new, m_new), h_out

    init = (jnp.zeros((bsz, h, dk, dv), F32), jnp.zeros((bsz, h, dk), F32),
            jnp.zeros((bsz, h), F32))
    _, hs = lax.scan(step, init, (chunks(q), chunks(k), chunks(v), chunks(log_i), chunks(log_f)))
    return jnp.moveaxis(hs, 0, 2).reshape(bsz, h, s, dv)


def causal_depthwise_conv(u, w, b):
    kw = w.shape[0]
    s = u.shape[1]
    up = jnp.pad(u, ((0, 0), (kw - 1, 0), (0, 0)))
    out = b
    for j in range(kw):
        out = out + up[:, j:j + s] * w[j]
    return out


def even_mixer(x, w_in, gate_bias, conv_w, conv_b, hnorm_g, w_out):
    bsz, s, _ = x.shape
    z = jnp.einsum('bsd,de->bse', x, w_in)
    q_a, k_a, v_a, qk_b, v_b, o_b, gates = jnp.split(z, list(EVEN_SPLITS), axis=-1)
    pos = jnp.arange(s)
    q_a = partial_rotary(split_heads(q_a, A_HEADS), pos)
    k_a = partial_rotary(split_heads(k_a, A_HEADS), pos)
    v_a = split_heads(v_a, A_HEADS)
    s_pad = -(-s // MOBA_BLOCK) * MOBA_BLOCK
    padw = ((0, 0), (0, 0), (0, s_pad - s), (0, 0))
    y_a = moba_attention(jnp.pad(q_a, padw), jnp.pad(k_a, padw), jnp.pad(v_a, padw))[:, :, :s]
    y_a = merge_heads(y_a).astype(x.dtype)
    qk_b = jax.nn.silu(causal_depthwise_conv(qk_b, conv_w, conv_b))
    q_b, k_b = jnp.split(qk_b, 2, axis=-1)
    gates = gates.astype(F32) + gate_bias.astype(F32).reshape(2 * B_HEADS)
    i_pre = gates[..., :B_HEADS].transpose(0, 2, 1)
    f_pre = gates[..., B_HEADS:].transpose(0, 2, 1)
    hb = mlstm_chunkwise(split_heads(q_b, B_HEADS), split_heads(k_b, B_HEADS),
                         split_heads(v_b, B_HEADS), i_pre, f_pre)
    hb = hb * lax.rsqrt(jnp.mean(hb * hb, -1, keepdims=True) + HNORM_EPS)
    y_b = merge_heads(hb).astype(x.dtype) * hnorm_g * jax.nn.sigmoid(o_b)
    y = jnp.concatenate([y_a, y_b], axis=-1)
    return jnp.einsum('bse,ed->bsd', y, w_out)


def rwkv7_mixer(x, v_first, mu, w_r, w_k, w_v, w_o, w0, w1, w2, a0, a1, a2, g1, g2,
                k_k, k_a, r_k, lnx_g, lnx_b, vres):
    bsz, s, d = x.shape
    dx = jnp.pad(x, ((0, 0), (1, 0), (0, 0)))[:, :s] - x
    xr, xw, xk, xv, xa, xg = (x + dx * mu[j] for j in range(6))
    r = xr @ w_r
    k = xk @ w_k
    v = xv @ w_v
    log_w = -jax.nn.softplus(-(w0 + jnp.tanh(xw @ w1) @ w2).astype(F32)) - 0.5
    decay = jnp.exp(-jnp.exp(log_w))
    a = jax.nn.sigmoid((a0 + (xa @ a1) @ a2).astype(F32))
    g = jax.nn.sigmoid(xg @ g1) @ g2
    if vres is None:
        v_first = v
    else:
        v0, v1, v2 = vres
        v = v + (v_first - v) * jax.nn.sigmoid(v0 + (xv @ v1) @ v2)

    def heads(t):
        return t.astype(F32).reshape(bsz, s, C_HEADS, C_HEAD_DIM)

    hd = (C_HEADS, C_HEAD_DIM)
    rh, vh, ah, wh = heads(r), heads(v), heads(a), heads(decay)
    kk = heads(k) * k_k.astype(F32).reshape(hd)
    kk = kk / jnp.maximum(jnp.sqrt(jnp.sum(kk * kk, -1, keepdims=True)), KK_EPS)
    kh = heads(k) * (1.0 + (ah - 1.0) * k_a.astype(F32).reshape(hd))

    def tm(t):
        return jnp.moveaxis(t, 1, 0)

    def step(state, xs):
        r_t, w_t, k_t, v_t, kk_t, a_t = xs
        u = jnp.einsum('bhvk,bhk->bhv', state, kk_t)
        state = (state * w_t[:, :, None, :] - u[..., None] * (kk_t * a_t)[:, :, None, :]
                 + v_t[..., None] * k_t[:, :, None, :])
        return state, jnp.einsum('bhvk,bhk->bhv', state, r_t)

    s0 = jnp.zeros((bsz, C_HEADS, C_HEAD_DIM, C_HEAD_DIM), F32)
    _, y = lax.scan(step, s0, (tm(rh), tm(wh), tm(kh), tm(vh), tm(kk), tm(ah)))
    y = jnp.moveaxis(y, 0, 1)
    ym = jnp.mean(y, -1, keepdims=True)
    yv = jnp.mean(jnp.square(y - ym), -1, keepdims=True)
    y = ((y - ym) * lax.rsqrt(yv + LNX_EPS)).reshape(bsz, s, d) * lnx_g.astype(F32) + lnx_b.astype(F32)
    bonus = jnp.sum(rh * kh * r_k.astype(F32).reshape(hd), -1, keepdims=True) * vh
    y = (y + bonus.reshape(bsz, s, d)).astype(x.dtype) * g
    return y @ w_o, v_first


def grouped_expert_mlp(xf, expert_idx, w_gate, w_up, w_down):
    t, kk = expert_idx.shape
    d = xf.shape[-1]
    m = t * kk
    e = w_gate.shape[0]
    flat_e = expert_idx.reshape(m)
    order = jnp.argsort(flat_e)
    sorted_e = flat_e[order]
    counts = jnp.bincount(flat_e, length=e)
    padded = (counts + MOE_BLOCK - 1) // MOE_BLOCK * MOE_BLOCK
    pad_end = jnp.cumsum(padded)
    pad_start = pad_end - padded
    start = jnp.cumsum(counts) - counts
    dest = pad_start[sorted_e] + jnp.arange(m) - start[sorted_e]
    n_blocks = (m + e * (MOE_BLOCK - 1) + MOE_BLOCK - 1) // MOE_BLOCK
    buf = jnp.zeros((n_blocks * MOE_BLOCK, d), xf.dtype).at[dest].set(xf[order // kk])
    block_e = jnp.minimum(
        jnp.searchsorted(pad_end, jnp.arange(n_blocks) * MOE_BLOCK, side='right'), e - 1)

    def expert_block(args):
        xb, ei = args
        hid = jax.nn.silu(xb @ w_gate[ei]) * (xb @ w_up[ei])
        return hid @ w_down[ei]

    yb = lax.map(expert_block, (buf.reshape(n_blocks, MOE_BLOCK, d), block_e))
    y_sorted = yb.reshape(-1, d)[dest]
    return jnp.zeros((m, d), yb.dtype).at[order].set(y_sorted).reshape(t, kk, d)


def hierarchical_moe(x, w_grp, b_grp, w_exp_r, b_exp_r, w_gate, w_up, w_down):
    bsz, s, d = x.shape
    xf = x.reshape(-1, d)
    t = xf.shape[0]
    g_prob = jax.nn.softmax((xf @ w_grp + b_grp).astype(F32), axis=-1)
    g_p, g_idx = lax.top_k(g_prob, 1)
    e_logits = (xf @ w_exp_r + b_exp_r).astype(F32).reshape(t, N_GROUPS, EXPERTS_PER_GROUP)
    e_logits = jnp.take_along_axis(e_logits, g_idx[:, :, None], axis=1)[:, 0]
    e_p, e_idx = lax.top_k(jax.nn.softmax(e_logits, axis=-1), MOE_TOPK)
    e_p = e_p / jnp.sum(e_p, -1, keepdims=True)
    gates = g_p * e_p
    expert = g_idx * EXPERTS_PER_GROUP + e_idx
    y = grouped_expert_mlp(xf, expert, w_gate, w_up, w_down)
    out = jnp.einsum('tk,tkd->td', gates.astype(y.dtype), y)
    return out.reshape(bsz, s, d)


def setup_inputs(seed: int = 0) -> dict:
    key = jax.random.key(seed)
    keys = iter(jax.random.split(key, 48))
    D = D_MODEL

    def nrm(shape, scale):
        return jax.random.normal(next(keys), shape, F32) * scale

    inp = {}
    inp['x'] = nrm((BATCH, SEQ, D), 1.0)
    inp['ev_w_in'] = nrm((N_EVEN, D, EVEN_IN), D ** -0.5)
    inp['ev_gate_bias'] = jnp.stack([nrm((N_EVEN, B_HEADS), 0.1),
                                     3.0 + nrm((N_EVEN, B_HEADS), 0.5)], axis=1)
    inp['ev_conv_w'] = nrm((N_EVEN, B_CONV, 2 * B_WIDTH), 0.5)
    inp['ev_conv_b'] = nrm((N_EVEN, 2 * B_WIDTH), 0.02)
    inp['ev_hnorm_g'] = 1.0 + nrm((N_EVEN, B_WIDTH), 0.05)
    inp['ev_w_out'] = nrm((N_EVEN, EVEN_MIX, D), EVEN_MIX ** -0.5 * DN_BETA)
    inp['od_mu'] = jax.random.uniform(next(keys), (N_ODD, 6, D), F32)
    inp['od_w_r'] = nrm((N_ODD, D, D), D ** -0.5)
    inp['od_w_k'] = nrm((N_ODD, D, D), D ** -0.5)
    inp['od_w_v'] = nrm((N_ODD, D, D), D ** -0.5)
    inp['od_w_o'] = nrm((N_ODD, D, D), D ** -0.5 * DN_BETA)
    inp['od_w0'] = -2.0 + nrm((N_ODD, D), 1.0)
    inp['od_w1'] = nrm((N_ODD, D, LORA_W), D ** -0.5)
    inp['od_w2'] = nrm((N_ODD, LORA_W, D), 0.1 * LORA_W ** -0.5)
    inp['od_a0'] = nrm((N_ODD, D), 0.1)
    inp['od_a1'] = nrm((N_ODD, D, LORA_A), D ** -0.5)
    inp['od_a2'] = nrm((N_ODD, LORA_A, D), 0.1 * LORA_A ** -0.5)
    inp['od_g1'] = nrm((N_ODD, D, LORA_G), D ** -0.5)
    inp['od_g2'] = nrm((N_ODD, LORA_G, D), LORA_G ** -0.5)
    inp['od_k_k'] = 0.85 + nrm((N_ODD, D), 0.05)
    inp['od_k_a'] = 1.0 + nrm((N_ODD, D), 0.05)
    inp['od_r_k'] = nrm((N_ODD, D), 0.1)
    inp['od_lnx_g'] = 1.0 + nrm((N_ODD, D), 0.05)
    inp['od_lnx_b'] = nrm((N_ODD, D), 0.02)
    inp['od_v0'] = 1.0 + nrm((N_VRES, D), 0.1)
    inp['od_v1'] = nrm((N_VRES, D, LORA_V), D ** -0.5)
    inp['od_v2'] = nrm((N_VRES, LORA_V, D), 0.1 * LORA_V ** -0.5)
    inp['ln_mix_g'] = 1.0 + nrm((DEPTH, D), 0.05)
    inp['ln_mix_b'] = nrm((DEPTH, D), 0.02)
    inp['ln_ffn_g'] = 1.0 + nrm((DEPTH, D), 0.05)
    inp['ln_ffn_b'] = nrm((DEPTH, D), 0.02)
    inp['moe_w_grp'] = nrm((DEPTH, D, N_GROUPS), D ** -0.5)
    inp['moe_b_grp'] = nrm((DEPTH, N_GROUPS), 0.01)
    inp['moe_w_exp_r'] = nrm((DEPTH, D, N_EXPERTS), D ** -0.5)
    inp['moe_b_exp_r'] = nrm((DEPTH, N_EXPERTS), 0.01)
    inp['moe_w_gate'] = nrm((DEPTH, N_EXPERTS, D, EXPERT_FF), D ** -0.5)
    inp['moe_w_up'] = nrm((DEPTH, N_EXPERTS, D, EXPERT_FF), D ** -0.5)
    inp['moe_w_down'] = nrm((DEPTH, N_EXPERTS, EXPERT_FF, D), EXPERT_FF ** -0.5 * DN_BETA)
    return inp


def reference(x, ev_w_in, ev_gate_bias, ev_conv_w, ev_conv_b, ev_hnorm_g, ev_w_out,
              od_mu, od_w_r, od_w_k, od_w_v, od_w_o, od_w0, od_w1, od_w2, od_a0, od_a1, od_a2,
              od_g1, od_g2, od_k_k, od_k_a, od_r_k, od_lnx_g, od_lnx_b, od_v0, od_v1, od_v2,
              ln_mix_g, ln_mix_b, ln_ffn_g, ln_ffn_b,
              moe_w_grp, moe_b_grp, moe_w_exp_r, moe_b_exp_r, moe_w_gate, moe_w_up, moe_w_down):
    v_first = None
    for layer in range(DEPTH):
        if layer % 2 == 0:
            e = layer // 2
            mix = even_mixer(x, ev_w_in[e], ev_gate_bias[e], ev_conv_w[e], ev_conv_b[e],
                             ev_hnorm_g[e], ev_w_out[e])
        else:
            o = layer // 2
            vres = None if o == 0 else (od_v0[o - 1], od_v1[o - 1], od_v2[o - 1])
            mix, v_first = rwkv7_mixer(x, v_first, od_mu[o], od_w_r[o], od_w_k[o], od_w_v[o],
                                       od_w_o[o], od_w0[o], od_w1[o], od_w2[o], od_a0[o],
                                       od_a1[o], od_a2[o], od_g1[o], od_g2[o], od_k_k[o],
                                       od_k_a[o], od_r_k[o], od_lnx_g[o], od_lnx_b[o], vres)
        x = layer_norm(DN_ALPHA * x + mix, ln_mix_g[layer], ln_mix_b[layer])
        ffn = hierarchical_moe(x, moe_w_grp[layer], moe_b_grp[layer], moe_w_exp_r[layer],
                               moe_b_exp_r[layer], moe_w_gate[layer], moe_w_up[layer],
                               moe_w_down[layer])
        x = layer_norm(DN_ALPHA * x + ffn, ln_ffn_g[layer], ln_ffn_b[layer])
    return x
```

```python
import functools
import math

import jax
import jax.numpy as jnp
from jax import lax
from jax.experimental import pallas as pl
from jax.experimental.pallas import tpu as pltpu

F32 = jnp.float32
BF16 = jnp.bfloat16

DEPTH = 4
DN_ALPHA = (2 * DEPTH) ** 0.25
LN_EPS = 1e-5

A_HEADS = 8
A_HEAD_DIM = 128
A_WIDTH = A_HEADS * A_HEAD_DIM
MOBA_BLOCK = 256
MOBA_TOPK = 3
ROPE_THETA = 500000.0
ROPE_DIM = A_HEAD_DIM // 4
NEG_INF = -1e30

B_HEADS = 4
B_HEAD_DIM = 256
B_WIDTH = B_HEADS * B_HEAD_DIM
MLSTM_CHUNK = 64
B_CONV = 4
HNORM_EPS = 1e-6
EVEN_MAIN = 3 * A_WIDTH + 4 * B_WIDTH

C_HEAD_DIM = 64
LNX_EPS = 64e-5
KK_EPS = 1e-12

N_GROUPS = 4
EXPERTS_PER_GROUP = 8
N_EXPERTS = N_GROUPS * EXPERTS_PER_GROUP
MOE_TOPK = 2
MOE_ROWS = 256

LANES = 128
SUBLANES = 8
VMEM_LIMIT = 56 * 1024 * 1024

_NT = (((1,), (1,)), ((), ()))
_TN = (((0,), (0,)), ((), ()))


def _params(*sem):
    return pltpu.CompilerParams(dimension_semantics=sem, vmem_limit_bytes=VMEM_LIMIT)


def _mm_kernel(x_ref, w_ref, o_ref):
    o_ref[...] = jnp.dot(x_ref[...].astype(BF16), w_ref[...],
                         preferred_element_type=F32).astype(o_ref.dtype)


def _matmul(x, w, out_dtype=F32, tm=512, tn=512):
    m, k = x.shape
    n = w.shape[1]
    tm, tn = min(tm, m), min(tn, n)
    return pl.pallas_call(
        _mm_kernel,
        grid=(m // tm, n // tn),
        in_specs=[pl.BlockSpec((tm, k), lambda i, j: (i, 0)),
                  pl.BlockSpec((k, tn), lambda i, j: (0, j))],
        out_specs=pl.BlockSpec((tm, tn), lambda i, j: (i, j)),
        out_shape=jax.ShapeDtypeStruct((m, n), out_dtype),
        compiler_params=_params("parallel", "arbitrary"),
        name="matmul",
    )(x, w)


def _mm2_kernel(xa_ref, xb_ref, wa_ref, wb_ref, o_ref):
    acc = jnp.dot(xa_ref[...], wa_ref[...], preferred_element_type=F32)
    acc = acc + jnp.dot(xb_ref[...], wb_ref[...], preferred_element_type=F32)
    o_ref[...] = acc


def _matmul2(xa, xb, wa, wb, tm=512, tn=512):
    m, ka = xa.shape
    kb = xb.shape[1]
    n = wa.shape[1]
    tm, tn = min(tm, m), min(tn, n)
    return pl.pallas_call(
        _mm2_kernel,
        grid=(m // tm, n // tn),
        in_specs=[pl.BlockSpec((tm, ka), lambda i, j: (i, 0)),
                  pl.BlockSpec((tm, kb), lambda i, j: (i, 0)),
                  pl.BlockSpec((ka, tn), lambda i, j: (0, j)),
                  pl.BlockSpec((kb, tn), lambda i, j: (0, j))],
        out_specs=pl.BlockSpec((tm, tn), lambda i, j: (i, j)),
        out_shape=jax.ShapeDtypeStruct((m, n), F32),
        compiler_params=_params("parallel", "arbitrary"),
        name="matmul2",
    )(xa, xb, wa, wb)


def _ln_rows(h, g, b):
    mu = jnp.mean(h, axis=-1, keepdims=True)
    hc = h - mu
    var = jnp.mean(hc * hc, axis=-1, keepdims=True)
    return hc * lax.rsqrt(var + LN_EPS) * g + b


def _resid_ln_kernel(x_ref, m_ref, g_ref, b_ref, o_ref):
    o_ref[...] = _ln_rows(DN_ALPHA * x_ref[...] + m_ref[...], g_ref[...], b_ref[...])


def _resid_ln(x, mix, g, b, tm=256):
    t, d = x.shape
    row = pl.BlockSpec((tm, d), lambda i: (i, 0))
    vec = pl.BlockSpec((1, d), lambda i: (0, 0))
    return pl.pallas_call(
        _resid_ln_kernel,
        grid=(t // tm,),
        in_specs=[row, row, vec, vec],
        out_specs=row,
        out_shape=jax.ShapeDtypeStruct((t, d), F32),
        compiler_params=_params("parallel"),
        name="resid_ln",
    )(x, mix, g.reshape(1, d), b.reshape(1, d))


def _rope_tables(s):
    half = ROPE_DIM // 2
    inv = jnp.power(jnp.float32(ROPE_THETA), -jnp.arange(half, dtype=F32) / half)
    ang = jnp.arange(s).astype(F32)[:, None] * inv[None, :]
    cos, sin = jnp.cos(ang), jnp.sin(ang)
    zero = jnp.zeros((s, A_HEAD_DIM - ROPE_DIM), F32)
    zh = jnp.zeros((s, half), F32)
    c = jnp.concatenate([cos, cos, zero + 1.0], axis=-1)
    s_up = jnp.concatenate([zh, sin, zero], axis=-1)
    s_dn = jnp.concatenate([-sin, zh, zero], axis=-1)
    return c, s_up, s_dn


def _rope(x, c, s_up, s_dn):
    half = ROPE_DIM // 2
    return (x * c + pltpu.roll(x, half, axis=1) * s_up
            + pltpu.roll(x, A_HEAD_DIM - half, axis=1) * s_dn)


def _moba_kernel(q_ref, k_ref, v_ref, c_ref, su_ref, sd_ref, o_ref, krot_ref, vb_ref, kmean_ref, *, nblk):
    qi = pl.program_id(2)
    blk = MOBA_BLOCK
    scale = A_HEAD_DIM ** -0.5

    @pl.when(qi == 0)
    def _prepare_keys():
        def body(j, carry):
            r = pl.multiple_of(j * blk, blk)
            rows = pl.ds(r, blk)
            kr = _rope(k_ref[rows, :], c_ref[rows, :], su_ref[rows, :], sd_ref[rows, :])
            krot_ref[rows, :] = kr.astype(BF16)
            vb_ref[rows, :] = v_ref[rows, :].astype(BF16)
            kmean_ref[pl.ds(j, 1), :] = jnp.mean(kr, axis=0, keepdims=True)
            return carry
        lax.fori_loop(0, nblk, body, 0)

    r0 = pl.multiple_of(qi * blk, blk)
    qrows = pl.ds(r0, blk)
    qb = _rope(q_ref[...], c_ref[qrows, :], su_ref[qrows, :], sd_ref[qrows, :]).astype(BF16)

    g = lax.dot_general(kmean_ref[...].astype(BF16), qb, _NT, preferred_element_type=F32)
    brow = lax.broadcasted_iota(jnp.int32, (nblk, blk), 0)
    g = jnp.where(brow < qi, g, -jnp.inf)
    sel_t = jnp.zeros((nblk, blk), F32)
    for j in range(nblk):
        gj = g[j:j + 1, :]
        earlier = jnp.where(brow < j, 1.0, 0.0)
        beats = jnp.where(g > gj, 1.0, jnp.where(g == gj, earlier, 0.0))
        cnt = jnp.sum(beats, axis=0, keepdims=True)
        sel_t = jnp.where(brow == j, jnp.where(cnt < float(min(MOBA_TOPK, nblk)), 1.0, 0.0), sel_t)
    sel_t = jnp.where(brow < qi, sel_t, 0.0)
    eye = jnp.where(lax.broadcasted_iota(jnp.int32, (blk, blk), 0)
                    == lax.broadcasted_iota(jnp.int32, (blk, blk), 1), 1.0, 0.0).astype(BF16)
    sel = lax.dot_general(eye, sel_t.astype(BF16), _NT, preferred_element_type=F32)
    bcol = lax.broadcasted_iota(jnp.int32, (blk, nblk), 1)

    kd = krot_ref[qrows, :]
    s = lax.dot_general(qb, kd, _NT, preferred_element_type=F32) * scale
    causal = (lax.broadcasted_iota(jnp.int32, (blk, blk), 1)
              <= lax.broadcasted_iota(jnp.int32, (blk, blk), 0))
    s = jnp.where(causal, s, NEG_INF)
    m0 = jnp.max(s, axis=1, keepdims=True)
    p = jnp.exp(s - m0)
    l0 = jnp.sum(p, axis=1, keepdims=True)
    acc0 = jnp.dot(p.astype(BF16), vb_ref[qrows, :], preferred_element_type=F32)

    def past_block(j, carry):
        m, l, acc = carry
        rows = pl.ds(pl.multiple_of(j * blk, blk), blk)
        sj = lax.dot_general(qb, krot_ref[rows, :], _NT, preferred_element_type=F32) * scale
        picked = jnp.sum(jnp.where(bcol == j, sel, 0.0), axis=1, keepdims=True)
        sj = jnp.where(picked > 0.5, sj, NEG_INF)
        m_new = jnp.maximum(m, jnp.max(sj, axis=1, keepdims=True))
        a = jnp.exp(m - m_new)
        pj = jnp.exp(sj - m_new)
        l = a * l + jnp.sum(pj, axis=1, keepdims=True)
        acc = a * acc + jnp.dot(pj.astype(BF16), vb_ref[rows, :], preferred_element_type=F32)
        return m_new, l, acc

    _, l, acc = lax.fori_loop(0, qi, past_block, (m0, l0, acc0))
    o_ref[...] = (acc / l).astype(o_ref.dtype)


def _moba(z, bsz, s):
    nblk = s // MOBA_BLOCK
    c, s_up, s_dn = _rope_tables(s)
    tab = pl.BlockSpec((s, A_HEAD_DIM), lambda b, h, qi: (0, 0))
    return pl.pallas_call(
        functools.partial(_moba_kernel, nblk=nblk),
        grid=(bsz, A_HEADS, nblk),
        in_specs=[pl.BlockSpec((MOBA_BLOCK, A_HEAD_DIM), lambda b, h, qi: (b * nblk + qi, h)),
                  pl.BlockSpec((s, A_HEAD_DIM), lambda b, h, qi: (b, A_HEADS + h)),
                  pl.BlockSpec((s, A_HEAD_DIM), lambda b, h, qi: (b, 2 * A_HEADS + h)),
                  tab, tab, tab],
        out_specs=pl.BlockSpec((MOBA_BLOCK, A_HEAD_DIM), lambda b, h, qi: (b * nblk + qi, h)),
        out_shape=jax.ShapeDtypeStruct((bsz * s, A_WIDTH), BF16),
        scratch_shapes=[pltpu.VMEM((s, A_HEAD_DIM), BF16), pltpu.VMEM((s, A_HEAD_DIM), BF16),
                        pltpu.VMEM((nblk, A_HEAD_DIM), F32)],
        compiler_params=_params("parallel", "parallel", "arbitrary"),
        name="moba",
    )(z, z, z, c, s_up, s_dn)


def _log_sigmoid(x):
    return jnp.minimum(x, 0.0) - jnp.log1p(jnp.exp(-jnp.abs(x)))


def _shift_rows(x, prev8, d):
    if d == 0:
        return x
    rolled = pltpu.roll(x, d, axis=0)
    top = jnp.where(lax.broadcasted_iota(jnp.int32, prev8.shape, 0) < d,
                    pltpu.roll(prev8, d, axis=0), rolled[:SUBLANES])
    return jnp.concatenate([top, rolled[SUBLANES:]], axis=0)


def _mlstm_kernel(q_ref, k_ref, v_ref, o_ref, gc_ref, gr_ref, bc_ref, br_ref,
                  cwq_ref, cwk_ref, cbq_ref, cbk_ref, hg_ref, y_ref,
                  c_ref, n_ref, m_ref, pq_ref, pk_ref, *, tt):
    h = pl.program_id(1)
    ti = pl.program_id(2)
    L = MLSTM_CHUNK
    dk = B_HEAD_DIM

    @pl.when(ti == 0)
    def _reset():
        c_ref[...] = jnp.zeros_like(c_ref)
        n_ref[...] = jnp.zeros_like(n_ref)
        m_ref[...] = jnp.zeros_like(m_ref)
        pq_ref[...] = jnp.zeros_like(pq_ref)
        pk_ref[...] = jnp.zeros_like(pk_ref)

    def conv_silu(x_ref, p_ref, w_ref, b_ref):
        x = x_ref[...]
        prev8 = p_ref[...]
        out = b_ref[...] + x * w_ref[B_CONV - 1:B_CONV, :]
        for d in range(1, B_CONV):
            out = out + _shift_rows(x, prev8, d) * w_ref[B_CONV - 1 - d:B_CONV - d, :]
        p_ref[...] = x[tt - SUBLANES:, :]
        return out * jax.nn.sigmoid(out)

    q_all = conv_silu(q_ref, pq_ref, cwq_ref, cbq_ref)
    k_all = conv_silu(k_ref, pk_ref, cwk_ref, cbk_ref) * (dk ** -0.5)

    lane8 = lax.broadcasted_iota(jnp.int32, (tt, 2 * B_HEADS), 1)
    gcol = gc_ref[...] + bc_ref[...]
    i_col_all = jnp.sum(jnp.where(lane8 == h, gcol, 0.0), axis=1, keepdims=True)
    f_col_all = _log_sigmoid(jnp.sum(jnp.where(lane8 == B_HEADS + h, gcol, 0.0), axis=1, keepdims=True))
    i_row_all = gr_ref[pl.ds(h, 1), :] + br_ref[pl.ds(h, 1), :]
    f_row_all = _log_sigmoid(gr_ref[pl.ds(B_HEADS + h, 1), :] + br_ref[pl.ds(B_HEADS + h, 1), :])

    rr = lax.broadcasted_iota(jnp.int32, (L, L), 0)
    cc = lax.broadcasted_iota(jnp.int32, (L, L), 1)
    tri = cc <= rr

    for c in range(tt // L):
        lo, hi = c * L, (c + 1) * L
        qc = q_all[lo:hi]
        kc = k_all[lo:hi]
        qcb = qc.astype(BF16)
        vcb = v_ref[lo:hi, :].astype(BF16)
        i_col, f_col = i_col_all[lo:hi], f_col_all[lo:hi]
        i_row, f_row = i_row_all[:, lo:hi], f_row_all[:, lo:hi]
        m_prev = m_ref[...]
        cst = c_ref[...]
        nst = n_ref[...]

        b_col = jnp.sum(jnp.where(tri, f_row, 0.0), axis=1, keepdims=True)
        b_row = jnp.sum(jnp.where(rr <= cc, f_col, 0.0), axis=0, keepdims=True)
        log_inter = b_col + m_prev
        log_intra = jnp.where(tri, b_col - b_row + i_row, -jnp.inf)
        m_t = jnp.maximum(log_inter, jnp.max(log_intra, axis=1, keepdims=True))
        w_intra = jnp.exp(log_intra - m_t)
        w_inter = jnp.exp(log_inter - m_t)
        qk = lax.dot_general(qcb, kc.astype(BF16), _NT, preferred_element_type=F32) * w_intra
        num = (w_inter * jnp.dot(qcb, cst.astype(BF16), preferred_element_type=F32)
               + jnp.dot(qk.astype(BF16), vcb, preferred_element_type=F32))
        den = (w_inter * jnp.sum(qc * nst, axis=1, keepdims=True)
               + jnp.sum(qk, axis=1, keepdims=True))
        hc = num / jnp.maximum(jnp.abs(den), jnp.exp(-m_t))

        b_last = jnp.sum(f_col, axis=0, keepdims=True)
        log_s = b_last - b_col + i_col
        m_new = jnp.maximum(b_last + m_prev, jnp.max(log_s, axis=0, keepdims=True))
        decay = jnp.exp(b_last + m_prev - m_new)
        kw = kc * jnp.exp(log_s - m_new)
        c_ref[...] = decay * cst + lax.dot_general(kw.astype(BF16), vcb, _TN, preferred_element_type=F32)
        n_ref[...] = decay * nst + jnp.sum(kw, axis=0, keepdims=True)
        m_ref[...] = m_new

        hn = hc * lax.rsqrt(jnp.mean(hc * hc, axis=1, keepdims=True) + HNORM_EPS)
        y_ref[lo:hi, :] = (hn * hg_ref[...] * jax.nn.sigmoid(o_ref[lo:hi, :])).astype(y_ref.dtype)


def _mlstm(z, gates, gate_bias, conv_w, conv_b, hnorm_g, bsz, s, tt=256):
    nt = s // tt
    cb = 3 * A_WIDTH // B_HEAD_DIM
    gates_t = gates.T
    bias = gate_bias.reshape(1, 2 * B_HEADS)

    def col(off):
        return pl.BlockSpec((tt, B_HEAD_DIM), lambda b, h, t: (b * nt + t, cb + off + h))

    def par(rows, off):
        return pl.BlockSpec((rows, B_HEAD_DIM), lambda b, h, t: (0, off + h))

    return pl.pallas_call(
        functools.partial(_mlstm_kernel, tt=tt),
        grid=(bsz, B_HEADS, nt),
        in_specs=[col(0), col(B_HEADS), col(2 * B_HEADS), col(3 * B_HEADS),
                  pl.BlockSpec((tt, 2 * B_HEADS), lambda b, h, t: (b * nt + t, 0)),
                  pl.BlockSpec((2 * B_HEADS, tt), lambda b, h, t: (0, b * nt + t)),
                  pl.BlockSpec((1, 2 * B_HEADS), lambda b, h, t: (0, 0)),
                  pl.BlockSpec((2 * B_HEADS, 1), lambda b, h, t: (0, 0)),
                  par(B_CONV, 0), par(B_CONV, B_HEADS), par(1, 0), par(1, B_HEADS), par(1, 0)],
        out_specs=pl.BlockSpec((tt, B_HEAD_DIM), lambda b, h, t: (b * nt + t, h)),
        out_shape=jax.ShapeDtypeStruct((bsz * s, B_WIDTH), BF16),
        scratch_shapes=[pltpu.VMEM((B_HEAD_DIM, B_HEAD_DIM), F32), pltpu.VMEM((1, B_HEAD_DIM), F32),
                        pltpu.VMEM((1, 1), F32), pltpu.VMEM((SUBLANES, B_HEAD_DIM), F32),
                        pltpu.VMEM((SUBLANES, B_HEAD_DIM), F32)],
        compiler_params=_params("parallel", "parallel", "arbitrary"),
        name="mlstm",
    )(z, z, z, z, gates, gates_t, bias, bias.T, conv_w, conv_w,
      conv_b.reshape(1, -1), conv_b.reshape(1, -1), hnorm_g.reshape(1, -1))


def _even_mixer(x, w_in, gate_bias, conv_w, conv_b, hnorm_g, w_out, bsz, s):
    d = x.shape[1]
    w_main = w_in[:, :EVEN_MAIN].astype(BF16)
    w_gate = jnp.pad(w_in[:, EVEN_MAIN:], ((0, 0), (0, LANES - 2 * B_HEADS))).astype(BF16)
    z = _matmul(x, w_main)
    gates = _matmul(x, w_gate, tn=LANES)[:, :2 * B_HEADS]
    y_a = _moba(z, bsz, s)
    y_b = _mlstm(z, gates, gate_bias, conv_w, conv_b, hnorm_g, bsz, s)
    w_o = w_out.astype(BF16)
    return _matmul2(y_a, y_b, w_o[:A_WIDTH], w_o[A_WIDTH:])


def _shift_mix_kernel(x_ref, p_ref, mu_ref, *o_refs, tiles_per_seq):
    x = x_ref[...]
    first = (pl.program_id(0) % tiles_per_seq) == 0
    prev_row = jnp.where(first, 0.0, p_ref[SUBLANES - 1:SUBLANES, :])
    row = lax.broadcasted_iota(jnp.int32, x.shape, 0)
    dx = jnp.where(row == 0, prev_row, pltpu.roll(x, 1, axis=0)) - x
    for j, o_ref in enumerate(o_refs):
        o_ref[...] = (x + dx * mu_ref[j:j + 1, :]).astype(o_ref.dtype)


def _shift_mix(x, mu, s, tm=256):
    t, d = x.shape
    n = mu.shape[0]
    row = pl.BlockSpec((tm, d), lambda i: (i, 0))
    prev = pl.BlockSpec((SUBLANES, d), lambda i: (jnp.maximum(i * (tm // SUBLANES) - 1, 0), 0))
    return pl.pallas_call(
        functools.partial(_shift_mix_kernel, tiles_per_seq=s // tm),
        grid=(t // tm,),
        in_specs=[row, prev, pl.BlockSpec((n, d), lambda i: (0, 0))],
        out_specs=[row] * n,
        out_shape=[jax.ShapeDtypeStruct((t, d), BF16)] * n,
        compiler_params=_params("parallel"),
        name="shift_mix",
    )(x, x, mu)


def _softplus(x):
    return jnp.maximum(x, 0.0) + jnp.log1p(jnp.exp(-jnp.abs(x)))


def _lora_kernel(x_ref, w1_ref, w2_ref, b_ref, o_ref, *, mid, out):
    hmid = jnp.dot(x_ref[...], w1_ref[...], preferred_element_type=F32)
    if mid == "tanh":
        hmid = jnp.tanh(hmid)
    elif mid == "sigmoid":
        hmid = jax.nn.sigmoid(hmid)
    y = jnp.dot(hmid.astype(BF16), w2_ref[...], preferred_element_type=F32)
    if out == "decay":
        y = jnp.exp(-jnp.exp(-_softplus(-(b_ref[...] + y)) - 0.5))
    elif out == "sigmoid":
        y = jax.nn.sigmoid(b_ref[...] + y)
    o_ref[...] = y


def _lora(x, w1, w2, bias, mid, out, tm=512):
    t, d = x.shape
    tm = min(tm, t)
    r = w1.shape[1]
    rp = -(-r // LANES) * LANES
    w1p = jnp.pad(w1, ((0, 0), (0, rp - r))).astype(BF16)
    w2p = jnp.pad(w2, ((0, rp - r), (0, 0))).astype(BF16)
    dout = w2.shape[1]
    if bias is None:
        bias = jnp.zeros((dout,), F32)
    return pl.pallas_call(
        functools.partial(_lora_kernel, mid=mid, out=out),
        grid=(t // tm,),
        in_specs=[pl.BlockSpec((tm, d), lambda i: (i, 0)),
                  pl.BlockSpec((d, rp), lambda i: (0, 0)),
                  pl.BlockSpec((rp, dout), lambda i: (0, 0)),
                  pl.BlockSpec((1, dout), lambda i: (0, 0))],
        out_specs=pl.BlockSpec((tm, dout), lambda i: (i, 0)),
        out_shape=jax.ShapeDtypeStruct((t, dout), F32),
        compiler_params=_params("parallel"),
        name="lora_" + out,
    )(x, w1p, w2p, bias.reshape(1, dout))


def _head_ones():
    i = jnp.arange(LANES) // C_HEAD_DIM
    return (i[:, None] == i[None, :]).astype(F32)


def _head_sum(x, ones):
    cols = [jnp.dot(x[:, c * LANES:(c + 1) * LANES], ones, precision=lax.Precision.HIGHEST,
                    preferred_element_type=F32) for c in range(x.shape[1] // LANES)]
    return jnp.concatenate(cols, axis=1)


def _rwkv_prep_kernel(*refs, vres):
    if vres:
        k_ref, a_ref, v_ref, vf_ref, vg_ref, kk_p, ka_p, ones_ref, kk_o, kh_o, bb_o, v_o = refs
    else:
        k_ref, a_ref, v_ref, kk_p, ka_p, ones_ref, kk_o, kh_o, bb_o, v_o = refs
    k = k_ref[...]
    a = a_ref[...]
    kk = k * kk_p[...]
    nrm = jnp.sqrt(_head_sum(kk * kk, ones_ref[...]))
    kk = kk / jnp.maximum(nrm, KK_EPS)
    kk_o[...] = kk
    bb_o[...] = kk * a
    kh_o[...] = k * (1.0 + (a - 1.0) * ka_p[...])
    v = v_ref[...]
    if vres:
        v = v + (vf_ref[...] - v) * vg_ref[...]
    v_o[...] = v


def _rwkv_prep(k, a, v, v_first, vgate, k_k, k_a, tm=256):
    t, d = k.shape
    row = pl.BlockSpec((tm, d), lambda i: (i, 0))
    vec = pl.BlockSpec((1, d), lambda i: (0, 0))
    vres = v_first is not None
    ins = [k, a, v] + ([v_first, vgate] if vres else []) + [k_k.reshape(1, d), k_a.reshape(1, d), _head_ones()]
    specs = [row] * (5 if vres else 3) + [vec, vec, pl.BlockSpec((LANES, LANES), lambda i: (0, 0))]
    return pl.pallas_call(
        functools.partial(_rwkv_prep_kernel, vres=vres),
        grid=(t // tm,),
        in_specs=specs,
        out_specs=[row] * 4,
        out_shape=[jax.ShapeDtypeStruct((t, d), F32)] * 4,
        compiler_params=_params("parallel"),
        name="rwkv_prep",
    )(*ins)


def _seg_bcast(x, ones_b):
    hi = x.astype(BF16)
    lo = (x - hi.astype(F32)).astype(BF16)
    return (jnp.dot(hi, ones_b, preferred_element_type=F32)
            + jnp.dot(lo, ones_b, preferred_element_type=F32))


def _rwkv_scan_kernel(r_ref, w_ref, kh_ref, v_ref, kk_ref, bb_ref, ones_ref, mask_ref, y_ref, s_ref, *, tt, ng):
    n = C_HEAD_DIM

    @pl.when(pl.program_id(1) == 0)
    def _reset():
        s_ref[...] = jnp.zeros_like(s_ref)

    ones_b = ones_ref[...]
    mask = mask_ref[...]
    shape4 = (ng, n, SUBLANES, LANES)
    rows = ng * n * SUBLANES

    def bsum(x):
        return _seg_bcast(x.reshape(rows, LANES), ones_b).reshape(shape4)

    def step(t, carry):
        st = s_ref[...]
        kk = kk_ref[t][:, None]
        u = bsum(st * kk)
        vv = bsum(v_ref[t][:, None] * mask[None])
        st = st * w_ref[t][:, None] - u * bb_ref[t][:, None] + vv * kh_ref[t][:, None]
        s_ref[...] = st
        yb = bsum(st * r_ref[t][:, None])
        y_ref[t] = jnp.sum(yb * mask[None], axis=1)
        return carry

    lax.fori_loop(0, tt, step, 0)


def _rwkv_scan(r, w, kh, v, kk, bb, bsz, s, tt=128):
    t, d = r.shape
    ng = d // (SUBLANES * LANES)
    nt = s // tt
    view = lambda x: x.reshape(t, ng, SUBLANES, LANES)
    ones_b = _head_ones().astype(BF16)
    lane = jnp.arange(LANES) % C_HEAD_DIM
    mask = (lane[None, None, :] == jnp.arange(C_HEAD_DIM)[:, None, None]).astype(F32)
    mask = jnp.broadcast_to(mask, (C_HEAD_DIM, SUBLANES, LANES))
    blk = pl.BlockSpec((tt, ng, SUBLANES, LANES), lambda b, i: (b * nt + i, 0, 0, 0))
    y = pl.pallas_call(
        functools.partial(_rwkv_scan_kernel, tt=tt, ng=ng),
        grid=(bsz, nt),
        in_specs=[blk] * 6 + [pl.BlockSpec((LANES, LANES), lambda b, i: (0, 0)),
                              pl.BlockSpec((C_HEAD_DIM, SUBLANES, LANES), lambda b, i: (0, 0, 0))],
        out_specs=blk,
        out_shape=jax.ShapeDtypeStruct((t, ng, SUBLANES, LANES), F32),
        scratch_shapes=[pltpu.VMEM((ng, C_HEAD_DIM, SUBLANES, LANES), F32)],
        compiler_params=_params("parallel", "arbitrary"),
        name="rwkv_scan",
    )(view(r), view(w), view(kh), view(v), view(kk), view(bb), ones_b, mask)
    return y.reshape(t, d)


def _rwkv_post_kernel(y_ref, r_ref, kh_ref, v_ref, g_ref, lg_ref, lb_ref, rk_ref, ones_ref, o_ref):
    ones = ones_ref[...]
    y = y_ref[...]
    inv_n = 1.0 / C_HEAD_DIM
    yc = y - _head_sum(y, ones) * inv_n
    yv = _head_sum(yc * yc, ones) * inv_n
    yn = yc * lax.rsqrt(yv + LNX_EPS) * lg_ref[...] + lb_ref[...]
    bonus = _head_sum(r_ref[...] * kh_ref[...] * rk_ref[...], ones) * v_ref[...]
    o_ref[...] = ((yn + bonus) * g_ref[...]).astype(o_ref.dtype)


def _rwkv_post(y, r, kh, v, g, lnx_g, lnx_b, r_k, tm=256):
    t, d = y.shape
    row = pl.BlockSpec((tm, d), lambda i: (i, 0))
    vec = pl.BlockSpec((1, d), lambda i: (0, 0))
    return pl.pallas_call(
        _rwkv_post_kernel,
        grid=(t // tm,),
        in_specs=[row] * 5 + [vec] * 3 + [pl.BlockSpec((LANES, LANES), lambda i: (0, 0))],
        out_specs=row,
        out_shape=jax.ShapeDtypeStruct((t, d), BF16),
        compiler_params=_params("parallel"),
        name="rwkv_post",
    )(y, r, kh, v, g, lnx_g.reshape(1, d), lnx_b.reshape(1, d), r_k.reshape(1, d), _head_ones())


def _rwkv7_mixer(x, v_first, mu, w_r, w_k, w_v, w_o, w0, w1, w2, a0, a1, a2, g1, g2,
                 k_k, k_a, r_k, lnx_g, lnx_b, vres, bsz, s):
    xr, xw, xk, xv, xa, xg = _shift_mix(x, mu, s)
    r = _matmul(xr, w_r.astype(BF16))
    k = _matmul(xk, w_k.astype(BF16))
    v = _matmul(xv, w_v.astype(BF16))
    decay = _lora(xw, w1, w2, w0, "tanh", "decay")
    a = _lora(xa, a1, a2, a0, "none", "sigmoid")
    g = _lora(xg, g1, g2, None, "sigmoid", "none")
    if vres is None:
        vgate = None
        v_keep = v
    else:
        v0, v1, v2 = vres
        vgate = _lora(xv, v1, v2, v0, "none", "sigmoid")
        v_keep = v_first
    kk, kh, bb, v = _rwkv_prep(k, a, v, None if vres is None else v_first, vgate, k_k, k_a)
    y = _rwkv_scan(r, decay, kh, v, kk, bb, bsz, s)
    yo = _rwkv_post(y, r, kh, v, g, lnx_g, lnx_b, r_k)
    return _matmul(yo, w_o.astype(BF16)), v_keep


def _gather_rows_kernel(idx_ref, src_ref, o_ref, sem, *, rows):
    def start(r, carry):
        pltpu.make_async_copy(src_ref.at[pl.ds(idx_ref[0, r], 1)], o_ref.at[pl.ds(r, 1)], sem).start()
        return carry

    def wait(r, carry):
        pltpu.make_async_copy(src_ref.at[pl.ds(0, 1)], o_ref.at[pl.ds(r, 1)], sem).wait()
        return carry

    lax.fori_loop(0, rows, start, 0)
    lax.fori_loop(0, rows, wait, 0)


def _gather_rows(src, idx, rows=MOE_ROWS):
    n = idx.shape[0]
    d = src.shape[1]
    nb = n // rows
    return pl.pallas_call(
        functools.partial(_gather_rows_kernel, rows=rows),
        grid=(nb,),
        in_specs=[pl.BlockSpec((None, 1, rows), lambda i: (i, 0, 0), memory_space=pltpu.SMEM),
                  pl.BlockSpec(memory_space=pl.ANY)],
        out_specs=pl.BlockSpec((rows, d), lambda i: (i, 0)),
        out_shape=jax.ShapeDtypeStruct((n, d), src.dtype),
        scratch_shapes=[pltpu.SemaphoreType.DMA(())],
        compiler_params=_params("arbitrary"),
        name="gather_rows",
    )(idx.reshape(nb, 1, rows), src)


def _expert_kernel(be_ref, nu_ref, x_ref, wg_ref, wu_ref, wd_ref, o_ref):
    used = pl.program_id(0) < nu_ref[0]

    @pl.when(used)
    def _():
        xb = x_ref[...].astype(BF16)
        hg = jnp.dot(xb, wg_ref[...], preferred_element_type=F32)
        hu = jnp.dot(xb, wu_ref[...], preferred_element_type=F32)
        hid = (hg * jax.nn.sigmoid(hg) * hu).astype(BF16)
        o_ref[...] = jnp.dot(hid, wd_ref[...], preferred_element_type=F32)

    @pl.when(jnp.logical_not(used))
    def _():
        o_ref[...] = jnp.zeros_like(o_ref)


def _expert_mlp(xs, block_e, n_used, w_gate, w_up, w_down, rows=MOE_ROWS):
    n, d = xs.shape
    ff = w_gate.shape[2]
    nb = n // rows
    grid_spec = pltpu.PrefetchScalarGridSpec(
        num_scalar_prefetch=2,
        grid=(nb,),
        in_specs=[pl.BlockSpec((rows, d), lambda i, be, nu: (jnp.minimum(i, nu[0] - 1), 0)),
                  pl.BlockSpec((None, d, ff), lambda i, be, nu: (be[i], 0, 0)),
                  pl.BlockSpec((None, d, ff), lambda i, be, nu: (be[i], 0, 0)),
                  pl.BlockSpec((None, ff, d), lambda i, be, nu: (be[i], 0, 0))],
        out_specs=pl.BlockSpec((rows, d), lambda i, be, nu: (i, 0)),
    )
    return pl.pallas_call(
        _expert_kernel,
        grid_spec=grid_spec,
        out_shape=jax.ShapeDtypeStruct((n, d), F32),
        compiler_params=_params("arbitrary"),
        name="expert_mlp",
    )(block_e, n_used, xs, w_gate, w_up, w_down)


def _combine_ln_kernel(idx_ref, y_ref, x_ref, gt_ref, g_ref, b_ref, o_ref, buf, sem, *, rows):
    def start(r, carry):
        for kslot in range(MOE_TOPK):
            pltpu.make_async_copy(y_ref.at[pl.ds(idx_ref[0, MOE_TOPK * r + kslot], 1)],
                                  buf.at[kslot, pl.ds(r, 1)], sem).start()
        return carry

    def wait(r, carry):
        for kslot in range(MOE_TOPK):
            pltpu.make_async_copy(y_ref.at[pl.ds(0, 1)], buf.at[kslot, pl.ds(r, 1)], sem).wait()
        return carry

    lax.fori_loop(0, rows, start, 0)
    lax.fori_loop(0, rows, wait, 0)
    gt = gt_ref[...]
    ffn = gt[:, 0:1] * buf[0]
    for kslot in range(1, MOE_TOPK):
        ffn = ffn + gt[:, kslot:kslot + 1] * buf[kslot]
    o_ref[...] = _ln_rows(DN_ALPHA * x_ref[...] + ffn, g_ref[...], b_ref[...])


def _combine_ln(y_sorted, dest, gates, x, g, b, rows=256):
    t, d = x.shape
    nb = t // rows
    row = pl.BlockSpec((rows, d), lambda i: (i, 0))
    vec = pl.BlockSpec((1, d), lambda i: (0, 0))
    return pl.pallas_call(
        functools.partial(_combine_ln_kernel, rows=rows),
        grid=(nb,),
        in_specs=[pl.BlockSpec((None, 1, rows * MOE_TOPK), lambda i: (i, 0, 0), memory_space=pltpu.SMEM),
                  pl.BlockSpec(memory_space=pl.ANY), row,
                  pl.BlockSpec((rows, MOE_TOPK), lambda i: (i, 0)), vec, vec],
        out_specs=row,
        out_shape=jax.ShapeDtypeStruct((t, d), F32),
        scratch_shapes=[pltpu.VMEM((MOE_TOPK, rows, d), F32), pltpu.SemaphoreType.DMA(())],
        compiler_params=_params("arbitrary"),
        name="combine_ln",
    )(dest.reshape(nb, 1, rows * MOE_TOPK), y_sorted, x, gates, g.reshape(1, d), b.reshape(1, d))


def _moe_layer(x, w_grp, b_grp, w_exp_r, b_exp_r, w_gate, w_up, w_down, ln_g, ln_b):
    t, d = x.shape
    n_r = N_GROUPS + N_EXPERTS
    w_router = jnp.pad(jnp.concatenate([w_grp, w_exp_r], axis=1), ((0, 0), (0, LANES - n_r))).astype(BF16)
    logits = _matmul(x, w_router, tn=LANES)[:, :n_r] + jnp.concatenate([b_grp, b_exp_r])[None, :]
    g_prob = jax.nn.softmax(logits[:, :N_GROUPS], axis=-1)
    g_p, g_idx = lax.top_k(g_prob, 1)
    e_logits = logits[:, N_GROUPS:].reshape(t, N_GROUPS, EXPERTS_PER_GROUP)
    e_logits = jnp.take_along_axis(e_logits, g_idx[:, :, None], axis=1)[:, 0]
    e_p, e_idx = lax.top_k(jax.nn.softmax(e_logits, axis=-1), MOE_TOPK)
    e_p = e_p / jnp.sum(e_p, -1, keepdims=True)
    gates = g_p * e_p
    flat_e = (g_idx * EXPERTS_PER_GROUP + e_idx).reshape(-1)
    m = t * MOE_TOPK
    onehot = (flat_e[:, None] == jnp.arange(N_EXPERTS)[None, :]).astype(jnp.int32)
    rank = jnp.take_along_axis(jnp.cumsum(onehot, axis=0), flat_e[:, None], axis=1)[:, 0] - 1
    counts = jnp.sum(onehot, axis=0)
    padded = (counts + MOE_ROWS - 1) // MOE_ROWS * MOE_ROWS
    pad_end = jnp.cumsum(padded)
    dest = (pad_end - padded)[flat_e] + rank
    n_blocks = (m + N_EXPERTS * (MOE_ROWS - 1) + MOE_ROWS - 1) // MOE_ROWS
    src_row = jnp.zeros((n_blocks * MOE_ROWS,), jnp.int32).at[dest].set(
        (jnp.arange(m) // MOE_TOPK).astype(jnp.int32))
    block_e = jnp.minimum(jnp.searchsorted(pad_end, jnp.arange(n_blocks) * MOE_ROWS, side='right'),
                          N_EXPERTS - 1).astype(jnp.int32)
    n_used = (pad_end[-1] // MOE_ROWS).astype(jnp.int32).reshape(1)

    xs = _gather_rows(x, src_row)
    ys = _expert_mlp(xs, block_e, n_used, w_gate.astype(BF16), w_up.astype(BF16), w_down.astype(BF16))
    return _combine_ln(ys, dest.astype(jnp.int32), gates, x, ln_g, ln_b)


def kernel(x, ev_w_in, ev_gate_bias, ev_conv_w, ev_conv_b, ev_hnorm_g, ev_w_out, od_mu, od_w_r, od_w_k, od_w_v, od_w_o, od_w0, od_w1, od_w2, od_a0, od_a1, od_a2, od_g1, od_g2, od_k_k, od_k_a, od_r_k, od_lnx_g, od_lnx_b, od_v0, od_v1, od_v2, ln_mix_g, ln_mix_b, ln_ffn_g, ln_ffn_b, moe_w_grp, moe_b_grp, moe_w_exp_r, moe_b_exp_r, moe_w_gate, moe_w_up, moe_w_down):
    bsz, s, d = x.shape
    depth = ln_mix_g.shape[0]
    xf = x.reshape(bsz * s, d)
    v_first = None
    for layer in range(depth):
        if layer % 2 == 0:
            e = layer // 2
            mix = _even_mixer(xf, ev_w_in[e], ev_gate_bias[e], ev_conv_w[e], ev_conv_b[e],
                              ev_hnorm_g[e], ev_w_out[e], bsz, s)
        else:
            o = layer // 2
            vres = None if o == 0 else (od_v0[o - 1], od_v1[o - 1], od_v2[o - 1])
            mix, v_first = _rwkv7_mixer(xf, v_first, od_mu[o], od_w_r[o], od_w_k[o], od_w_v[o],
                                        od_w_o[o], od_w0[o], od_w1[o], od_w2[o], od_a0[o],
                                        od_a1[o], od_a2[o], od_g1[o], od_g2[o], od_k_k[o],
                                        od_k_a[o], od_r_k[o], od_lnx_g[o], od_lnx_b[o], vres, bsz, s)
        xf = _resid_ln(xf, mix, ln_mix_g[layer], ln_mix_b[layer])
        xf = _moe_layer(xf, moe_w_grp[layer], moe_b_grp[layer], moe_w_exp_r[layer],
                        moe_b_exp_r[layer], moe_w_gate[layer], moe_w_up[layer], moe_w_down[layer],
                        ln_ffn_g[layer], ln_ffn_b[layer])
    return xf.reshape(bsz, s, d)
```

```python
import functools
import math

import jax
import jax.numpy as jnp
from jax import lax
from jax.experimental import pallas as pl
from jax.experimental.pallas import tpu as pltpu

F32 = jnp.float32
BF16 = jnp.bfloat16

DEPTH = 4
DN_ALPHA = (2 * DEPTH) ** 0.25
LN_EPS = 1e-5

A_HEADS = 8
A_HEAD_DIM = 128
A_WIDTH = A_HEADS * A_HEAD_DIM
MOBA_BLOCK = 256
MOBA_TOPK = 3
ROPE_THETA = 500000.0
ROPE_DIM = A_HEAD_DIM // 4
NEG_INF = -1e30

B_HEADS = 4
B_HEAD_DIM = 256
B_WIDTH = B_HEADS * B_HEAD_DIM
MLSTM_CHUNK = 64
B_CONV = 4
HNORM_EPS = 1e-6
EVEN_MAIN = 3 * A_WIDTH + 4 * B_WIDTH

C_HEAD_DIM = 64
LNX_EPS = 64e-5
KK_EPS = 1e-12

N_GROUPS = 4
EXPERTS_PER_GROUP = 8
N_EXPERTS = N_GROUPS * EXPERTS_PER_GROUP
MOE_TOPK = 2
MOE_ROWS = 256

LANES = 128
SUBLANES = 8
VMEM_LIMIT = 56 * 1024 * 1024

_NT = (((1,), (1,)), ((), ()))
_TN = (((0,), (0,)), ((), ()))


def _params(*sem):
    return pltpu.CompilerParams(dimension_semantics=sem, vmem_limit_bytes=VMEM_LIMIT)


def _mm_kernel(x_ref, w_ref, o_ref):
    o_ref[...] = jnp.dot(x_ref[...].astype(BF16), w_ref[...],
                         preferred_element_type=F32).astype(o_ref.dtype)


def _matmul(x, w, out_dtype=F32, tm=512, tn=512):
    m, k = x.shape
    n = w.shape[1]
    tm, tn = min(tm, m), min(tn, n)
    return pl.pallas_call(
        _mm_kernel,
        grid=(m // tm, n // tn),
        in_specs=[pl.BlockSpec((tm, k), lambda i, j: (i, 0)),
                  pl.BlockSpec((k, tn), lambda i, j: (0, j))],
        out_specs=pl.BlockSpec((tm, tn), lambda i, j: (i, j)),
        out_shape=jax.ShapeDtypeStruct((m, n), out_dtype),
        compiler_params=_params("parallel", "arbitrary"),
        name="matmul",
    )(x, w)


def _mm2_kernel(xa_ref, xb_ref, wa_ref, wb_ref, o_ref):
    acc = jnp.dot(xa_ref[...], wa_ref[...], preferred_element_type=F32)
    acc = acc + jnp.dot(xb_ref[...], wb_ref[...], preferred_element_type=F32)
    o_ref[...] = acc


def _matmul2(xa, xb, wa, wb, tm=512, tn=512):
    m, ka = xa.shape
    kb = xb.shape[1]
    n = wa.shape[1]
    tm, tn = min(tm, m), min(tn, n)
    return pl.pallas_call(
        _mm2_kernel,
        grid=(m // tm, n // tn),
        in_specs=[pl.BlockSpec((tm, ka), lambda i, j: (i, 0)),
                  pl.BlockSpec((tm, kb), lambda i, j: (i, 0)),
                  pl.BlockSpec((ka, tn), lambda i, j: (0, j)),
                  pl.BlockSpec((kb, tn), lambda i, j: (0, j))],
        out_specs=pl.BlockSpec((tm, tn), lambda i, j: (i, j)),
        out_shape=jax.ShapeDtypeStruct((m, n), F32),
        compiler_params=_params("parallel", "arbitrary"),
        name="matmul2",
    )(xa, xb, wa, wb)


def _ln_rows(h, g, b):
    mu = jnp.mean(h, axis=-1, keepdims=True)
    hc = h - mu
    var = jnp.mean(hc * hc, axis=-1, keepdims=True)
    return hc * lax.rsqrt(var + LN_EPS) * g + b


def _resid_ln_kernel(x_ref, m_ref, g_ref, b_ref, o_ref):
    o_ref[...] = _ln_rows(DN_ALPHA * x_ref[...] + m_ref[...], g_ref[...], b_ref[...])


def _resid_ln(x, mix, g, b, tm=256):
    t, d = x.shape
    row = pl.BlockSpec((tm, d), lambda i: (i, 0))
    vec = pl.BlockSpec((1, d), lambda i: (0, 0))
    return pl.pallas_call(
        _resid_ln_kernel,
        grid=(t // tm,),
        in_specs=[row, row, vec, vec],
        out_specs=row,
        out_shape=jax.ShapeDtypeStruct((t, d), F32),
        compiler_params=_params("parallel"),
        name="resid_ln",
    )(x, mix, g.reshape(1, d), b.reshape(1, d))


def _rope_tables(s):
    half = ROPE_DIM // 2
    inv = jnp.power(jnp.float32(ROPE_THETA), -jnp.arange(half, dtype=F32) / half)
    ang = jnp.arange(s).astype(F32)[:, None] * inv[None, :]
    cos, sin = jnp.cos(ang), jnp.sin(ang)
    zero = jnp.zeros((s, A_HEAD_DIM - ROPE_DIM), F32)
    zh = jnp.zeros((s, half), F32)
    c = jnp.concatenate([cos, cos, zero + 1.0], axis=-1)
    s_up = jnp.concatenate([zh, sin, zero], axis=-1)
    s_dn = jnp.concatenate([-sin, zh, zero], axis=-1)
    return c, s_up, s_dn


def _rope(x, c, s_up, s_dn):
    half = ROPE_DIM // 2
    return (x * c + pltpu.roll(x, half, axis=1) * s_up
            + pltpu.roll(x, A_HEAD_DIM - half, axis=1) * s_dn)


def _moba_kernel(q_ref, k_ref, v_ref, c_ref, su_ref, sd_ref, o_ref, krot_ref, vb_ref, kmean_ref, *, nblk):
    qi = pl.program_id(2)
    blk = MOBA_BLOCK
    scale = A_HEAD_DIM ** -0.5

    @pl.when(qi == 0)
    def _prepare_keys():
        def body(j, carry):
            r = pl.multiple_of(j * blk, blk)
            rows = pl.ds(r, blk)
            kr = _rope(k_ref[rows, :], c_ref[rows, :], su_ref[rows, :], sd_ref[rows, :])
            krot_ref[rows, :] = kr.astype(BF16)
            vb_ref[rows, :] = v_ref[rows, :].astype(BF16)
            kmean_ref[pl.ds(j, 1), :] = jnp.mean(kr, axis=0, keepdims=True)
            return carry
        lax.fori_loop(0, nblk, body, 0)

    r0 = pl.multiple_of(qi * blk, blk)
    qrows = pl.ds(r0, blk)
    qb = _rope(q_ref[...], c_ref[qrows, :], su_ref[qrows, :], sd_ref[qrows, :]).astype(BF16)

    g = lax.dot_general(kmean_ref[...].astype(BF16), qb, _NT, preferred_element_type=F32)
    brow = lax.broadcasted_iota(jnp.int32, (nblk, blk), 0)
    g = jnp.where(brow < qi, g, -jnp.inf)
    sel_t = jnp.zeros((nblk, blk), F32)
    for j in range(nblk):
        gj = g[j:j + 1, :]
        earlier = jnp.where(brow < j, 1.0, 0.0)
        beats = jnp.where(g > gj, 1.0, jnp.where(g == gj, earlier, 0.0))
        cnt = jnp.sum(beats, axis=0, keepdims=True)
        sel_t = jnp.where(brow == j, jnp.where(cnt < float(min(MOBA_TOPK, nblk)), 1.0, 0.0), sel_t)
    sel_t = jnp.where(brow < qi, sel_t, 0.0)
    eye = jnp.where(lax.broadcasted_iota(jnp.int32, (blk, blk), 0)
                    == lax.broadcasted_iota(jnp.int32, (blk, blk), 1), 1.0, 0.0).astype(BF16)
    sel = lax.dot_general(eye, sel_t.astype(BF16), _NT, preferred_element_type=F32)
    bcol = lax.broadcasted_iota(jnp.int32, (blk, nblk), 1)

    kd = krot_ref[qrows, :]
    s = lax.dot_general(qb, kd, _NT, preferred_element_type=F32) * scale
    causal = (lax.broadcasted_iota(jnp.int32, (blk, blk), 1)
              <= lax.broadcasted_iota(jnp.int32, (blk, blk), 0))
    s = jnp.where(causal, s, NEG_INF)
    m0 = jnp.max(s, axis=1, keepdims=True)
    p = jnp.exp(s - m0)
    l0 = jnp.sum(p, axis=1, keepdims=True)
    acc0 = jnp.dot(p.astype(BF16), vb_ref[qrows, :], preferred_element_type=F32)

    def past_block(j, carry):
        m, l, acc = carry
        rows = pl.ds(pl.multiple_of(j * blk, blk), blk)
        sj = lax.dot_general(qb, krot_ref[rows, :], _NT, preferred_element_type=F32) * scale
        picked = jnp.sum(jnp.where(bcol == j, sel, 0.0), axis=1, keepdims=True)
        sj = jnp.where(picked > 0.5, sj, NEG_INF)
        m_new = jnp.maximum(m, jnp.max(sj, axis=1, keepdims=True))
        a = jnp.exp(m - m_new)
        pj = jnp.exp(sj - m_new)
        l = a * l + jnp.sum(pj, axis=1, keepdims=True)
        acc = a * acc + jnp.dot(pj.astype(BF16), vb_ref[rows, :], preferred_element_type=F32)
        return m_new, l, acc

    _, l, acc = lax.fori_loop(0, qi, past_block, (m0, l0, acc0))
    o_ref[...] = (acc / l).astype(o_ref.dtype)


def _moba(z, bsz, s):
    nblk = s // MOBA_BLOCK
    c, s_up, s_dn = _rope_tables(s)
    tab = pl.BlockSpec((s, A_HEAD_DIM), lambda b, h, qi: (0, 0))
    return pl.pallas_call(
        functools.partial(_moba_kernel, nblk=nblk),
        grid=(bsz, A_HEADS, nblk),
        in_specs=[pl.BlockSpec((MOBA_BLOCK, A_HEAD_DIM), lambda b, h, qi: (b * nblk + qi, h)),
                  pl.BlockSpec((s, A_HEAD_DIM), lambda b, h, qi: (b, A_HEADS + h)),
                  pl.BlockSpec((s, A_HEAD_DIM), lambda b, h, qi: (b, 2 * A_HEADS + h)),
                  tab, tab, tab],
        out_specs=pl.BlockSpec((MOBA_BLOCK, A_HEAD_DIM), lambda b, h, qi: (b * nblk + qi, h)),
        out_shape=jax.ShapeDtypeStruct((bsz * s, A_WIDTH), BF16),
        scratch_shapes=[pltpu.VMEM((s, A_HEAD_DIM), BF16), pltpu.VMEM((s, A_HEAD_DIM), BF16),
                        pltpu.VMEM((nblk, A_HEAD_DIM), F32)],
        compiler_params=_params("parallel", "parallel", "arbitrary"),
        name="moba",
    )(z, z, z, c, s_up, s_dn)


def _log_sigmoid(x):
    return jnp.minimum(x, 0.0) - jnp.log1p(jnp.exp(-jnp.abs(x)))


def _shift_rows(x, prev8, d):
    if d == 0:
        return x
    rolled = pltpu.roll(x, d, axis=0)
    top = jnp.where(lax.broadcasted_iota(jnp.int32, prev8.shape, 0) < d,
                    pltpu.roll(prev8, d, axis=0), rolled[:SUBLANES])
    return jnp.concatenate([top, rolled[SUBLANES:]], axis=0)


def _mlstm_kernel(q_ref, k_ref, v_ref, o_ref, gc_ref, gr_ref, bc_ref, br_ref,
                  cwq_ref, cwk_ref, cbq_ref, cbk_ref, hg_ref, y_ref,
                  c_ref, n_ref, m_ref, pq_ref, pk_ref, *, tt):
    h = pl.program_id(1)
    ti = pl.program_id(2)
    L = MLSTM_CHUNK
    dk = B_HEAD_DIM

    @pl.when(ti == 0)
    def _reset():
        c_ref[...] = jnp.zeros_like(c_ref)
        n_ref[...] = jnp.zeros_like(n_ref)
        m_ref[...] = jnp.zeros_like(m_ref)
        pq_ref[...] = jnp.zeros_like(pq_ref)
        pk_ref[...] = jnp.zeros_like(pk_ref)

    def conv_silu(x_ref, p_ref, w_ref, b_ref):
        x = x_ref[...]
        prev8 = p_ref[...]
        out = b_ref[...] + x * w_ref[B_CONV - 1:B_CONV, :]
        for d in range(1, B_CONV):
            out = out + _shift_rows(x, prev8, d) * w_ref[B_CONV - 1 - d:B_CONV - d, :]
        p_ref[...] = x[tt - SUBLANES:, :]
        return out * jax.nn.sigmoid(out)

    q_all = conv_silu(q_ref, pq_ref, cwq_ref, cbq_ref)
    k_all = conv_silu(k_ref, pk_ref, cwk_ref, cbk_ref) * (dk ** -0.5)

    lane8 = lax.broadcasted_iota(jnp.int32, (tt, 2 * B_HEADS), 1)
    gcol = gc_ref[...] + bc_ref[...]
    i_col_all = jnp.sum(jnp.where(lane8 == h, gcol, 0.0), axis=1, keepdims=True)
    f_col_all = _log_sigmoid(jnp.sum(jnp.where(lane8 == B_HEADS + h, gcol, 0.0), axis=1, keepdims=True))
    i_row_all = gr_ref[pl.ds(h, 1), :] + br_ref[pl.ds(h, 1), :]
    f_row_all = _log_sigmoid(gr_ref[pl.ds(B_HEADS + h, 1), :] + br_ref[pl.ds(B_HEADS + h, 1), :])

    rr = lax.broadcasted_iota(jnp.int32, (L, L), 0)
    cc = lax.broadcasted_iota(jnp.int32, (L, L), 1)
    tri = cc <= rr

    for c in range(tt // L):
        lo, hi = c * L, (c + 1) * L
        qc = q_all[lo:hi]
        kc = k_all[lo:hi]
        qcb = qc.astype(BF16)
        vcb = v_ref[lo:hi, :].astype(BF16)
        i_col, f_col = i_col_all[lo:hi], f_col_all[lo:hi]
        i_row, f_row = i_row_all[:, lo:hi], f_row_all[:, lo:hi]
        m_prev = m_ref[...]
        cst = c_ref[...]
        nst = n_ref[...]

        b_col = jnp.sum(jnp.where(tri, f_row, 0.0), axis=1, keepdims=True)
        b_row = jnp.sum(jnp.where(rr <= cc, f_col, 0.0), axis=0, keepdims=True)
        log_inter = b_col + m_prev
        log_intra = jnp.where(tri, b_col - b_row + i_row, -jnp.inf)
        m_t = jnp.maximum(log_inter, jnp.max(log_intra, axis=1, keepdims=True))
        w_intra = jnp.exp(log_intra - m_t)
        w_inter = jnp.exp(log_inter - m_t)
        qk = lax.dot_general(qcb, kc.astype(BF16), _NT, preferred_element_type=F32) * w_intra
        num = (w_inter * jnp.dot(qcb, cst.astype(BF16), preferred_element_type=F32)
               + jnp.dot(qk.astype(BF16), vcb, preferred_element_type=F32))
        den = (w_inter * jnp.sum(qc * nst, axis=1, keepdims=True)
               + jnp.sum(qk, axis=1, keepdims=True))
        hc = num / jnp.maximum(jnp.abs(den), jnp.exp(-m_t))

        b_last = jnp.sum(f_col, axis=0, keepdims=True)
        log_s = b_last - b_col + i_col
        m_new = jnp.maximum(b_last + m_prev, jnp.max(log_s, axis=0, keepdims=True))
        decay = jnp.exp(b_last + m_prev - m_new)
        kw = kc * jnp.exp(log_s - m_new)
        c_ref[...] = decay * cst + lax.dot_general(kw.astype(BF16), vcb, _TN, preferred_element_type=F32)
        n_ref[...] = decay * nst + jnp.sum(kw, axis=0, keepdims=True)
        m_ref[...] = m_new

        hn = hc * lax.rsqrt(jnp.mean(hc * hc, axis=1, keepdims=True) + HNORM_EPS)
        y_ref[lo:hi, :] = (hn * hg_ref[...] * jax.nn.sigmoid(o_ref[lo:hi, :])).astype(y_ref.dtype)


def _mlstm(z, gates, gate_bias, conv_w, conv_b, hnorm_g, bsz, s, tt=256):
    nt = s // tt
    cb = 3 * A_WIDTH // B_HEAD_DIM
    gates_t = gates.T
    bias = gate_bias.reshape(1, 2 * B_HEADS)

    def col(off):
        return pl.BlockSpec((tt, B_HEAD_DIM), lambda b, h, t: (b * nt + t, cb + off + h))

    def par(rows, off):
        return pl.BlockSpec((rows, B_HEAD_DIM), lambda b, h, t: (0, off + h))

    return pl.pallas_call(
        functools.partial(_mlstm_kernel, tt=tt),
        grid=(bsz, B_HEADS, nt),
        in_specs=[col(0), col(B_HEADS), col(2 * B_HEADS), col(3 * B_HEADS),
                  pl.BlockSpec((tt, 2 * B_HEADS), lambda b, h, t: (b * nt + t, 0)),
                  pl.BlockSpec((2 * B_HEADS, tt), lambda b, h, t: (0, b * nt + t)),
                  pl.BlockSpec((1, 2 * B_HEADS), lambda b, h, t: (0, 0)),
                  pl.BlockSpec((2 * B_HEADS, 1), lambda b, h, t: (0, 0)),
                  par(B_CONV, 0), par(B_CONV, B_HEADS), par(1, 0), par(1, B_HEADS), par(1, 0)],
        out_specs=pl.BlockSpec((tt, B_HEAD_DIM), lambda b, h, t: (b * nt + t, h)),
        out_shape=jax.ShapeDtypeStruct((bsz * s, B_WIDTH), BF16),
        scratch_shapes=[pltpu.VMEM((B_HEAD_DIM, B_HEAD_DIM), F32), pltpu.VMEM((1, B_HEAD_DIM), F32),
                        pltpu.VMEM((1, 1), F32), pltpu.VMEM((SUBLANES, B_HEAD_DIM), F32),
                        pltpu.VMEM((SUBLANES, B_HEAD_DIM), F32)],
        compiler_params=_params("parallel", "parallel", "arbitrary"),
        name="mlstm",
    )(z, z, z, z, gates, gates_t, bias, bias.T, conv_w, conv_w,
      conv_b.reshape(1, -1), conv_b.reshape(1, -1), hnorm_g.reshape(1, -1))


def _even_mixer(x, w_in, gate_bias, conv_w, conv_b, hnorm_g, w_out, bsz, s):
    d = x.shape[1]
    w_main = w_in[:, :EVEN_MAIN].astype(BF16)
    w_gate = jnp.pad(w_in[:, EVEN_MAIN:], ((0, 0), (0, LANES - 2 * B_HEADS))).astype(BF16)
    z = _matmul(x, w_main)
    gates = _matmul(x, w_gate, tn=LANES)[:, :2 * B_HEADS]
    y_a = _moba(z, bsz, s)
    y_b = _mlstm(z, gates, gate_bias, conv_w, conv_b, hnorm_g, bsz, s)
    w_o = w_out.astype(BF16)
    return _matmul2(y_a, y_b, w_o[:A_WIDTH], w_o[A_WIDTH:])


def _shift_mix_kernel(x_ref, p_ref, mu_ref, *o_refs, tiles_per_seq):
    x = x_ref[...]
    first = (pl.program_id(0) % tiles_per_seq) == 0
    prev_row = jnp.where(first, 0.0, p_ref[SUBLANES - 1:SUBLANES, :])
    row = lax.broadcasted_iota(jnp.int32, x.shape, 0)
    dx = jnp.where(row == 0, prev_row, pltpu.roll(x, 1, axis=0)) - x
    for j, o_ref in enumerate(o_refs):
        o_ref[...] = (x + dx * mu_ref[j:j + 1, :]).astype(o_ref.dtype)


def _shift_mix(x, mu, s, tm=256):
    t, d = x.shape
    n = mu.shape[0]
    row = pl.BlockSpec((tm, d), lambda i: (i, 0))
    prev = pl.BlockSpec((SUBLANES, d), lambda i: (jnp.maximum(i * (tm // SUBLANES) - 1, 0), 0))
    return pl.pallas_call(
        functools.partial(_shift_mix_kernel, tiles_per_seq=s // tm),
        grid=(t // tm,),
        in_specs=[row, prev, pl.BlockSpec((n, d), lambda i: (0, 0))],
        out_specs=[row] * n,
        out_shape=[jax.ShapeDtypeStruct((t, d), BF16)] * n,
        compiler_params=_params("parallel"),
        name="shift_mix",
    )(x, x, mu)


def _softplus(x):
    return jnp.maximum(x, 0.0) + jnp.log1p(jnp.exp(-jnp.abs(x)))


def _lora_kernel(x_ref, w1_ref, w2_ref, b_ref, o_ref, *, mid, out):
    hmid = jnp.dot(x_ref[...], w1_ref[...], preferred_element_type=F32)
    if mid == "tanh":
        hmid = jnp.tanh(hmid)
    elif mid == "sigmoid":
        hmid = jax.nn.sigmoid(hmid)
    y = jnp.dot(hmid.astype(BF16), w2_ref[...], preferred_element_type=F32)
    if out == "decay":
        y = jnp.exp(-jnp.exp(-_softplus(-(b_ref[...] + y)) - 0.5))
    elif out == "sigmoid":
        y = jax.nn.sigmoid(b_ref[...] + y)
    o_ref[...] = y


def _lora(x, w1, w2, bias, mid, out, tm=512):
    t, d = x.shape
    tm = min(tm, t)
    r = w1.shape[1]
    rp = -(-r // LANES) * LANES
    w1p = jnp.pad(w1, ((0, 0), (0, rp - r))).astype(BF16)
    w2p = jnp.pad(w2, ((0, rp - r), (0, 0))).astype(BF16)
    dout = w2.shape[1]
    if bias is None:
        bias = jnp.zeros((dout,), F32)
    return pl.pallas_call(
        functools.partial(_lora_kernel, mid=mid, out=out),
        grid=(t // tm,),
        in_specs=[pl.BlockSpec((tm, d), lambda i: (i, 0)),
                  pl.BlockSpec((d, rp), lambda i: (0, 0)),
                  pl.BlockSpec((rp, dout), lambda i: (0, 0)),
                  pl.BlockSpec((1, dout), lambda i: (0, 0))],
        out_specs=pl.BlockSpec((tm, dout), lambda i: (i, 0)),
        out_shape=jax.ShapeDtypeStruct((t, dout), F32),
        compiler_params=_params("parallel"),
        name="lora_" + out,
    )(x, w1p, w2p, bias.reshape(1, dout))


def _head_ones():
    i = jnp.arange(LANES) // C_HEAD_DIM
    return (i[:, None] == i[None, :]).astype(F32)


def _head_sum(x, ones):
    cols = [jnp.dot(x[:, c * LANES:(c + 1) * LANES], ones, precision=lax.Precision.HIGHEST,
                    preferred_element_type=F32) for c in range(x.shape[1] // LANES)]
    return jnp.concatenate(cols, axis=1)


def _rwkv_prep_kernel(*refs, vres):
    if vres:
        k_ref, a_ref, v_ref, vf_ref, vg_ref, kk_p, ka_p, ones_ref, kk_o, kh_o, bb_o, v_o = refs
    else:
        k_ref, a_ref, v_ref, kk_p, ka_p, ones_ref, kk_o, kh_o, bb_o, v_o = refs
    k = k_ref[...]
    a = a_ref[...]
    kk = k * kk_p[...]
    nrm = jnp.sqrt(_head_sum(kk * kk, ones_ref[...]))
    kk = kk / jnp.maximum(nrm, KK_EPS)
    kk_o[...] = kk
    bb_o[...] = kk * a
    kh_o[...] = k * (1.0 + (a - 1.0) * ka_p[...])
    v = v_ref[...]
    if vres:
        v = v + (vf_ref[...] - v) * vg_ref[...]
    v_o[...] = v


def _rwkv_prep(k, a, v, v_first, vgate, k_k, k_a, tm=256):
    t, d = k.shape
    row = pl.BlockSpec((tm, d), lambda i: (i, 0))
    vec = pl.BlockSpec((1, d), lambda i: (0, 0))
    vres = v_first is not None
    ins = [k, a, v] + ([v_first, vgate] if vres else []) + [k_k.reshape(1, d), k_a.reshape(1, d), _head_ones()]
    specs = [row] * (5 if vres else 3) + [vec, vec, pl.BlockSpec((LANES, LANES), lambda i: (0, 0))]
    return pl.pallas_call(
        functools.partial(_rwkv_prep_kernel, vres=vres),
        grid=(t // tm,),
        in_specs=specs,
        out_specs=[row] * 4,
        out_shape=[jax.ShapeDtypeStruct((t, d), F32)] * 4,
        compiler_params=_params("parallel"),
        name="rwkv_prep",
    )(*ins)


def _rwkv_scan_kernel(r_ref, w_ref, kh_ref, v_ref, kk_ref, bb_ref, ones_ref, mask_ref, y_ref, s_ref,
                      *, tt, nb, ng):
    n = C_HEAD_DIM
    rows = n * SUBLANES

    @pl.when(pl.program_id(0) == 0)
    def _reset():
        s_ref[...] = jnp.zeros_like(s_ref)

    ones_b = ones_ref[...]
    ones_2 = jnp.concatenate([ones_b, ones_b], axis=0)
    mask = mask_ref[...]

    def flat(x):
        return x.reshape(rows, LANES)

    def head_sum(x_b, ones):
        return jnp.dot(x_b, ones, preferred_element_type=F32).reshape(n, SUBLANES, LANES)

    def chain_step(b, g, t):
        st = s_ref[b, g]
        p = flat(st * kk_ref[b, t, g][None])
        hi = p.astype(BF16)
        lo = (p - hi.astype(F32)).astype(BF16)
        u = head_sum(jnp.concatenate([hi, lo], axis=-1), ones_2)
        vv = head_sum(flat(v_ref[b, t, g][None] * mask).astype(BF16), ones_b)
        st = st * w_ref[b, t, g][None] - u * bb_ref[b, t, g][None] + vv * kh_ref[b, t, g][None]
        s_ref[b, g] = st
        yb = head_sum(flat(st * r_ref[b, t, g][None]).astype(BF16), ones_b)
        y_ref[b, t, g] = jnp.sum(yb * mask, axis=0)

    def step(t, carry):
        for b in range(nb):
            for g in range(ng):
                chain_step(b, g, t)
        return carry

    lax.fori_loop(0, tt, step, 0)


def _rwkv_scan(r, w, kh, v, kk, bb, bsz, s, tt=64):
    t, d = r.shape
    ng = d // (SUBLANES * LANES)
    tt = min(tt, s)
    view = lambda x: x.reshape(bsz, s, ng, SUBLANES, LANES)
    ones_b = _head_ones().astype(BF16)
    lane = jnp.arange(LANES) % C_HEAD_DIM
    mask = (lane[None, None, :] == jnp.arange(C_HEAD_DIM)[:, None, None]).astype(F32)
    mask = jnp.broadcast_to(mask, (C_HEAD_DIM, SUBLANES, LANES))
    blk = pl.BlockSpec((bsz, tt, ng, SUBLANES, LANES), lambda i: (0, i, 0, 0, 0))
    y = pl.pallas_call(
        functools.partial(_rwkv_scan_kernel, tt=tt, nb=bsz, ng=ng),
        grid=(s // tt,),
        in_specs=[blk] * 6 + [pl.BlockSpec((LANES, LANES), lambda i: (0, 0)),
                              pl.BlockSpec((C_HEAD_DIM, SUBLANES, LANES), lambda i: (0, 0, 0))],
        out_specs=blk,
        out_shape=jax.ShapeDtypeStruct((bsz, s, ng, SUBLANES, LANES), F32),
        scratch_shapes=[pltpu.VMEM((bsz, ng, C_HEAD_DIM, SUBLANES, LANES), F32)],
        compiler_params=_params("arbitrary"),
        name="rwkv_scan",
    )(view(r), view(w), view(kh), view(v), view(kk), view(bb), ones_b, mask)
    return y.reshape(t, d)


def _rwkv_post_kernel(y_ref, r_ref, kh_ref, v_ref, g_ref, lg_ref, lb_ref, rk_ref, ones_ref, o_ref):
    ones = ones_ref[...]
    y = y_ref[...]
    inv_n = 1.0 / C_HEAD_DIM
    yc = y - _head_sum(y, ones) * inv_n
    yv = _head_sum(yc * yc, ones) * inv_n
    yn = yc * lax.rsqrt(yv + LNX_EPS) * lg_ref[...] + lb_ref[...]
    bonus = _head_sum(r_ref[...] * kh_ref[...] * rk_ref[...], ones) * v_ref[...]
    o_ref[...] = ((yn + bonus) * g_ref[...]).astype(o_ref.dtype)


def _rwkv_post(y, r, kh, v, g, lnx_g, lnx_b, r_k, tm=256):
    t, d = y.shape
    row = pl.BlockSpec((tm, d), lambda i: (i, 0))
    vec = pl.BlockSpec((1, d), lambda i: (0, 0))
    return pl.pallas_call(
        _rwkv_post_kernel,
        grid=(t // tm,),
        in_specs=[row] * 5 + [vec] * 3 + [pl.BlockSpec((LANES, LANES), lambda i: (0, 0))],
        out_specs=row,
        out_shape=jax.ShapeDtypeStruct((t, d), BF16),
        compiler_params=_params("parallel"),
        name="rwkv_post",
    )(y, r, kh, v, g, lnx_g.reshape(1, d), lnx_b.reshape(1, d), r_k.reshape(1, d), _head_ones())


def _rwkv7_mixer(x, v_first, mu, w_r, w_k, w_v, w_o, w0, w1, w2, a0, a1, a2, g1, g2,
                 k_k, k_a, r_k, lnx_g, lnx_b, vres, bsz, s):
    xr, xw, xk, xv, xa, xg = _shift_mix(x, mu, s)
    r = _matmul(xr, w_r.astype(BF16))
    k = _matmul(xk, w_k.astype(BF16))
    v = _matmul(xv, w_v.astype(BF16))
    decay = _lora(xw, w1, w2, w0, "tanh", "decay")
    a = _lora(xa, a1, a2, a0, "none", "sigmoid")
    g = _lora(xg, g1, g2, None, "sigmoid", "none")
    if vres is None:
        vgate = None
        v_keep = v
    else:
        v0, v1, v2 = vres
        vgate = _lora(xv, v1, v2, v0, "none", "sigmoid")
        v_keep = v_first
    kk, kh, bb, v = _rwkv_prep(k, a, v, None if vres is None else v_first, vgate, k_k, k_a)
    y = _rwkv_scan(r, decay, kh, v, kk, bb, bsz, s)
    yo = _rwkv_post(y, r, kh, v, g, lnx_g, lnx_b, r_k)
    return _matmul(yo, w_o.astype(BF16)), v_keep


def _lane_pick(cond, lanef):
    return jnp.min(jnp.where(cond, lanef, float(LANES)), axis=1, keepdims=True)


def _router_kernel(x_ref, w_ref, b_ref, route_ref, cnt_ref, run_ref, *, tm):
    @pl.when(pl.program_id(0) == 0)
    def _reset():
        run_ref[...] = jnp.zeros_like(run_ref)

    logits = jnp.dot(x_ref[...].astype(BF16), w_ref[...], preferred_element_type=F32) + b_ref[...]
    lane = lax.broadcasted_iota(jnp.int32, (tm, LANES), 1)
    lanef = lane.astype(F32)

    def masked_softmax(keep):
        z = jnp.where(keep, logits, -jnp.inf)
        e = jnp.exp(z - jnp.max(z, axis=1, keepdims=True))
        return e / jnp.sum(e, axis=1, keepdims=True)

    in_groups = lane < N_GROUPS
    pg = masked_softmax(in_groups)
    g_p = jnp.max(pg, axis=1, keepdims=True)
    g_idx = _lane_pick(jnp.where(in_groups, pg, -1.0) == g_p, lanef)
    lo = float(N_GROUPS) + float(EXPERTS_PER_GROUP) * g_idx
    half = 0.5 * float(EXPERTS_PER_GROUP - 1)
    in_group = jnp.abs(lanef - lo - half) < half + 0.5
    pe = jnp.where(in_group, masked_softmax(in_group), -1.0)
    p1 = jnp.max(pe, axis=1, keepdims=True)
    i1 = _lane_pick(pe == p1, lanef)
    pe2 = jnp.where(lanef == i1, -1.0, pe)
    p2 = jnp.max(pe2, axis=1, keepdims=True)
    i2 = _lane_pick(pe2 == p2, lanef)
    den = p1 + p2
    gate = (g_p * (p1 / den), g_p * (p2 / den))
    expert = (i1 - float(N_GROUPS), i2 - float(N_GROUPS))

    earlier = jnp.where(lax.broadcasted_iota(jnp.int32, (tm, tm), 1)
                        < lax.broadcasted_iota(jnp.int32, (tm, tm), 0), 1.0, 0.0).astype(BF16)
    run = run_ref[...]
    rank = []
    for e in expert:
        onehot = jnp.where(lanef == e, 1.0, 0.0)
        before = jnp.dot(earlier, onehot.astype(BF16), preferred_element_type=F32) + run
        rank.append(jnp.sum(onehot * before, axis=1, keepdims=True))
        run = run + jnp.sum(onehot, axis=0, keepdims=True)
    run_ref[...] = run
    cnt_ref[...] = run
    cols = (expert[0], expert[1], rank[0], rank[1], gate[0], gate[1])
    route = jnp.zeros((tm, LANES), F32)
    for c, val in enumerate(cols):
        route = jnp.where(lane == c, val, route)
    route_ref[...] = route


def _router(x, w_router, b_router, tm=256):
    t, d = x.shape
    tm = min(tm, t)
    return pl.pallas_call(
        functools.partial(_router_kernel, tm=tm),
        grid=(t // tm,),
        in_specs=[pl.BlockSpec((tm, d), lambda i: (i, 0)),
                  pl.BlockSpec((d, LANES), lambda i: (0, 0)),
                  pl.BlockSpec((1, LANES), lambda i: (0, 0))],
        out_specs=[pl.BlockSpec((tm, LANES), lambda i: (i, 0)),
                   pl.BlockSpec((1, LANES), lambda i: (0, 0))],
        out_shape=[jax.ShapeDtypeStruct((t, LANES), F32), jax.ShapeDtypeStruct((1, LANES), F32)],
        scratch_shapes=[pltpu.VMEM((1, LANES), F32)],
        compiler_params=_params("arbitrary"),
        name="router",
    )(x, w_router, b_router)


def _gather_rows_kernel(idx_ref, src_ref, o_ref, sem, *, rows):
    def start(r, carry):
        pltpu.make_async_copy(src_ref.at[pl.ds(idx_ref[0, r], 1)], o_ref.at[pl.ds(r, 1)], sem).start()
        return carry

    def wait(r, carry):
        pltpu.make_async_copy(src_ref.at[pl.ds(0, 1)], o_ref.at[pl.ds(r, 1)], sem).wait()
        return carry

    lax.fori_loop(0, rows, start, 0)
    lax.fori_loop(0, rows, wait, 0)


def _gather_rows(src, idx, rows=MOE_ROWS):
    n = idx.shape[0]
    d = src.shape[1]
    nb = n // rows
    return pl.pallas_call(
        functools.partial(_gather_rows_kernel, rows=rows),
        grid=(nb,),
        in_specs=[pl.BlockSpec((None, 1, rows), lambda i: (i, 0, 0), memory_space=pltpu.SMEM),
                  pl.BlockSpec(memory_space=pl.ANY)],
        out_specs=pl.BlockSpec((rows, d), lambda i: (i, 0)),
        out_shape=jax.ShapeDtypeStruct((n, d), src.dtype),
        scratch_shapes=[pltpu.SemaphoreType.DMA(())],
        compiler_params=_params("arbitrary"),
        name="gather_rows",
    )(idx.reshape(nb, 1, rows), src)


def _expert_kernel(be_ref, nu_ref, x_ref, wg_ref, wu_ref, wd_ref, o_ref, wg_b, wu_b, wd_b):
    i = pl.program_id(0)
    used = i < nu_ref[0]

    @pl.when(jnp.logical_or(i == 0, be_ref[i] != be_ref[jnp.maximum(i - 1, 0)]))
    def _():
        wg_b[...] = wg_ref[...].astype(BF16)
        wu_b[...] = wu_ref[...].astype(BF16)
        wd_b[...] = wd_ref[...].astype(BF16)

    @pl.when(used)
    def _():
        xb = x_ref[...].astype(BF16)
        hg = jnp.dot(xb, wg_b[...], preferred_element_type=F32)
        hu = jnp.dot(xb, wu_b[...], preferred_element_type=F32)
        hid = (hg * jax.nn.sigmoid(hg) * hu).astype(BF16)
        o_ref[...] = jnp.dot(hid, wd_b[...], preferred_element_type=F32)

    @pl.when(jnp.logical_not(used))
    def _():
        o_ref[...] = jnp.zeros_like(o_ref)


def _expert_mlp(xs, block_e, n_used, w_gate, w_up, w_down, layer, rows=MOE_ROWS):
    n, d = xs.shape
    ff = w_gate.shape[3]
    nb = n // rows
    grid_spec = pltpu.PrefetchScalarGridSpec(
        num_scalar_prefetch=2,
        grid=(nb,),
        in_specs=[pl.BlockSpec((rows, d), lambda i, be, nu: (jnp.minimum(i, nu[0] - 1), 0)),
                  pl.BlockSpec((None, None, d, ff), lambda i, be, nu: (layer, be[i], 0, 0)),
                  pl.BlockSpec((None, None, d, ff), lambda i, be, nu: (layer, be[i], 0, 0)),
                  pl.BlockSpec((None, None, ff, d), lambda i, be, nu: (layer, be[i], 0, 0))],
        out_specs=pl.BlockSpec((rows, d), lambda i, be, nu: (i, 0)),
        scratch_shapes=[pltpu.VMEM((d, ff), BF16), pltpu.VMEM((d, ff), BF16), pltpu.VMEM((ff, d), BF16)],
    )
    return pl.pallas_call(
        _expert_kernel,
        grid_spec=grid_spec,
        out_shape=jax.ShapeDtypeStruct((n, d), F32),
        compiler_params=_params("arbitrary"),
        name="expert_mlp",
    )(block_e, n_used, xs, w_gate, w_up, w_down)


def _combine_ln_kernel(idx_ref, y_ref, x_ref, gt_ref, g_ref, b_ref, o_ref, buf, sem, *, rows):
    def start(r, carry):
        for kslot in range(MOE_TOPK):
            pltpu.make_async_copy(y_ref.at[pl.ds(idx_ref[0, MOE_TOPK * r + kslot], 1)],
                                  buf.at[kslot, pl.ds(r, 1)], sem).start()
        return carry

    def wait(r, carry):
        for kslot in range(MOE_TOPK):
            pltpu.make_async_copy(y_ref.at[pl.ds(0, 1)], buf.at[kslot, pl.ds(r, 1)], sem).wait()
        return carry

    lax.fori_loop(0, rows, start, 0)
    lax.fori_loop(0, rows, wait, 0)
    gt = gt_ref[...]
    ffn = gt[:, 0:1] * buf[0]
    for kslot in range(1, MOE_TOPK):
        ffn = ffn + gt[:, kslot:kslot + 1] * buf[kslot]
    o_ref[...] = _ln_rows(DN_ALPHA * x_ref[...] + ffn, g_ref[...], b_ref[...])


def _combine_ln(y_sorted, dest, gates, x, g, b, rows=256):
    t, d = x.shape
    nb = t // rows
    row = pl.BlockSpec((rows, d), lambda i: (i, 0))
    vec = pl.BlockSpec((1, d), lambda i: (0, 0))
    return pl.pallas_call(
        functools.partial(_combine_ln_kernel, rows=rows),
        grid=(nb,),
        in_specs=[pl.BlockSpec((None, 1, rows * MOE_TOPK), lambda i: (i, 0, 0), memory_space=pltpu.SMEM),
                  pl.BlockSpec(memory_space=pl.ANY), row,
                  pl.BlockSpec((rows, MOE_TOPK), lambda i: (i, 0)), vec, vec],
        out_specs=row,
        out_shape=jax.ShapeDtypeStruct((t, d), F32),
        scratch_shapes=[pltpu.VMEM((MOE_TOPK, rows, d), F32), pltpu.SemaphoreType.DMA(())],
        compiler_params=_params("arbitrary"),
        name="combine_ln",
    )(dest.reshape(nb, 1, rows * MOE_TOPK), y_sorted, x, gates, g.reshape(1, d), b.reshape(1, d))


def _moe_layer(x, w_grp, b_grp, w_exp_r, b_exp_r, w_gate, w_up, w_down, ln_g, ln_b, layer):
    t, d = x.shape
    n_r = N_GROUPS + N_EXPERTS
    w_router = jnp.pad(jnp.concatenate([w_grp, w_exp_r], axis=1), ((0, 0), (0, LANES - n_r))).astype(BF16)
    b_router = jnp.pad(jnp.concatenate([b_grp, b_exp_r]), (0, LANES - n_r)).reshape(1, LANES)
    route, cnt = _router(x, w_router, b_router)
    flat_e = route[:, 0:MOE_TOPK].astype(jnp.int32).reshape(-1)
    rank = route[:, MOE_TOPK:2 * MOE_TOPK].astype(jnp.int32).reshape(-1)
    gates = route[:, 2 * MOE_TOPK:3 * MOE_TOPK]
    counts = cnt[0, :N_EXPERTS].astype(jnp.int32)
    m = t * MOE_TOPK
    padded = (counts + MOE_ROWS - 1) // MOE_ROWS * MOE_ROWS
    pad_end = jnp.cumsum(padded)
    dest = (pad_end - padded)[flat_e] + rank
    n_blocks = (m + N_EXPERTS * (MOE_ROWS - 1) + MOE_ROWS - 1) // MOE_ROWS
    src_row = jnp.zeros((n_blocks * MOE_ROWS,), jnp.int32).at[dest].set(
        (jnp.arange(m) // MOE_TOPK).astype(jnp.int32))
    block_e = jnp.minimum(jnp.searchsorted(pad_end, jnp.arange(n_blocks) * MOE_ROWS, side='right'),
                          N_EXPERTS - 1).astype(jnp.int32)
    n_used = (pad_end[-1] // MOE_ROWS).astype(jnp.int32).reshape(1)

    xs = _gather_rows(x, src_row)
    ys = _expert_mlp(xs, block_e, n_used, w_gate, w_up, w_down, layer)
    return _combine_ln(ys, dest.astype(jnp.int32), gates, x, ln_g, ln_b)


def kernel(x, ev_w_in, ev_gate_bias, ev_conv_w, ev_conv_b, ev_hnorm_g, ev_w_out, od_mu, od_w_r, od_w_k, od_w_v, od_w_o, od_w0, od_w1, od_w2, od_a0, od_a1, od_a2, od_g1, od_g2, od_k_k, od_k_a, od_r_k, od_lnx_g, od_lnx_b, od_v0, od_v1, od_v2, ln_mix_g, ln_mix_b, ln_ffn_g, ln_ffn_b, moe_w_grp, moe_b_grp, moe_w_exp_r, moe_b_exp_r, moe_w_gate, moe_w_up, moe_w_down):
    bsz, s, d = x.shape
    depth = ln_mix_g.shape[0]
    xf = x.reshape(bsz * s, d)
    v_first = None
    for layer in range(depth):
        if layer % 2 == 0:
            e = layer // 2
            mix = _even_mixer(xf, ev_w_in[e], ev_gate_bias[e], ev_conv_w[e], ev_conv_b[e],
                              ev_hnorm_g[e], ev_w_out[e], bsz, s)
        else:
            o = layer // 2
            vres = None if o == 0 else (od_v0[o - 1], od_v1[o - 1], od_v2[o - 1])
            mix, v_first = _rwkv7_mixer(xf, v_first, od_mu[o], od_w_r[o], od_w_k[o], od_w_v[o],
                                        od_w_o[o], od_w0[o], od_w1[o], od_w2[o], od_a0[o],
                                        od_a1[o], od_a2[o], od_g1[o], od_g2[o], od_k_k[o],
                                        od_k_a[o], od_r_k[o], od_lnx_g[o], od_lnx_b[o], vres, bsz, s)
        xf = _resid_ln(xf, mix, ln_mix_g[layer], ln_mix_b[layer])
        xf = _moe_layer(xf, moe_w_grp[layer], moe_b_grp[layer], moe_w_exp_r[layer],
                        moe_b_exp_r[layer], moe_w_gate, moe_w_up, moe_w_down,
                        ln_ffn_g[layer], ln_ffn_b[layer], layer)
    return xf.reshape(bsz, s, d)
```

```python
import functools
import math

import jax
import jax.numpy as jnp
from jax import lax
from jax.experimental import pallas as pl
from jax.experimental.pallas import tpu as pltpu

F32 = jnp.float32
BF16 = jnp.bfloat16

DEPTH = 4
DN_ALPHA = (2 * DEPTH) ** 0.25
LN_EPS = 1e-5

A_HEADS = 8
A_HEAD_DIM = 128
A_WIDTH = A_HEADS * A_HEAD_DIM
MOBA_BLOCK = 256
MOBA_TOPK = 3
ROPE_THETA = 500000.0
ROPE_DIM = A_HEAD_DIM // 4
NEG_INF = -1e30

B_HEADS = 4
B_HEAD_DIM = 256
B_WIDTH = B_HEADS * B_HEAD_DIM
MLSTM_CHUNK = 64
B_CONV = 4
HNORM_EPS = 1e-6
EVEN_MAIN = 3 * A_WIDTH + 4 * B_WIDTH

C_HEAD_DIM = 64
LNX_EPS = 64e-5
KK_EPS = 1e-12

N_GROUPS = 4
EXPERTS_PER_GROUP = 8
N_EXPERTS = N_GROUPS * EXPERTS_PER_GROUP
MOE_TOPK = 2
MOE_ROWS = 256

LANES = 128
SUBLANES = 8
VMEM_LIMIT = 56 * 1024 * 1024

_NT = (((1,), (1,)), ((), ()))
_TN = (((0,), (0,)), ((), ()))


def _params(*sem):
    return pltpu.CompilerParams(dimension_semantics=sem, vmem_limit_bytes=VMEM_LIMIT)


def _mm_kernel(x_ref, w_ref, o_ref):
    o_ref[...] = jnp.dot(x_ref[...].astype(BF16), w_ref[...],
                         preferred_element_type=F32).astype(o_ref.dtype)


def _matmul(x, w, out_dtype=F32, tm=512, tn=512):
    m, k = x.shape
    n = w.shape[1]
    tm, tn = min(tm, m), min(tn, n)
    return pl.pallas_call(
        _mm_kernel,
        grid=(m // tm, n // tn),
        in_specs=[pl.BlockSpec((tm, k), lambda i, j: (i, 0)),
                  pl.BlockSpec((k, tn), lambda i, j: (0, j))],
        out_specs=pl.BlockSpec((tm, tn), lambda i, j: (i, j)),
        out_shape=jax.ShapeDtypeStruct((m, n), out_dtype),
        compiler_params=_params("parallel", "arbitrary"),
        name="matmul",
    )(x, w)


def _mm2_kernel(xa_ref, xb_ref, wa_ref, wb_ref, o_ref):
    acc = jnp.dot(xa_ref[...], wa_ref[...], preferred_element_type=F32)
    acc = acc + jnp.dot(xb_ref[...], wb_ref[...], preferred_element_type=F32)
    o_ref[...] = acc


def _matmul2(xa, xb, wa, wb, tm=512, tn=512):
    m, ka = xa.shape
    kb = xb.shape[1]
    n = wa.shape[1]
    tm, tn = min(tm, m), min(tn, n)
    return pl.pallas_call(
        _mm2_kernel,
        grid=(m // tm, n // tn),
        in_specs=[pl.BlockSpec((tm, ka), lambda i, j: (i, 0)),
                  pl.BlockSpec((tm, kb), lambda i, j: (i, 0)),
                  pl.BlockSpec((ka, tn), lambda i, j: (0, j)),
                  pl.BlockSpec((kb, tn), lambda i, j: (0, j))],
        out_specs=pl.BlockSpec((tm, tn), lambda i, j: (i, j)),
        out_shape=jax.ShapeDtypeStruct((m, n), F32),
        compiler_params=_params("parallel", "arbitrary"),
        name="matmul2",
    )(xa, xb, wa, wb)


def _ln_rows(h, g, b):
    mu = jnp.mean(h, axis=-1, keepdims=True)
    hc = h - mu
    var = jnp.mean(hc * hc, axis=-1, keepdims=True)
    return hc * lax.rsqrt(var + LN_EPS) * g + b


def _resid_ln_kernel(x_ref, m_ref, g_ref, b_ref, o_ref):
    o_ref[...] = _ln_rows(DN_ALPHA * x_ref[...] + m_ref[...], g_ref[...], b_ref[...])


def _resid_ln(x, mix, g, b, tm=256):
    t, d = x.shape
    row = pl.BlockSpec((tm, d), lambda i: (i, 0))
    vec = pl.BlockSpec((1, d), lambda i: (0, 0))
    return pl.pallas_call(
        _resid_ln_kernel,
        grid=(t // tm,),
        in_specs=[row, row, vec, vec],
        out_specs=row,
        out_shape=jax.ShapeDtypeStruct((t, d), F32),
        compiler_params=_params("parallel"),
        name="resid_ln",
    )(x, mix, g.reshape(1, d), b.reshape(1, d))


def _rope_tables(s):
    half = ROPE_DIM // 2
    inv = jnp.power(jnp.float32(ROPE_THETA), -jnp.arange(half, dtype=F32) / half)
    ang = jnp.arange(s).astype(F32)[:, None] * inv[None, :]
    cos, sin = jnp.cos(ang), jnp.sin(ang)
    zero = jnp.zeros((s, A_HEAD_DIM - ROPE_DIM), F32)
    zh = jnp.zeros((s, half), F32)
    c = jnp.concatenate([cos, cos, zero + 1.0], axis=-1)
    s_up = jnp.concatenate([zh, sin, zero], axis=-1)
    s_dn = jnp.concatenate([-sin, zh, zero], axis=-1)
    return c, s_up, s_dn


def _rope(x, c, s_up, s_dn):
    half = ROPE_DIM // 2
    return (x * c + pltpu.roll(x, half, axis=1) * s_up
            + pltpu.roll(x, A_HEAD_DIM - half, axis=1) * s_dn)


def _moba_kernel(q_ref, k_ref, v_ref, c_ref, su_ref, sd_ref, o_ref, krot_ref, vb_ref, kmean_ref, *, nblk):
    qi = pl.program_id(2)
    blk = MOBA_BLOCK
    scale = A_HEAD_DIM ** -0.5

    @pl.when(qi == 0)
    def _prepare_keys():
        def body(j, carry):
            r = pl.multiple_of(j * blk, blk)
            rows = pl.ds(r, blk)
            kr = _rope(k_ref[rows, :], c_ref[rows, :], su_ref[rows, :], sd_ref[rows, :])
            krot_ref[rows, :] = kr.astype(BF16)
            vb_ref[rows, :] = v_ref[rows, :].astype(BF16)
            kmean_ref[pl.ds(j, 1), :] = jnp.mean(kr, axis=0, keepdims=True)
            return carry
        lax.fori_loop(0, nblk, body, 0)

    r0 = pl.multiple_of(qi * blk, blk)
    qrows = pl.ds(r0, blk)
    qb = _rope(q_ref[...], c_ref[qrows, :], su_ref[qrows, :], sd_ref[qrows, :]).astype(BF16)

    g = lax.dot_general(kmean_ref[...].astype(BF16), qb, _NT, preferred_element_type=F32)
    brow = lax.broadcasted_iota(jnp.int32, (nblk, blk), 0)
    g = jnp.where(brow < qi, g, -jnp.inf)
    sel_t = jnp.zeros((nblk, blk), F32)
    for j in range(nblk):
        gj = g[j:j + 1, :]
        earlier = jnp.where(brow < j, 1.0, 0.0)
        beats = jnp.where(g > gj, 1.0, jnp.where(g == gj, earlier, 0.0))
        cnt = jnp.sum(beats, axis=0, keepdims=True)
        sel_t = jnp.where(brow == j, jnp.where(cnt < float(min(MOBA_TOPK, nblk)), 1.0, 0.0), sel_t)
    sel_t = jnp.where(brow < qi, sel_t, 0.0)
    eye = jnp.where(lax.broadcasted_iota(jnp.int32, (blk, blk), 0)
                    == lax.broadcasted_iota(jnp.int32, (blk, blk), 1), 1.0, 0.0).astype(BF16)
    sel = lax.dot_general(eye, sel_t.astype(BF16), _NT, preferred_element_type=F32)
    bcol = lax.broadcasted_iota(jnp.int32, (blk, nblk), 1)

    kd = krot_ref[qrows, :]
    s = lax.dot_general(qb, kd, _NT, preferred_element_type=F32) * scale
    causal = (lax.broadcasted_iota(jnp.int32, (blk, blk), 1)
              <= lax.broadcasted_iota(jnp.int32, (blk, blk), 0))
    s = jnp.where(causal, s, NEG_INF)
    m0 = jnp.max(s, axis=1, keepdims=True)
    p = jnp.exp(s - m0)
    l0 = jnp.sum(p, axis=1, keepdims=True)
    acc0 = jnp.dot(p.astype(BF16), vb_ref[qrows, :], preferred_element_type=F32)

    def past_block(j, carry):
        m, l, acc = carry
        rows = pl.ds(pl.multiple_of(j * blk, blk), blk)
        sj = lax.dot_general(qb, krot_ref[rows, :], _NT, preferred_element_type=F32) * scale
        picked = jnp.sum(jnp.where(bcol == j, sel, 0.0), axis=1, keepdims=True)
        sj = jnp.where(picked > 0.5, sj, NEG_INF)
        m_new = jnp.maximum(m, jnp.max(sj, axis=1, keepdims=True))
        a = jnp.exp(m - m_new)
        pj = jnp.exp(sj - m_new)
        l = a * l + jnp.sum(pj, axis=1, keepdims=True)
        acc = a * acc + jnp.dot(pj.astype(BF16), vb_ref[rows, :], preferred_element_type=F32)
        return m_new, l, acc

    _, l, acc = lax.fori_loop(0, qi, past_block, (m0, l0, acc0))
    o_ref[...] = (acc / l).astype(o_ref.dtype)


def _moba(z, bsz, s):
    nblk = s // MOBA_BLOCK
    c, s_up, s_dn = _rope_tables(s)
    tab = pl.BlockSpec((s, A_HEAD_DIM), lambda b, h, qi: (0, 0))
    return pl.pallas_call(
        functools.partial(_moba_kernel, nblk=nblk),
        grid=(bsz, A_HEADS, nblk),
        in_specs=[pl.BlockSpec((MOBA_BLOCK, A_HEAD_DIM), lambda b, h, qi: (b * nblk + qi, h)),
                  pl.BlockSpec((s, A_HEAD_DIM), lambda b, h, qi: (b, A_HEADS + h)),
                  pl.BlockSpec((s, A_HEAD_DIM), lambda b, h, qi: (b, 2 * A_HEADS + h)),
                  tab, tab, tab],
        out_specs=pl.BlockSpec((MOBA_BLOCK, A_HEAD_DIM), lambda b, h, qi: (b * nblk + qi, h)),
        out_shape=jax.ShapeDtypeStruct((bsz * s, A_WIDTH), BF16),
        scratch_shapes=[pltpu.VMEM((s, A_HEAD_DIM), BF16), pltpu.VMEM((s, A_HEAD_DIM), BF16),
                        pltpu.VMEM((nblk, A_HEAD_DIM), F32)],
        compiler_params=_params("parallel", "parallel", "arbitrary"),
        name="moba",
    )(z, z, z, c, s_up, s_dn)


def _log_sigmoid(x):
    return jnp.minimum(x, 0.0) - jnp.log1p(jnp.exp(-jnp.abs(x)))


def _shift_rows(x, prev8, d):
    if d == 0:
        return x
    rolled = pltpu.roll(x, d, axis=0)
    top = jnp.where(lax.broadcasted_iota(jnp.int32, prev8.shape, 0) < d,
                    pltpu.roll(prev8, d, axis=0), rolled[:SUBLANES])
    return jnp.concatenate([top, rolled[SUBLANES:]], axis=0)


def _mlstm_kernel(q_ref, k_ref, v_ref, o_ref, gc_ref, gr_ref, bc_ref, br_ref,
                  cwq_ref, cwk_ref, cbq_ref, cbk_ref, hg_ref, y_ref,
                  c_ref, n_ref, m_ref, pq_ref, pk_ref, *, tt):
    h = pl.program_id(1)
    ti = pl.program_id(2)
    L = MLSTM_CHUNK
    dk = B_HEAD_DIM

    @pl.when(ti == 0)
    def _reset():
        c_ref[...] = jnp.zeros_like(c_ref)
        n_ref[...] = jnp.zeros_like(n_ref)
        m_ref[...] = jnp.zeros_like(m_ref)
        pq_ref[...] = jnp.zeros_like(pq_ref)
        pk_ref[...] = jnp.zeros_like(pk_ref)

    def conv_silu(x_ref, p_ref, w_ref, b_ref):
        x = x_ref[...]
        prev8 = p_ref[...]
        out = b_ref[...] + x * w_ref[B_CONV - 1:B_CONV, :]
        for d in range(1, B_CONV):
            out = out + _shift_rows(x, prev8, d) * w_ref[B_CONV - 1 - d:B_CONV - d, :]
        p_ref[...] = x[tt - SUBLANES:, :]
        return out * jax.nn.sigmoid(out)

    q_all = conv_silu(q_ref, pq_ref, cwq_ref, cbq_ref)
    k_all = conv_silu(k_ref, pk_ref, cwk_ref, cbk_ref) * (dk ** -0.5)

    lane8 = lax.broadcasted_iota(jnp.int32, (tt, 2 * B_HEADS), 1)
    gcol = gc_ref[...] + bc_ref[...]
    i_col_all = jnp.sum(jnp.where(lane8 == h, gcol, 0.0), axis=1, keepdims=True)
    f_col_all = _log_sigmoid(jnp.sum(jnp.where(lane8 == B_HEADS + h, gcol, 0.0), axis=1, keepdims=True))
    i_row_all = gr_ref[pl.ds(h, 1), :] + br_ref[pl.ds(h, 1), :]
    f_row_all = _log_sigmoid(gr_ref[pl.ds(B_HEADS + h, 1), :] + br_ref[pl.ds(B_HEADS + h, 1), :])

    rr = lax.broadcasted_iota(jnp.int32, (L, L), 0)
    cc = lax.broadcasted_iota(jnp.int32, (L, L), 1)
    tri = cc <= rr

    for c in range(tt // L):
        lo, hi = c * L, (c + 1) * L
        qc = q_all[lo:hi]
        kc = k_all[lo:hi]
        qcb = qc.astype(BF16)
        vcb = v_ref[lo:hi, :].astype(BF16)
        i_col, f_col = i_col_all[lo:hi], f_col_all[lo:hi]
        i_row, f_row = i_row_all[:, lo:hi], f_row_all[:, lo:hi]
        m_prev = m_ref[...]
        cst = c_ref[...]
        nst = n_ref[...]

        b_col = jnp.sum(jnp.where(tri, f_row, 0.0), axis=1, keepdims=True)
        b_row = jnp.sum(jnp.where(rr <= cc, f_col, 0.0), axis=0, keepdims=True)
        log_inter = b_col + m_prev
        log_intra = jnp.where(tri, b_col - b_row + i_row, -jnp.inf)
        m_t = jnp.maximum(log_inter, jnp.max(log_intra, axis=1, keepdims=True))
        w_intra = jnp.exp(log_intra - m_t)
        w_inter = jnp.exp(log_inter - m_t)
        qk = lax.dot_general(qcb, kc.astype(BF16), _NT, preferred_element_type=F32) * w_intra
        num = (w_inter * jnp.dot(qcb, cst.astype(BF16), preferred_element_type=F32)
               + jnp.dot(qk.astype(BF16), vcb, preferred_element_type=F32))
        den = (w_inter * jnp.sum(qc * nst, axis=1, keepdims=True)
               + jnp.sum(qk, axis=1, keepdims=True))
        hc = num / jnp.maximum(jnp.abs(den), jnp.exp(-m_t))

        b_last = jnp.sum(f_col, axis=0, keepdims=True)
        log_s = b_last - b_col + i_col
        m_new = jnp.maximum(b_last + m_prev, jnp.max(log_s, axis=0, keepdims=True))
        decay = jnp.exp(b_last + m_prev - m_new)
        kw = kc * jnp.exp(log_s - m_new)
        c_ref[...] = decay * cst + lax.dot_general(kw.astype(BF16), vcb, _TN, preferred_element_type=F32)
        n_ref[...] = decay * nst + jnp.sum(kw, axis=0, keepdims=True)
        m_ref[...] = m_new

        hn = hc * lax.rsqrt(jnp.mean(hc * hc, axis=1, keepdims=True) + HNORM_EPS)
        y_ref[lo:hi, :] = (hn * hg_ref[...] * jax.nn.sigmoid(o_ref[lo:hi, :])).astype(y_ref.dtype)


def _mlstm(z, gates, gate_bias, conv_w, conv_b, hnorm_g, bsz, s, tt=256):
    nt = s // tt
    cb = 3 * A_WIDTH // B_HEAD_DIM
    gates_t = gates.T
    bias = gate_bias.reshape(1, 2 * B_HEADS)

    def col(off):
        return pl.BlockSpec((tt, B_HEAD_DIM), lambda b, h, t: (b * nt + t, cb + off + h))

    def par(rows, off):
        return pl.BlockSpec((rows, B_HEAD_DIM), lambda b, h, t: (0, off + h))

    return pl.pallas_call(
        functools.partial(_mlstm_kernel, tt=tt),
        grid=(bsz, B_HEADS, nt),
        in_specs=[col(0), col(B_HEADS), col(2 * B_HEADS), col(3 * B_HEADS),
                  pl.BlockSpec((tt, 2 * B_HEADS), lambda b, h, t: (b * nt + t, 0)),
                  pl.BlockSpec((2 * B_HEADS, tt), lambda b, h, t: (0, b * nt + t)),
                  pl.BlockSpec((1, 2 * B_HEADS), lambda b, h, t: (0, 0)),
                  pl.BlockSpec((2 * B_HEADS, 1), lambda b, h, t: (0, 0)),
                  par(B_CONV, 0), par(B_CONV, B_HEADS), par(1, 0), par(1, B_HEADS), par(1, 0)],
        out_specs=pl.BlockSpec((tt, B_HEAD_DIM), lambda b, h, t: (b * nt + t, h)),
        out_shape=jax.ShapeDtypeStruct((bsz * s, B_WIDTH), BF16),
        scratch_shapes=[pltpu.VMEM((B_HEAD_DIM, B_HEAD_DIM), F32), pltpu.VMEM((1, B_HEAD_DIM), F32),
                        pltpu.VMEM((1, 1), F32), pltpu.VMEM((SUBLANES, B_HEAD_DIM), F32),
                        pltpu.VMEM((SUBLANES, B_HEAD_DIM), F32)],
        compiler_params=_params("parallel", "parallel", "arbitrary"),
        name="mlstm",
    )(z, z, z, z, gates, gates_t, bias, bias.T, conv_w, conv_w,
      conv_b.reshape(1, -1), conv_b.reshape(1, -1), hnorm_g.reshape(1, -1))


def _even_mixer(x, w_in, gate_bias, conv_w, conv_b, hnorm_g, w_out, bsz, s):
    d = x.shape[1]
    w_main = w_in[:, :EVEN_MAIN].astype(BF16)
    w_gate = jnp.pad(w_in[:, EVEN_MAIN:], ((0, 0), (0, LANES - 2 * B_HEADS))).astype(BF16)
    z = _matmul(x, w_main)
    gates = _matmul(x, w_gate, tn=LANES)[:, :2 * B_HEADS]
    y_a = _moba(z, bsz, s)
    y_b = _mlstm(z, gates, gate_bias, conv_w, conv_b, hnorm_g, bsz, s)
    w_o = w_out.astype(BF16)
    return _matmul2(y_a, y_b, w_o[:A_WIDTH], w_o[A_WIDTH:])


def _shift_mix_kernel(x_ref, p_ref, mu_ref, *o_refs, tiles_per_seq):
    x = x_ref[...]
    first = (pl.program_id(0) % tiles_per_seq) == 0
    prev_row = jnp.where(first, 0.0, p_ref[SUBLANES - 1:SUBLANES, :])
    row = lax.broadcasted_iota(jnp.int32, x.shape, 0)
    dx = jnp.where(row == 0, prev_row, pltpu.roll(x, 1, axis=0)) - x
    for j, o_ref in enumerate(o_refs):
        o_ref[...] = (x + dx * mu_ref[j:j + 1, :]).astype(o_ref.dtype)


def _shift_mix(x, mu, s, tm=256):
    t, d = x.shape
    n = mu.shape[0]
    row = pl.BlockSpec((tm, d), lambda i: (i, 0))
    prev = pl.BlockSpec((SUBLANES, d), lambda i: (jnp.maximum(i * (tm // SUBLANES) - 1, 0), 0))
    return pl.pallas_call(
        functools.partial(_shift_mix_kernel, tiles_per_seq=s // tm),
        grid=(t // tm,),
        in_specs=[row, prev, pl.BlockSpec((n, d), lambda i: (0, 0))],
        out_specs=[row] * n,
        out_shape=[jax.ShapeDtypeStruct((t, d), BF16)] * n,
        compiler_params=_params("parallel"),
        name="shift_mix",
    )(x, x, mu)


def _softplus(x):
    return jnp.maximum(x, 0.0) + jnp.log1p(jnp.exp(-jnp.abs(x)))


def _lora_kernel(x_ref, w1_ref, w2_ref, b_ref, o_ref, *, mid, out):
    hmid = jnp.dot(x_ref[...], w1_ref[...], preferred_element_type=F32)
    if mid == "tanh":
        hmid = jnp.tanh(hmid)
    elif mid == "sigmoid":
        hmid = jax.nn.sigmoid(hmid)
    y = jnp.dot(hmid.astype(BF16), w2_ref[...], preferred_element_type=F32)
    if out == "logdecay":
        y = -jnp.exp(-_softplus(-(b_ref[...] + y)) - 0.5)
    elif out == "sigmoid":
        y = jax.nn.sigmoid(b_ref[...] + y)
    o_ref[...] = y


def _lora(x, w1, w2, bias, mid, out, tm=512):
    t, d = x.shape
    tm = min(tm, t)
    r = w1.shape[1]
    rp = -(-r // LANES) * LANES
    w1p = jnp.pad(w1, ((0, 0), (0, rp - r))).astype(BF16)
    w2p = jnp.pad(w2, ((0, rp - r), (0, 0))).astype(BF16)
    dout = w2.shape[1]
    if bias is None:
        bias = jnp.zeros((dout,), F32)
    return pl.pallas_call(
        functools.partial(_lora_kernel, mid=mid, out=out),
        grid=(t // tm,),
        in_specs=[pl.BlockSpec((tm, d), lambda i: (i, 0)),
                  pl.BlockSpec((d, rp), lambda i: (0, 0)),
                  pl.BlockSpec((rp, dout), lambda i: (0, 0)),
                  pl.BlockSpec((1, dout), lambda i: (0, 0))],
        out_specs=pl.BlockSpec((tm, dout), lambda i: (i, 0)),
        out_shape=jax.ShapeDtypeStruct((t, dout), F32),
        compiler_params=_params("parallel"),
        name="lora_" + out,
    )(x, w1p, w2p, bias.reshape(1, dout))


def _head_ones():
    i = jnp.arange(LANES) // C_HEAD_DIM
    return (i[:, None] == i[None, :]).astype(F32)


def _head_sum(x, ones):
    cols = [jnp.dot(x[:, c * LANES:(c + 1) * LANES], ones, precision=lax.Precision.HIGHEST,
                    preferred_element_type=F32) for c in range(x.shape[1] // LANES)]
    return jnp.concatenate(cols, axis=1)


def _rwkv_prep_kernel(*refs, vres):
    if vres:
        k_ref, a_ref, v_ref, vf_ref, vg_ref, kk_p, ka_p, ones_ref, kk_o, kh_o, bb_o, v_o = refs
    else:
        k_ref, a_ref, v_ref, kk_p, ka_p, ones_ref, kk_o, kh_o, bb_o, v_o = refs
    k = k_ref[...]
    a = a_ref[...]
    kk = k * kk_p[...]
    nrm = jnp.sqrt(_head_sum(kk * kk, ones_ref[...]))
    kk = kk / jnp.maximum(nrm, KK_EPS)
    kk_o[...] = kk
    bb_o[...] = kk * a
    kh_o[...] = k * (1.0 + (a - 1.0) * ka_p[...])
    v = v_ref[...]
    if vres:
        v = v + (vf_ref[...] - v) * vg_ref[...]
    v_o[...] = v


def _rwkv_prep(k, a, v, v_first, vgate, k_k, k_a, tm=256):
    t, d = k.shape
    row = pl.BlockSpec((tm, d), lambda i: (i, 0))
    vec = pl.BlockSpec((1, d), lambda i: (0, 0))
    vres = v_first is not None
    ins = [k, a, v] + ([v_first, vgate] if vres else []) + [k_k.reshape(1, d), k_a.reshape(1, d), _head_ones()]
    specs = [row] * (5 if vres else 3) + [vec, vec, pl.BlockSpec((LANES, LANES), lambda i: (0, 0))]
    return pl.pallas_call(
        functools.partial(_rwkv_prep_kernel, vres=vres),
        grid=(t // tm,),
        in_specs=specs,
        out_specs=[row] * 4,
        out_shape=[jax.ShapeDtypeStruct((t, d), F32)] * 4,
        compiler_params=_params("parallel"),
        name="rwkv_prep",
    )(*ins)


def _bdot(a, b):
    return jnp.dot(a.astype(BF16), b.astype(BF16), preferred_element_type=F32)


def _rwkv_chunk_pairs(r, ld, kh, v, kk, bb, g_cum, s0, c):
    pairs = range(len(r))
    lane = lax.broadcasted_iota(jnp.int32, (1, LANES), 1)
    m0 = jnp.where(lane < C_HEAD_DIM, 1.0, 0.0)
    m1 = 1.0 - m0
    n2 = 2 * c
    row = lax.broadcasted_iota(jnp.int32, (n2, n2), 0)
    col = lax.broadcasted_iota(jnp.int32, (n2, n2), 1)
    lower, lower_eq = col < row, col <= row

    def stack(x):
        return jnp.concatenate([x * m0, x * m1], axis=0)

    e_g = [jnp.exp(g_cum[p]) for p in pairs]
    e_ng = [jnp.exp(-g_cum[p]) for p in pairs]
    al2 = [stack(-(kk[p] * jnp.exp(g_cum[p] - ld[p]))) for p in pairs]
    be2 = [stack(bb[p] * e_ng[p]).astype(BF16) for p in pairs]
    kb2 = [stack(kh[p] * e_ng[p]).astype(BF16) for p in pairs]
    rb = [r[p] * e_g[p] for p in pairs]
    v2 = [stack(v[p]) for p in pairs]
    sc = [lax.dot_general(jnp.concatenate([al2[p], stack(rb[p])], axis=0).astype(BF16),
                          jnp.concatenate([be2[p], kb2[p]], axis=0), _NT,
                          preferred_element_type=F32) for p in pairs]
    l_ab = [jnp.where(lower, sc[p][:n2, :n2], 0.0) for p in pairs]
    l_ak = [jnp.where(lower, sc[p][:n2, n2:], 0.0) for p in pairs]
    l_r = [jnp.concatenate([jnp.where(lower_eq, sc[p][n2:, :n2], 0.0),
                            jnp.where(lower_eq, sc[p][n2:, n2:], 0.0)], axis=1).astype(BF16) for p in pairs]

    t_off = l_ab
    pw = [x.astype(BF16) for x in l_ab]
    for _ in range(max(c.bit_length() - 2, 0)):
        pw_f = [jnp.dot(pw[p], pw[p], preferred_element_type=F32) for p in pairs]
        pw = [x.astype(BF16) for x in pw_f]
        t_off = [t_off[p] + pw_f[p] + jnp.dot(t_off[p].astype(BF16), pw[p], preferred_element_type=F32)
                 for p in pairs]

    w0 = [jnp.concatenate([al2[p], _bdot(l_ak[p], v2[p])], axis=1) for p in pairs]
    x = [w0[p] + _bdot(t_off[p], w0[p]) for p in pairs]
    xb = [x[p].astype(BF16) for p in pairs]
    z = [jnp.dot(l_r[p],
                 jnp.concatenate([xb[p], jnp.concatenate([jnp.zeros_like(v2[p]), v2[p]], axis=1).astype(BF16)],
                                 axis=0), preferred_element_type=F32) for p in pairs]
    r_eff = [(rb[p] + z[p][:c, :LANES] + z[p][c:, :LANES]).astype(BF16) for p in pairs]
    y0 = [z[p][:c, LANES:] + z[p][c:, LANES:] for p in pairs]
    mn = [lax.dot_general(xb[p], be2[p], _TN, preferred_element_type=F32) for p in pairs]
    n_add = [mn[p][LANES:] + lax.dot_general(v2[p].astype(BF16), kb2[p], _TN, preferred_element_type=F32)
             for p in pairs]
    s0b = [s0[p].astype(BF16) for p in pairs]
    y = [y0[p] + lax.dot_general(r_eff[p], s0b[p], _NT, preferred_element_type=F32) for p in pairs]
    s_new = [(s0[p] + jnp.dot(s0b[p], mn[p][:LANES].astype(BF16), preferred_element_type=F32) + n_add[p])
             * e_g[p][c - 1:c, :] for p in pairs]
    return y, s_new


def _rwkv_scan_kernel(r_ref, ld_ref, kh_ref, v_ref, kk_ref, bb_ref, tri_ref, y_ref, s_ref, *, c, npair):
    @pl.when(pl.program_id(2) == 0)
    def _reset():
        s_ref[...] = jnp.zeros_like(s_ref)

    g_all = jnp.dot(tri_ref[...], ld_ref[...], precision=lax.Precision.HIGHEST, preferred_element_type=F32)
    sl = [slice(p * LANES, (p + 1) * LANES) for p in range(npair)]
    take = lambda ref: [ref[:, q] for q in sl]
    y, s_new = _rwkv_chunk_pairs(take(r_ref), take(ld_ref), take(kh_ref), take(v_ref), take(kk_ref),
                                 take(bb_ref), [g_all[:, q] for q in sl], [s_ref[p] for p in range(npair)], c)
    for p in range(npair):
        y_ref[:, sl[p]] = y[p]
        s_ref[p] = s_new[p]


RWKV_CHUNK = 64
RWKV_PAIRS = 8


def _rwkv_scan(r, log_w, kh, v, kk, bb, bsz, s):
    t, d = r.shape
    c = min(RWKV_CHUNK, s)
    wid = RWKV_PAIRS * LANES
    nc = s // c
    tri = (jnp.arange(c)[None, :] <= jnp.arange(c)[:, None]).astype(F32)
    blk = pl.BlockSpec((c, wid), lambda b, g, i: (b * nc + i, g))
    return pl.pallas_call(
        functools.partial(_rwkv_scan_kernel, c=c, npair=RWKV_PAIRS),
        grid=(bsz, d // wid, nc),
        in_specs=[blk] * 6 + [pl.BlockSpec((c, c), lambda b, g, i: (0, 0))],
        out_specs=blk,
        out_shape=jax.ShapeDtypeStruct((t, d), F32),
        scratch_shapes=[pltpu.VMEM((RWKV_PAIRS, LANES, LANES), F32)],
        compiler_params=_params("parallel", "parallel", "arbitrary"),
        name="rwkv_scan",
    )(r, log_w, kh, v, kk, bb, tri)


def _rwkv_post_kernel(y_ref, r_ref, kh_ref, v_ref, g_ref, lg_ref, lb_ref, rk_ref, ones_ref, o_ref):
    ones = ones_ref[...]
    y = y_ref[...]
    inv_n = 1.0 / C_HEAD_DIM
    yc = y - _head_sum(y, ones) * inv_n
    yv = _head_sum(yc * yc, ones) * inv_n
    yn = yc * lax.rsqrt(yv + LNX_EPS) * lg_ref[...] + lb_ref[...]
    bonus = _head_sum(r_ref[...] * kh_ref[...] * rk_ref[...], ones) * v_ref[...]
    o_ref[...] = ((yn + bonus) * g_ref[...]).astype(o_ref.dtype)


def _rwkv_post(y, r, kh, v, g, lnx_g, lnx_b, r_k, tm=256):
    t, d = y.shape
    row = pl.BlockSpec((tm, d), lambda i: (i, 0))
    vec = pl.BlockSpec((1, d), lambda i: (0, 0))
    return pl.pallas_call(
        _rwkv_post_kernel,
        grid=(t // tm,),
        in_specs=[row] * 5 + [vec] * 3 + [pl.BlockSpec((LANES, LANES), lambda i: (0, 0))],
        out_specs=row,
        out_shape=jax.ShapeDtypeStruct((t, d), BF16),
        compiler_params=_params("parallel"),
        name="rwkv_post",
    )(y, r, kh, v, g, lnx_g.reshape(1, d), lnx_b.reshape(1, d), r_k.reshape(1, d), _head_ones())


def _rwkv7_mixer(x, v_first, mu, w_r, w_k, w_v, w_o, w0, w1, w2, a0, a1, a2, g1, g2,
                 k_k, k_a, r_k, lnx_g, lnx_b, vres, bsz, s):
    xr, xw, xk, xv, xa, xg = _shift_mix(x, mu, s)
    r = _matmul(xr, w_r.astype(BF16))
    k = _matmul(xk, w_k.astype(BF16))
    v = _matmul(xv, w_v.astype(BF16))
    log_decay = _lora(xw, w1, w2, w0, "tanh", "logdecay")
    a = _lora(xa, a1, a2, a0, "none", "sigmoid")
    g = _lora(xg, g1, g2, None, "sigmoid", "none")
    if vres is None:
        vgate = None
        v_keep = v
    else:
        v0, v1, v2 = vres
        vgate = _lora(xv, v1, v2, v0, "none", "sigmoid")
        v_keep = v_first
    kk, kh, bb, v = _rwkv_prep(k, a, v, None if vres is None else v_first, vgate, k_k, k_a)
    y = _rwkv_scan(r, log_decay, kh, v, kk, bb, bsz, s)
    yo = _rwkv_post(y, r, kh, v, g, lnx_g, lnx_b, r_k)
    return _matmul(yo, w_o.astype(BF16)), v_keep


def _lane_pick(cond, lanef):
    return jnp.min(jnp.where(cond, lanef, float(LANES)), axis=1, keepdims=True)


def _router_kernel(x_ref, w_ref, b_ref, route_ref, cnt_ref, run_ref, *, tm):
    @pl.when(pl.program_id(0) == 0)
    def _reset():
        run_ref[...] = jnp.zeros_like(run_ref)

    logits = jnp.dot(x_ref[...].astype(BF16), w_ref[...], preferred_element_type=F32) + b_ref[...]
    lane = lax.broadcasted_iota(jnp.int32, (tm, LANES), 1)
    lanef = lane.astype(F32)

    def masked_softmax(keep):
        z = jnp.where(keep, logits, -jnp.inf)
        e = jnp.exp(z - jnp.max(z, axis=1, keepdims=True))
        return e / jnp.sum(e, axis=1, keepdims=True)

    in_groups = lane < N_GROUPS
    pg = masked_softmax(in_groups)
    g_p = jnp.max(pg, axis=1, keepdims=True)
    g_idx = _lane_pick(jnp.where(in_groups, pg, -1.0) == g_p, lanef)
    lo = float(N_GROUPS) + float(EXPERTS_PER_GROUP) * g_idx
    half = 0.5 * float(EXPERTS_PER_GROUP - 1)
    in_group = jnp.abs(lanef - lo - half) < half + 0.5
    pe = jnp.where(in_group, masked_softmax(in_group), -1.0)
    p1 = jnp.max(pe, axis=1, keepdims=True)
    i1 = _lane_pick(pe == p1, lanef)
    pe2 = jnp.where(lanef == i1, -1.0, pe)
    p2 = jnp.max(pe2, axis=1, keepdims=True)
    i2 = _lane_pick(pe2 == p2, lanef)
    den = p1 + p2
    gate = (g_p * (p1 / den), g_p * (p2 / den))
    expert = (i1 - float(N_GROUPS), i2 - float(N_GROUPS))

    earlier = jnp.where(lax.broadcasted_iota(jnp.int32, (tm, tm), 1)
                        < lax.broadcasted_iota(jnp.int32, (tm, tm), 0), 1.0, 0.0).astype(BF16)
    run = run_ref[...]
    rank = []
    for e in expert:
        onehot = jnp.where(lanef == e, 1.0, 0.0)
        before = jnp.dot(earlier, onehot.astype(BF16), preferred_element_type=F32) + run
        rank.append(jnp.sum(onehot * before, axis=1, keepdims=True))
        run = run + jnp.sum(onehot, axis=0, keepdims=True)
    run_ref[...] = run
    cnt_ref[...] = run
    cols = (expert[0], expert[1], rank[0], rank[1], gate[0], gate[1])
    route = jnp.zeros((tm, LANES), F32)
    for c, val in enumerate(cols):
        route = jnp.where(lane == c, val, route)
    route_ref[...] = route


def _router(x, w_router, b_router, tm=256):
    t, d = x.shape
    tm = min(tm, t)
    return pl.pallas_call(
        functools.partial(_router_kernel, tm=tm),
        grid=(t // tm,),
        in_specs=[pl.BlockSpec((tm, d), lambda i: (i, 0)),
                  pl.BlockSpec((d, LANES), lambda i: (0, 0)),
                  pl.BlockSpec((1, LANES), lambda i: (0, 0))],
        out_specs=[pl.BlockSpec((tm, LANES), lambda i: (i, 0)),
                   pl.BlockSpec((1, LANES), lambda i: (0, 0))],
        out_shape=[jax.ShapeDtypeStruct((t, LANES), F32), jax.ShapeDtypeStruct((1, LANES), F32)],
        scratch_shapes=[pltpu.VMEM((1, LANES), F32)],
        compiler_params=_params("arbitrary"),
        name="router",
    )(x, w_router, b_router)


def _gather_rows_kernel(idx_ref, src_ref, o_ref, sem, *, rows):
    def start(r, carry):
        pltpu.make_async_copy(src_ref.at[pl.ds(idx_ref[0, r], 1)], o_ref.at[pl.ds(r, 1)], sem).start()
        return carry

    def wait(r, carry):
        pltpu.make_async_copy(src_ref.at[pl.ds(0, 1)], o_ref.at[pl.ds(r, 1)], sem).wait()
        return carry

    lax.fori_loop(0, rows, start, 0)
    lax.fori_loop(0, rows, wait, 0)


def _gather_rows(src, idx, rows=MOE_ROWS):
    n = idx.shape[0]
    d = src.shape[1]
    nb = n // rows
    return pl.pallas_call(
        functools.partial(_gather_rows_kernel, rows=rows),
        grid=(nb,),
        in_specs=[pl.BlockSpec((None, 1, rows), lambda i: (i, 0, 0), memory_space=pltpu.SMEM),
                  pl.BlockSpec(memory_space=pl.ANY)],
        out_specs=pl.BlockSpec((rows, d), lambda i: (i, 0)),
        out_shape=jax.ShapeDtypeStruct((n, d), src.dtype),
        scratch_shapes=[pltpu.SemaphoreType.DMA(())],
        compiler_params=_params("arbitrary"),
        name="gather_rows",
    )(idx.reshape(nb, 1, rows), src)


def _expert_kernel(be_ref, nu_ref, x_ref, wg_ref, wu_ref, wd_ref, o_ref, wg_b, wu_b, wd_b):
    i = pl.program_id(0)
    used = i < nu_ref[0]

    @pl.when(jnp.logical_or(i == 0, be_ref[i] != be_ref[jnp.maximum(i - 1, 0)]))
    def _():
        wg_b[...] = wg_ref[...].astype(BF16)
        wu_b[...] = wu_ref[...].astype(BF16)
        wd_b[...] = wd_ref[...].astype(BF16)

    @pl.when(used)
    def _():
        xb = x_ref[...].astype(BF16)
        hg = jnp.dot(xb, wg_b[...], preferred_element_type=F32)
        hu = jnp.dot(xb, wu_b[...], preferred_element_type=F32)
        hid = (hg * jax.nn.sigmoid(hg) * hu).astype(BF16)
        o_ref[...] = jnp.dot(hid, wd_b[...], preferred_element_type=F32)

    @pl.when(jnp.logical_not(used))
    def _():
        o_ref[...] = jnp.zeros_like(o_ref)


def _expert_mlp(xs, block_e, n_used, w_gate, w_up, w_down, layer, rows=MOE_ROWS):
    n, d = xs.shape
    ff = w_gate.shape[3]
    nb = n // rows
    grid_spec = pltpu.PrefetchScalarGridSpec(
        num_scalar_prefetch=2,
        grid=(nb,),
        in_specs=[pl.BlockSpec((rows, d), lambda i, be, nu: (jnp.minimum(i, nu[0] - 1), 0)),
                  pl.BlockSpec((None, None, d, ff), lambda i, be, nu: (layer, be[i], 0, 0)),
                  pl.BlockSpec((None, None, d, ff), lambda i, be, nu: (layer, be[i], 0, 0)),
                  pl.BlockSpec((None, None, ff, d), lambda i, be, nu: (layer, be[i], 0, 0))],
        out_specs=pl.BlockSpec((rows, d), lambda i, be, nu: (i, 0)),
        scratch_shapes=[pltpu.VMEM((d, ff), BF16), pltpu.VMEM((d, ff), BF16), pltpu.VMEM((ff, d), BF16)],
    )
    return pl.pallas_call(
        _expert_kernel,
        grid_spec=grid_spec,
        out_shape=jax.ShapeDtypeStruct((n, d), F32),
        compiler_params=_params("arbitrary"),
        name="expert_mlp",
    )(block_e, n_used, xs, w_gate, w_up, w_down)


def _combine_ln_kernel(idx_ref, y_ref, x_ref, gt_ref, g_ref, b_ref, o_ref, buf, sem, *, rows):
    def start(r, carry):
        for kslot in range(MOE_TOPK):
            pltpu.make_async_copy(y_ref.at[pl.ds(idx_ref[0, MOE_TOPK * r + kslot], 1)],
                                  buf.at[kslot, pl.ds(r, 1)], sem).start()
        return carry

    def wait(r, carry):
        for kslot in range(MOE_TOPK):
            pltpu.make_async_copy(y_ref.at[pl.ds(0, 1)], buf.at[kslot, pl.ds(r, 1)], sem).wait()
        return carry

    lax.fori_loop(0, rows, start, 0)
    lax.fori_loop(0, rows, wait, 0)
    gt = gt_ref[...]
    ffn = gt[:, 0:1] * buf[0]
    for kslot in range(1, MOE_TOPK):
        ffn = ffn + gt[:, kslot:kslot + 1] * buf[kslot]
    o_ref[...] = _ln_rows(DN_ALPHA * x_ref[...] + ffn, g_ref[...], b_ref[...])


def _combine_ln(y_sorted, dest, gates, x, g, b, rows=256):
    t, d = x.shape
    nb = t // rows
    row = pl.BlockSpec((rows, d), lambda i: (i, 0))
    vec = pl.BlockSpec((1, d), lambda i: (0, 0))
    return pl.pallas_call(
        functools.partial(_combine_ln_kernel, rows=rows),
        grid=(nb,),
        in_specs=[pl.BlockSpec((None, 1, rows * MOE_TOPK), lambda i: (i, 0, 0), memory_space=pltpu.SMEM),
                  pl.BlockSpec(memory_space=pl.ANY), row,
                  pl.BlockSpec((rows, MOE_TOPK), lambda i: (i, 0)), vec, vec],
        out_specs=row,
        out_shape=jax.ShapeDtypeStruct((t, d), F32),
        scratch_shapes=[pltpu.VMEM((MOE_TOPK, rows, d), F32), pltpu.SemaphoreType.DMA(())],
        compiler_params=_params("arbitrary"),
        name="combine_ln",
    )(dest.reshape(nb, 1, rows * MOE_TOPK), y_sorted, x, gates, g.reshape(1, d), b.reshape(1, d))


def _moe_layer(x, w_grp, b_grp, w_exp_r, b_exp_r, w_gate, w_up, w_down, ln_g, ln_b, layer):
    t, d = x.shape
    n_r = N_GROUPS + N_EXPERTS
    w_router = jnp.pad(jnp.concatenate([w_grp, w_exp_r], axis=1), ((0, 0), (0, LANES - n_r))).astype(BF16)
    b_router = jnp.pad(jnp.concatenate([b_grp, b_exp_r]), (0, LANES - n_r)).reshape(1, LANES)
    route, cnt = _router(x, w_router, b_router)
    flat_e = route[:, 0:MOE_TOPK].astype(jnp.int32).reshape(-1)
    rank = route[:, MOE_TOPK:2 * MOE_TOPK].astype(jnp.int32).reshape(-1)
    gates = route[:, 2 * MOE_TOPK:3 * MOE_TOPK]
    counts = cnt[0, :N_EXPERTS].astype(jnp.int32)
    m = t * MOE_TOPK
    padded = (counts + MOE_ROWS - 1) // MOE_ROWS * MOE_ROWS
    pad_end = jnp.cumsum(padded)
    dest = (pad_end - padded)[flat_e] + rank
    n_blocks = (m + N_EXPERTS * (MOE_ROWS - 1) + MOE_ROWS - 1) // MOE_ROWS
    src_row = jnp.zeros((n_blocks * MOE_ROWS,), jnp.int32).at[dest].set(
        (jnp.arange(m) // MOE_TOPK).astype(jnp.int32))
    block_e = jnp.minimum(jnp.searchsorted(pad_end, jnp.arange(n_blocks) * MOE_ROWS, side='right'),
                          N_EXPERTS - 1).astype(jnp.int32)
    n_used = (pad_end[-1] // MOE_ROWS).astype(jnp.int32).reshape(1)

    xs = _gather_rows(x, src_row)
    ys = _expert_mlp(xs, block_e, n_used, w_gate, w_up, w_down, layer)
    return _combine_ln(ys, dest.astype(jnp.int32), gates, x, ln_g, ln_b)


def kernel(x, ev_w_in, ev_gate_bias, ev_conv_w, ev_conv_b, ev_hnorm_g, ev_w_out, od_mu, od_w_r, od_w_k, od_w_v, od_w_o, od_w0, od_w1, od_w2, od_a0, od_a1, od_a2, od_g1, od_g2, od_k_k, od_k_a, od_r_k, od_lnx_g, od_lnx_b, od_v0, od_v1, od_v2, ln_mix_g, ln_mix_b, ln_ffn_g, ln_ffn_b, moe_w_grp, moe_b_grp, moe_w_exp_r, moe_b_exp_r, moe_w_gate, moe_w_up, moe_w_down):
    bsz, s, d = x.shape
    depth = ln_mix_g.shape[0]
    xf = x.reshape(bsz * s, d)
    v_first = None
    for layer in range(depth):
        if layer % 2 == 0:
            e = layer // 2
            mix = _even_mixer(xf, ev_w_in[e], ev_gate_bias[e], ev_conv_w[e], ev_conv_b[e],
                              ev_hnorm_g[e], ev_w_out[e], bsz, s)
        else:
            o = layer // 2
            vres = None if o == 0 else (od_v0[o - 1], od_v1[o - 1], od_v2[o - 1])
            mix, v_first = _rwkv7_mixer(xf, v_first, od_mu[o], od_w_r[o], od_w_k[o], od_w_v[o],
                                        od_w_o[o], od_w0[o], od_w1[o], od_w2[o], od_a0[o],
                                        od_a1[o], od_a2[o], od_g1[o], od_g2[o], od_k_k[o],
                                        od_k_a[o], od_r_k[o], od_lnx_g[o], od_lnx_b[o], vres, bsz, s)
        xf = _resid_ln(xf, mix, ln_mix_g[layer], ln_mix_b[layer])
        xf = _moe_layer(xf, moe_w_grp[layer], moe_b_grp[layer], moe_w_exp_r[layer],
                        moe_b_exp_r[layer], moe_w_gate, moe_w_up, moe_w_down,
                        ln_ffn_g[layer], ln_ffn_b[layer], layer)
    return xf.reshape(bsz, s, d)
```

```python
import functools
import math

import jax
import jax.numpy as jnp
from jax import lax
from jax.experimental import pallas as pl
from jax.experimental.pallas import tpu as pltpu

F32 = jnp.float32
BF16 = jnp.bfloat16

DEPTH = 4
DN_ALPHA = (2 * DEPTH) ** 0.25
LN_EPS = 1e-5

A_HEADS = 8
A_HEAD_DIM = 128
A_WIDTH = A_HEADS * A_HEAD_DIM
MOBA_BLOCK = 256
MOBA_TOPK = 3
MOBA_GROUP = 4
ROPE_THETA = 500000.0
ROPE_DIM = A_HEAD_DIM // 4
NEG_INF = -1e30

B_HEADS = 4
B_HEAD_DIM = 256
B_WIDTH = B_HEADS * B_HEAD_DIM
MLSTM_CHUNK = 64
B_CONV = 4
HNORM_EPS = 1e-6
EVEN_MAIN = 3 * A_WIDTH + 4 * B_WIDTH

C_HEAD_DIM = 64
LNX_EPS = 64e-5
KK_EPS = 1e-12

N_GROUPS = 4
EXPERTS_PER_GROUP = 8
N_EXPERTS = N_GROUPS * EXPERTS_PER_GROUP
MOE_TOPK = 2
MOE_ROWS = 256

LANES = 128
SUBLANES = 8
VMEM_LIMIT = 56 * 1024 * 1024

_NT = (((1,), (1,)), ((), ()))
_TN = (((0,), (0,)), ((), ()))


def _params(*sem):
    return pltpu.CompilerParams(dimension_semantics=sem, vmem_limit_bytes=VMEM_LIMIT)


def _mm_kernel(x_ref, w_ref, o_ref):
    o_ref[...] = jnp.dot(x_ref[...].astype(BF16), w_ref[...],
                         preferred_element_type=F32).astype(o_ref.dtype)


def _matmul(x, w, out_dtype=F32, tm=512, tn=512):
    m, k = x.shape
    n = w.shape[1]
    tm, tn = min(tm, m), min(tn, n)
    return pl.pallas_call(
        _mm_kernel,
        grid=(m // tm, n // tn),
        in_specs=[pl.BlockSpec((tm, k), lambda i, j: (i, 0)),
                  pl.BlockSpec((k, tn), lambda i, j: (0, j))],
        out_specs=pl.BlockSpec((tm, tn), lambda i, j: (i, j)),
        out_shape=jax.ShapeDtypeStruct((m, n), out_dtype),
        compiler_params=_params("parallel", "arbitrary"),
        name="matmul",
    )(x, w)


def _mm2_kernel(xa_ref, xb_ref, wa_ref, wb_ref, o_ref):
    acc = jnp.dot(xa_ref[...], wa_ref[...], preferred_element_type=F32)
    acc = acc + jnp.dot(xb_ref[...], wb_ref[...], preferred_element_type=F32)
    o_ref[...] = acc


def _matmul2(xa, xb, wa, wb, tm=512, tn=512):
    m, ka = xa.shape
    kb = xb.shape[1]
    n = wa.shape[1]
    tm, tn = min(tm, m), min(tn, n)
    return pl.pallas_call(
        _mm2_kernel,
        grid=(m // tm, n // tn),
        in_specs=[pl.BlockSpec((tm, ka), lambda i, j: (i, 0)),
                  pl.BlockSpec((tm, kb), lambda i, j: (i, 0)),
                  pl.BlockSpec((ka, tn), lambda i, j: (0, j)),
                  pl.BlockSpec((kb, tn), lambda i, j: (0, j))],
        out_specs=pl.BlockSpec((tm, tn), lambda i, j: (i, j)),
        out_shape=jax.ShapeDtypeStruct((m, n), F32),
        compiler_params=_params("parallel", "arbitrary"),
        name="matmul2",
    )(xa, xb, wa, wb)


def _ln_rows(h, g, b):
    mu = jnp.mean(h, axis=-1, keepdims=True)
    hc = h - mu
    var = jnp.mean(hc * hc, axis=-1, keepdims=True)
    return hc * lax.rsqrt(var + LN_EPS) * g + b


def _resid_ln_kernel(x_ref, m_ref, g_ref, b_ref, o_ref):
    o_ref[...] = _ln_rows(DN_ALPHA * x_ref[...] + m_ref[...], g_ref[...], b_ref[...])


def _resid_ln(x, mix, g, b, tm=256):
    t, d = x.shape
    row = pl.BlockSpec((tm, d), lambda i: (i, 0))
    vec = pl.BlockSpec((1, d), lambda i: (0, 0))
    return pl.pallas_call(
        _resid_ln_kernel,
        grid=(t // tm,),
        in_specs=[row, row, vec, vec],
        out_specs=row,
        out_shape=jax.ShapeDtypeStruct((t, d), F32),
        compiler_params=_params("parallel"),
        name="resid_ln",
    )(x, mix, g.reshape(1, d), b.reshape(1, d))


def _rope_tables(s):
    half = ROPE_DIM // 2
    inv = jnp.power(jnp.float32(ROPE_THETA), -jnp.arange(half, dtype=F32) / half)
    ang = jnp.arange(s).astype(F32)[:, None] * inv[None, :]
    cos, sin = jnp.cos(ang), jnp.sin(ang)
    zero = jnp.zeros((s, A_HEAD_DIM - ROPE_DIM), F32)
    zh = jnp.zeros((s, half), F32)
    c = jnp.concatenate([cos, cos, zero + 1.0], axis=-1)
    s_up = jnp.concatenate([zh, sin, zero], axis=-1)
    s_dn = jnp.concatenate([-sin, zh, zero], axis=-1)
    return c, s_up, s_dn


def _rope(x, c, s_up, s_dn):
    half = ROPE_DIM // 2
    return (x * c + pltpu.roll(x, half, axis=1) * s_up
            + pltpu.roll(x, A_HEAD_DIM - half, axis=1) * s_dn)


def _moba_kernel(q_ref, k_ref, v_ref, c_ref, su_ref, sd_ref, o_ref, krot_ref, vb_ref, kmean_ref, *, nblk):
    qi = pl.program_id(2)
    blk = MOBA_BLOCK
    scale = A_HEAD_DIM ** -0.5

    @pl.when(qi == 0)
    def _prepare_keys():
        def body(j, carry):
            r = pl.multiple_of(j * blk, blk)
            rows = pl.ds(r, blk)
            kr = _rope(k_ref[rows, :], c_ref[rows, :], su_ref[rows, :], sd_ref[rows, :])
            krot_ref[rows, :] = kr.astype(BF16)
            vb_ref[rows, :] = v_ref[rows, :].astype(BF16)
            kmean_ref[pl.ds(j, 1), :] = jnp.mean(kr, axis=0, keepdims=True)
            return carry
        lax.fori_loop(0, nblk, body, 0)

    r0 = pl.multiple_of(qi * blk, blk)
    qrows = pl.ds(r0, blk)
    qb = _rope(q_ref[...], c_ref[qrows, :], su_ref[qrows, :], sd_ref[qrows, :]).astype(BF16)

    g = lax.dot_general(kmean_ref[...].astype(BF16), qb, _NT, preferred_element_type=F32)
    brow = lax.broadcasted_iota(jnp.int32, (nblk, blk), 0)
    g = jnp.where(brow < qi, g, -jnp.inf)
    sel_t = jnp.zeros((nblk, blk), F32)
    for j in range(nblk):
        gj = g[j:j + 1, :]
        earlier = jnp.where(brow < j, 1.0, 0.0)
        beats = jnp.where(g > gj, 1.0, jnp.where(g == gj, earlier, 0.0))
        cnt = jnp.sum(beats, axis=0, keepdims=True)
        sel_t = jnp.where(brow == j, jnp.where(cnt < float(min(MOBA_TOPK, nblk)), 1.0, 0.0), sel_t)
    sel_t = jnp.where(brow < qi, sel_t, 0.0)
    eye = jnp.where(lax.broadcasted_iota(jnp.int32, (blk, blk), 0)
                    == lax.broadcasted_iota(jnp.int32, (blk, blk), 1), 1.0, 0.0).astype(BF16)
    sel = lax.dot_general(eye, sel_t.astype(BF16), _NT, preferred_element_type=F32)

    kd = krot_ref[qrows, :]
    s = lax.dot_general(qb, kd, _NT, preferred_element_type=F32) * scale
    causal = (lax.broadcasted_iota(jnp.int32, (blk, blk), 1)
              <= lax.broadcasted_iota(jnp.int32, (blk, blk), 0))
    s = jnp.where(causal, s, NEG_INF)
    m0 = jnp.max(s, axis=1, keepdims=True)
    p = jnp.exp(s - m0)
    l0 = jnp.sum(p, axis=1, keepdims=True)
    acc0 = jnp.dot(p.astype(BF16), vb_ref[qrows, :], preferred_element_type=F32)

    span = MOBA_GROUP * blk
    sel_b = sel.astype(BF16)
    erow = lax.broadcasted_iota(jnp.int32, (nblk, span), 0)
    eblk = jnp.right_shift(lax.broadcasted_iota(jnp.int32, (nblk, span), 1), blk.bit_length() - 1)

    def past_group(jg, carry):
        m, l, acc = carry
        rows = pl.ds(pl.multiple_of(jg * span, span), span)
        sj = lax.dot_general(qb, krot_ref[rows, :], _NT, preferred_element_type=F32) * scale
        expand = jnp.where(erow == jg * MOBA_GROUP + eblk, 1.0, 0.0).astype(BF16)
        picked = jnp.dot(sel_b, expand, preferred_element_type=F32)
        sj = jnp.where(picked > 0.5, sj, NEG_INF)
        m_new = jnp.maximum(m, jnp.max(sj, axis=1, keepdims=True))
        a = jnp.exp(m - m_new)
        pj = jnp.exp(sj - m_new)
        l = a * l + jnp.sum(pj, axis=1, keepdims=True)
        acc = a * acc + jnp.dot(pj.astype(BF16), vb_ref[rows, :], preferred_element_type=F32)
        return m_new, l, acc

    n_groups = (qi + MOBA_GROUP - 1) // MOBA_GROUP
    _, l, acc = lax.fori_loop(0, n_groups, past_group, (m0, l0, acc0))
    o_ref[...] = (acc / l).astype(o_ref.dtype)


def _moba(z, bsz, s):
    nblk = s // MOBA_BLOCK
    c, s_up, s_dn = _rope_tables(s)
    tab = pl.BlockSpec((s, A_HEAD_DIM), lambda b, h, qi: (0, 0))
    return pl.pallas_call(
        functools.partial(_moba_kernel, nblk=nblk),
        grid=(bsz, A_HEADS, nblk),
        in_specs=[pl.BlockSpec((MOBA_BLOCK, A_HEAD_DIM), lambda b, h, qi: (b * nblk + qi, h)),
                  pl.BlockSpec((s, A_HEAD_DIM), lambda b, h, qi: (b, A_HEADS + h)),
                  pl.BlockSpec((s, A_HEAD_DIM), lambda b, h, qi: (b, 2 * A_HEADS + h)),
                  tab, tab, tab],
        out_specs=pl.BlockSpec((MOBA_BLOCK, A_HEAD_DIM), lambda b, h, qi: (b * nblk + qi, h)),
        out_shape=jax.ShapeDtypeStruct((bsz * s, A_WIDTH), BF16),
        scratch_shapes=[pltpu.VMEM((s, A_HEAD_DIM), BF16), pltpu.VMEM((s, A_HEAD_DIM), BF16),
                        pltpu.VMEM((nblk, A_HEAD_DIM), F32)],
        compiler_params=_params("parallel", "parallel", "arbitrary"),
        name="moba",
    )(z, z, z, c, s_up, s_dn)


def _log_sigmoid(x):
    return jnp.minimum(x, 0.0) - jnp.log1p(jnp.exp(-jnp.abs(x)))


def _shift_rows(x, prev8, d):
    if d == 0:
        return x
    rolled = pltpu.roll(x, d, axis=0)
    top = jnp.where(lax.broadcasted_iota(jnp.int32, prev8.shape, 0) < d,
                    pltpu.roll(prev8, d, axis=0), rolled[:SUBLANES])
    return jnp.concatenate([top, rolled[SUBLANES:]], axis=0)


def _mlstm_kernel(q_ref, k_ref, v_ref, o_ref, gc_ref, gr_ref, bc_ref, br_ref,
                  cwq_ref, cwk_ref, cbq_ref, cbk_ref, hg_ref, y_ref,
                  c_ref, n_ref, m_ref, pq_ref, pk_ref, *, tt):
    h = pl.program_id(1)
    ti = pl.program_id(2)
    L = MLSTM_CHUNK
    dk = B_HEAD_DIM

    @pl.when(ti == 0)
    def _reset():
        c_ref[...] = jnp.zeros_like(c_ref)
        n_ref[...] = jnp.zeros_like(n_ref)
        m_ref[...] = jnp.zeros_like(m_ref)
        pq_ref[...] = jnp.zeros_like(pq_ref)
        pk_ref[...] = jnp.zeros_like(pk_ref)

    def conv_silu(x_ref, p_ref, w_ref, b_ref):
        x = x_ref[...]
        prev8 = p_ref[...]
        out = b_ref[...] + x * w_ref[B_CONV - 1:B_CONV, :]
        for d in range(1, B_CONV):
            out = out + _shift_rows(x, prev8, d) * w_ref[B_CONV - 1 - d:B_CONV - d, :]
        p_ref[...] = x[tt - SUBLANES:, :]
        return out * jax.nn.sigmoid(out)

    q_all = conv_silu(q_ref, pq_ref, cwq_ref, cbq_ref)
    k_all = conv_silu(k_ref, pk_ref, cwk_ref, cbk_ref) * (dk ** -0.5)

    lane8 = lax.broadcasted_iota(jnp.int32, (tt, 2 * B_HEADS), 1)
    gcol = gc_ref[...] + bc_ref[...]
    i_col_all = jnp.sum(jnp.where(lane8 == h, gcol, 0.0), axis=1, keepdims=True)
    f_col_all = _log_sigmoid(jnp.sum(jnp.where(lane8 == B_HEADS + h, gcol, 0.0), axis=1, keepdims=True))
    i_row_all = gr_ref[pl.ds(h, 1), :] + br_ref[pl.ds(h, 1), :]
    f_row_all = _log_sigmoid(gr_ref[pl.ds(B_HEADS + h, 1), :] + br_ref[pl.ds(B_HEADS + h, 1), :])

    rr = lax.broadcasted_iota(jnp.int32, (L, L), 0)
    cc = lax.broadcasted_iota(jnp.int32, (L, L), 1)
    tri = cc <= rr

    for c in range(tt // L):
        lo, hi = c * L, (c + 1) * L
        qc = q_all[lo:hi]
        kc = k_all[lo:hi]
        qcb = qc.astype(BF16)
        vcb = v_ref[lo:hi, :].astype(BF16)
        i_col, f_col = i_col_all[lo:hi], f_col_all[lo:hi]
        i_row, f_row = i_row_all[:, lo:hi], f_row_all[:, lo:hi]
        m_prev = m_ref[...]
        cst = c_ref[...]
        nst = n_ref[...]

        b_col = jnp.sum(jnp.where(tri, f_row, 0.0), axis=1, keepdims=True)
        b_row = jnp.sum(jnp.where(rr <= cc, f_col, 0.0), axis=0, keepdims=True)
        log_inter = b_col + m_prev
        log_intra = jnp.where(tri, b_col - b_row + i_row, -jnp.inf)
        m_t = jnp.maximum(log_inter, jnp.max(log_intra, axis=1, keepdims=True))
        w_intra = jnp.exp(log_intra - m_t)
        w_inter = jnp.exp(log_inter - m_t)
        qk = lax.dot_general(qcb, kc.astype(BF16), _NT, preferred_element_type=F32) * w_intra
        num = (w_inter * jnp.dot(qcb, cst.astype(BF16), preferred_element_type=F32)
               + jnp.dot(qk.astype(BF16), vcb, preferred_element_type=F32))
        den = (w_inter * jnp.sum(qc * nst, axis=1, keepdims=True)
               + jnp.sum(qk, axis=1, keepdims=True))
        hc = num / jnp.maximum(jnp.abs(den), jnp.exp(-m_t))

        b_last = jnp.sum(f_col, axis=0, keepdims=True)
        log_s = b_last - b_col + i_col
        m_new = jnp.maximum(b_last + m_prev, jnp.max(log_s, axis=0, keepdims=True))
        decay = jnp.exp(b_last + m_prev - m_new)
        kw = kc * jnp.exp(log_s - m_new)
        c_ref[...] = decay * cst + lax.dot_general(kw.astype(BF16), vcb, _TN, preferred_element_type=F32)
        n_ref[...] = decay * nst + jnp.sum(kw, axis=0, keepdims=True)
        m_ref[...] = m_new

        hn = hc * lax.rsqrt(jnp.mean(hc * hc, axis=1, keepdims=True) + HNORM_EPS)
        y_ref[lo:hi, :] = (hn * hg_ref[...] * jax.nn.sigmoid(o_ref[lo:hi, :])).astype(y_ref.dtype)


def _mlstm(z, gates, gate_bias, conv_w, conv_b, hnorm_g, bsz, s, tt=256):
    nt = s // tt
    cb = 3 * A_WIDTH // B_HEAD_DIM
    gates_t = gates.T
    bias = gate_bias.reshape(1, 2 * B_HEADS)

    def col(off):
        return pl.BlockSpec((tt, B_HEAD_DIM), lambda b, h, t: (b * nt + t, cb + off + h))

    def par(rows, off):
        return pl.BlockSpec((rows, B_HEAD_DIM), lambda b, h, t: (0, off + h))

    return pl.pallas_call(
        functools.partial(_mlstm_kernel, tt=tt),
        grid=(bsz, B_HEADS, nt),
        in_specs=[col(0), col(B_HEADS), col(2 * B_HEADS), col(3 * B_HEADS),
                  pl.BlockSpec((tt, 2 * B_HEADS), lambda b, h, t: (b * nt + t, 0)),
                  pl.BlockSpec((2 * B_HEADS, tt), lambda b, h, t: (0, b * nt + t)),
                  pl.BlockSpec((1, 2 * B_HEADS), lambda b, h, t: (0, 0)),
                  pl.BlockSpec((2 * B_HEADS, 1), lambda b, h, t: (0, 0)),
                  par(B_CONV, 0), par(B_CONV, B_HEADS), par(1, 0), par(1, B_HEADS), par(1, 0)],
        out_specs=pl.BlockSpec((tt, B_HEAD_DIM), lambda b, h, t: (b * nt + t, h)),
        out_shape=jax.ShapeDtypeStruct((bsz * s, B_WIDTH), BF16),
        scratch_shapes=[pltpu.VMEM((B_HEAD_DIM, B_HEAD_DIM), F32), pltpu.VMEM((1, B_HEAD_DIM), F32),
                        pltpu.VMEM((1, 1), F32), pltpu.VMEM((SUBLANES, B_HEAD_DIM), F32),
                        pltpu.VMEM((SUBLANES, B_HEAD_DIM), F32)],
        compiler_params=_params("parallel", "parallel", "arbitrary"),
        name="mlstm",
    )(z, z, z, z, gates, gates_t, bias, bias.T, conv_w, conv_w,
      conv_b.reshape(1, -1), conv_b.reshape(1, -1), hnorm_g.reshape(1, -1))


def _even_mixer(x, w_in, gate_bias, conv_w, conv_b, hnorm_g, w_out, bsz, s):
    d = x.shape[1]
    w_main = w_in[:, :EVEN_MAIN].astype(BF16)
    w_gate = jnp.pad(w_in[:, EVEN_MAIN:], ((0, 0), (0, LANES - 2 * B_HEADS))).astype(BF16)
    z = _matmul(x, w_main)
    gates = _matmul(x, w_gate, tn=LANES)[:, :2 * B_HEADS]
    y_a = _moba(z, bsz, s)
    y_b = _mlstm(z, gates, gate_bias, conv_w, conv_b, hnorm_g, bsz, s)
    w_o = w_out.astype(BF16)
    return _matmul2(y_a, y_b, w_o[:A_WIDTH], w_o[A_WIDTH:])


def _shift_mix_kernel(x_ref, p_ref, mu_ref, *o_refs, tiles_per_seq):
    x = x_ref[...]
    first = (pl.program_id(0) % tiles_per_seq) == 0
    prev_row = jnp.where(first, 0.0, p_ref[SUBLANES - 1:SUBLANES, :])
    row = lax.broadcasted_iota(jnp.int32, x.shape, 0)
    dx = jnp.where(row == 0, prev_row, pltpu.roll(x, 1, axis=0)) - x
    for j, o_ref in enumerate(o_refs):
        o_ref[...] = (x + dx * mu_ref[j:j + 1, :]).astype(o_ref.dtype)


def _shift_mix(x, mu, s, tm=256):
    t, d = x.shape
    n = mu.shape[0]
    row = pl.BlockSpec((tm, d), lambda i: (i, 0))
    prev = pl.BlockSpec((SUBLANES, d), lambda i: (jnp.maximum(i * (tm // SUBLANES) - 1, 0), 0))
    return pl.pallas_call(
        functools.partial(_shift_mix_kernel, tiles_per_seq=s // tm),
        grid=(t // tm,),
        in_specs=[row, prev, pl.BlockSpec((n, d), lambda i: (0, 0))],
        out_specs=[row] * n,
        out_shape=[jax.ShapeDtypeStruct((t, d), BF16)] * n,
        compiler_params=_params("parallel"),
        name="shift_mix",
    )(x, x, mu)


def _softplus(x):
    return jnp.maximum(x, 0.0) + jnp.log1p(jnp.exp(-jnp.abs(x)))


def _lora_kernel(x_ref, w1_ref, w2_ref, b_ref, o_ref, *, mid, out):
    hmid = jnp.dot(x_ref[...], w1_ref[...], preferred_element_type=F32)
    if mid == "tanh":
        hmid = jnp.tanh(hmid)
    elif mid == "sigmoid":
        hmid = jax.nn.sigmoid(hmid)
    y = jnp.dot(hmid.astype(BF16), w2_ref[...], preferred_element_type=F32)
    if out == "logdecay":
        y = -jnp.exp(-_softplus(-(b_ref[...] + y)) - 0.5)
    elif out == "sigmoid":
        y = jax.nn.sigmoid(b_ref[...] + y)
    o_ref[...] = y


def _lora(x, w1, w2, bias, mid, out, tm=512):
    t, d = x.shape
    tm = min(tm, t)
    r = w1.shape[1]
    rp = -(-r // LANES) * LANES
    w1p = jnp.pad(w1, ((0, 0), (0, rp - r))).astype(BF16)
    w2p = jnp.pad(w2, ((0, rp - r), (0, 0))).astype(BF16)
    dout = w2.shape[1]
    if bias is None:
        bias = jnp.zeros((dout,), F32)
    return pl.pallas_call(
        functools.partial(_lora_kernel, mid=mid, out=out),
        grid=(t // tm,),
        in_specs=[pl.BlockSpec((tm, d), lambda i: (i, 0)),
                  pl.BlockSpec((d, rp), lambda i: (0, 0)),
                  pl.BlockSpec((rp, dout), lambda i: (0, 0)),
                  pl.BlockSpec((1, dout), lambda i: (0, 0))],
        out_specs=pl.BlockSpec((tm, dout), lambda i: (i, 0)),
        out_shape=jax.ShapeDtypeStruct((t, dout), F32),
        compiler_params=_params("parallel"),
        name="lora_" + out,
    )(x, w1p, w2p, bias.reshape(1, dout))


def _head_ones():
    i = jnp.arange(LANES) // C_HEAD_DIM
    return (i[:, None] == i[None, :]).astype(F32)


def _head_sum(x, ones):
    cols = [jnp.dot(x[:, c * LANES:(c + 1) * LANES], ones, precision=lax.Precision.HIGHEST,
                    preferred_element_type=F32) for c in range(x.shape[1] // LANES)]
    return jnp.concatenate(cols, axis=1)


def _rwkv_prep_kernel(*refs, vres):
    if vres:
        k_ref, a_ref, v_ref, vf_ref, vg_ref, kk_p, ka_p, ones_ref, kk_o, kh_o, bb_o, v_o = refs
    else:
        k_ref, a_ref, v_ref, kk_p, ka_p, ones_ref, kk_o, kh_o, bb_o, v_o = refs
    k = k_ref[...]
    a = a_ref[...]
    kk = k * kk_p[...]
    nrm = jnp.sqrt(_head_sum(kk * kk, ones_ref[...]))
    kk = kk / jnp.maximum(nrm, KK_EPS)
    kk_o[...] = kk
    bb_o[...] = kk * a
    kh_o[...] = k * (1.0 + (a - 1.0) * ka_p[...])
    v = v_ref[...]
    if vres:
        v = v + (vf_ref[...] - v) * vg_ref[...]
    v_o[...] = v


def _rwkv_prep(k, a, v, v_first, vgate, k_k, k_a, tm=256):
    t, d = k.shape
    row = pl.BlockSpec((tm, d), lambda i: (i, 0))
    vec = pl.BlockSpec((1, d), lambda i: (0, 0))
    vres = v_first is not None
    ins = [k, a, v] + ([v_first, vgate] if vres else []) + [k_k.reshape(1, d), k_a.reshape(1, d), _head_ones()]
    specs = [row] * (5 if vres else 3) + [vec, vec, pl.BlockSpec((LANES, LANES), lambda i: (0, 0))]
    return pl.pallas_call(
        functools.partial(_rwkv_prep_kernel, vres=vres),
        grid=(t // tm,),
        in_specs=specs,
        out_specs=[row] * 4,
        out_shape=[jax.ShapeDtypeStruct((t, d), F32)] * 4,
        compiler_params=_params("parallel"),
        name="rwkv_prep",
    )(*ins)


def _bdot(a, b):
    return jnp.dot(a.astype(BF16), b.astype(BF16), preferred_element_type=F32)


def _rwkv_chunk_pairs(r, ld, kh, v, kk, bb, g_cum, s0, c):
    pairs = range(len(r))
    lane = lax.broadcasted_iota(jnp.int32, (1, LANES), 1)
    m0 = jnp.where(lane < C_HEAD_DIM, 1.0, 0.0)
    m1 = 1.0 - m0
    n2 = 2 * c
    row = lax.broadcasted_iota(jnp.int32, (n2, n2), 0)
    col = lax.broadcasted_iota(jnp.int32, (n2, n2), 1)
    lower, lower_eq = col < row, col <= row

    def stack(x):
        return jnp.concatenate([x * m0, x * m1], axis=0)

    e_g = [jnp.exp(g_cum[p]) for p in pairs]
    e_ng = [jnp.exp(-g_cum[p]) for p in pairs]
    al2 = [stack(-(kk[p] * jnp.exp(g_cum[p] - ld[p]))) for p in pairs]
    be2 = [stack(bb[p] * e_ng[p]).astype(BF16) for p in pairs]
    kb2 = [stack(kh[p] * e_ng[p]).astype(BF16) for p in pairs]
    rb = [r[p] * e_g[p] for p in pairs]
    v2 = [stack(v[p]) for p in pairs]
    sc = [lax.dot_general(jnp.concatenate([al2[p], stack(rb[p])], axis=0).astype(BF16),
                          jnp.concatenate([be2[p], kb2[p]], axis=0), _NT,
                          preferred_element_type=F32) for p in pairs]
    l_ab = [jnp.where(lower, sc[p][:n2, :n2], 0.0) for p in pairs]
    l_ak = [jnp.where(lower, sc[p][:n2, n2:], 0.0) for p in pairs]
    l_r = [jnp.concatenate([jnp.where(lower_eq, sc[p][n2:, :n2], 0.0),
                            jnp.where(lower_eq, sc[p][n2:, n2:], 0.0)], axis=1).astype(BF16) for p in pairs]

    t_off = l_ab
    pw = [x.astype(BF16) for x in l_ab]
    for _ in range(max(c.bit_length() - 2, 0)):
        pw_f = [jnp.dot(pw[p], pw[p], preferred_element_type=F32) for p in pairs]
        pw = [x.astype(BF16) for x in pw_f]
        t_off = [t_off[p] + pw_f[p] + jnp.dot(t_off[p].astype(BF16), pw[p], preferred_element_type=F32)
                 for p in pairs]

    w0 = [jnp.concatenate([al2[p], _bdot(l_ak[p], v2[p])], axis=1) for p in pairs]
    x = [w0[p] + _bdot(t_off[p], w0[p]) for p in pairs]
    xb = [x[p].astype(BF16) for p in pairs]
    z = [jnp.dot(l_r[p],
                 jnp.concatenate([xb[p], jnp.concatenate([jnp.zeros_like(v2[p]), v2[p]], axis=1).astype(BF16)],
                                 axis=0), preferred_element_type=F32) for p in pairs]
    r_eff = [(rb[p] + z[p][:c, :LANES] + z[p][c:, :LANES]).astype(BF16) for p in pairs]
    y0 = [z[p][:c, LANES:] + z[p][c:, LANES:] for p in pairs]
    mn = [lax.dot_general(xb[p], be2[p], _TN, preferred_element_type=F32) for p in pairs]
    n_add = [mn[p][LANES:] + lax.dot_general(v2[p].astype(BF16), kb2[p], _TN, preferred_element_type=F32)
             for p in pairs]
    s0b = [s0[p].astype(BF16) for p in pairs]
    y = [y0[p] + lax.dot_general(r_eff[p], s0b[p], _NT, preferred_element_type=F32) for p in pairs]
    s_new = [(s0[p] + jnp.dot(s0b[p], mn[p][:LANES].astype(BF16), preferred_element_type=F32) + n_add[p])
             * e_g[p][c - 1:c, :] for p in pairs]
    return y, s_new


def _rwkv_scan_kernel(r_ref, ld_ref, kh_ref, v_ref, kk_ref, bb_ref, tri_ref, y_ref, s_ref, *, c, npair):
    @pl.when(pl.program_id(2) == 0)
    def _reset():
        s_ref[...] = jnp.zeros_like(s_ref)

    g_all = jnp.dot(tri_ref[...], ld_ref[...], precision=lax.Precision.HIGHEST, preferred_element_type=F32)
    sl = [slice(p * LANES, (p + 1) * LANES) for p in range(npair)]
    take = lambda ref: [ref[:, q] for q in sl]
    y, s_new = _rwkv_chunk_pairs(take(r_ref), take(ld_ref), take(kh_ref), take(v_ref), take(kk_ref),
                                 take(bb_ref), [g_all[:, q] for q in sl], [s_ref[p] for p in range(npair)], c)
    for p in range(npair):
        y_ref[:, sl[p]] = y[p]
        s_ref[p] = s_new[p]


RWKV_CHUNK = 64
RWKV_PAIRS = 8


def _rwkv_scan(r, log_w, kh, v, kk, bb, bsz, s):
    t, d = r.shape
    c = min(RWKV_CHUNK, s)
    wid = RWKV_PAIRS * LANES
    nc = s // c
    tri = (jnp.arange(c)[None, :] <= jnp.arange(c)[:, None]).astype(F32)
    blk = pl.BlockSpec((c, wid), lambda b, g, i: (b * nc + i, g))
    return pl.pallas_call(
        functools.partial(_rwkv_scan_kernel, c=c, npair=RWKV_PAIRS),
        grid=(bsz, d // wid, nc),
        in_specs=[blk] * 6 + [pl.BlockSpec((c, c), lambda b, g, i: (0, 0))],
        out_specs=blk,
        out_shape=jax.ShapeDtypeStruct((t, d), F32),
        scratch_shapes=[pltpu.VMEM((RWKV_PAIRS, LANES, LANES), F32)],
        compiler_params=_params("parallel", "parallel", "arbitrary"),
        name="rwkv_scan",
    )(r, log_w, kh, v, kk, bb, tri)


def _rwkv_post_kernel(y_ref, r_ref, kh_ref, v_ref, g_ref, lg_ref, lb_ref, rk_ref, ones_ref, o_ref):
    ones = ones_ref[...]
    y = y_ref[...]
    inv_n = 1.0 / C_HEAD_DIM
    yc = y - _head_sum(y, ones) * inv_n
    yv = _head_sum(yc * yc, ones) * inv_n
    yn = yc * lax.rsqrt(yv + LNX_EPS) * lg_ref[...] + lb_ref[...]
    bonus = _head_sum(r_ref[...] * kh_ref[...] * rk_ref[...], ones) * v_ref[...]
    o_ref[...] = ((yn + bonus) * g_ref[...]).astype(o_ref.dtype)


def _rwkv_post(y, r, kh, v, g, lnx_g, lnx_b, r_k, tm=256):
    t, d = y.shape
    row = pl.BlockSpec((tm, d), lambda i: (i, 0))
    vec = pl.BlockSpec((1, d), lambda i: (0, 0))
    return pl.pallas_call(
        _rwkv_post_kernel,
        grid=(t // tm,),
        in_specs=[row] * 5 + [vec] * 3 + [pl.BlockSpec((LANES, LANES), lambda i: (0, 0))],
        out_specs=row,
        out_shape=jax.ShapeDtypeStruct((t, d), BF16),
        compiler_params=_params("parallel"),
        name="rwkv_post",
    )(y, r, kh, v, g, lnx_g.reshape(1, d), lnx_b.reshape(1, d), r_k.reshape(1, d), _head_ones())


def _rwkv7_mixer(x, v_first, mu, w_r, w_k, w_v, w_o, w0, w1, w2, a0, a1, a2, g1, g2,
                 k_k, k_a, r_k, lnx_g, lnx_b, vres, bsz, s):
    xr, xw, xk, xv, xa, xg = _shift_mix(x, mu, s)
    r = _matmul(xr, w_r.astype(BF16))
    k = _matmul(xk, w_k.astype(BF16))
    v = _matmul(xv, w_v.astype(BF16))
    log_decay = _lora(xw, w1, w2, w0, "tanh", "logdecay")
    a = _lora(xa, a1, a2, a0, "none", "sigmoid")
    g = _lora(xg, g1, g2, None, "sigmoid", "none")
    if vres is None:
        vgate = None
        v_keep = v
    else:
        v0, v1, v2 = vres
        vgate = _lora(xv, v1, v2, v0, "none", "sigmoid")
        v_keep = v_first
    kk, kh, bb, v = _rwkv_prep(k, a, v, None if vres is None else v_first, vgate, k_k, k_a)
    y = _rwkv_scan(r, log_decay, kh, v, kk, bb, bsz, s)
    yo = _rwkv_post(y, r, kh, v, g, lnx_g, lnx_b, r_k)
    return _matmul(yo, w_o.astype(BF16)), v_keep


def _lane_pick(cond, lanef):
    return jnp.min(jnp.where(cond, lanef, float(LANES)), axis=1, keepdims=True)


def _router_kernel(x_ref, w_ref, b_ref, route_ref, cnt_ref, run_ref, *, tm):
    @pl.when(pl.program_id(0) == 0)
    def _reset():
        run_ref[...] = jnp.zeros_like(run_ref)

    logits = jnp.dot(x_ref[...].astype(BF16), w_ref[...], preferred_element_type=F32) + b_ref[...]
    lane = lax.broadcasted_iota(jnp.int32, (tm, LANES), 1)
    lanef = lane.astype(F32)

    def masked_softmax(keep):
        z = jnp.where(keep, logits, -jnp.inf)
        e = jnp.exp(z - jnp.max(z, axis=1, keepdims=True))
        return e / jnp.sum(e, axis=1, keepdims=True)

    in_groups = lane < N_GROUPS
    pg = masked_softmax(in_groups)
    g_p = jnp.max(pg, axis=1, keepdims=True)
    g_idx = _lane_pick(jnp.where(in_groups, pg, -1.0) == g_p, lanef)
    lo = float(N_GROUPS) + float(EXPERTS_PER_GROUP) * g_idx
    half = 0.5 * float(EXPERTS_PER_GROUP - 1)
    in_group = jnp.abs(lanef - lo - half) < half + 0.5
    pe = jnp.where(in_group, masked_softmax(in_group), -1.0)
    p1 = jnp.max(pe, axis=1, keepdims=True)
    i1 = _lane_pick(pe == p1, lanef)
    pe2 = jnp.where(lanef == i1, -1.0, pe)
    p2 = jnp.max(pe2, axis=1, keepdims=True)
    i2 = _lane_pick(pe2 == p2, lanef)
    den = p1 + p2
    gate = (g_p * (p1 / den), g_p * (p2 / den))
    expert = (i1 - float(N_GROUPS), i2 - float(N_GROUPS))

    earlier = jnp.where(lax.broadcasted_iota(jnp.int32, (tm, tm), 1)
                        < lax.broadcasted_iota(jnp.int32, (tm, tm), 0), 1.0, 0.0).astype(BF16)
    run = run_ref[...]
    rank = []
    for e in expert:
        onehot = jnp.where(lanef == e, 1.0, 0.0)
        before = jnp.dot(earlier, onehot.astype(BF16), preferred_element_type=F32) + run
        rank.append(jnp.sum(onehot * before, axis=1, keepdims=True))
        run = run + jnp.sum(onehot, axis=0, keepdims=True)
    run_ref[...] = run
    cnt_ref[...] = run
    cols = (expert[0], expert[1], rank[0], rank[1], gate[0], gate[1])
    route = jnp.zeros((tm, LANES), F32)
    for c, val in enumerate(cols):
        route = jnp.where(lane == c, val, route)
    route_ref[...] = route


def _router(x, w_router, b_router, tm=256):
    t, d = x.shape
    tm = min(tm, t)
    return pl.pallas_call(
        functools.partial(_router_kernel, tm=tm),
        grid=(t // tm,),
        in_specs=[pl.BlockSpec((tm, d), lambda i: (i, 0)),
                  pl.BlockSpec((d, LANES), lambda i: (0, 0)),
                  pl.BlockSpec((1, LANES), lambda i: (0, 0))],
        out_specs=[pl.BlockSpec((tm, LANES), lambda i: (i, 0)),
                   pl.BlockSpec((1, LANES), lambda i: (0, 0))],
        out_shape=[jax.ShapeDtypeStruct((t, LANES), F32), jax.ShapeDtypeStruct((1, LANES), F32)],
        scratch_shapes=[pltpu.VMEM((1, LANES), F32)],
        compiler_params=_params("arbitrary"),
        name="router",
    )(x, w_router, b_router)


SLAB = 2048 // LANES


def _to_slabs(x):
    assert x.shape[1] == SLAB * LANES
    return x.reshape(x.shape[0] * SLAB, LANES)


def _slab(ref, r):
    return ref.at[pl.ds(pl.multiple_of(r * SLAB, SLAB), SLAB)]


def _slab_rows(ref, rows):
    return jnp.concatenate([ref[pl.ds(c, rows, stride=SLAB), :] for c in range(SLAB)], axis=1)


def _store_slab_rows(ref, val, rows):
    for c in range(SLAB):
        ref[pl.ds(c, rows, stride=SLAB), :] = val[:, c * LANES:(c + 1) * LANES]


def _gather_rows_kernel(idx_ref, src_ref, o_ref, sem, *, rows):
    def start(r, carry):
        pltpu.make_async_copy(_slab(src_ref, idx_ref[0, r]), _slab(o_ref, r), sem).start()
        return carry

    def wait(r, carry):
        pltpu.make_async_copy(_slab(src_ref, 0), _slab(o_ref, r), sem).wait()
        return carry

    lax.fori_loop(0, rows, start, 0)
    lax.fori_loop(0, rows, wait, 0)


def _gather_rows(src, idx, rows=MOE_ROWS):
    n = idx.shape[0]
    nb = n // rows
    return pl.pallas_call(
        functools.partial(_gather_rows_kernel, rows=rows),
        grid=(nb,),
        in_specs=[pl.BlockSpec((None, 1, rows), lambda i: (i, 0, 0), memory_space=pltpu.SMEM),
                  pl.BlockSpec(memory_space=pl.ANY)],
        out_specs=pl.BlockSpec((rows * SLAB, LANES), lambda i: (i, 0)),
        out_shape=jax.ShapeDtypeStruct((n * SLAB, LANES), src.dtype),
        scratch_shapes=[pltpu.SemaphoreType.DMA(())],
        compiler_params=_params("arbitrary"),
        name="gather_rows",
    )(idx.reshape(nb, 1, rows), src)


def _expert_kernel(be_ref, nu_ref, x_ref, wg_ref, wu_ref, wd_ref, o_ref, wg_b, wu_b, wd_b, *, rows):
    i = pl.program_id(0)
    used = i < nu_ref[0]

    @pl.when(jnp.logical_or(i == 0, be_ref[i] != be_ref[jnp.maximum(i - 1, 0)]))
    def _():
        wg_b[...] = wg_ref[...].astype(BF16)
        wu_b[...] = wu_ref[...].astype(BF16)
        wd_b[...] = wd_ref[...].astype(BF16)

    @pl.when(used)
    def _():
        xb = _slab_rows(x_ref, rows).astype(BF16)
        hg = jnp.dot(xb, wg_b[...], preferred_element_type=F32)
        hu = jnp.dot(xb, wu_b[...], preferred_element_type=F32)
        hid = (hg * jax.nn.sigmoid(hg) * hu).astype(BF16)
        _store_slab_rows(o_ref, jnp.dot(hid, wd_b[...], preferred_element_type=F32), rows)

    @pl.when(jnp.logical_not(used))
    def _():
        o_ref[...] = jnp.zeros_like(o_ref)


def _expert_mlp(xs, block_e, n_used, w_gate, w_up, w_down, layer, rows=MOE_ROWS):
    d, ff = w_gate.shape[2], w_gate.shape[3]
    nb = xs.shape[0] // (rows * SLAB)
    grid_spec = pltpu.PrefetchScalarGridSpec(
        num_scalar_prefetch=2,
        grid=(nb,),
        in_specs=[pl.BlockSpec((rows * SLAB, LANES), lambda i, be, nu: (jnp.minimum(i, nu[0] - 1), 0)),
                  pl.BlockSpec((None, None, d, ff), lambda i, be, nu: (layer, be[i], 0, 0)),
                  pl.BlockSpec((None, None, d, ff), lambda i, be, nu: (layer, be[i], 0, 0)),
                  pl.BlockSpec((None, None, ff, d), lambda i, be, nu: (layer, be[i], 0, 0))],
        out_specs=pl.BlockSpec((rows * SLAB, LANES), lambda i, be, nu: (i, 0)),
        scratch_shapes=[pltpu.VMEM((d, ff), BF16), pltpu.VMEM((d, ff), BF16), pltpu.VMEM((ff, d), BF16)],
    )
    return pl.pallas_call(
        functools.partial(_expert_kernel, rows=rows),
        grid_spec=grid_spec,
        out_shape=jax.ShapeDtypeStruct(xs.shape, F32),
        compiler_params=_params("arbitrary"),
        name="expert_mlp",
    )(block_e, n_used, xs, w_gate, w_up, w_down)


def _combine_ln_kernel(idx_ref, y_ref, x_ref, gt_ref, g_ref, b_ref, o_ref, buf, sem, *, rows):
    def start(r, carry):
        for kslot in range(MOE_TOPK):
            pltpu.make_async_copy(_slab(y_ref, idx_ref[0, MOE_TOPK * r + kslot]),
                                  _slab(buf.at[kslot], r), sem).start()
        return carry

    def wait(r, carry):
        for kslot in range(MOE_TOPK):
            pltpu.make_async_copy(_slab(y_ref, 0), _slab(buf.at[kslot], r), sem).wait()
        return carry

    lax.fori_loop(0, rows, start, 0)
    lax.fori_loop(0, rows, wait, 0)
    gt = gt_ref[...]
    ffn = gt[:, 0:1] * _slab_rows(buf.at[0], rows)
    for kslot in range(1, MOE_TOPK):
        ffn = ffn + gt[:, kslot:kslot + 1] * _slab_rows(buf.at[kslot], rows)
    o_ref[...] = _ln_rows(DN_ALPHA * x_ref[...] + ffn, g_ref[...], b_ref[...])


def _combine_ln(y_sorted, dest, gates, x, g, b, rows=256):
    t, d = x.shape
    nb = t // rows
    row = pl.BlockSpec((rows, d), lambda i: (i, 0))
    vec = pl.BlockSpec((1, d), lambda i: (0, 0))
    return pl.pallas_call(
        functools.partial(_combine_ln_kernel, rows=rows),
        grid=(nb,),
        in_specs=[pl.BlockSpec((None, 1, rows * MOE_TOPK), lambda i: (i, 0, 0), memory_space=pltpu.SMEM),
                  pl.BlockSpec(memory_space=pl.ANY), row,
                  pl.BlockSpec((rows, MOE_TOPK), lambda i: (i, 0)), vec, vec],
        out_specs=row,
        out_shape=jax.ShapeDtypeStruct((t, d), F32),
        scratch_shapes=[pltpu.VMEM((MOE_TOPK, rows * SLAB, LANES), F32), pltpu.SemaphoreType.DMA(())],
        compiler_params=_params("arbitrary"),
        name="combine_ln",
    )(dest.reshape(nb, 1, rows * MOE_TOPK), y_sorted, x, gates, g.reshape(1, d), b.reshape(1, d))


def _moe_layer(x, w_grp, b_grp, w_exp_r, b_exp_r, w_gate, w_up, w_down, ln_g, ln_b, layer):
    t, d = x.shape
    n_r = N_GROUPS + N_EXPERTS
    w_router = jnp.pad(jnp.concatenate([w_grp, w_exp_r], axis=1), ((0, 0), (0, LANES - n_r))).astype(BF16)
    b_router = jnp.pad(jnp.concatenate([b_grp, b_exp_r]), (0, LANES - n_r)).reshape(1, LANES)
    route, cnt = _router(x, w_router, b_router)
    flat_e = route[:, 0:MOE_TOPK].astype(jnp.int32).reshape(-1)
    rank = route[:, MOE_TOPK:2 * MOE_TOPK].astype(jnp.int32).reshape(-1)
    gates = route[:, 2 * MOE_TOPK:3 * MOE_TOPK]
    counts = cnt[0, :N_EXPERTS].astype(jnp.int32)
    m = t * MOE_TOPK
    padded = (counts + MOE_ROWS - 1) // MOE_ROWS * MOE_ROWS
    pad_end = jnp.cumsum(padded)
    dest = (pad_end - padded)[flat_e] + rank
    n_blocks = (m + N_EXPERTS * (MOE_ROWS - 1) + MOE_ROWS - 1) // MOE_ROWS
    src_row = jnp.zeros((n_blocks * MOE_ROWS,), jnp.int32).at[dest].set(
        (jnp.arange(m) // MOE_TOPK).astype(jnp.int32))
    starts = jnp.arange(n_blocks, dtype=jnp.int32) * MOE_ROWS
    block_e = jnp.minimum(jnp.sum((pad_end[None, :] <= starts[:, None]).astype(jnp.int32), axis=1),
                          N_EXPERTS - 1).astype(jnp.int32)
    n_used = (pad_end[-1] // MOE_ROWS).astype(jnp.int32).reshape(1)

    xs = _gather_rows(_to_slabs(x), src_row)
    ys = _expert_mlp(xs, block_e, n_used, w_gate, w_up, w_down, layer)
    return _combine_ln(ys, dest.astype(jnp.int32), gates, x, ln_g, ln_b)


def kernel(x, ev_w_in, ev_gate_bias, ev_conv_w, ev_conv_b, ev_hnorm_g, ev_w_out, od_mu, od_w_r, od_w_k, od_w_v, od_w_o, od_w0, od_w1, od_w2, od_a0, od_a1, od_a2, od_g1, od_g2, od_k_k, od_k_a, od_r_k, od_lnx_g, od_lnx_b, od_v0, od_v1, od_v2, ln_mix_g, ln_mix_b, ln_ffn_g, ln_ffn_b, moe_w_grp, moe_b_grp, moe_w_exp_r, moe_b_exp_r, moe_w_gate, moe_w_up, moe_w_down):
    bsz, s, d = x.shape
    depth = ln_mix_g.shape[0]
    xf = x.reshape(bsz * s, d)
    v_first = None
    for layer in range(depth):
        if layer % 2 == 0:
            e = layer // 2
            mix = _even_mixer(xf, ev_w_in[e], ev_gate_bias[e], ev_conv_w[e], ev_conv_b[e],
                              ev_hnorm_g[e], ev_w_out[e], bsz, s)
        else:
            o = layer // 2
            vres = None if o == 0 else (od_v0[o - 1], od_v1[o - 1], od_v2[o - 1])
            mix, v_first = _rwkv7_mixer(xf, v_first, od_mu[o], od_w_r[o], od_w_k[o], od_w_v[o],
                                        od_w_o[o], od_w0[o], od_w1[o], od_w2[o], od_a0[o],
                                        od_a1[o], od_a2[o], od_g1[o], od_g2[o], od_k_k[o],
                                        od_k_a[o], od_r_k[o], od_lnx_g[o], od_lnx_b[o], vres, bsz, s)
        xf = _resid_ln(xf, mix, ln_mix_g[layer], ln_mix_b[layer])
        xf = _moe_layer(xf, moe_w_grp[layer], moe_b_grp[layer], moe_w_exp_r[layer],
                        moe_b_exp_r[layer], moe_w_gate, moe_w_up, moe_w_down,
                        ln_ffn_g[layer], ln_ffn_b[layer], layer)
    return xf.reshape(bsz, s, d)
```

```python
import functools
import math

import jax
import jax.numpy as jnp
from jax import lax
from jax.experimental import pallas as pl
from jax.experimental.pallas import tpu as pltpu

F32 = jnp.float32
BF16 = jnp.bfloat16

DEPTH = 4
DN_ALPHA = (2 * DEPTH) ** 0.25
LN_EPS = 1e-5

A_HEADS = 8
A_HEAD_DIM = 128
A_WIDTH = A_HEADS * A_HEAD_DIM
MOBA_BLOCK = 256
MOBA_TOPK = 3
MOBA_GROUP = 4
ROPE_THETA = 500000.0
ROPE_DIM = A_HEAD_DIM // 4
NEG_INF = -1e30

B_HEADS = 4
B_HEAD_DIM = 256
B_WIDTH = B_HEADS * B_HEAD_DIM
MLSTM_CHUNK = 64
B_CONV = 4
HNORM_EPS = 1e-6
EVEN_MAIN = 3 * A_WIDTH + 4 * B_WIDTH

C_HEAD_DIM = 64
LNX_EPS = 64e-5
KK_EPS = 1e-12

N_GROUPS = 4
EXPERTS_PER_GROUP = 8
N_EXPERTS = N_GROUPS * EXPERTS_PER_GROUP
MOE_TOPK = 2
MOE_ROWS = 256

LANES = 128
SUBLANES = 8
VMEM_LIMIT = 56 * 1024 * 1024

_NT = (((1,), (1,)), ((), ()))
_TN = (((0,), (0,)), ((), ()))


def _params(*sem):
    return pltpu.CompilerParams(dimension_semantics=sem, vmem_limit_bytes=VMEM_LIMIT)


def _mm_kernel(x_ref, w_ref, o_ref):
    o_ref[...] = jnp.dot(x_ref[...].astype(BF16), w_ref[...],
                         preferred_element_type=F32).astype(o_ref.dtype)


def _matmul(x, w, out_dtype=F32, tm=1024, tn=512):
    m, k = x.shape
    n = w.shape[1]
    tm, tn = min(tm, m), min(tn, n)
    return pl.pallas_call(
        _mm_kernel,
        grid=(m // tm, n // tn),
        in_specs=[pl.BlockSpec((tm, k), lambda i, j: (i, 0)),
                  pl.BlockSpec((k, tn), lambda i, j: (0, j))],
        out_specs=pl.BlockSpec((tm, tn), lambda i, j: (i, j)),
        out_shape=jax.ShapeDtypeStruct((m, n), out_dtype),
        compiler_params=_params("parallel", "arbitrary"),
        name="matmul",
    )(x, w)


def _mm2_kernel(xa_ref, xb_ref, wa_ref, wb_ref, o_ref):
    acc = jnp.dot(xa_ref[...], wa_ref[...], preferred_element_type=F32)
    acc = acc + jnp.dot(xb_ref[...], wb_ref[...], preferred_element_type=F32)
    o_ref[...] = acc


def _matmul2(xa, xb, wa, wb, tm=1024, tn=512):
    m, ka = xa.shape
    kb = xb.shape[1]
    n = wa.shape[1]
    tm, tn = min(tm, m), min(tn, n)
    return pl.pallas_call(
        _mm2_kernel,
        grid=(m // tm, n // tn),
        in_specs=[pl.BlockSpec((tm, ka), lambda i, j: (i, 0)),
                  pl.BlockSpec((tm, kb), lambda i, j: (i, 0)),
                  pl.BlockSpec((ka, tn), lambda i, j: (0, j)),
                  pl.BlockSpec((kb, tn), lambda i, j: (0, j))],
        out_specs=pl.BlockSpec((tm, tn), lambda i, j: (i, j)),
        out_shape=jax.ShapeDtypeStruct((m, n), F32),
        compiler_params=_params("parallel", "arbitrary"),
        name="matmul2",
    )(xa, xb, wa, wb)


def _ln_rows(h, g, b):
    mu = jnp.mean(h, axis=-1, keepdims=True)
    hc = h - mu
    var = jnp.mean(hc * hc, axis=-1, keepdims=True)
    return hc * lax.rsqrt(var + LN_EPS) * g + b


def _resid_ln_kernel(x_ref, m_ref, g_ref, b_ref, o_ref):
    o_ref[...] = _ln_rows(DN_ALPHA * x_ref[...] + m_ref[...], g_ref[...], b_ref[...])


def _resid_ln(x, mix, g, b, tm=256):
    t, d = x.shape
    row = pl.BlockSpec((tm, d), lambda i: (i, 0))
    vec = pl.BlockSpec((1, d), lambda i: (0, 0))
    return pl.pallas_call(
        _resid_ln_kernel,
        grid=(t // tm,),
        in_specs=[row, row, vec, vec],
        out_specs=row,
        out_shape=jax.ShapeDtypeStruct((t, d), F32),
        compiler_params=_params("parallel"),
        name="resid_ln",
    )(x, mix, g.reshape(1, d), b.reshape(1, d))


def _rope_tables(s):
    half = ROPE_DIM // 2
    inv = jnp.power(jnp.float32(ROPE_THETA), -jnp.arange(half, dtype=F32) / half)
    ang = jnp.arange(s).astype(F32)[:, None] * inv[None, :]
    cos, sin = jnp.cos(ang), jnp.sin(ang)
    zero = jnp.zeros((s, A_HEAD_DIM - ROPE_DIM), F32)
    zh = jnp.zeros((s, half), F32)
    c = jnp.concatenate([cos, cos, zero + 1.0], axis=-1)
    s_up = jnp.concatenate([zh, sin, zero], axis=-1)
    s_dn = jnp.concatenate([-sin, zh, zero], axis=-1)
    return c, s_up, s_dn


def _rope(x, c, s_up, s_dn):
    half = ROPE_DIM // 2
    return (x * c + pltpu.roll(x, half, axis=1) * s_up
            + pltpu.roll(x, A_HEAD_DIM - half, axis=1) * s_dn)


def _moba_kernel(q_ref, k_ref, v_ref, c_ref, su_ref, sd_ref, o_ref, krot_ref, vb_ref, kmean_ref, *, nblk):
    qi = pl.program_id(2)
    blk = MOBA_BLOCK
    scale = A_HEAD_DIM ** -0.5

    @pl.when(qi == 0)
    def _prepare_keys():
        def body(j, carry):
            r = pl.multiple_of(j * blk, blk)
            rows = pl.ds(r, blk)
            kr = _rope(k_ref[rows, :].astype(F32), c_ref[rows, :], su_ref[rows, :], sd_ref[rows, :])
            krot_ref[rows, :] = kr.astype(BF16)
            vb_ref[rows, :] = v_ref[rows, :].astype(BF16)
            kmean_ref[pl.ds(j, 1), :] = jnp.mean(kr, axis=0, keepdims=True)
            return carry
        lax.fori_loop(0, nblk, body, 0)

    r0 = pl.multiple_of(qi * blk, blk)
    qrows = pl.ds(r0, blk)
    qb = _rope(q_ref[...].astype(F32), c_ref[qrows, :], su_ref[qrows, :], sd_ref[qrows, :]).astype(BF16)

    g = lax.dot_general(kmean_ref[...].astype(BF16), qb, _NT, preferred_element_type=F32)
    brow = lax.broadcasted_iota(jnp.int32, (nblk, blk), 0)
    g = jnp.where(brow < qi, g, -jnp.inf)
    sel_t = jnp.zeros((nblk, blk), F32)
    for j in range(nblk):
        gj = g[j:j + 1, :]
        earlier = jnp.where(brow < j, 1.0, 0.0)
        beats = jnp.where(g > gj, 1.0, jnp.where(g == gj, earlier, 0.0))
        cnt = jnp.sum(beats, axis=0, keepdims=True)
        sel_t = jnp.where(brow == j, jnp.where(cnt < float(min(MOBA_TOPK, nblk)), 1.0, 0.0), sel_t)
    sel_t = jnp.where(brow < qi, sel_t, 0.0)
    eye = jnp.where(lax.broadcasted_iota(jnp.int32, (blk, blk), 0)
                    == lax.broadcasted_iota(jnp.int32, (blk, blk), 1), 1.0, 0.0).astype(BF16)
    sel = lax.dot_general(eye, sel_t.astype(BF16), _NT, preferred_element_type=F32)

    kd = krot_ref[qrows, :]
    s = lax.dot_general(qb, kd, _NT, preferred_element_type=F32) * scale
    causal = (lax.broadcasted_iota(jnp.int32, (blk, blk), 1)
              <= lax.broadcasted_iota(jnp.int32, (blk, blk), 0))
    s = jnp.where(causal, s, NEG_INF)
    m0 = jnp.max(s, axis=1, keepdims=True)
    p = jnp.exp(s - m0)
    l0 = jnp.sum(p, axis=1, keepdims=True)
    acc0 = jnp.dot(p.astype(BF16), vb_ref[qrows, :], preferred_element_type=F32)

    span = MOBA_GROUP * blk
    sel_b = sel.astype(BF16)
    erow = lax.broadcasted_iota(jnp.int32, (nblk, span), 0)
    eblk = jnp.right_shift(lax.broadcasted_iota(jnp.int32, (nblk, span), 1), blk.bit_length() - 1)

    def past_group(jg, carry):
        m, l, acc = carry
        rows = pl.ds(pl.multiple_of(jg * span, span), span)
        sj = lax.dot_general(qb, krot_ref[rows, :], _NT, preferred_element_type=F32) * scale
        expand = jnp.where(erow == jg * MOBA_GROUP + eblk, 1.0, 0.0).astype(BF16)
        picked = jnp.dot(sel_b, expand, preferred_element_type=F32)
        sj = jnp.where(picked > 0.5, sj, NEG_INF)
        m_new = jnp.maximum(m, jnp.max(sj, axis=1, keepdims=True))
        a = jnp.exp(m - m_new)
        pj = jnp.exp(sj - m_new)
        l = a * l + jnp.sum(pj, axis=1, keepdims=True)
        acc = a * acc + jnp.dot(pj.astype(BF16), vb_ref[rows, :], preferred_element_type=F32)
        return m_new, l, acc

    n_groups = (qi + MOBA_GROUP - 1) // MOBA_GROUP
    _, l, acc = lax.fori_loop(0, n_groups, past_group, (m0, l0, acc0))
    o_ref[...] = (acc / l).astype(o_ref.dtype)


def _moba(z, bsz, s):
    nblk = s // MOBA_BLOCK
    c, s_up, s_dn = _rope_tables(s)
    tab = pl.BlockSpec((s, A_HEAD_DIM), lambda b, h, qi: (0, 0))
    return pl.pallas_call(
        functools.partial(_moba_kernel, nblk=nblk),
        grid=(bsz, A_HEADS, nblk),
        in_specs=[pl.BlockSpec((MOBA_BLOCK, A_HEAD_DIM), lambda b, h, qi: (b * nblk + qi, h)),
                  pl.BlockSpec((s, A_HEAD_DIM), lambda b, h, qi: (b, A_HEADS + h)),
                  pl.BlockSpec((s, A_HEAD_DIM), lambda b, h, qi: (b, 2 * A_HEADS + h)),
                  tab, tab, tab],
        out_specs=pl.BlockSpec((MOBA_BLOCK, A_HEAD_DIM), lambda b, h, qi: (b * nblk + qi, h)),
        out_shape=jax.ShapeDtypeStruct((bsz * s, A_WIDTH), BF16),
        scratch_shapes=[pltpu.VMEM((s, A_HEAD_DIM), BF16), pltpu.VMEM((s, A_HEAD_DIM), BF16),
                        pltpu.VMEM((nblk, A_HEAD_DIM), F32)],
        compiler_params=_params("parallel", "parallel", "arbitrary"),
        name="moba",
    )(z, z, z, c, s_up, s_dn)


def _log_sigmoid(x):
    return jnp.minimum(x, 0.0) - jnp.log1p(jnp.exp(-jnp.abs(x)))


def _shift_rows(x, prev8, d):
    if d == 0:
        return x
    rolled = pltpu.roll(x, d, axis=0)
    top = jnp.where(lax.broadcasted_iota(jnp.int32, prev8.shape, 0) < d,
                    pltpu.roll(prev8, d, axis=0), rolled[:SUBLANES])
    return jnp.concatenate([top, rolled[SUBLANES:]], axis=0)


def _mlstm_kernel(q_ref, k_ref, v_ref, o_ref, gc_ref, gr_ref, bc_ref, br_ref,
                  cwq_ref, cwk_ref, cbq_ref, cbk_ref, hg_ref, y_ref,
                  c_ref, n_ref, m_ref, pq_ref, pk_ref, *, tt):
    h = pl.program_id(1)
    ti = pl.program_id(2)
    L = MLSTM_CHUNK
    dk = B_HEAD_DIM

    @pl.when(ti == 0)
    def _reset():
        c_ref[...] = jnp.zeros_like(c_ref)
        n_ref[...] = jnp.zeros_like(n_ref)
        m_ref[...] = jnp.zeros_like(m_ref)
        pq_ref[...] = jnp.zeros_like(pq_ref)
        pk_ref[...] = jnp.zeros_like(pk_ref)

    def conv_silu(x_ref, p_ref, w_ref, b_ref):
        x = x_ref[...].astype(F32)
        prev8 = p_ref[...]
        out = b_ref[...] + x * w_ref[B_CONV - 1:B_CONV, :]
        for d in range(1, B_CONV):
            out = out + _shift_rows(x, prev8, d) * w_ref[B_CONV - 1 - d:B_CONV - d, :]
        p_ref[...] = x[tt - SUBLANES:, :]
        return out * jax.nn.sigmoid(out)

    q_all = conv_silu(q_ref, pq_ref, cwq_ref, cbq_ref)
    k_all = conv_silu(k_ref, pk_ref, cwk_ref, cbk_ref) * (dk ** -0.5)

    lane8 = lax.broadcasted_iota(jnp.int32, (tt, 2 * B_HEADS), 1)
    gcol = gc_ref[...] + bc_ref[...]
    i_col_all = jnp.sum(jnp.where(lane8 == h, gcol, 0.0), axis=1, keepdims=True)
    f_col_all = _log_sigmoid(jnp.sum(jnp.where(lane8 == B_HEADS + h, gcol, 0.0), axis=1, keepdims=True))
    i_row_all = gr_ref[pl.ds(h, 1), :] + br_ref[pl.ds(h, 1), :]
    f_row_all = _log_sigmoid(gr_ref[pl.ds(B_HEADS + h, 1), :] + br_ref[pl.ds(B_HEADS + h, 1), :])

    rr = lax.broadcasted_iota(jnp.int32, (L, L), 0)
    cc = lax.broadcasted_iota(jnp.int32, (L, L), 1)
    tri = cc <= rr

    for c in range(tt // L):
        lo, hi = c * L, (c + 1) * L
        qc = q_all[lo:hi]
        kc = k_all[lo:hi]
        qcb = qc.astype(BF16)
        vcb = v_ref[lo:hi, :].astype(BF16)
        i_col, f_col = i_col_all[lo:hi], f_col_all[lo:hi]
        i_row, f_row = i_row_all[:, lo:hi], f_row_all[:, lo:hi]
        m_prev = m_ref[...]
        cst = c_ref[...]
        nst = n_ref[...]

        b_col = jnp.sum(jnp.where(tri, f_row, 0.0), axis=1, keepdims=True)
        b_row = jnp.sum(jnp.where(rr <= cc, f_col, 0.0), axis=0, keepdims=True)
        log_inter = b_col + m_prev
        log_intra = jnp.where(tri, b_col - b_row + i_row, -jnp.inf)
        m_t = jnp.maximum(log_inter, jnp.max(log_intra, axis=1, keepdims=True))
        w_intra = jnp.exp(log_intra - m_t)
        w_inter = jnp.exp(log_inter - m_t)
        qk = lax.dot_general(qcb, kc.astype(BF16), _NT, preferred_element_type=F32) * w_intra
        num = (w_inter * jnp.dot(qcb, cst.astype(BF16), preferred_element_type=F32)
               + jnp.dot(qk.astype(BF16), vcb, preferred_element_type=F32))
        den = (w_inter * jnp.sum(qc * nst, axis=1, keepdims=True)
               + jnp.sum(qk, axis=1, keepdims=True))
        hc = num / jnp.maximum(jnp.abs(den), jnp.exp(-m_t))

        b_last = jnp.sum(f_col, axis=0, keepdims=True)
        log_s = b_last - b_col + i_col
        m_new = jnp.maximum(b_last + m_prev, jnp.max(log_s, axis=0, keepdims=True))
        decay = jnp.exp(b_last + m_prev - m_new)
        kw = kc * jnp.exp(log_s - m_new)
        c_ref[...] = decay * cst + lax.dot_general(kw.astype(BF16), vcb, _TN, preferred_element_type=F32)
        n_ref[...] = decay * nst + jnp.sum(kw, axis=0, keepdims=True)
        m_ref[...] = m_new

        hn = hc * lax.rsqrt(jnp.mean(hc * hc, axis=1, keepdims=True) + HNORM_EPS)
        y_ref[lo:hi, :] = (hn * hg_ref[...] * jax.nn.sigmoid(o_ref[lo:hi, :].astype(F32))).astype(y_ref.dtype)


def _mlstm(z, gates, gate_bias, conv_w, conv_b, hnorm_g, bsz, s, tt=256):
    nt = s // tt
    cb = 3 * A_WIDTH // B_HEAD_DIM
    gates_t = gates.T
    bias = gate_bias.reshape(1, 2 * B_HEADS)

    def col(off):
        return pl.BlockSpec((tt, B_HEAD_DIM), lambda b, h, t: (b * nt + t, cb + off + h))

    def par(rows, off):
        return pl.BlockSpec((rows, B_HEAD_DIM), lambda b, h, t: (0, off + h))

    return pl.pallas_call(
        functools.partial(_mlstm_kernel, tt=tt),
        grid=(bsz, B_HEADS, nt),
        in_specs=[col(0), col(B_HEADS), col(2 * B_HEADS), col(3 * B_HEADS),
                  pl.BlockSpec((tt, 2 * B_HEADS), lambda b, h, t: (b * nt + t, 0)),
                  pl.BlockSpec((2 * B_HEADS, tt), lambda b, h, t: (0, b * nt + t)),
                  pl.BlockSpec((1, 2 * B_HEADS), lambda b, h, t: (0, 0)),
                  pl.BlockSpec((2 * B_HEADS, 1), lambda b, h, t: (0, 0)),
                  par(B_CONV, 0), par(B_CONV, B_HEADS), par(1, 0), par(1, B_HEADS), par(1, 0)],
        out_specs=pl.BlockSpec((tt, B_HEAD_DIM), lambda b, h, t: (b * nt + t, h)),
        out_shape=jax.ShapeDtypeStruct((bsz * s, B_WIDTH), BF16),
        scratch_shapes=[pltpu.VMEM((B_HEAD_DIM, B_HEAD_DIM), F32), pltpu.VMEM((1, B_HEAD_DIM), F32),
                        pltpu.VMEM((1, 1), F32), pltpu.VMEM((SUBLANES, B_HEAD_DIM), F32),
                        pltpu.VMEM((SUBLANES, B_HEAD_DIM), F32)],
        compiler_params=_params("parallel", "parallel", "arbitrary"),
        name="mlstm",
    )(z, z, z, z, gates, gates_t, bias, bias.T, conv_w, conv_w,
      conv_b.reshape(1, -1), conv_b.reshape(1, -1), hnorm_g.reshape(1, -1))


def _even_mixer(x, w_in, gate_bias, conv_w, conv_b, hnorm_g, w_out, bsz, s):
    d = x.shape[1]
    w_main = w_in[:, :EVEN_MAIN].astype(BF16)
    w_gate = jnp.pad(w_in[:, EVEN_MAIN:], ((0, 0), (0, LANES - 2 * B_HEADS))).astype(BF16)
    z = _matmul(x, w_main, out_dtype=BF16)
    gates = _matmul(x, w_gate, tn=LANES)[:, :2 * B_HEADS]
    y_a = _moba(z, bsz, s)
    y_b = _mlstm(z, gates, gate_bias, conv_w, conv_b, hnorm_g, bsz, s)
    w_o = w_out.astype(BF16)
    return _matmul2(y_a, y_b, w_o[:A_WIDTH], w_o[A_WIDTH:])


def _shift_mix_kernel(x_ref, p_ref, mu_ref, *o_refs, tiles_per_seq):
    x = x_ref[...]
    first = (pl.program_id(0) % tiles_per_seq) == 0
    prev_row = jnp.where(first, 0.0, p_ref[SUBLANES - 1:SUBLANES, :])
    row = lax.broadcasted_iota(jnp.int32, x.shape, 0)
    dx = jnp.where(row == 0, prev_row, pltpu.roll(x, 1, axis=0)) - x
    for j, o_ref in enumerate(o_refs):
        o_ref[...] = (x + dx * mu_ref[j:j + 1, :]).astype(o_ref.dtype)


def _shift_mix(x, mu, s, tm=256):
    t, d = x.shape
    n = mu.shape[0]
    row = pl.BlockSpec((tm, d), lambda i: (i, 0))
    prev = pl.BlockSpec((SUBLANES, d), lambda i: (jnp.maximum(i * (tm // SUBLANES) - 1, 0), 0))
    return pl.pallas_call(
        functools.partial(_shift_mix_kernel, tiles_per_seq=s // tm),
        grid=(t // tm,),
        in_specs=[row, prev, pl.BlockSpec((n, d), lambda i: (0, 0))],
        out_specs=[row] * n,
        out_shape=[jax.ShapeDtypeStruct((t, d), BF16)] * n,
        compiler_params=_params("parallel"),
        name="shift_mix",
    )(x, x, mu)


def _softplus(x):
    return jnp.maximum(x, 0.0) + jnp.log1p(jnp.exp(-jnp.abs(x)))


def _lora_kernel(x_ref, w1_ref, w2_ref, b_ref, o_ref, *, mid, out):
    hmid = jnp.dot(x_ref[...], w1_ref[...], preferred_element_type=F32)
    if mid == "tanh":
        hmid = jnp.tanh(hmid)
    elif mid == "sigmoid":
        hmid = jax.nn.sigmoid(hmid)
    y = jnp.dot(hmid.astype(BF16), w2_ref[...], preferred_element_type=F32)
    if out == "logdecay":
        y = -jnp.exp(-_softplus(-(b_ref[...] + y)) - 0.5)
    elif out == "sigmoid":
        y = jax.nn.sigmoid(b_ref[...] + y)
    o_ref[...] = y


def _lora(x, w1, w2, bias, mid, out, tm=512):
    t, d = x.shape
    tm = min(tm, t)
    r = w1.shape[1]
    rp = -(-r // LANES) * LANES
    w1p = jnp.pad(w1, ((0, 0), (0, rp - r))).astype(BF16)
    w2p = jnp.pad(w2, ((0, rp - r), (0, 0))).astype(BF16)
    dout = w2.shape[1]
    if bias is None:
        bias = jnp.zeros((dout,), F32)
    return pl.pallas_call(
        functools.partial(_lora_kernel, mid=mid, out=out),
        grid=(t // tm,),
        in_specs=[pl.BlockSpec((tm, d), lambda i: (i, 0)),
                  pl.BlockSpec((d, rp), lambda i: (0, 0)),
                  pl.BlockSpec((rp, dout), lambda i: (0, 0)),
                  pl.BlockSpec((1, dout), lambda i: (0, 0))],
        out_specs=pl.BlockSpec((tm, dout), lambda i: (i, 0)),
        out_shape=jax.ShapeDtypeStruct((t, dout), F32),
        compiler_params=_params("parallel"),
        name="lora_" + out,
    )(x, w1p, w2p, bias.reshape(1, dout))


def _head_ones():
    i = jnp.arange(LANES) // C_HEAD_DIM
    return (i[:, None] == i[None, :]).astype(F32)


def _head_sum(x, ones):
    cols = [jnp.dot(x[:, c * LANES:(c + 1) * LANES], ones, precision=lax.Precision.HIGHEST,
                    preferred_element_type=F32) for c in range(x.shape[1] // LANES)]
    return jnp.concatenate(cols, axis=1)


def _rwkv_prep_kernel(*refs, vres):
    if vres:
        k_ref, a_ref, v_ref, vf_ref, vg_ref, kk_p, ka_p, ones_ref, kk_o, kh_o, bb_o, v_o = refs
    else:
        k_ref, a_ref, v_ref, kk_p, ka_p, ones_ref, kk_o, kh_o, bb_o, v_o = refs
    k = k_ref[...]
    a = a_ref[...]
    kk = k * kk_p[...]
    nrm = jnp.sqrt(_head_sum(kk * kk, ones_ref[...]))
    kk = kk / jnp.maximum(nrm, KK_EPS)
    kk_o[...] = kk
    bb_o[...] = kk * a
    kh_o[...] = k * (1.0 + (a - 1.0) * ka_p[...])
    v = v_ref[...]
    if vres:
        v = v + (vf_ref[...] - v) * vg_ref[...]
    v_o[...] = v


def _rwkv_prep(k, a, v, v_first, vgate, k_k, k_a, tm=256):
    t, d = k.shape
    row = pl.BlockSpec((tm, d), lambda i: (i, 0))
    vec = pl.BlockSpec((1, d), lambda i: (0, 0))
    vres = v_first is not None
    ins = [k, a, v] + ([v_first, vgate] if vres else []) + [k_k.reshape(1, d), k_a.reshape(1, d), _head_ones()]
    specs = [row] * (5 if vres else 3) + [vec, vec, pl.BlockSpec((LANES, LANES), lambda i: (0, 0))]
    return pl.pallas_call(
        functools.partial(_rwkv_prep_kernel, vres=vres),
        grid=(t // tm,),
        in_specs=specs,
        out_specs=[row] * 4,
        out_shape=[jax.ShapeDtypeStruct((t, d), F32)] * 4,
        compiler_params=_params("parallel"),
        name="rwkv_prep",
    )(*ins)


def _bdot(a, b):
    return jnp.dot(a.astype(BF16), b.astype(BF16), preferred_element_type=F32)


def _rwkv_chunk_pairs(r, ld, kh, v, kk, bb, g_cum, s0, c):
    pairs = range(len(r))
    lane = lax.broadcasted_iota(jnp.int32, (1, LANES), 1)
    m0 = jnp.where(lane < C_HEAD_DIM, 1.0, 0.0)
    m1 = 1.0 - m0
    n2 = 2 * c
    row = lax.broadcasted_iota(jnp.int32, (n2, n2), 0)
    col = lax.broadcasted_iota(jnp.int32, (n2, n2), 1)
    lower, lower_eq = col < row, col <= row

    def stack(x):
        return jnp.concatenate([x * m0, x * m1], axis=0)

    e_g = [jnp.exp(g_cum[p]) for p in pairs]
    e_ng = [jnp.exp(-g_cum[p]) for p in pairs]
    al2 = [stack(-(kk[p] * jnp.exp(g_cum[p] - ld[p]))) for p in pairs]
    be2 = [stack(bb[p] * e_ng[p]).astype(BF16) for p in pairs]
    kb2 = [stack(kh[p] * e_ng[p]).astype(BF16) for p in pairs]
    rb = [r[p] * e_g[p] for p in pairs]
    v2 = [stack(v[p]) for p in pairs]
    sc = [lax.dot_general(jnp.concatenate([al2[p], stack(rb[p])], axis=0).astype(BF16),
                          jnp.concatenate([be2[p], kb2[p]], axis=0), _NT,
                          preferred_element_type=F32) for p in pairs]
    l_ab = [jnp.where(lower, sc[p][:n2, :n2], 0.0) for p in pairs]
    l_ak = [jnp.where(lower, sc[p][:n2, n2:], 0.0) for p in pairs]
    l_r = [jnp.concatenate([jnp.where(lower_eq, sc[p][n2:, :n2], 0.0),
                            jnp.where(lower_eq, sc[p][n2:, n2:], 0.0)], axis=1).astype(BF16) for p in pairs]

    t_off = l_ab
    pw = [x.astype(BF16) for x in l_ab]
    for _ in range(max(c.bit_length() - 2, 0)):
        pw_f = [jnp.dot(pw[p], pw[p], preferred_element_type=F32) for p in pairs]
        pw = [x.astype(BF16) for x in pw_f]
        t_off = [t_off[p] + pw_f[p] + jnp.dot(t_off[p].astype(BF16), pw[p], preferred_element_type=F32)
                 for p in pairs]

    w0 = [jnp.concatenate([al2[p], _bdot(l_ak[p], v2[p])], axis=1) for p in pairs]
    x = [w0[p] + _bdot(t_off[p], w0[p]) for p in pairs]
    xb = [x[p].astype(BF16) for p in pairs]
    z = [jnp.dot(l_r[p],
                 jnp.concatenate([xb[p], jnp.concatenate([jnp.zeros_like(v2[p]), v2[p]], axis=1).astype(BF16)],
                                 axis=0), preferred_element_type=F32) for p in pairs]
    r_eff = [(rb[p] + z[p][:c, :LANES] + z[p][c:, :LANES]).astype(BF16) for p in pairs]
    y0 = [z[p][:c, LANES:] + z[p][c:, LANES:] for p in pairs]
    mn = [lax.dot_general(xb[p], be2[p], _TN, preferred_element_type=F32) for p in pairs]
    n_add = [mn[p][LANES:] + lax.dot_general(v2[p].astype(BF16), kb2[p], _TN, preferred_element_type=F32)
             for p in pairs]
    s0b = [s0[p].astype(BF16) for p in pairs]
    y = [y0[p] + lax.dot_general(r_eff[p], s0b[p], _NT, preferred_element_type=F32) for p in pairs]
    s_new = [(s0[p] + jnp.dot(s0b[p], mn[p][:LANES].astype(BF16), preferred_element_type=F32) + n_add[p])
             * e_g[p][c - 1:c, :] for p in pairs]
    return y, s_new


def _rwkv_scan_kernel(r_ref, ld_ref, kh_ref, v_ref, kk_ref, bb_ref, tri_ref, y_ref, s_ref, *, c, npair):
    @pl.when(pl.program_id(2) == 0)
    def _reset():
        s_ref[...] = jnp.zeros_like(s_ref)

    g_all = jnp.dot(tri_ref[...], ld_ref[...], precision=lax.Precision.HIGHEST, preferred_element_type=F32)
    sl = [slice(p * LANES, (p + 1) * LANES) for p in range(npair)]
    take = lambda ref: [ref[:, q] for q in sl]
    y, s_new = _rwkv_chunk_pairs(take(r_ref), take(ld_ref), take(kh_ref), take(v_ref), take(kk_ref),
                                 take(bb_ref), [g_all[:, q] for q in sl], [s_ref[p] for p in range(npair)], c)
    for p in range(npair):
        y_ref[:, sl[p]] = y[p]
        s_ref[p] = s_new[p]


RWKV_CHUNK = 64
RWKV_PAIRS = 8


def _rwkv_scan(r, log_w, kh, v, kk, bb, bsz, s):
    t, d = r.shape
    c = min(RWKV_CHUNK, s)
    wid = RWKV_PAIRS * LANES
    nc = s // c
    tri = (jnp.arange(c)[None, :] <= jnp.arange(c)[:, None]).astype(F32)
    blk = pl.BlockSpec((c, wid), lambda b, g, i: (b * nc + i, g))
    return pl.pallas_call(
        functools.partial(_rwkv_scan_kernel, c=c, npair=RWKV_PAIRS),
        grid=(bsz, d // wid, nc),
        in_specs=[blk] * 6 + [pl.BlockSpec((c, c), lambda b, g, i: (0, 0))],
        out_specs=blk,
        out_shape=jax.ShapeDtypeStruct((t, d), F32),
        scratch_shapes=[pltpu.VMEM((RWKV_PAIRS, LANES, LANES), F32)],
        compiler_params=_params("parallel", "parallel", "arbitrary"),
        name="rwkv_scan",
    )(r, log_w, kh, v, kk, bb, tri)


def _rwkv_post_kernel(y_ref, r_ref, kh_ref, v_ref, g_ref, lg_ref, lb_ref, rk_ref, ones_ref, o_ref):
    ones = ones_ref[...]
    y = y_ref[...]
    inv_n = 1.0 / C_HEAD_DIM
    yc = y - _head_sum(y, ones) * inv_n
    yv = _head_sum(yc * yc, ones) * inv_n
    yn = yc * lax.rsqrt(yv + LNX_EPS) * lg_ref[...] + lb_ref[...]
    bonus = _head_sum(r_ref[...] * kh_ref[...] * rk_ref[...], ones) * v_ref[...]
    o_ref[...] = ((yn + bonus) * g_ref[...]).astype(o_ref.dtype)


def _rwkv_post(y, r, kh, v, g, lnx_g, lnx_b, r_k, tm=256):
    t, d = y.shape
    row = pl.BlockSpec((tm, d), lambda i: (i, 0))
    vec = pl.BlockSpec((1, d), lambda i: (0, 0))
    return pl.pallas_call(
        _rwkv_post_kernel,
        grid=(t // tm,),
        in_specs=[row] * 5 + [vec] * 3 + [pl.BlockSpec((LANES, LANES), lambda i: (0, 0))],
        out_specs=row,
        out_shape=jax.ShapeDtypeStruct((t, d), BF16),
        compiler_params=_params("parallel"),
        name="rwkv_post",
    )(y, r, kh, v, g, lnx_g.reshape(1, d), lnx_b.reshape(1, d), r_k.reshape(1, d), _head_ones())


def _rwkv7_mixer(x, v_first, mu, w_r, w_k, w_v, w_o, w0, w1, w2, a0, a1, a2, g1, g2,
                 k_k, k_a, r_k, lnx_g, lnx_b, vres, bsz, s):
    xr, xw, xk, xv, xa, xg = _shift_mix(x, mu, s)
    r = _matmul(xr, w_r.astype(BF16))
    k = _matmul(xk, w_k.astype(BF16))
    v = _matmul(xv, w_v.astype(BF16))
    log_decay = _lora(xw, w1, w2, w0, "tanh", "logdecay")
    a = _lora(xa, a1, a2, a0, "none", "sigmoid")
    g = _lora(xg, g1, g2, None, "sigmoid", "none")
    if vres is None:
        vgate = None
        v_keep = v
    else:
        v0, v1, v2 = vres
        vgate = _lora(xv, v1, v2, v0, "none", "sigmoid")
        v_keep = v_first
    kk, kh, bb, v = _rwkv_prep(k, a, v, None if vres is None else v_first, vgate, k_k, k_a)
    y = _rwkv_scan(r, log_decay, kh, v, kk, bb, bsz, s)
    yo = _rwkv_post(y, r, kh, v, g, lnx_g, lnx_b, r_k)
    return _matmul(yo, w_o.astype(BF16)), v_keep


def _lane_pick(cond, lanef):
    return jnp.min(jnp.where(cond, lanef, float(LANES)), axis=1, keepdims=True)


def _router_kernel(x_ref, w_ref, b_ref, route_ref, cnt_ref, run_ref, *, tm):
    @pl.when(pl.program_id(0) == 0)
    def _reset():
        run_ref[...] = jnp.zeros_like(run_ref)

    logits = jnp.dot(x_ref[...].astype(BF16), w_ref[...], preferred_element_type=F32) + b_ref[...]
    lane = lax.broadcasted_iota(jnp.int32, (tm, LANES), 1)
    lanef = lane.astype(F32)

    def masked_softmax(keep):
        z = jnp.where(keep, logits, -jnp.inf)
        e = jnp.exp(z - jnp.max(z, axis=1, keepdims=True))
        return e / jnp.sum(e, axis=1, keepdims=True)

    in_groups = lane < N_GROUPS
    pg = masked_softmax(in_groups)
    g_p = jnp.max(pg, axis=1, keepdims=True)
    g_idx = _lane_pick(jnp.where(in_groups, pg, -1.0) == g_p, lanef)
    lo = float(N_GROUPS) + float(EXPERTS_PER_GROUP) * g_idx
    half = 0.5 * float(EXPERTS_PER_GROUP - 1)
    in_group = jnp.abs(lanef - lo - half) < half + 0.5
    pe = jnp.where(in_group, masked_softmax(in_group), -1.0)
    p1 = jnp.max(pe, axis=1, keepdims=True)
    i1 = _lane_pick(pe == p1, lanef)
    pe2 = jnp.where(lanef == i1, -1.0, pe)
    p2 = jnp.max(pe2, axis=1, keepdims=True)
    i2 = _lane_pick(pe2 == p2, lanef)
    den = p1 + p2
    gate = (g_p * (p1 / den), g_p * (p2 / den))
    expert = (i1 - float(N_GROUPS), i2 - float(N_GROUPS))

    earlier = jnp.where(lax.broadcasted_iota(jnp.int32, (tm, tm), 1)
                        < lax.broadcasted_iota(jnp.int32, (tm, tm), 0), 1.0, 0.0).astype(BF16)
    run = run_ref[...]
    rank = []
    for e in expert:
        onehot = jnp.where(lanef == e, 1.0, 0.0)
        before = jnp.dot(earlier, onehot.astype(BF16), preferred_element_type=F32) + run
        rank.append(jnp.sum(onehot * before, axis=1, keepdims=True))
        run = run + jnp.sum(onehot, axis=0, keepdims=True)
    run_ref[...] = run
    cnt_ref[...] = run
    cols = (expert[0], expert[1], rank[0], rank[1], gate[0], gate[1])
    route = jnp.zeros((tm, LANES), F32)
    for c, val in enumerate(cols):
        route = jnp.where(lane == c, val, route)
    route_ref[...] = route


def _router(x, w_router, b_router, tm=256):
    t, d = x.shape
    tm = min(tm, t)
    return pl.pallas_call(
        functools.partial(_router_kernel, tm=tm),
        grid=(t // tm,),
        in_specs=[pl.BlockSpec((tm, d), lambda i: (i, 0)),
                  pl.BlockSpec((d, LANES), lambda i: (0, 0)),
                  pl.BlockSpec((1, LANES), lambda i: (0, 0))],
        out_specs=[pl.BlockSpec((tm, LANES), lambda i: (i, 0)),
                   pl.BlockSpec((1, LANES), lambda i: (0, 0))],
        out_shape=[jax.ShapeDtypeStruct((t, LANES), F32), jax.ShapeDtypeStruct((1, LANES), F32)],
        scratch_shapes=[pltpu.VMEM((1, LANES), F32)],
        compiler_params=_params("arbitrary"),
        name="router",
    )(x, w_router, b_router)


SLAB = 2048 // LANES


def _to_slabs(x):
    assert x.shape[1] == SLAB * LANES
    return x.reshape(x.shape[0] * SLAB, LANES)


def _slab(ref, r):
    return ref.at[pl.ds(pl.multiple_of(r * SLAB, SLAB), SLAB)]


def _slab_rows(ref, rows):
    return jnp.concatenate([ref[pl.ds(c, rows, stride=SLAB), :] for c in range(SLAB)], axis=1)


def _store_slab_rows(ref, val, rows):
    for c in range(SLAB):
        ref[pl.ds(c, rows, stride=SLAB), :] = val[:, c * LANES:(c + 1) * LANES]


def _expert_kernel(be_ref, nu_ref, idx_ref, nxt_ref, x_hbm, wg_ref, wu_ref, wd_ref, o_ref,
                   xbuf, sems, wg_b, wu_b, wd_b, *, rows):
    i = pl.program_id(0)
    n_used = nu_ref[0]
    slot = i % 2

    def token_copy(ids_ref, r, dst_slot):
        return pltpu.make_async_copy(_slab(x_hbm, ids_ref[0, r]), _slab(xbuf.at[dst_slot], r), sems.at[dst_slot])

    def gather(ids_ref, dst_slot):
        def start(r, carry):
            token_copy(ids_ref, r, dst_slot).start()
            return carry
        lax.fori_loop(0, rows, start, 0)

    @pl.when(i == 0)
    def _():
        gather(idx_ref, 0)

    @pl.when(i + 1 < n_used)
    def _():
        gather(nxt_ref, 1 - slot)

    @pl.when(jnp.logical_or(i == 0, be_ref[i] != be_ref[jnp.maximum(i - 1, 0)]))
    def _():
        wg_b[...] = wg_ref[...].astype(BF16)
        wu_b[...] = wu_ref[...].astype(BF16)
        wd_b[...] = wd_ref[...].astype(BF16)

    @pl.when(i < n_used)
    def _():
        def wait(r, carry):
            token_copy(idx_ref, r, slot).wait()
            return carry
        lax.fori_loop(0, rows, wait, 0)
        xb = _slab_rows(xbuf.at[slot], rows).astype(BF16)
        hg = jnp.dot(xb, wg_b[...], preferred_element_type=F32)
        hu = jnp.dot(xb, wu_b[...], preferred_element_type=F32)
        hid = (hg * jax.nn.sigmoid(hg) * hu).astype(BF16)
        _store_slab_rows(o_ref, jnp.dot(hid, wd_b[...], preferred_element_type=F32), rows)

    @pl.when(i >= n_used)
    def _():
        o_ref[...] = jnp.zeros_like(o_ref)


def _expert_mlp(x_slabs, src_row, block_e, n_used, w_gate, w_up, w_down, layer, rows=MOE_ROWS):
    d, ff = w_gate.shape[2], w_gate.shape[3]
    nb = src_row.shape[0] // rows
    ids = src_row.reshape(nb, 1, rows)
    grid_spec = pltpu.PrefetchScalarGridSpec(
        num_scalar_prefetch=2,
        grid=(nb,),
        in_specs=[pl.BlockSpec((None, 1, rows), lambda i, be, nu: (i, 0, 0), memory_space=pltpu.SMEM),
                  pl.BlockSpec((None, 1, rows), lambda i, be, nu: (jnp.minimum(i + 1, nb - 1), 0, 0),
                               memory_space=pltpu.SMEM),
                  pl.BlockSpec(memory_space=pl.ANY),
                  pl.BlockSpec((None, None, d, ff), lambda i, be, nu: (layer, be[i], 0, 0)),
                  pl.BlockSpec((None, None, d, ff), lambda i, be, nu: (layer, be[i], 0, 0)),
                  pl.BlockSpec((None, None, ff, d), lambda i, be, nu: (layer, be[i], 0, 0))],
        out_specs=pl.BlockSpec((rows * SLAB, LANES), lambda i, be, nu: (i, 0)),
        scratch_shapes=[pltpu.VMEM((2, rows * SLAB, LANES), F32), pltpu.SemaphoreType.DMA((2,)),
                        pltpu.VMEM((d, ff), BF16), pltpu.VMEM((d, ff), BF16), pltpu.VMEM((ff, d), BF16)],
    )
    return pl.pallas_call(
        functools.partial(_expert_kernel, rows=rows),
        grid_spec=grid_spec,
        out_shape=jax.ShapeDtypeStruct((nb * rows * SLAB, LANES), F32),
        compiler_params=_params("arbitrary"),
        name="expert_mlp",
    )(block_e, n_used, ids, ids, x_slabs, w_gate, w_up, w_down)


def _combine_ln_kernel(idx_ref, y_ref, x_ref, gt_ref, g_ref, b_ref, o_ref, buf, sem, *, rows):
    def start(r, carry):
        for kslot in range(MOE_TOPK):
            pltpu.make_async_copy(_slab(y_ref, idx_ref[0, MOE_TOPK * r + kslot]),
                                  _slab(buf.at[kslot], r), sem).start()
        return carry

    def wait(r, carry):
        for kslot in range(MOE_TOPK):
            pltpu.make_async_copy(_slab(y_ref, 0), _slab(buf.at[kslot], r), sem).wait()
        return carry

    lax.fori_loop(0, rows, start, 0)
    lax.fori_loop(0, rows, wait, 0)
    gt = gt_ref[...]
    ffn = gt[:, 0:1] * _slab_rows(buf.at[0], rows)
    for kslot in range(1, MOE_TOPK):
        ffn = ffn + gt[:, kslot:kslot + 1] * _slab_rows(buf.at[kslot], rows)
    o_ref[...] = _ln_rows(DN_ALPHA * x_ref[...] + ffn, g_ref[...], b_ref[...])


def _combine_ln(y_sorted, dest, gates, x, g, b, rows=256):
    t, d = x.shape
    nb = t // rows
    row = pl.BlockSpec((rows, d), lambda i: (i, 0))
    vec = pl.BlockSpec((1, d), lambda i: (0, 0))
    return pl.pallas_call(
        functools.partial(_combine_ln_kernel, rows=rows),
        grid=(nb,),
        in_specs=[pl.BlockSpec((None, 1, rows * MOE_TOPK), lambda i: (i, 0, 0), memory_space=pltpu.SMEM),
                  pl.BlockSpec(memory_space=pl.ANY), row,
                  pl.BlockSpec((rows, MOE_TOPK), lambda i: (i, 0)), vec, vec],
        out_specs=row,
        out_shape=jax.ShapeDtypeStruct((t, d), F32),
        scratch_shapes=[pltpu.VMEM((MOE_TOPK, rows * SLAB, LANES), F32), pltpu.SemaphoreType.DMA(())],
        compiler_params=_params("arbitrary"),
        name="combine_ln",
    )(dest.reshape(nb, 1, rows * MOE_TOPK), y_sorted, x, gates, g.reshape(1, d), b.reshape(1, d))


def _moe_layer(x, w_grp, b_grp, w_exp_r, b_exp_r, w_gate, w_up, w_down, ln_g, ln_b, layer):
    t, d = x.shape
    n_r = N_GROUPS + N_EXPERTS
    w_router = jnp.pad(jnp.concatenate([w_grp, w_exp_r], axis=1), ((0, 0), (0, LANES - n_r))).astype(BF16)
    b_router = jnp.pad(jnp.concatenate([b_grp, b_exp_r]), (0, LANES - n_r)).reshape(1, LANES)
    route, cnt = _router(x, w_router, b_router)
    flat_e = route[:, 0:MOE_TOPK].astype(jnp.int32).reshape(-1)
    rank = route[:, MOE_TOPK:2 * MOE_TOPK].astype(jnp.int32).reshape(-1)
    gates = route[:, 2 * MOE_TOPK:3 * MOE_TOPK]
    counts = cnt[0, :N_EXPERTS].astype(jnp.int32)
    m = t * MOE_TOPK
    padded = (counts + MOE_ROWS - 1) // MOE_ROWS * MOE_ROWS
    pad_end = jnp.cumsum(padded)
    dest = (pad_end - padded)[flat_e] + rank
    n_blocks = (m + N_EXPERTS * (MOE_ROWS - 1) + MOE_ROWS - 1) // MOE_ROWS
    src_row = jnp.zeros((n_blocks * MOE_ROWS,), jnp.int32).at[dest].set(
        (jnp.arange(m) // MOE_TOPK).astype(jnp.int32))
    starts = jnp.arange(n_blocks, dtype=jnp.int32) * MOE_ROWS
    block_e = jnp.minimum(jnp.sum((pad_end[None, :] <= starts[:, None]).astype(jnp.int32), axis=1),
                          N_EXPERTS - 1).astype(jnp.int32)
    n_used = (pad_end[-1] // MOE_ROWS).astype(jnp.int32).reshape(1)

    ys = _expert_mlp(_to_slabs(x), src_row, block_e, n_used, w_gate, w_up, w_down, layer)
    return _combine_ln(ys, dest.astype(jnp.int32), gates, x, ln_g, ln_b)


def kernel(x, ev_w_in, ev_gate_bias, ev_conv_w, ev_conv_b, ev_hnorm_g, ev_w_out, od_mu, od_w_r, od_w_k, od_w_v, od_w_o, od_w0, od_w1, od_w2, od_a0, od_a1, od_a2, od_g1, od_g2, od_k_k, od_k_a, od_r_k, od_lnx_g, od_lnx_b, od_v0, od_v1, od_v2, ln_mix_g, ln_mix_b, ln_ffn_g, ln_ffn_b, moe_w_grp, moe_b_grp, moe_w_exp_r, moe_b_exp_r, moe_w_gate, moe_w_up, moe_w_down):
    bsz, s, d = x.shape
    depth = ln_mix_g.shape[0]
    xf = x.reshape(bsz * s, d)
    v_first = None
    for layer in range(depth):
        if layer % 2 == 0:
            e = layer // 2
            mix = _even_mixer(xf, ev_w_in[e], ev_gate_bias[e], ev_conv_w[e], ev_conv_b[e],
                              ev_hnorm_g[e], ev_w_out[e], bsz, s)
        else:
            o = layer // 2
            vres = None if o == 0 else (od_v0[o - 1], od_v1[o - 1], od_v2[o - 1])
            mix, v_first = _rwkv7_mixer(xf, v_first, od_mu[o], od_w_r[o], od_w_k[o], od_w_v[o],
                                        od_w_o[o], od_w0[o], od_w1[o], od_w2[o], od_a0[o],
                                        od_a1[o], od_a2[o], od_g1[o], od_g2[o], od_k_k[o],
                                        od_k_a[o], od_r_k[o], od_lnx_g[o], od_lnx_b[o], vres, bsz, s)
        xf = _resid_ln(xf, mix, ln_mix_g[layer], ln_mix_b[layer])
        xf = _moe_layer(xf, moe_w_grp[layer], moe_b_grp[layer], moe_w_exp_r[layer],
                        moe_b_exp_r[layer], moe_w_gate, moe_w_up, moe_w_down,
                        ln_ffn_g[layer], ln_ffn_b[layer], layer)
    return xf.reshape(bsz, s, d)
```

```python
import functools
import math

import jax
import jax.numpy as jnp
from jax import lax
from jax.experimental import pallas as pl
from jax.experimental.pallas import tpu as pltpu

F32 = jnp.float32
BF16 = jnp.bfloat16

DEPTH = 4
DN_ALPHA = (2 * DEPTH) ** 0.25
LN_EPS = 1e-5

A_HEADS = 8
A_HEAD_DIM = 128
A_WIDTH = A_HEADS * A_HEAD_DIM
MOBA_BLOCK = 256
MOBA_TOPK = 3
MOBA_GROUP = 4
ROPE_THETA = 500000.0
ROPE_DIM = A_HEAD_DIM // 4
NEG_INF = -1e30

B_HEADS = 4
B_HEAD_DIM = 256
B_WIDTH = B_HEADS * B_HEAD_DIM
MLSTM_CHUNK = 64
B_CONV = 4
HNORM_EPS = 1e-6
EVEN_MAIN = 3 * A_WIDTH + 4 * B_WIDTH

C_HEAD_DIM = 64
LNX_EPS = 64e-5
KK_EPS = 1e-12

N_GROUPS = 4
EXPERTS_PER_GROUP = 8
N_EXPERTS = N_GROUPS * EXPERTS_PER_GROUP
MOE_TOPK = 2
MOE_ROWS = 256

LANES = 128
SUBLANES = 8
VMEM_LIMIT = 56 * 1024 * 1024

_NT = (((1,), (1,)), ((), ()))
_TN = (((0,), (0,)), ((), ()))


def _params(*sem):
    return pltpu.CompilerParams(dimension_semantics=sem, vmem_limit_bytes=VMEM_LIMIT)


def _mm_kernel(x_ref, w_ref, o_ref):
    o_ref[...] = jnp.dot(x_ref[...].astype(BF16), w_ref[...],
                         preferred_element_type=F32).astype(o_ref.dtype)


def _matmul(x, w, out_dtype=F32, tm=1024, tn=512):
    m, k = x.shape
    n = w.shape[1]
    tm, tn = min(tm, m), min(tn, n)
    return pl.pallas_call(
        _mm_kernel,
        grid=(m // tm, n // tn),
        in_specs=[pl.BlockSpec((tm, k), lambda i, j: (i, 0)),
                  pl.BlockSpec((k, tn), lambda i, j: (0, j))],
        out_specs=pl.BlockSpec((tm, tn), lambda i, j: (i, j)),
        out_shape=jax.ShapeDtypeStruct((m, n), out_dtype),
        compiler_params=_params("parallel", "arbitrary"),
        name="matmul",
    )(x, w)


def _mm2_kernel(xa_ref, xb_ref, wa_ref, wb_ref, o_ref):
    acc = jnp.dot(xa_ref[...], wa_ref[...], preferred_element_type=F32)
    acc = acc + jnp.dot(xb_ref[...], wb_ref[...], preferred_element_type=F32)
    o_ref[...] = acc


def _matmul2(xa, xb, wa, wb, tm=1024, tn=512):
    m, ka = xa.shape
    kb = xb.shape[1]
    n = wa.shape[1]
    tm, tn = min(tm, m), min(tn, n)
    return pl.pallas_call(
        _mm2_kernel,
        grid=(m // tm, n // tn),
        in_specs=[pl.BlockSpec((tm, ka), lambda i, j: (i, 0)),
                  pl.BlockSpec((tm, kb), lambda i, j: (i, 0)),
                  pl.BlockSpec((ka, tn), lambda i, j: (0, j)),
                  pl.BlockSpec((kb, tn), lambda i, j: (0, j))],
        out_specs=pl.BlockSpec((tm, tn), lambda i, j: (i, j)),
        out_shape=jax.ShapeDtypeStruct((m, n), F32),
        compiler_params=_params("parallel", "arbitrary"),
        name="matmul2",
    )(xa, xb, wa, wb)


def _ln_rows(h, g, b):
    mu = jnp.mean(h, axis=-1, keepdims=True)
    hc = h - mu
    var = jnp.mean(hc * hc, axis=-1, keepdims=True)
    return hc * lax.rsqrt(var + LN_EPS) * g + b


def _resid_ln_kernel(x_ref, m_ref, g_ref, b_ref, o_ref, s_ref, *, tm):
    y = _ln_rows(DN_ALPHA * x_ref[...] + m_ref[...], g_ref[...], b_ref[...])
    o_ref[...] = y
    _store_slab_rows(s_ref, y, tm)


def _resid_ln(x, mix, g, b, tm=256):
    t, d = x.shape
    tm = min(tm, t)
    row = pl.BlockSpec((tm, d), lambda i: (i, 0))
    vec = pl.BlockSpec((1, d), lambda i: (0, 0))
    return pl.pallas_call(
        functools.partial(_resid_ln_kernel, tm=tm),
        grid=(t // tm,),
        in_specs=[row, row, vec, vec],
        out_specs=[row, pl.BlockSpec((tm * SLAB, LANES), lambda i: (i, 0))],
        out_shape=[jax.ShapeDtypeStruct((t, d), F32), jax.ShapeDtypeStruct((t * SLAB, LANES), F32)],
        compiler_params=_params("parallel"),
        name="resid_ln",
    )(x, mix, g.reshape(1, d), b.reshape(1, d))


def _rope_tables(s):
    half = ROPE_DIM // 2
    inv = jnp.power(jnp.float32(ROPE_THETA), -jnp.arange(half, dtype=F32) / half)
    ang = jnp.arange(s).astype(F32)[:, None] * inv[None, :]
    cos, sin = jnp.cos(ang), jnp.sin(ang)
    zero = jnp.zeros((s, A_HEAD_DIM - ROPE_DIM), F32)
    zh = jnp.zeros((s, half), F32)
    c = jnp.concatenate([cos, cos, zero + 1.0], axis=-1)
    s_up = jnp.concatenate([zh, sin, zero], axis=-1)
    s_dn = jnp.concatenate([-sin, zh, zero], axis=-1)
    return c, s_up, s_dn


def _rope(x, c, s_up, s_dn):
    half = ROPE_DIM // 2
    return (x * c + pltpu.roll(x, half, axis=1) * s_up
            + pltpu.roll(x, A_HEAD_DIM - half, axis=1) * s_dn)


def _moba_kernel(q_ref, k_ref, v_ref, c_ref, su_ref, sd_ref, o_ref, krot_ref, vb_ref, kmean_ref, *, nblk):
    qi = pl.program_id(2)
    blk = MOBA_BLOCK
    scale = A_HEAD_DIM ** -0.5

    @pl.when(qi == 0)
    def _prepare_keys():
        def body(j, carry):
            r = pl.multiple_of(j * blk, blk)
            rows = pl.ds(r, blk)
            kr = _rope(k_ref[rows, :].astype(F32), c_ref[rows, :], su_ref[rows, :], sd_ref[rows, :])
            block_tag = jnp.where(lax.broadcasted_iota(jnp.int32, (blk, LANES), 1) == j, 1.0, 0.0)
            krot_ref[rows, :] = jnp.concatenate([kr, block_tag], axis=1).astype(BF16)
            vb_ref[rows, :] = v_ref[rows, :].astype(BF16)
            kmean_ref[pl.ds(j, 1), :] = jnp.mean(kr, axis=0, keepdims=True)
            return carry
        lax.fori_loop(0, nblk, body, 0)

    r0 = pl.multiple_of(qi * blk, blk)
    qrows = pl.ds(r0, blk)
    qb = _rope(q_ref[...].astype(F32), c_ref[qrows, :], su_ref[qrows, :], sd_ref[qrows, :]).astype(BF16)

    g = lax.dot_general(kmean_ref[...].astype(BF16), qb, _NT, preferred_element_type=F32)
    brow = lax.broadcasted_iota(jnp.int32, (nblk, blk), 0)
    g = jnp.where(brow < qi, g, -jnp.inf)
    sel_t = jnp.zeros((nblk, blk), F32)
    for j in range(nblk):
        gj = g[j:j + 1, :]
        earlier = jnp.where(brow < j, 1.0, 0.0)
        beats = jnp.where(g > gj, 1.0, jnp.where(g == gj, earlier, 0.0))
        cnt = jnp.sum(beats, axis=0, keepdims=True)
        sel_t = jnp.where(brow == j, jnp.where(cnt < float(min(MOBA_TOPK, nblk)), 1.0, 0.0), sel_t)
    sel_t = jnp.where(brow < qi, sel_t, 0.0)
    eye = jnp.where(lax.broadcasted_iota(jnp.int32, (blk, blk), 0)
                    == lax.broadcasted_iota(jnp.int32, (blk, blk), 1), 1.0, 0.0).astype(BF16)
    sel_pad = jnp.concatenate([sel_t, jnp.zeros((LANES - nblk, blk), F32)], axis=0).astype(BF16)
    sel = lax.dot_general(eye, sel_pad, _NT, preferred_element_type=F32)
    lane = lax.broadcasted_iota(jnp.int32, (blk, LANES), 1)
    bias = jnp.where(sel > 0.5, 0.0, jnp.where(lane < nblk, NEG_INF / scale, 0.0))
    q_aug = jnp.concatenate([qb, bias.astype(BF16)], axis=1)

    kd = krot_ref[qrows, :A_HEAD_DIM]
    s = lax.dot_general(qb, kd, _NT, preferred_element_type=F32) * scale
    causal = (lax.broadcasted_iota(jnp.int32, (blk, blk), 1)
              <= lax.broadcasted_iota(jnp.int32, (blk, blk), 0))
    s = jnp.where(causal, s, NEG_INF)
    m0 = jnp.max(s, axis=1, keepdims=True)
    p = jnp.exp(s - m0)
    l0 = jnp.sum(p, axis=1, keepdims=True)
    acc0 = jnp.dot(p.astype(BF16), vb_ref[qrows, :], preferred_element_type=F32)

    span = MOBA_GROUP * blk

    def past_group(jg, carry):
        m, l, acc = carry
        rows = pl.ds(pl.multiple_of(jg * span, span), span)
        sj = lax.dot_general(q_aug, krot_ref[rows, :], _NT, preferred_element_type=F32) * scale
        m_new = jnp.maximum(m, jnp.max(sj, axis=1, keepdims=True))
        a = jnp.exp(m - m_new)
        pj = jnp.exp(sj - m_new)
        l = a * l + jnp.sum(pj, axis=1, keepdims=True)
        acc = a * acc + jnp.dot(pj.astype(BF16), vb_ref[rows, :], preferred_element_type=F32)
        return m_new, l, acc

    n_groups = (qi + MOBA_GROUP - 1) // MOBA_GROUP
    _, l, acc = lax.fori_loop(0, n_groups, past_group, (m0, l0, acc0))
    o_ref[...] = (acc / l).astype(o_ref.dtype)


def _moba(z, bsz, s):
    nblk = s // MOBA_BLOCK
    c, s_up, s_dn = _rope_tables(s)
    tab = pl.BlockSpec((s, A_HEAD_DIM), lambda b, h, qi: (0, 0))
    return pl.pallas_call(
        functools.partial(_moba_kernel, nblk=nblk),
        grid=(bsz, A_HEADS, nblk),
        in_specs=[pl.BlockSpec((MOBA_BLOCK, A_HEAD_DIM), lambda b, h, qi: (b * nblk + qi, h)),
                  pl.BlockSpec((s, A_HEAD_DIM), lambda b, h, qi: (b, A_HEADS + h)),
                  pl.BlockSpec((s, A_HEAD_DIM), lambda b, h, qi: (b, 2 * A_HEADS + h)),
                  tab, tab, tab],
        out_specs=pl.BlockSpec((MOBA_BLOCK, A_HEAD_DIM), lambda b, h, qi: (b * nblk + qi, h)),
        out_shape=jax.ShapeDtypeStruct((bsz * s, A_WIDTH), BF16),
        scratch_shapes=[pltpu.VMEM((s, A_HEAD_DIM + LANES), BF16), pltpu.VMEM((s, A_HEAD_DIM), BF16),
                        pltpu.VMEM((nblk, A_HEAD_DIM), F32)],
        compiler_params=_params("parallel", "parallel", "arbitrary"),
        name="moba",
    )(z, z, z, c, s_up, s_dn)


def _log_sigmoid(x):
    return jnp.minimum(x, 0.0) - jnp.log1p(jnp.exp(-jnp.abs(x)))


def _shift_rows(x, prev8, d):
    if d == 0:
        return x
    rolled = pltpu.roll(x, d, axis=0)
    top = jnp.where(lax.broadcasted_iota(jnp.int32, prev8.shape, 0) < d,
                    pltpu.roll(prev8, d, axis=0), rolled[:SUBLANES])
    return jnp.concatenate([top, rolled[SUBLANES:]], axis=0)


def _mlstm_kernel(q_ref, k_ref, v_ref, o_ref, gc_ref, gr_ref, bc_ref, br_ref,
                  cwq_ref, cwk_ref, cbq_ref, cbk_ref, hg_ref, y_ref,
                  c_ref, n_ref, m_ref, pq_ref, pk_ref, *, tt):
    h = pl.program_id(1)
    ti = pl.program_id(2)
    L = MLSTM_CHUNK
    dk = B_HEAD_DIM

    @pl.when(ti == 0)
    def _reset():
        c_ref[...] = jnp.zeros_like(c_ref)
        n_ref[...] = jnp.zeros_like(n_ref)
        m_ref[...] = jnp.zeros_like(m_ref)
        pq_ref[...] = jnp.zeros_like(pq_ref)
        pk_ref[...] = jnp.zeros_like(pk_ref)

    def conv_silu(x_ref, p_ref, w_ref, b_ref):
        x = x_ref[...].astype(F32)
        prev8 = p_ref[...]
        out = b_ref[...] + x * w_ref[B_CONV - 1:B_CONV, :]
        for d in range(1, B_CONV):
            out = out + _shift_rows(x, prev8, d) * w_ref[B_CONV - 1 - d:B_CONV - d, :]
        p_ref[...] = x[tt - SUBLANES:, :]
        return out * jax.nn.sigmoid(out)

    q_all = conv_silu(q_ref, pq_ref, cwq_ref, cbq_ref)
    k_all = conv_silu(k_ref, pk_ref, cwk_ref, cbk_ref) * (dk ** -0.5)

    lane8 = lax.broadcasted_iota(jnp.int32, (tt, 2 * B_HEADS), 1)
    gcol = gc_ref[...] + bc_ref[...]
    i_col_all = jnp.sum(jnp.where(lane8 == h, gcol, 0.0), axis=1, keepdims=True)
    f_col_all = _log_sigmoid(jnp.sum(jnp.where(lane8 == B_HEADS + h, gcol, 0.0), axis=1, keepdims=True))
    i_row_all = gr_ref[pl.ds(h, 1), :] + br_ref[pl.ds(h, 1), :]
    f_row_all = _log_sigmoid(gr_ref[pl.ds(B_HEADS + h, 1), :] + br_ref[pl.ds(B_HEADS + h, 1), :])

    rr = lax.broadcasted_iota(jnp.int32, (L, L), 0)
    cc = lax.broadcasted_iota(jnp.int32, (L, L), 1)
    tri = cc <= rr

    for c in range(tt // L):
        lo, hi = c * L, (c + 1) * L
        qc = q_all[lo:hi]
        kc = k_all[lo:hi]
        qcb = qc.astype(BF16)
        vcb = v_ref[lo:hi, :].astype(BF16)
        i_col, f_col = i_col_all[lo:hi], f_col_all[lo:hi]
        i_row, f_row = i_row_all[:, lo:hi], f_row_all[:, lo:hi]
        m_prev = m_ref[...]
        cst = c_ref[...]
        nst = n_ref[...]

        b_col = jnp.sum(jnp.where(tri, f_row, 0.0), axis=1, keepdims=True)
        b_row = jnp.sum(jnp.where(rr <= cc, f_col, 0.0), axis=0, keepdims=True)
        log_inter = b_col + m_prev
        log_intra = jnp.where(tri, b_col - b_row + i_row, -jnp.inf)
        m_t = jnp.maximum(log_inter, jnp.max(log_intra, axis=1, keepdims=True))
        w_intra = jnp.exp(log_intra - m_t)
        w_inter = jnp.exp(log_inter - m_t)
        qk = lax.dot_general(qcb, kc.astype(BF16), _NT, preferred_element_type=F32) * w_intra
        num = (w_inter * jnp.dot(qcb, cst.astype(BF16), preferred_element_type=F32)
               + jnp.dot(qk.astype(BF16), vcb, preferred_element_type=F32))
        den = (w_inter * jnp.sum(qc * nst, axis=1, keepdims=True)
               + jnp.sum(qk, axis=1, keepdims=True))
        hc = num / jnp.maximum(jnp.abs(den), jnp.exp(-m_t))

        b_last = jnp.sum(f_col, axis=0, keepdims=True)
        log_s = b_last - b_col + i_col
        m_new = jnp.maximum(b_last + m_prev, jnp.max(log_s, axis=0, keepdims=True))
        decay = jnp.exp(b_last + m_prev - m_new)
        kw = kc * jnp.exp(log_s - m_new)
        c_ref[...] = decay * cst + lax.dot_general(kw.astype(BF16), vcb, _TN, preferred_element_type=F32)
        n_ref[...] = decay * nst + jnp.sum(kw, axis=0, keepdims=True)
        m_ref[...] = m_new

        hn = hc * lax.rsqrt(jnp.mean(hc * hc, axis=1, keepdims=True) + HNORM_EPS)
        y_ref[lo:hi, :] = (hn * hg_ref[...] * jax.nn.sigmoid(o_ref[lo:hi, :].astype(F32))).astype(y_ref.dtype)


def _mlstm(z, gates, gate_bias, conv_w, conv_b, hnorm_g, bsz, s, tt=256):
    nt = s // tt
    cb = 3 * A_WIDTH // B_HEAD_DIM
    gates_t = gates.T
    bias = gate_bias.reshape(1, 2 * B_HEADS)

    def col(off):
        return pl.BlockSpec((tt, B_HEAD_DIM), lambda b, h, t: (b * nt + t, cb + off + h))

    def par(rows, off):
        return pl.BlockSpec((rows, B_HEAD_DIM), lambda b, h, t: (0, off + h))

    return pl.pallas_call(
        functools.partial(_mlstm_kernel, tt=tt),
        grid=(bsz, B_HEADS, nt),
        in_specs=[col(0), col(B_HEADS), col(2 * B_HEADS), col(3 * B_HEADS),
                  pl.BlockSpec((tt, 2 * B_HEADS), lambda b, h, t: (b * nt + t, 0)),
                  pl.BlockSpec((2 * B_HEADS, tt), lambda b, h, t: (0, b * nt + t)),
                  pl.BlockSpec((1, 2 * B_HEADS), lambda b, h, t: (0, 0)),
                  pl.BlockSpec((2 * B_HEADS, 1), lambda b, h, t: (0, 0)),
                  par(B_CONV, 0), par(B_CONV, B_HEADS), par(1, 0), par(1, B_HEADS), par(1, 0)],
        out_specs=pl.BlockSpec((tt, B_HEAD_DIM), lambda b, h, t: (b * nt + t, h)),
        out_shape=jax.ShapeDtypeStruct((bsz * s, B_WIDTH), BF16),
        scratch_shapes=[pltpu.VMEM((B_HEAD_DIM, B_HEAD_DIM), F32), pltpu.VMEM((1, B_HEAD_DIM), F32),
                        pltpu.VMEM((1, 1), F32), pltpu.VMEM((SUBLANES, B_HEAD_DIM), F32),
                        pltpu.VMEM((SUBLANES, B_HEAD_DIM), F32)],
        compiler_params=_params("parallel", "parallel", "arbitrary"),
        name="mlstm",
    )(z, z, z, z, gates, gates_t, bias, bias.T, conv_w, conv_w,
      conv_b.reshape(1, -1), conv_b.reshape(1, -1), hnorm_g.reshape(1, -1))


def _even_mixer(x, w_in, gate_bias, conv_w, conv_b, hnorm_g, w_out, bsz, s):
    d = x.shape[1]
    w_main = w_in[:, :EVEN_MAIN].astype(BF16)
    w_gate = jnp.pad(w_in[:, EVEN_MAIN:], ((0, 0), (0, LANES - 2 * B_HEADS))).astype(BF16)
    z = _matmul(x, w_main, out_dtype=BF16)
    gates = _matmul(x, w_gate, tn=LANES)[:, :2 * B_HEADS]
    y_a = _moba(z, bsz, s)
    y_b = _mlstm(z, gates, gate_bias, conv_w, conv_b, hnorm_g, bsz, s)
    w_o = w_out.astype(BF16)
    return _matmul2(y_a, y_b, w_o[:A_WIDTH], w_o[A_WIDTH:])


def _shift_mix_kernel(x_ref, p_ref, mu_ref, *o_refs, tiles_per_seq):
    x = x_ref[...]
    first = (pl.program_id(0) % tiles_per_seq) == 0
    prev_row = jnp.where(first, 0.0, p_ref[SUBLANES - 1:SUBLANES, :])
    row = lax.broadcasted_iota(jnp.int32, x.shape, 0)
    dx = jnp.where(row == 0, prev_row, pltpu.roll(x, 1, axis=0)) - x
    for j, o_ref in enumerate(o_refs):
        o_ref[...] = (x + dx * mu_ref[j:j + 1, :]).astype(o_ref.dtype)


def _shift_mix(x, mu, s, tm=256):
    t, d = x.shape
    n = mu.shape[0]
    row = pl.BlockSpec((tm, d), lambda i: (i, 0))
    prev = pl.BlockSpec((SUBLANES, d), lambda i: (jnp.maximum(i * (tm // SUBLANES) - 1, 0), 0))
    return pl.pallas_call(
        functools.partial(_shift_mix_kernel, tiles_per_seq=s // tm),
        grid=(t // tm,),
        in_specs=[row, prev, pl.BlockSpec((n, d), lambda i: (0, 0))],
        out_specs=[row] * n,
        out_shape=[jax.ShapeDtypeStruct((t, d), BF16)] * n,
        compiler_params=_params("parallel"),
        name="shift_mix",
    )(x, x, mu)


def _softplus(x):
    return jnp.maximum(x, 0.0) + jnp.log1p(jnp.exp(-jnp.abs(x)))


def _lora_kernel(x_ref, w1_ref, w2_ref, b_ref, o_ref, *, mid, out):
    hmid = jnp.dot(x_ref[...], w1_ref[...], preferred_element_type=F32)
    if mid == "tanh":
        hmid = jnp.tanh(hmid)
    elif mid == "sigmoid":
        hmid = jax.nn.sigmoid(hmid)
    y = jnp.dot(hmid.astype(BF16), w2_ref[...], preferred_element_type=F32)
    if out == "logdecay":
        y = -jnp.exp(-_softplus(-(b_ref[...] + y)) - 0.5)
    elif out == "sigmoid":
        y = jax.nn.sigmoid(b_ref[...] + y)
    o_ref[...] = y


def _lora(x, w1, w2, bias, mid, out, tm=512):
    t, d = x.shape
    tm = min(tm, t)
    r = w1.shape[1]
    rp = -(-r // LANES) * LANES
    w1p = jnp.pad(w1, ((0, 0), (0, rp - r))).astype(BF16)
    w2p = jnp.pad(w2, ((0, rp - r), (0, 0))).astype(BF16)
    dout = w2.shape[1]
    if bias is None:
        bias = jnp.zeros((dout,), F32)
    return pl.pallas_call(
        functools.partial(_lora_kernel, mid=mid, out=out),
        grid=(t // tm,),
        in_specs=[pl.BlockSpec((tm, d), lambda i: (i, 0)),
                  pl.BlockSpec((d, rp), lambda i: (0, 0)),
                  pl.BlockSpec((rp, dout), lambda i: (0, 0)),
                  pl.BlockSpec((1, dout), lambda i: (0, 0))],
        out_specs=pl.BlockSpec((tm, dout), lambda i: (i, 0)),
        out_shape=jax.ShapeDtypeStruct((t, dout), F32),
        compiler_params=_params("parallel"),
        name="lora_" + out,
    )(x, w1p, w2p, bias.reshape(1, dout))


def _head_ones():
    i = jnp.arange(LANES) // C_HEAD_DIM
    return (i[:, None] == i[None, :]).astype(F32)


def _head_sum(x, ones):
    cols = [jnp.dot(x[:, c * LANES:(c + 1) * LANES], ones, precision=lax.Precision.HIGHEST,
                    preferred_element_type=F32) for c in range(x.shape[1] // LANES)]
    return jnp.concatenate(cols, axis=1)


def _rwkv_prep_kernel(*refs, vres):
    if vres:
        k_ref, a_ref, v_ref, vf_ref, vg_ref, kk_p, ka_p, ones_ref, kk_o, kh_o, bb_o, v_o = refs
    else:
        k_ref, a_ref, v_ref, kk_p, ka_p, ones_ref, kk_o, kh_o, bb_o, v_o = refs
    k = k_ref[...]
    a = a_ref[...]
    kk = k * kk_p[...]
    nrm = jnp.sqrt(_head_sum(kk * kk, ones_ref[...]))
    kk = kk / jnp.maximum(nrm, KK_EPS)
    kk_o[...] = kk
    bb_o[...] = kk * a
    kh_o[...] = k * (1.0 + (a - 1.0) * ka_p[...])
    v = v_ref[...]
    if vres:
        v = v + (vf_ref[...] - v) * vg_ref[...]
    v_o[...] = v


def _rwkv_prep(k, a, v, v_first, vgate, k_k, k_a, tm=256):
    t, d = k.shape
    row = pl.BlockSpec((tm, d), lambda i: (i, 0))
    vec = pl.BlockSpec((1, d), lambda i: (0, 0))
    vres = v_first is not None
    ins = [k, a, v] + ([v_first, vgate] if vres else []) + [k_k.reshape(1, d), k_a.reshape(1, d), _head_ones()]
    specs = [row] * (5 if vres else 3) + [vec, vec, pl.BlockSpec((LANES, LANES), lambda i: (0, 0))]
    return pl.pallas_call(
        functools.partial(_rwkv_prep_kernel, vres=vres),
        grid=(t // tm,),
        in_specs=specs,
        out_specs=[row] * 4,
        out_shape=[jax.ShapeDtypeStruct((t, d), F32)] * 4,
        compiler_params=_params("parallel"),
        name="rwkv_prep",
    )(*ins)


def _bdot(a, b):
    return jnp.dot(a.astype(BF16), b.astype(BF16), preferred_element_type=F32)


def _rwkv_chunk_pairs(r, ld, kh, v, kk, bb, g_cum, s0, c):
    pairs = range(len(r))
    lane = lax.broadcasted_iota(jnp.int32, (1, LANES), 1)
    m0 = jnp.where(lane < C_HEAD_DIM, 1.0, 0.0)
    m1 = 1.0 - m0
    n2 = 2 * c
    row = lax.broadcasted_iota(jnp.int32, (n2, n2), 0)
    col = lax.broadcasted_iota(jnp.int32, (n2, n2), 1)
    lower, lower_eq = col < row, col <= row

    def stack(x):
        return jnp.concatenate([x * m0, x * m1], axis=0)

    e_g = [jnp.exp(g_cum[p]) for p in pairs]
    e_ng = [jnp.exp(-g_cum[p]) for p in pairs]
    al2 = [stack(-(kk[p] * jnp.exp(g_cum[p] - ld[p]))) for p in pairs]
    be2 = [stack(bb[p] * e_ng[p]).astype(BF16) for p in pairs]
    kb2 = [stack(kh[p] * e_ng[p]).astype(BF16) for p in pairs]
    rb = [r[p] * e_g[p] for p in pairs]
    v2 = [stack(v[p]) for p in pairs]
    sc = [lax.dot_general(jnp.concatenate([al2[p], stack(rb[p])], axis=0).astype(BF16),
                          jnp.concatenate([be2[p], kb2[p]], axis=0), _NT,
                          preferred_element_type=F32) for p in pairs]
    l_ab = [jnp.where(lower, sc[p][:n2, :n2], 0.0) for p in pairs]
    l_ak = [jnp.where(lower, sc[p][:n2, n2:], 0.0) for p in pairs]
    l_r = [jnp.concatenate([jnp.where(lower_eq, sc[p][n2:, :n2], 0.0),
                            jnp.where(lower_eq, sc[p][n2:, n2:], 0.0)], axis=1).astype(BF16) for p in pairs]

    t_off = l_ab
    pw = [x.astype(BF16) for x in l_ab]
    for _ in range(max(c.bit_length() - 2, 0)):
        pw_f = [jnp.dot(pw[p], pw[p], preferred_element_type=F32) for p in pairs]
        pw = [x.astype(BF16) for x in pw_f]
        t_off = [t_off[p] + pw_f[p] + jnp.dot(t_off[p].astype(BF16), pw[p], preferred_element_type=F32)
                 for p in pairs]

    w0 = [jnp.concatenate([al2[p], _bdot(l_ak[p], v2[p])], axis=1) for p in pairs]
    x = [w0[p] + _bdot(t_off[p], w0[p]) for p in pairs]
    xb = [x[p].astype(BF16) for p in pairs]
    z = [jnp.dot(l_r[p],
                 jnp.concatenate([xb[p], jnp.concatenate([jnp.zeros_like(v2[p]), v2[p]], axis=1).astype(BF16)],
                                 axis=0), preferred_element_type=F32) for p in pairs]
    r_eff = [(rb[p] + z[p][:c, :LANES] + z[p][c:, :LANES]).astype(BF16) for p in pairs]
    y0 = [z[p][:c, LANES:] + z[p][c:, LANES:] for p in pairs]
    mn = [lax.dot_general(xb[p], be2[p], _TN, preferred_element_type=F32) for p in pairs]
    n_add = [mn[p][LANES:] + lax.dot_general(v2[p].astype(BF16), kb2[p], _TN, preferred_element_type=F32)
             for p in pairs]
    s0b = [s0[p].astype(BF16) for p in pairs]
    y = [y0[p] + lax.dot_general(r_eff[p], s0b[p], _NT, preferred_element_type=F32) for p in pairs]
    s_new = [(s0[p] + jnp.dot(s0b[p], mn[p][:LANES].astype(BF16), preferred_element_type=F32) + n_add[p])
             * e_g[p][c - 1:c, :] for p in pairs]
    return y, s_new


def _rwkv_scan_kernel(r_ref, ld_ref, kh_ref, v_ref, kk_ref, bb_ref, tri_ref, y_ref, s_ref, *, c, npair):
    @pl.when(pl.program_id(2) == 0)
    def _reset():
        s_ref[...] = jnp.zeros_like(s_ref)

    g_all = jnp.dot(tri_ref[...], ld_ref[...], precision=lax.Precision.HIGHEST, preferred_element_type=F32)
    sl = [slice(p * LANES, (p + 1) * LANES) for p in range(npair)]
    take = lambda ref: [ref[:, q] for q in sl]
    y, s_new = _rwkv_chunk_pairs(take(r_ref), take(ld_ref), take(kh_ref), take(v_ref), take(kk_ref),
                                 take(bb_ref), [g_all[:, q] for q in sl], [s_ref[p] for p in range(npair)], c)
    for p in range(npair):
        y_ref[:, sl[p]] = y[p]
        s_ref[p] = s_new[p]


RWKV_CHUNK = 64
RWKV_PAIRS = 8


def _rwkv_scan(r, log_w, kh, v, kk, bb, bsz, s):
    t, d = r.shape
    c = min(RWKV_CHUNK, s)
    wid = RWKV_PAIRS * LANES
    nc = s // c
    tri = (jnp.arange(c)[None, :] <= jnp.arange(c)[:, None]).astype(F32)
    blk = pl.BlockSpec((c, wid), lambda b, g, i: (b * nc + i, g))
    return pl.pallas_call(
        functools.partial(_rwkv_scan_kernel, c=c, npair=RWKV_PAIRS),
        grid=(bsz, d // wid, nc),
        in_specs=[blk] * 6 + [pl.BlockSpec((c, c), lambda b, g, i: (0, 0))],
        out_specs=blk,
        out_shape=jax.ShapeDtypeStruct((t, d), F32),
        scratch_shapes=[pltpu.VMEM((RWKV_PAIRS, LANES, LANES), F32)],
        compiler_params=_params("parallel", "parallel", "arbitrary"),
        name="rwkv_scan",
    )(r, log_w, kh, v, kk, bb, tri)


def _rwkv_post_kernel(y_ref, r_ref, kh_ref, v_ref, g_ref, lg_ref, lb_ref, rk_ref, ones_ref, o_ref):
    ones = ones_ref[...]
    y = y_ref[...]
    inv_n = 1.0 / C_HEAD_DIM
    yc = y - _head_sum(y, ones) * inv_n
    yv = _head_sum(yc * yc, ones) * inv_n
    yn = yc * lax.rsqrt(yv + LNX_EPS) * lg_ref[...] + lb_ref[...]
    bonus = _head_sum(r_ref[...] * kh_ref[...] * rk_ref[...], ones) * v_ref[...]
    o_ref[...] = ((yn + bonus) * g_ref[...]).astype(o_ref.dtype)


def _rwkv_post(y, r, kh, v, g, lnx_g, lnx_b, r_k, tm=256):
    t, d = y.shape
    row = pl.BlockSpec((tm, d), lambda i: (i, 0))
    vec = pl.BlockSpec((1, d), lambda i: (0, 0))
    return pl.pallas_call(
        _rwkv_post_kernel,
        grid=(t // tm,),
        in_specs=[row] * 5 + [vec] * 3 + [pl.BlockSpec((LANES, LANES), lambda i: (0, 0))],
        out_specs=row,
        out_shape=jax.ShapeDtypeStruct((t, d), BF16),
        compiler_params=_params("parallel"),
        name="rwkv_post",
    )(y, r, kh, v, g, lnx_g.reshape(1, d), lnx_b.reshape(1, d), r_k.reshape(1, d), _head_ones())


def _rwkv7_mixer(x, v_first, mu, w_r, w_k, w_v, w_o, w0, w1, w2, a0, a1, a2, g1, g2,
                 k_k, k_a, r_k, lnx_g, lnx_b, vres, bsz, s):
    xr, xw, xk, xv, xa, xg = _shift_mix(x, mu, s)
    r = _matmul(xr, w_r.astype(BF16))
    k = _matmul(xk, w_k.astype(BF16))
    v = _matmul(xv, w_v.astype(BF16))
    log_decay = _lora(xw, w1, w2, w0, "tanh", "logdecay")
    a = _lora(xa, a1, a2, a0, "none", "sigmoid")
    g = _lora(xg, g1, g2, None, "sigmoid", "none")
    if vres is None:
        vgate = None
        v_keep = v
    else:
        v0, v1, v2 = vres
        vgate = _lora(xv, v1, v2, v0, "none", "sigmoid")
        v_keep = v_first
    kk, kh, bb, v = _rwkv_prep(k, a, v, None if vres is None else v_first, vgate, k_k, k_a)
    y = _rwkv_scan(r, log_decay, kh, v, kk, bb, bsz, s)
    yo = _rwkv_post(y, r, kh, v, g, lnx_g, lnx_b, r_k)
    return _matmul(yo, w_o.astype(BF16)), v_keep


def _lane_pick(cond, lanef):
    return jnp.min(jnp.where(cond, lanef, float(LANES)), axis=1, keepdims=True)


def _router_kernel(x_ref, w_ref, b_ref, route_ref, cnt_ref, run_ref, *, tm):
    @pl.when(pl.program_id(0) == 0)
    def _reset():
        run_ref[...] = jnp.zeros_like(run_ref)

    logits = jnp.dot(x_ref[...].astype(BF16), w_ref[...], preferred_element_type=F32) + b_ref[...]
    lane = lax.broadcasted_iota(jnp.int32, (tm, LANES), 1)
    lanef = lane.astype(F32)

    def masked_softmax(keep):
        z = jnp.where(keep, logits, -jnp.inf)
        e = jnp.exp(z - jnp.max(z, axis=1, keepdims=True))
        return e / jnp.sum(e, axis=1, keepdims=True)

    in_groups = lane < N_GROUPS
    pg = masked_softmax(in_groups)
    g_p = jnp.max(pg, axis=1, keepdims=True)
    g_idx = _lane_pick(jnp.where(in_groups, pg, -1.0) == g_p, lanef)
    lo = float(N_GROUPS) + float(EXPERTS_PER_GROUP) * g_idx
    half = 0.5 * float(EXPERTS_PER_GROUP - 1)
    in_group = jnp.abs(lanef - lo - half) < half + 0.5
    pe = jnp.where(in_group, masked_softmax(in_group), -1.0)
    p1 = jnp.max(pe, axis=1, keepdims=True)
    i1 = _lane_pick(pe == p1, lanef)
    pe2 = jnp.where(lanef == i1, -1.0, pe)
    p2 = jnp.max(pe2, axis=1, keepdims=True)
    i2 = _lane_pick(pe2 == p2, lanef)
    den = p1 + p2
    gate = (g_p * (p1 / den), g_p * (p2 / den))
    expert = (i1 - float(N_GROUPS), i2 - float(N_GROUPS))

    earlier = jnp.where(lax.broadcasted_iota(jnp.int32, (tm, tm), 1)
                        < lax.broadcasted_iota(jnp.int32, (tm, tm), 0), 1.0, 0.0).astype(BF16)
    run = run_ref[...]
    rank = []
    for e in expert:
        onehot = jnp.where(lanef == e, 1.0, 0.0)
        before = jnp.dot(earlier, onehot.astype(BF16), preferred_element_type=F32) + run
        rank.append(jnp.sum(onehot * before, axis=1, keepdims=True))
        run = run + jnp.sum(onehot, axis=0, keepdims=True)
    run_ref[...] = run
    cnt_ref[...] = run
    cols = (expert[0], expert[1], rank[0], rank[1], gate[0], gate[1])
    route = jnp.zeros((tm, LANES), F32)
    for c, val in enumerate(cols):
        route = jnp.where(lane == c, val, route)
    route_ref[...] = route


def _router(x, w_router, b_router, tm=256):
    t, d = x.shape
    tm = min(tm, t)
    return pl.pallas_call(
        functools.partial(_router_kernel, tm=tm),
        grid=(t // tm,),
        in_specs=[pl.BlockSpec((tm, d), lambda i: (i, 0)),
                  pl.BlockSpec((d, LANES), lambda i: (0, 0)),
                  pl.BlockSpec((1, LANES), lambda i: (0, 0))],
        out_specs=[pl.BlockSpec((tm, LANES), lambda i: (i, 0)),
                   pl.BlockSpec((1, LANES), lambda i: (0, 0))],
        out_shape=[jax.ShapeDtypeStruct((t, LANES), F32), jax.ShapeDtypeStruct((1, LANES), F32)],
        scratch_shapes=[pltpu.VMEM((1, LANES), F32)],
        compiler_params=_params("arbitrary"),
        name="router",
    )(x, w_router, b_router)


SLAB = 2048 // LANES


def _slab(ref, r):
    return ref.at[pl.ds(pl.multiple_of(r * SLAB, SLAB), SLAB)]


def _slab_rows(ref, rows):
    return jnp.concatenate([ref[pl.ds(c, rows, stride=SLAB), :] for c in range(SLAB)], axis=1)


def _store_slab_rows(ref, val, rows):
    for c in range(SLAB):
        ref[pl.ds(c, rows, stride=SLAB), :] = val[:, c * LANES:(c + 1) * LANES]


def _expert_kernel(be_ref, nu_ref, idx_ref, nxt_ref, x_hbm, wg_ref, wu_ref, wd_ref, o_ref,
                   xbuf, sems, wg_b, wu_b, wd_b, *, rows):
    i = pl.program_id(0)
    n_used = nu_ref[0]
    slot = i % 2

    def token_copy(ids_ref, r, dst_slot):
        return pltpu.make_async_copy(_slab(x_hbm, ids_ref[0, r]), _slab(xbuf.at[dst_slot], r), sems.at[dst_slot])

    def gather(ids_ref, dst_slot):
        def start(r, carry):
            token_copy(ids_ref, r, dst_slot).start()
            return carry
        lax.fori_loop(0, rows, start, 0)

    @pl.when(i == 0)
    def _():
        gather(idx_ref, 0)

    @pl.when(i + 1 < n_used)
    def _():
        gather(nxt_ref, 1 - slot)

    @pl.when(jnp.logical_or(i == 0, be_ref[i] != be_ref[jnp.maximum(i - 1, 0)]))
    def _():
        wg_b[...] = wg_ref[...].astype(BF16)
        wu_b[...] = wu_ref[...].astype(BF16)
        wd_b[...] = wd_ref[...].astype(BF16)

    @pl.when(i < n_used)
    def _():
        def wait(r, carry):
            token_copy(idx_ref, r, slot).wait()
            return carry
        lax.fori_loop(0, rows, wait, 0)
        xb = _slab_rows(xbuf.at[slot], rows).astype(BF16)
        hg = jnp.dot(xb, wg_b[...], preferred_element_type=F32)
        hu = jnp.dot(xb, wu_b[...], preferred_element_type=F32)
        hid = (hg * jax.nn.sigmoid(hg) * hu).astype(BF16)
        _store_slab_rows(o_ref, jnp.dot(hid, wd_b[...], preferred_element_type=F32), rows)

    @pl.when(i >= n_used)
    def _():
        o_ref[...] = jnp.zeros_like(o_ref)


def _expert_mlp(x_slabs, src_row, block_e, n_used, w_gate, w_up, w_down, layer, rows=MOE_ROWS):
    d, ff = w_gate.shape[2], w_gate.shape[3]
    nb = src_row.shape[0] // rows
    ids = src_row.reshape(nb, 1, rows)
    grid_spec = pltpu.PrefetchScalarGridSpec(
        num_scalar_prefetch=2,
        grid=(nb,),
        in_specs=[pl.BlockSpec((None, 1, rows), lambda i, be, nu: (i, 0, 0), memory_space=pltpu.SMEM),
                  pl.BlockSpec((None, 1, rows), lambda i, be, nu: (jnp.minimum(i + 1, nb - 1), 0, 0),
                               memory_space=pltpu.SMEM),
                  pl.BlockSpec(memory_space=pl.ANY),
                  pl.BlockSpec((None, None, d, ff), lambda i, be, nu: (layer, be[i], 0, 0)),
                  pl.BlockSpec((None, None, d, ff), lambda i, be, nu: (layer, be[i], 0, 0)),
                  pl.BlockSpec((None, None, ff, d), lambda i, be, nu: (layer, be[i], 0, 0))],
        out_specs=pl.BlockSpec((rows * SLAB, LANES), lambda i, be, nu: (i, 0)),
        scratch_shapes=[pltpu.VMEM((2, rows * SLAB, LANES), F32), pltpu.SemaphoreType.DMA((2,)),
                        pltpu.VMEM((d, ff), BF16), pltpu.VMEM((d, ff), BF16), pltpu.VMEM((ff, d), BF16)],
    )
    return pl.pallas_call(
        functools.partial(_expert_kernel, rows=rows),
        grid_spec=grid_spec,
        out_shape=jax.ShapeDtypeStruct((nb * rows * SLAB, LANES), F32),
        compiler_params=_params("arbitrary"),
        name="expert_mlp",
    )(block_e, n_used, ids, ids, x_slabs, w_gate, w_up, w_down)


def _combine_ln_kernel(idx_ref, nxt_ref, y_ref, x_ref, gt_ref, g_ref, b_ref, o_ref, buf, sems, *, rows):
    i = pl.program_id(0)
    slot = i % 2

    def row_copy(ids_ref, r, kslot, dst_slot):
        return pltpu.make_async_copy(_slab(y_ref, ids_ref[0, MOE_TOPK * r + kslot]),
                                     _slab(buf.at[dst_slot, kslot], r), sems.at[dst_slot])

    def gather(ids_ref, dst_slot):
        def start(r, carry):
            for kslot in range(MOE_TOPK):
                row_copy(ids_ref, r, kslot, dst_slot).start()
            return carry
        lax.fori_loop(0, rows, start, 0)

    @pl.when(i == 0)
    def _():
        gather(idx_ref, 0)

    @pl.when(i + 1 < pl.num_programs(0))
    def _():
        gather(nxt_ref, 1 - slot)

    def wait(r, carry):
        for kslot in range(MOE_TOPK):
            row_copy(idx_ref, r, kslot, slot).wait()
        return carry
    lax.fori_loop(0, rows, wait, 0)
    gt = gt_ref[...]
    ffn = gt[:, 0:1] * _slab_rows(buf.at[slot, 0], rows)
    for kslot in range(1, MOE_TOPK):
        ffn = ffn + gt[:, kslot:kslot + 1] * _slab_rows(buf.at[slot, kslot], rows)
    o_ref[...] = _ln_rows(DN_ALPHA * x_ref[...] + ffn, g_ref[...], b_ref[...])


def _combine_ln(y_sorted, dest, gates, x, g, b, rows=256):
    t, d = x.shape
    nb = t // rows
    row = pl.BlockSpec((rows, d), lambda i: (i, 0))
    vec = pl.BlockSpec((1, d), lambda i: (0, 0))
    ids = dest.reshape(nb, 1, rows * MOE_TOPK)
    return pl.pallas_call(
        functools.partial(_combine_ln_kernel, rows=rows),
        grid=(nb,),
        in_specs=[pl.BlockSpec((None, 1, rows * MOE_TOPK), lambda i: (i, 0, 0), memory_space=pltpu.SMEM),
                  pl.BlockSpec((None, 1, rows * MOE_TOPK), lambda i: (jnp.minimum(i + 1, nb - 1), 0, 0),
                               memory_space=pltpu.SMEM),
                  pl.BlockSpec(memory_space=pl.ANY), row,
                  pl.BlockSpec((rows, MOE_TOPK), lambda i: (i, 0)), vec, vec],
        out_specs=row,
        out_shape=jax.ShapeDtypeStruct((t, d), F32),
        scratch_shapes=[pltpu.VMEM((2, MOE_TOPK, rows * SLAB, LANES), F32), pltpu.SemaphoreType.DMA((2,))],
        compiler_params=_params("arbitrary"),
        name="combine_ln",
    )(ids, ids, y_sorted, x, gates, g.reshape(1, d), b.reshape(1, d))


def _moe_layer(x, x_slabs, w_grp, b_grp, w_exp_r, b_exp_r, w_gate, w_up, w_down, ln_g, ln_b, layer):
    t, d = x.shape
    n_r = N_GROUPS + N_EXPERTS
    w_router = jnp.pad(jnp.concatenate([w_grp, w_exp_r], axis=1), ((0, 0), (0, LANES - n_r))).astype(BF16)
    b_router = jnp.pad(jnp.concatenate([b_grp, b_exp_r]), (0, LANES - n_r)).reshape(1, LANES)
    route, cnt = _router(x, w_router, b_router)
    flat_e = route[:, 0:MOE_TOPK].astype(jnp.int32).reshape(-1)
    rank = route[:, MOE_TOPK:2 * MOE_TOPK].astype(jnp.int32).reshape(-1)
    gates = route[:, 2 * MOE_TOPK:3 * MOE_TOPK]
    counts = cnt[0, :N_EXPERTS].astype(jnp.int32)
    m = t * MOE_TOPK
    padded = (counts + MOE_ROWS - 1) // MOE_ROWS * MOE_ROWS
    pad_end = jnp.cumsum(padded)
    dest = (pad_end - padded)[flat_e] + rank
    n_blocks = (m + N_EXPERTS * (MOE_ROWS - 1) + MOE_ROWS - 1) // MOE_ROWS
    src_row = jnp.zeros((n_blocks * MOE_ROWS,), jnp.int32).at[dest].set(
        (jnp.arange(m) // MOE_TOPK).astype(jnp.int32))
    starts = jnp.arange(n_blocks, dtype=jnp.int32) * MOE_ROWS
    block_e = jnp.minimum(jnp.sum((pad_end[None, :] <= starts[:, None]).astype(jnp.int32), axis=1),
                          N_EXPERTS - 1).astype(jnp.int32)
    n_used = (pad_end[-1] // MOE_ROWS).astype(jnp.int32).reshape(1)

    ys = _expert_mlp(x_slabs, src_row, block_e, n_used, w_gate, w_up, w_down, layer)
    return _combine_ln(ys, dest.astype(jnp.int32), gates, x, ln_g, ln_b)


def kernel(x, ev_w_in, ev_gate_bias, ev_conv_w, ev_conv_b, ev_hnorm_g, ev_w_out, od_mu, od_w_r, od_w_k, od_w_v, od_w_o, od_w0, od_w1, od_w2, od_a0, od_a1, od_a2, od_g1, od_g2, od_k_k, od_k_a, od_r_k, od_lnx_g, od_lnx_b, od_v0, od_v1, od_v2, ln_mix_g, ln_mix_b, ln_ffn_g, ln_ffn_b, moe_w_grp, moe_b_grp, moe_w_exp_r, moe_b_exp_r, moe_w_gate, moe_w_up, moe_w_down):
    bsz, s, d = x.shape
    depth = ln_mix_g.shape[0]
    xf = x.reshape(bsz * s, d)
    v_first = None
    for layer in range(depth):
        if layer % 2 == 0:
            e = layer // 2
            mix = _even_mixer(xf, ev_w_in[e], ev_gate_bias[e], ev_conv_w[e], ev_conv_b[e],
                              ev_hnorm_g[e], ev_w_out[e], bsz, s)
        else:
            o = layer // 2
            vres = None if o == 0 else (od_v0[o - 1], od_v1[o - 1], od_v2[o - 1])
            mix, v_first = _rwkv7_mixer(xf, v_first, od_mu[o], od_w_r[o], od_w_k[o], od_w_v[o],
                                        od_w_o[o], od_w0[o], od_w1[o], od_w2[o], od_a0[o],
                                        od_a1[o], od_a2[o], od_g1[o], od_g2[o], od_k_k[o],
                                        od_k_a[o], od_r_k[o], od_lnx_g[o], od_lnx_b[o], vres, bsz, s)
        xf, x_slabs = _resid_ln(xf, mix, ln_mix_g[layer], ln_mix_b[layer])
        xf = _moe_layer(xf, x_slabs, moe_w_grp[layer], moe_b_grp[layer], moe_w_exp_r[layer],
                        moe_b_exp_r[layer], moe_w_gate, moe_w_up, moe_w_down,
                        ln_ffn_g[layer], ln_ffn_b[layer], layer)
    return xf.reshape(bsz, s, d)
```

```python
import functools
import math

import jax
import jax.numpy as jnp
from jax import lax
from jax.experimental import pallas as pl
from jax.experimental.pallas import tpu as pltpu

F32 = jnp.float32
BF16 = jnp.bfloat16

DEPTH = 4
DN_ALPHA = (2 * DEPTH) ** 0.25
LN_EPS = 1e-5

A_HEADS = 8
A_HEAD_DIM = 128
A_WIDTH = A_HEADS * A_HEAD_DIM
MOBA_BLOCK = 256
MOBA_TOPK = 3
MOBA_GROUP = 4
ROPE_THETA = 500000.0
ROPE_DIM = A_HEAD_DIM // 4
NEG_INF = -1e30

B_HEADS = 4
B_HEAD_DIM = 256
B_WIDTH = B_HEADS * B_HEAD_DIM
MLSTM_CHUNK = 64
B_CONV = 4
HNORM_EPS = 1e-6
EVEN_MAIN = 3 * A_WIDTH + 4 * B_WIDTH

C_HEAD_DIM = 64
LNX_EPS = 64e-5
KK_EPS = 1e-12

N_GROUPS = 4
EXPERTS_PER_GROUP = 8
N_EXPERTS = N_GROUPS * EXPERTS_PER_GROUP
MOE_TOPK = 2
MOE_ROWS = 256

LANES = 128
SUBLANES = 8
VMEM_LIMIT = 56 * 1024 * 1024

_NT = (((1,), (1,)), ((), ()))
_TN = (((0,), (0,)), ((), ()))


def _params(*sem):
    return pltpu.CompilerParams(dimension_semantics=sem, vmem_limit_bytes=VMEM_LIMIT)


def _mm_kernel(x_ref, w_ref, o_ref):
    o_ref[...] = jnp.dot(x_ref[...].astype(BF16), w_ref[...],
                         preferred_element_type=F32).astype(o_ref.dtype)


def _matmul(x, w, out_dtype=F32, tm=1024, tn=512):
    m, k = x.shape
    n = w.shape[1]
    tm, tn = min(tm, m), min(tn, n)
    return pl.pallas_call(
        _mm_kernel,
        grid=(m // tm, n // tn),
        in_specs=[pl.BlockSpec((tm, k), lambda i, j: (i, 0)),
                  pl.BlockSpec((k, tn), lambda i, j: (0, j))],
        out_specs=pl.BlockSpec((tm, tn), lambda i, j: (i, j)),
        out_shape=jax.ShapeDtypeStruct((m, n), out_dtype),
        compiler_params=_params("parallel", "arbitrary"),
        name="matmul",
    )(x, w)


def _mm2_kernel(xa_ref, xb_ref, wa_ref, wb_ref, o_ref):
    acc = jnp.dot(xa_ref[...], wa_ref[...], preferred_element_type=F32)
    acc = acc + jnp.dot(xb_ref[...], wb_ref[...], preferred_element_type=F32)
    o_ref[...] = acc


def _matmul2(xa, xb, wa, wb, tm=1024, tn=512):
    m, ka = xa.shape
    kb = xb.shape[1]
    n = wa.shape[1]
    tm, tn = min(tm, m), min(tn, n)
    return pl.pallas_call(
        _mm2_kernel,
        grid=(m // tm, n // tn),
        in_specs=[pl.BlockSpec((tm, ka), lambda i, j: (i, 0)),
                  pl.BlockSpec((tm, kb), lambda i, j: (i, 0)),
                  pl.BlockSpec((ka, tn), lambda i, j: (0, j)),
                  pl.BlockSpec((kb, tn), lambda i, j: (0, j))],
        out_specs=pl.BlockSpec((tm, tn), lambda i, j: (i, j)),
        out_shape=jax.ShapeDtypeStruct((m, n), F32),
        compiler_params=_params("parallel", "arbitrary"),
        name="matmul2",
    )(xa, xb, wa, wb)


def _ln_rows(h, g, b):
    mu = jnp.mean(h, axis=-1, keepdims=True)
    hc = h - mu
    var = jnp.mean(hc * hc, axis=-1, keepdims=True)
    return hc * lax.rsqrt(var + LN_EPS) * g + b


def _resid_ln_kernel(x_ref, m_ref, g_ref, b_ref, o_ref, s_ref, *, tm):
    y = _ln_rows(DN_ALPHA * x_ref[...] + m_ref[...], g_ref[...], b_ref[...])
    o_ref[...] = y
    _store_slab_rows(s_ref, y, tm)


def _resid_ln(x, mix, g, b, tm=256):
    t, d = x.shape
    tm = min(tm, t)
    row = pl.BlockSpec((tm, d), lambda i: (i, 0))
    vec = pl.BlockSpec((1, d), lambda i: (0, 0))
    return pl.pallas_call(
        functools.partial(_resid_ln_kernel, tm=tm),
        grid=(t // tm,),
        in_specs=[row, row, vec, vec],
        out_specs=[row, pl.BlockSpec((tm * SLAB, LANES), lambda i: (i, 0))],
        out_shape=[jax.ShapeDtypeStruct((t, d), F32), jax.ShapeDtypeStruct((t * SLAB, LANES), F32)],
        compiler_params=_params("parallel"),
        name="resid_ln",
    )(x, mix, g.reshape(1, d), b.reshape(1, d))


def _rope_tables(s):
    half = ROPE_DIM // 2
    inv = jnp.power(jnp.float32(ROPE_THETA), -jnp.arange(half, dtype=F32) / half)
    ang = jnp.arange(s).astype(F32)[:, None] * inv[None, :]
    cos, sin = jnp.cos(ang), jnp.sin(ang)
    zero = jnp.zeros((s, A_HEAD_DIM - ROPE_DIM), F32)
    zh = jnp.zeros((s, half), F32)
    c = jnp.concatenate([cos, cos, zero + 1.0], axis=-1)
    s_up = jnp.concatenate([zh, sin, zero], axis=-1)
    s_dn = jnp.concatenate([-sin, zh, zero], axis=-1)
    return c, s_up, s_dn


def _rope(x, c, s_up, s_dn):
    half = ROPE_DIM // 2
    return (x * c + pltpu.roll(x, half, axis=1) * s_up
            + pltpu.roll(x, A_HEAD_DIM - half, axis=1) * s_dn)


def _moba_kernel(q_ref, k_ref, v_ref, c_ref, su_ref, sd_ref, o_ref, krot_ref, vb_ref, kmean_ref, *, nblk):
    qi = pl.program_id(2)
    blk = MOBA_BLOCK
    scale = A_HEAD_DIM ** -0.5

    @pl.when(qi == 0)
    def _prepare_keys():
        def body(j, carry):
            r = pl.multiple_of(j * blk, blk)
            rows = pl.ds(r, blk)
            kr = _rope(k_ref[rows, :].astype(F32), c_ref[rows, :], su_ref[rows, :], sd_ref[rows, :])
            block_tag = jnp.where(lax.broadcasted_iota(jnp.int32, (blk, LANES), 1) == j, 1.0, 0.0)
            krot_ref[rows, :] = jnp.concatenate([kr, block_tag], axis=1).astype(BF16)
            vb_ref[rows, :] = v_ref[rows, :].astype(BF16)
            kmean_ref[pl.ds(j, 1), :] = jnp.mean(kr, axis=0, keepdims=True)
            return carry
        lax.fori_loop(0, nblk, body, 0)

    r0 = pl.multiple_of(qi * blk, blk)
    qrows = pl.ds(r0, blk)
    qb = _rope(q_ref[...].astype(F32), c_ref[qrows, :], su_ref[qrows, :], sd_ref[qrows, :]).astype(BF16)

    g = lax.dot_general(kmean_ref[...].astype(BF16), qb, _NT, preferred_element_type=F32)
    brow = lax.broadcasted_iota(jnp.int32, (nblk, blk), 0)
    g = jnp.where(brow < qi, g, -jnp.inf)
    sel_t = jnp.zeros((nblk, blk), F32)
    for j in range(nblk):
        gj = g[j:j + 1, :]
        earlier = jnp.where(brow < j, 1.0, 0.0)
        beats = jnp.where(g > gj, 1.0, jnp.where(g == gj, earlier, 0.0))
        cnt = jnp.sum(beats, axis=0, keepdims=True)
        sel_t = jnp.where(brow == j, jnp.where(cnt < float(min(MOBA_TOPK, nblk)), 1.0, 0.0), sel_t)
    sel_t = jnp.where(brow < qi, sel_t, 0.0)
    eye = jnp.where(lax.broadcasted_iota(jnp.int32, (blk, blk), 0)
                    == lax.broadcasted_iota(jnp.int32, (blk, blk), 1), 1.0, 0.0).astype(BF16)
    sel_pad = jnp.concatenate([sel_t, jnp.zeros((LANES - nblk, blk), F32)], axis=0).astype(BF16)
    sel = lax.dot_general(eye, sel_pad, _NT, preferred_element_type=F32)
    lane = lax.broadcasted_iota(jnp.int32, (blk, LANES), 1)
    bias = jnp.where(sel > 0.5, 0.0, jnp.where(lane < nblk, NEG_INF / scale, 0.0))
    q_aug = jnp.concatenate([qb, bias.astype(BF16)], axis=1)

    kd = krot_ref[qrows, :A_HEAD_DIM]
    s = lax.dot_general(qb, kd, _NT, preferred_element_type=F32) * scale
    causal = (lax.broadcasted_iota(jnp.int32, (blk, blk), 1)
              <= lax.broadcasted_iota(jnp.int32, (blk, blk), 0))
    s = jnp.where(causal, s, NEG_INF)
    m0 = jnp.max(s, axis=1, keepdims=True)
    p = jnp.exp(s - m0)
    l0 = jnp.sum(p, axis=1, keepdims=True)
    acc0 = jnp.dot(p.astype(BF16), vb_ref[qrows, :], preferred_element_type=F32)

    span = MOBA_GROUP * blk

    def past_group(jg, carry):
        m, l, acc = carry
        rows = pl.ds(pl.multiple_of(jg * span, span), span)
        sj = lax.dot_general(q_aug, krot_ref[rows, :], _NT, preferred_element_type=F32) * scale
        m_new = jnp.maximum(m, jnp.max(sj, axis=1, keepdims=True))
        a = jnp.exp(m - m_new)
        pj = jnp.exp(sj - m_new)
        l = a * l + jnp.sum(pj, axis=1, keepdims=True)
        acc = a * acc + jnp.dot(pj.astype(BF16), vb_ref[rows, :], preferred_element_type=F32)
        return m_new, l, acc

    n_groups = (qi + MOBA_GROUP - 1) // MOBA_GROUP
    _, l, acc = lax.fori_loop(0, n_groups, past_group, (m0, l0, acc0))
    o_ref[...] = (acc / l).astype(o_ref.dtype)


def _moba(z, bsz, s):
    nblk = s // MOBA_BLOCK
    c, s_up, s_dn = _rope_tables(s)
    tab = pl.BlockSpec((s, A_HEAD_DIM), lambda b, h, qi: (0, 0))
    return pl.pallas_call(
        functools.partial(_moba_kernel, nblk=nblk),
        grid=(bsz, A_HEADS, nblk),
        in_specs=[pl.BlockSpec((MOBA_BLOCK, A_HEAD_DIM), lambda b, h, qi: (b * nblk + qi, h)),
                  pl.BlockSpec((s, A_HEAD_DIM), lambda b, h, qi: (b, A_HEADS + h)),
                  pl.BlockSpec((s, A_HEAD_DIM), lambda b, h, qi: (b, 2 * A_HEADS + h)),
                  tab, tab, tab],
        out_specs=pl.BlockSpec((MOBA_BLOCK, A_HEAD_DIM), lambda b, h, qi: (b * nblk + qi, h)),
        out_shape=jax.ShapeDtypeStruct((bsz * s, A_WIDTH), BF16),
        scratch_shapes=[pltpu.VMEM((s, A_HEAD_DIM + LANES), BF16), pltpu.VMEM((s, A_HEAD_DIM), BF16),
                        pltpu.VMEM((nblk, A_HEAD_DIM), F32)],
        compiler_params=_params("parallel", "parallel", "arbitrary"),
        name="moba",
    )(z, z, z, c, s_up, s_dn)


def _log_sigmoid(x):
    return jnp.minimum(x, 0.0) - jnp.log1p(jnp.exp(-jnp.abs(x)))


def _shift_rows(x, prev8, d):
    if d == 0:
        return x
    rolled = pltpu.roll(x, d, axis=0)
    top = jnp.where(lax.broadcasted_iota(jnp.int32, prev8.shape, 0) < d,
                    pltpu.roll(prev8, d, axis=0), rolled[:SUBLANES])
    return jnp.concatenate([top, rolled[SUBLANES:]], axis=0)


def _mlstm_kernel(q_ref, k_ref, v_ref, o_ref, gc_ref, gr_ref, bc_ref, br_ref,
                  cwq_ref, cwk_ref, cbq_ref, cbk_ref, hg_ref, y_ref,
                  c_ref, n_ref, m_ref, pq_ref, pk_ref, *, tt):
    h = pl.program_id(1)
    ti = pl.program_id(2)
    L = MLSTM_CHUNK
    dk = B_HEAD_DIM

    @pl.when(ti == 0)
    def _reset():
        c_ref[...] = jnp.zeros_like(c_ref)
        n_ref[...] = jnp.zeros_like(n_ref)
        m_ref[...] = jnp.zeros_like(m_ref)
        pq_ref[...] = jnp.zeros_like(pq_ref)
        pk_ref[...] = jnp.zeros_like(pk_ref)

    def conv_silu(x_ref, p_ref, w_ref, b_ref):
        x = x_ref[...].astype(F32)
        prev8 = p_ref[...]
        out = b_ref[...] + x * w_ref[B_CONV - 1:B_CONV, :]
        for d in range(1, B_CONV):
            out = out + _shift_rows(x, prev8, d) * w_ref[B_CONV - 1 - d:B_CONV - d, :]
        p_ref[...] = x[tt - SUBLANES:, :]
        return out * jax.nn.sigmoid(out)

    q_all = conv_silu(q_ref, pq_ref, cwq_ref, cbq_ref)
    k_all = conv_silu(k_ref, pk_ref, cwk_ref, cbk_ref) * (dk ** -0.5)

    lane8 = lax.broadcasted_iota(jnp.int32, (tt, 2 * B_HEADS), 1)
    gcol = gc_ref[...] + bc_ref[...]
    i_col_all = jnp.sum(jnp.where(lane8 == h, gcol, 0.0), axis=1, keepdims=True)
    f_col_all = _log_sigmoid(jnp.sum(jnp.where(lane8 == B_HEADS + h, gcol, 0.0), axis=1, keepdims=True))
    i_row_all = gr_ref[pl.ds(h, 1), :] + br_ref[pl.ds(h, 1), :]
    f_row_all = _log_sigmoid(gr_ref[pl.ds(B_HEADS + h, 1), :] + br_ref[pl.ds(B_HEADS + h, 1), :])

    rr = lax.broadcasted_iota(jnp.int32, (L, L), 0)
    cc = lax.broadcasted_iota(jnp.int32, (L, L), 1)
    tri = cc <= rr

    for c in range(tt // L):
        lo, hi = c * L, (c + 1) * L
        qc = q_all[lo:hi]
        kc = k_all[lo:hi]
        qcb = qc.astype(BF16)
        vcb = v_ref[lo:hi, :].astype(BF16)
        i_col, f_col = i_col_all[lo:hi], f_col_all[lo:hi]
        i_row, f_row = i_row_all[:, lo:hi], f_row_all[:, lo:hi]
        m_prev = m_ref[...]
        cst = c_ref[...]
        nst = n_ref[...]

        b_col = jnp.sum(jnp.where(tri, f_row, 0.0), axis=1, keepdims=True)
        b_row = jnp.sum(jnp.where(rr <= cc, f_col, 0.0), axis=0, keepdims=True)
        log_inter = b_col + m_prev
        log_intra = jnp.where(tri, b_col - b_row + i_row, -jnp.inf)
        m_t = jnp.maximum(log_inter, jnp.max(log_intra, axis=1, keepdims=True))
        w_intra = jnp.exp(log_intra - m_t)
        w_inter = jnp.exp(log_inter - m_t)
        qk = lax.dot_general(qcb, kc.astype(BF16), _NT, preferred_element_type=F32) * w_intra
        num = (w_inter * jnp.dot(qcb, cst.astype(BF16), preferred_element_type=F32)
               + jnp.dot(qk.astype(BF16), vcb, preferred_element_type=F32))
        den = (w_inter * jnp.sum(qc * nst, axis=1, keepdims=True)
               + jnp.sum(qk, axis=1, keepdims=True))
        hc = num / jnp.maximum(jnp.abs(den), jnp.exp(-m_t))

        b_last = jnp.sum(f_col, axis=0, keepdims=True)
        log_s = b_last - b_col + i_col
        m_new = jnp.maximum(b_last + m_prev, jnp.max(log_s, axis=0, keepdims=True))
        decay = jnp.exp(b_last + m_prev - m_new)
        kw = kc * jnp.exp(log_s - m_new)
        c_ref[...] = decay * cst + lax.dot_general(kw.astype(BF16), vcb, _TN, preferred_element_type=F32)
        n_ref[...] = decay * nst + jnp.sum(kw, axis=0, keepdims=True)
        m_ref[...] = m_new

        hn = hc * lax.rsqrt(jnp.mean(hc * hc, axis=1, keepdims=True) + HNORM_EPS)
        y_ref[lo:hi, :] = (hn * hg_ref[...] * jax.nn.sigmoid(o_ref[lo:hi, :].astype(F32))).astype(y_ref.dtype)


def _mlstm(z, gates, gate_bias, conv_w, conv_b, hnorm_g, bsz, s, tt=256):
    nt = s // tt
    cb = 3 * A_WIDTH // B_HEAD_DIM
    gates_t = gates.T
    bias = gate_bias.reshape(1, 2 * B_HEADS)

    def col(off):
        return pl.BlockSpec((tt, B_HEAD_DIM), lambda b, h, t: (b * nt + t, cb + off + h))

    def par(rows, off):
        return pl.BlockSpec((rows, B_HEAD_DIM), lambda b, h, t: (0, off + h))

    return pl.pallas_call(
        functools.partial(_mlstm_kernel, tt=tt),
        grid=(bsz, B_HEADS, nt),
        in_specs=[col(0), col(B_HEADS), col(2 * B_HEADS), col(3 * B_HEADS),
                  pl.BlockSpec((tt, 2 * B_HEADS), lambda b, h, t: (b * nt + t, 0)),
                  pl.BlockSpec((2 * B_HEADS, tt), lambda b, h, t: (0, b * nt + t)),
                  pl.BlockSpec((1, 2 * B_HEADS), lambda b, h, t: (0, 0)),
                  pl.BlockSpec((2 * B_HEADS, 1), lambda b, h, t: (0, 0)),
                  par(B_CONV, 0), par(B_CONV, B_HEADS), par(1, 0), par(1, B_HEADS), par(1, 0)],
        out_specs=pl.BlockSpec((tt, B_HEAD_DIM), lambda b, h, t: (b * nt + t, h)),
        out_shape=jax.ShapeDtypeStruct((bsz * s, B_WIDTH), BF16),
        scratch_shapes=[pltpu.VMEM((B_HEAD_DIM, B_HEAD_DIM), F32), pltpu.VMEM((1, B_HEAD_DIM), F32),
                        pltpu.VMEM((1, 1), F32), pltpu.VMEM((SUBLANES, B_HEAD_DIM), F32),
                        pltpu.VMEM((SUBLANES, B_HEAD_DIM), F32)],
        compiler_params=_params("parallel", "parallel", "arbitrary"),
        name="mlstm",
    )(z, z, z, z, gates, gates_t, bias, bias.T, conv_w, conv_w,
      conv_b.reshape(1, -1), conv_b.reshape(1, -1), hnorm_g.reshape(1, -1))


def _even_mixer(x, w_in, gate_bias, conv_w, conv_b, hnorm_g, w_out, bsz, s):
    d = x.shape[1]
    w_main = w_in[:, :EVEN_MAIN].astype(BF16)
    w_gate = jnp.pad(w_in[:, EVEN_MAIN:], ((0, 0), (0, LANES - 2 * B_HEADS))).astype(BF16)
    z = _matmul(x, w_main, out_dtype=BF16)
    gates = _matmul(x, w_gate, tn=LANES)[:, :2 * B_HEADS]
    y_a = _moba(z, bsz, s)
    y_b = _mlstm(z, gates, gate_bias, conv_w, conv_b, hnorm_g, bsz, s)
    w_o = w_out.astype(BF16)
    return _matmul2(y_a, y_b, w_o[:A_WIDTH], w_o[A_WIDTH:])


def _shift_mix_kernel(x_ref, p_ref, mu_ref, *o_refs, tiles_per_seq):
    x = x_ref[...]
    first = (pl.program_id(0) % tiles_per_seq) == 0
    prev_row = jnp.where(first, 0.0, p_ref[SUBLANES - 1:SUBLANES, :])
    row = lax.broadcasted_iota(jnp.int32, x.shape, 0)
    dx = jnp.where(row == 0, prev_row, pltpu.roll(x, 1, axis=0)) - x
    for j, o_ref in enumerate(o_refs):
        o_ref[...] = (x + dx * mu_ref[j:j + 1, :]).astype(o_ref.dtype)


def _shift_mix(x, mu, s, tm=256):
    t, d = x.shape
    n = mu.shape[0]
    row = pl.BlockSpec((tm, d), lambda i: (i, 0))
    prev = pl.BlockSpec((SUBLANES, d), lambda i: (jnp.maximum(i * (tm // SUBLANES) - 1, 0), 0))
    return pl.pallas_call(
        functools.partial(_shift_mix_kernel, tiles_per_seq=s // tm),
        grid=(t // tm,),
        in_specs=[row, prev, pl.BlockSpec((n, d), lambda i: (0, 0))],
        out_specs=[row] * n,
        out_shape=[jax.ShapeDtypeStruct((t, d), BF16)] * n,
        compiler_params=_params("parallel"),
        name="shift_mix",
    )(x, x, mu)


def _softplus(x):
    return jnp.maximum(x, 0.0) + jnp.log1p(jnp.exp(-jnp.abs(x)))


def _lora_kernel(x_ref, w1_ref, w2_ref, b_ref, o_ref, *, mid, out):
    hmid = jnp.dot(x_ref[...], w1_ref[...], preferred_element_type=F32)
    if mid == "tanh":
        hmid = jnp.tanh(hmid)
    elif mid == "sigmoid":
        hmid = jax.nn.sigmoid(hmid)
    y = jnp.dot(hmid.astype(BF16), w2_ref[...], preferred_element_type=F32)
    if out == "logdecay":
        y = -jnp.exp(-_softplus(-(b_ref[...] + y)) - 0.5)
    elif out == "sigmoid":
        y = jax.nn.sigmoid(b_ref[...] + y)
    o_ref[...] = y


def _lora(x, w1, w2, bias, mid, out, tm=512):
    t, d = x.shape
    tm = min(tm, t)
    r = w1.shape[1]
    rp = -(-r // LANES) * LANES
    w1p = jnp.pad(w1, ((0, 0), (0, rp - r))).astype(BF16)
    w2p = jnp.pad(w2, ((0, rp - r), (0, 0))).astype(BF16)
    dout = w2.shape[1]
    if bias is None:
        bias = jnp.zeros((dout,), F32)
    return pl.pallas_call(
        functools.partial(_lora_kernel, mid=mid, out=out),
        grid=(t // tm,),
        in_specs=[pl.BlockSpec((tm, d), lambda i: (i, 0)),
                  pl.BlockSpec((d, rp), lambda i: (0, 0)),
                  pl.BlockSpec((rp, dout), lambda i: (0, 0)),
                  pl.BlockSpec((1, dout), lambda i: (0, 0))],
        out_specs=pl.BlockSpec((tm, dout), lambda i: (i, 0)),
        out_shape=jax.ShapeDtypeStruct((t, dout), F32),
        compiler_params=_params("parallel"),
        name="lora_" + out,
    )(x, w1p, w2p, bias.reshape(1, dout))


def _head_sums(tiles, ones_b):
    c = tiles[0].shape[0]
    x = jnp.concatenate(tiles, axis=0)
    hi = x.astype(BF16)
    lo = (x - hi.astype(F32)).astype(BF16)
    tot = jnp.dot(hi, ones_b, preferred_element_type=F32) + jnp.dot(lo, ones_b, preferred_element_type=F32)
    return [tot[p * c:(p + 1) * c] for p in range(len(tiles))]


def _bdot(a, b):
    return jnp.dot(a.astype(BF16), b.astype(BF16), preferred_element_type=F32)


def _rwkv_chunk_pairs(r, ld, kh, v, kk, bb, g_cum, s0, c):
    pairs = range(len(r))
    lane = lax.broadcasted_iota(jnp.int32, (1, LANES), 1)
    m0 = jnp.where(lane < C_HEAD_DIM, 1.0, 0.0)
    m1 = 1.0 - m0
    n2 = 2 * c
    row = lax.broadcasted_iota(jnp.int32, (n2, n2), 0)
    col = lax.broadcasted_iota(jnp.int32, (n2, n2), 1)
    lower, lower_eq = col < row, col <= row

    def stack(x):
        return jnp.concatenate([x * m0, x * m1], axis=0)

    e_g = [jnp.exp(g_cum[p]) for p in pairs]
    e_ng = [jnp.exp(-g_cum[p]) for p in pairs]
    al2 = [stack(-(kk[p] * jnp.exp(g_cum[p] - ld[p]))) for p in pairs]
    be2 = [stack(bb[p] * e_ng[p]).astype(BF16) for p in pairs]
    kb2 = [stack(kh[p] * e_ng[p]).astype(BF16) for p in pairs]
    rb = [r[p] * e_g[p] for p in pairs]
    v2 = [stack(v[p]) for p in pairs]
    sc = [lax.dot_general(jnp.concatenate([al2[p], stack(rb[p])], axis=0).astype(BF16),
                          jnp.concatenate([be2[p], kb2[p]], axis=0), _NT,
                          preferred_element_type=F32) for p in pairs]
    l_ab = [jnp.where(lower, sc[p][:n2, :n2], 0.0) for p in pairs]
    l_ak = [jnp.where(lower, sc[p][:n2, n2:], 0.0) for p in pairs]
    l_r = [jnp.concatenate([jnp.where(lower_eq, sc[p][n2:, :n2], 0.0),
                            jnp.where(lower_eq, sc[p][n2:, n2:], 0.0)], axis=1).astype(BF16) for p in pairs]

    t_off = l_ab
    pw = [x.astype(BF16) for x in l_ab]
    for _ in range(max(c.bit_length() - 2, 0)):
        pw_f = [jnp.dot(pw[p], pw[p], preferred_element_type=F32) for p in pairs]
        pw = [x.astype(BF16) for x in pw_f]
        t_off = [t_off[p] + pw_f[p] + jnp.dot(t_off[p].astype(BF16), pw[p], preferred_element_type=F32)
                 for p in pairs]

    w0 = [jnp.concatenate([al2[p], _bdot(l_ak[p], v2[p])], axis=1) for p in pairs]
    x = [w0[p] + _bdot(t_off[p], w0[p]) for p in pairs]
    xb = [x[p].astype(BF16) for p in pairs]
    z = [jnp.dot(l_r[p],
                 jnp.concatenate([xb[p], jnp.concatenate([jnp.zeros_like(v2[p]), v2[p]], axis=1).astype(BF16)],
                                 axis=0), preferred_element_type=F32) for p in pairs]
    r_eff = [(rb[p] + z[p][:c, :LANES] + z[p][c:, :LANES]).astype(BF16) for p in pairs]
    y0 = [z[p][:c, LANES:] + z[p][c:, LANES:] for p in pairs]
    mn = [lax.dot_general(xb[p], be2[p], _TN, preferred_element_type=F32) for p in pairs]
    n_add = [mn[p][LANES:] + lax.dot_general(v2[p].astype(BF16), kb2[p], _TN, preferred_element_type=F32)
             for p in pairs]
    s0b = [s0[p].astype(BF16) for p in pairs]
    y = [y0[p] + lax.dot_general(r_eff[p], s0b[p], _NT, preferred_element_type=F32) for p in pairs]
    s_new = [(s0[p] + jnp.dot(s0b[p], mn[p][:LANES].astype(BF16), preferred_element_type=F32) + n_add[p])
             * e_g[p][c - 1:c, :] for p in pairs]
    return y, s_new


def _rwkv_scan_kernel(*refs, c, npair, vres):
    if vres:
        (r_ref, ld_ref, k_ref, a_ref, v_ref, g_ref, vf_ref, vg_ref,
         kkp_ref, kap_ref, rkp_ref, lg_ref, lb_ref, tri_ref, o_ref, s_ref) = refs
    else:
        (r_ref, ld_ref, k_ref, a_ref, v_ref, g_ref,
         kkp_ref, kap_ref, rkp_ref, lg_ref, lb_ref, tri_ref, o_ref, s_ref) = refs

    @pl.when(pl.program_id(2) == 0)
    def _reset():
        s_ref[...] = jnp.zeros_like(s_ref)

    pairs = range(npair)
    sl = [slice(p * LANES, (p + 1) * LANES) for p in pairs]
    take = lambda ref: [ref[:, q] for q in sl]
    shift = C_HEAD_DIM.bit_length() - 1
    ones_b = jnp.where(jnp.right_shift(lax.broadcasted_iota(jnp.int32, (LANES, LANES), 0), shift)
                       == jnp.right_shift(lax.broadcasted_iota(jnp.int32, (LANES, LANES), 1), shift),
                       1.0, 0.0).astype(BF16)
    g_all = jnp.dot(tri_ref[...], ld_ref[...], precision=lax.Precision.HIGHEST, preferred_element_type=F32)

    r, ld, k, a, v, gate = (take(x) for x in (r_ref, ld_ref, k_ref, a_ref, v_ref, g_ref))
    kk_raw = [k[p] * kkp_ref[:, sl[p]] for p in pairs]
    sq = _head_sums([x * x for x in kk_raw], ones_b)
    kk = [kk_raw[p] / jnp.maximum(jnp.sqrt(sq[p]), KK_EPS) for p in pairs]
    bb = [kk[p] * a[p] for p in pairs]
    kh = [k[p] * (1.0 + (a[p] - 1.0) * kap_ref[:, sl[p]]) for p in pairs]
    if vres:
        vf, vg = take(vf_ref), take(vg_ref)
        v = [v[p] + (vf[p] - v[p]) * vg[p] for p in pairs]

    y, s_new = _rwkv_chunk_pairs(r, ld, kh, v, kk, bb, [g_all[:, q] for q in sl],
                                 [s_ref[p] for p in pairs], c)

    inv_n = 1.0 / C_HEAD_DIM
    mean = _head_sums(y, ones_b)
    yc = [y[p] - mean[p] * inv_n for p in pairs]
    var = _head_sums([x * x for x in yc], ones_b)
    bonus = _head_sums([r[p] * kh[p] * rkp_ref[:, sl[p]] for p in pairs], ones_b)
    for p in pairs:
        yn = yc[p] * lax.rsqrt(var[p] * inv_n + LNX_EPS) * lg_ref[:, sl[p]] + lb_ref[:, sl[p]]
        o_ref[:, sl[p]] = ((yn + bonus[p] * v[p]) * gate[p]).astype(o_ref.dtype)
        s_ref[p] = s_new[p]


RWKV_CHUNK = 64
RWKV_PAIRS = 8


def _rwkv_scan(r, log_w, k, a, v, gate, v_first, vgate, k_k, k_a, r_k, lnx_g, lnx_b, bsz, s):
    t, d = r.shape
    c = min(RWKV_CHUNK, s)
    wid = RWKV_PAIRS * LANES
    nc = s // c
    vres = v_first is not None
    tri = (jnp.arange(c)[None, :] <= jnp.arange(c)[:, None]).astype(F32)
    blk = pl.BlockSpec((c, wid), lambda b, g, i: (b * nc + i, g))
    vec = pl.BlockSpec((1, wid), lambda b, g, i: (0, g))
    rows = [r, log_w, k, a, v, gate] + ([v_first, vgate] if vres else [])
    vecs = [x.reshape(1, d) for x in (k_k, k_a, r_k, lnx_g, lnx_b)]
    return pl.pallas_call(
        functools.partial(_rwkv_scan_kernel, c=c, npair=RWKV_PAIRS, vres=vres),
        grid=(bsz, d // wid, nc),
        in_specs=[blk] * len(rows) + [vec] * len(vecs) + [pl.BlockSpec((c, c), lambda b, g, i: (0, 0))],
        out_specs=blk,
        out_shape=jax.ShapeDtypeStruct((t, d), BF16),
        scratch_shapes=[pltpu.VMEM((RWKV_PAIRS, LANES, LANES), F32)],
        compiler_params=_params("parallel", "parallel", "arbitrary"),
        name="rwkv_scan",
    )(*rows, *vecs, tri)


def _rwkv7_mixer(x, v_first, mu, w_r, w_k, w_v, w_o, w0, w1, w2, a0, a1, a2, g1, g2,
                 k_k, k_a, r_k, lnx_g, lnx_b, vres, bsz, s):
    xr, xw, xk, xv, xa, xg = _shift_mix(x, mu, s)
    r = _matmul(xr, w_r.astype(BF16))
    k = _matmul(xk, w_k.astype(BF16))
    v = _matmul(xv, w_v.astype(BF16))
    log_decay = _lora(xw, w1, w2, w0, "tanh", "logdecay")
    a = _lora(xa, a1, a2, a0, "none", "sigmoid")
    g = _lora(xg, g1, g2, None, "sigmoid", "none")
    if vres is None:
        vgate = None
        v_keep = v
    else:
        v0, v1, v2 = vres
        vgate = _lora(xv, v1, v2, v0, "none", "sigmoid")
        v_keep = v_first
    yo = _rwkv_scan(r, log_decay, k, a, v, g, None if vres is None else v_first, vgate,
                    k_k, k_a, r_k, lnx_g, lnx_b, bsz, s)
    return _matmul(yo, w_o.astype(BF16)), v_keep


def _lane_pick(cond, lanef):
    return jnp.min(jnp.where(cond, lanef, float(LANES)), axis=1, keepdims=True)


def _router_kernel(x_ref, w_ref, b_ref, route_ref, cnt_ref, run_ref, *, tm):
    @pl.when(pl.program_id(0) == 0)
    def _reset():
        run_ref[...] = jnp.zeros_like(run_ref)

    logits = jnp.dot(x_ref[...].astype(BF16), w_ref[...], preferred_element_type=F32) + b_ref[...]
    lane = lax.broadcasted_iota(jnp.int32, (tm, LANES), 1)
    lanef = lane.astype(F32)

    def masked_softmax(keep):
        z = jnp.where(keep, logits, -jnp.inf)
        e = jnp.exp(z - jnp.max(z, axis=1, keepdims=True))
        return e / jnp.sum(e, axis=1, keepdims=True)

    in_groups = lane < N_GROUPS
    pg = masked_softmax(in_groups)
    g_p = jnp.max(pg, axis=1, keepdims=True)
    g_idx = _lane_pick(jnp.where(in_groups, pg, -1.0) == g_p, lanef)
    lo = float(N_GROUPS) + float(EXPERTS_PER_GROUP) * g_idx
    half = 0.5 * float(EXPERTS_PER_GROUP - 1)
    in_group = jnp.abs(lanef - lo - half) < half + 0.5
    pe = jnp.where(in_group, masked_softmax(in_group), -1.0)
    p1 = jnp.max(pe, axis=1, keepdims=True)
    i1 = _lane_pick(pe == p1, lanef)
    pe2 = jnp.where(lanef == i1, -1.0, pe)
    p2 = jnp.max(pe2, axis=1, keepdims=True)
    i2 = _lane_pick(pe2 == p2, lanef)
    den = p1 + p2
    gate = (g_p * (p1 / den), g_p * (p2 / den))
    expert = (i1 - float(N_GROUPS), i2 - float(N_GROUPS))

    earlier = jnp.where(lax.broadcasted_iota(jnp.int32, (tm, tm), 1)
                        < lax.broadcasted_iota(jnp.int32, (tm, tm), 0), 1.0, 0.0).astype(BF16)
    run = run_ref[...]
    rank = []
    for e in expert:
        onehot = jnp.where(lanef == e, 1.0, 0.0)
        before = jnp.dot(earlier, onehot.astype(BF16), preferred_element_type=F32) + run
        rank.append(jnp.sum(onehot * before, axis=1, keepdims=True))
        run = run + jnp.sum(onehot, axis=0, keepdims=True)
    run_ref[...] = run
    cnt_ref[...] = run
    cols = (expert[0], expert[1], rank[0], rank[1], gate[0], gate[1])
    route = jnp.zeros((tm, LANES), F32)
    for c, val in enumerate(cols):
        route = jnp.where(lane == c, val, route)
    route_ref[...] = route


def _router(x, w_router, b_router, tm=256):
    t, d = x.shape
    tm = min(tm, t)
    return pl.pallas_call(
        functools.partial(_router_kernel, tm=tm),
        grid=(t // tm,),
        in_specs=[pl.BlockSpec((tm, d), lambda i: (i, 0)),
                  pl.BlockSpec((d, LANES), lambda i: (0, 0)),
                  pl.BlockSpec((1, LANES), lambda i: (0, 0))],
        out_specs=[pl.BlockSpec((tm, LANES), lambda i: (i, 0)),
                   pl.BlockSpec((1, LANES), lambda i: (0, 0))],
        out_shape=[jax.ShapeDtypeStruct((t, LANES), F32), jax.ShapeDtypeStruct((1, LANES), F32)],
        scratch_shapes=[pltpu.VMEM((1, LANES), F32)],
        compiler_params=_params("arbitrary"),
        name="router",
    )(x, w_router, b_router)


SLAB = 2048 // LANES


def _slab(ref, r):
    return ref.at[pl.ds(pl.multiple_of(r * SLAB, SLAB), SLAB)]


def _slab_rows(ref, rows):
    return jnp.concatenate([ref[pl.ds(c, rows, stride=SLAB), :] for c in range(SLAB)], axis=1)


def _store_slab_rows(ref, val, rows):
    for c in range(SLAB):
        ref[pl.ds(c, rows, stride=SLAB), :] = val[:, c * LANES:(c + 1) * LANES]


def _expert_kernel(be_ref, nu_ref, idx_ref, nxt_ref, x_hbm, wg_ref, wu_ref, wd_ref, o_ref,
                   xbuf, sems, wg_b, wu_b, wd_b, *, rows):
    i = pl.program_id(0)
    n_used = nu_ref[0]
    slot = i % 2

    def token_copy(ids_ref, r, dst_slot):
        return pltpu.make_async_copy(_slab(x_hbm, ids_ref[0, r]), _slab(xbuf.at[dst_slot], r), sems.at[dst_slot])

    def gather(ids_ref, dst_slot):
        def start(r, carry):
            token_copy(ids_ref, r, dst_slot).start()
            token_copy(ids_ref, r + rows // 2, dst_slot).start()
            return carry
        lax.fori_loop(0, rows // 2, start, 0)

    @pl.when(i == 0)
    def _():
        gather(idx_ref, 0)

    @pl.when(i + 1 < n_used)
    def _():
        gather(nxt_ref, 1 - slot)

    @pl.when(jnp.logical_or(i == 0, be_ref[i] != be_ref[jnp.maximum(i - 1, 0)]))
    def _():
        wg_b[...] = wg_ref[...].astype(BF16)
        wu_b[...] = wu_ref[...].astype(BF16)
        wd_b[...] = wd_ref[...].astype(BF16)

    @pl.when(i < n_used)
    def _():
        def wait(r, carry):
            token_copy(idx_ref, r, slot).wait()
            return carry
        lax.fori_loop(0, rows, wait, 0)
        xb = _slab_rows(xbuf.at[slot], rows).astype(BF16)
        hg = jnp.dot(xb, wg_b[...], preferred_element_type=F32)
        hu = jnp.dot(xb, wu_b[...], preferred_element_type=F32)
        hid = (hg * jax.nn.sigmoid(hg) * hu).astype(BF16)
        _store_slab_rows(o_ref, jnp.dot(hid, wd_b[...], preferred_element_type=F32), rows)

    @pl.when(i >= n_used)
    def _():
        o_ref[...] = jnp.zeros_like(o_ref)


def _expert_mlp(x_slabs, src_row, block_e, n_used, w_gate, w_up, w_down, layer, rows=MOE_ROWS):
    d, ff = w_gate.shape[2], w_gate.shape[3]
    nb = src_row.shape[0] // rows
    ids = src_row.reshape(nb, 1, rows)
    grid_spec = pltpu.PrefetchScalarGridSpec(
        num_scalar_prefetch=2,
        grid=(nb,),
        in_specs=[pl.BlockSpec((None, 1, rows), lambda i, be, nu: (i, 0, 0), memory_space=pltpu.SMEM),
                  pl.BlockSpec((None, 1, rows), lambda i, be, nu: (jnp.minimum(i + 1, nb - 1), 0, 0),
                               memory_space=pltpu.SMEM),
                  pl.BlockSpec(memory_space=pl.ANY),
                  pl.BlockSpec((None, None, d, ff), lambda i, be, nu: (layer, be[i], 0, 0)),
                  pl.BlockSpec((None, None, d, ff), lambda i, be, nu: (layer, be[i], 0, 0)),
                  pl.BlockSpec((None, None, ff, d), lambda i, be, nu: (layer, be[i], 0, 0))],
        out_specs=pl.BlockSpec((rows * SLAB, LANES), lambda i, be, nu: (i, 0)),
        scratch_shapes=[pltpu.VMEM((2, rows * SLAB, LANES), F32), pltpu.SemaphoreType.DMA((2,)),
                        pltpu.VMEM((d, ff), BF16), pltpu.VMEM((d, ff), BF16), pltpu.VMEM((ff, d), BF16)],
    )
    return pl.pallas_call(
        functools.partial(_expert_kernel, rows=rows),
        grid_spec=grid_spec,
        out_shape=jax.ShapeDtypeStruct((nb * rows * SLAB, LANES), F32),
        compiler_params=_params("arbitrary"),
        name="expert_mlp",
    )(block_e, n_used, ids, ids, x_slabs, w_gate, w_up, w_down)


def _combine_ln_kernel(idx_ref, nxt_ref, y_ref, x_ref, gt_ref, g_ref, b_ref, o_ref, buf, sems, *, rows):
    i = pl.program_id(0)
    slot = i % 2

    def row_copy(ids_ref, r, kslot, dst_slot):
        return pltpu.make_async_copy(_slab(y_ref, ids_ref[0, MOE_TOPK * r + kslot]),
                                     _slab(buf.at[dst_slot, kslot], r), sems.at[dst_slot])

    def gather(ids_ref, dst_slot):
        def start(r, carry):
            for kslot in range(MOE_TOPK):
                row_copy(ids_ref, r, kslot, dst_slot).start()
                row_copy(ids_ref, r + rows // 2, kslot, dst_slot).start()
            return carry
        lax.fori_loop(0, rows // 2, start, 0)

    @pl.when(i == 0)
    def _():
        gather(idx_ref, 0)

    @pl.when(i + 1 < pl.num_programs(0))
    def _():
        gather(nxt_ref, 1 - slot)

    def wait(r, carry):
        for kslot in range(MOE_TOPK):
            row_copy(idx_ref, r, kslot, slot).wait()
        return carry
    lax.fori_loop(0, rows, wait, 0)
    gt = gt_ref[...]
    ffn = gt[:, 0:1] * _slab_rows(buf.at[slot, 0], rows)
    for kslot in range(1, MOE_TOPK):
        ffn = ffn + gt[:, kslot:kslot + 1] * _slab_rows(buf.at[slot, kslot], rows)
    o_ref[...] = _ln_rows(DN_ALPHA * x_ref[...] + ffn, g_ref[...], b_ref[...])


def _combine_ln(y_sorted, dest, gates, x, g, b, rows=256):
    t, d = x.shape
    nb = t // rows
    row = pl.BlockSpec((rows, d), lambda i: (i, 0))
    vec = pl.BlockSpec((1, d), lambda i: (0, 0))
    ids = dest.reshape(nb, 1, rows * MOE_TOPK)
    return pl.pallas_call(
        functools.partial(_combine_ln_kernel, rows=rows),
        grid=(nb,),
        in_specs=[pl.BlockSpec((None, 1, rows * MOE_TOPK), lambda i: (i, 0, 0), memory_space=pltpu.SMEM),
                  pl.BlockSpec((None, 1, rows * MOE_TOPK), lambda i: (jnp.minimum(i + 1, nb - 1), 0, 0),
                               memory_space=pltpu.SMEM),
                  pl.BlockSpec(memory_space=pl.ANY), row,
                  pl.BlockSpec((rows, MOE_TOPK), lambda i: (i, 0)), vec, vec],
        out_specs=row,
        out_shape=jax.ShapeDtypeStruct((t, d), F32),
        scratch_shapes=[pltpu.VMEM((2, MOE_TOPK, rows * SLAB, LANES), F32), pltpu.SemaphoreType.DMA((2,))],
        compiler_params=_params("arbitrary"),
        name="combine_ln",
    )(ids, ids, y_sorted, x, gates, g.reshape(1, d), b.reshape(1, d))


def _moe_layer(x, x_slabs, w_grp, b_grp, w_exp_r, b_exp_r, w_gate, w_up, w_down, ln_g, ln_b, layer):
    t, d = x.shape
    n_r = N_GROUPS + N_EXPERTS
    w_router = jnp.pad(jnp.concatenate([w_grp, w_exp_r], axis=1), ((0, 0), (0, LANES - n_r))).astype(BF16)
    b_router = jnp.pad(jnp.concatenate([b_grp, b_exp_r]), (0, LANES - n_r)).reshape(1, LANES)
    route, cnt = _router(x, w_router, b_router)
    flat_e = route[:, 0:MOE_TOPK].astype(jnp.int32).reshape(-1)
    rank = route[:, MOE_TOPK:2 * MOE_TOPK].astype(jnp.int32).reshape(-1)
    gates = route[:, 2 * MOE_TOPK:3 * MOE_TOPK]
    counts = cnt[0, :N_EXPERTS].astype(jnp.int32)
    m = t * MOE_TOPK
    padded = (counts + MOE_ROWS - 1) // MOE_ROWS * MOE_ROWS
    pad_end = jnp.cumsum(padded)
    dest = (pad_end - padded)[flat_e] + rank
    n_blocks = (m + N_EXPERTS * (MOE_ROWS - 1) + MOE_ROWS - 1) // MOE_ROWS
    src_row = jnp.zeros((n_blocks * MOE_ROWS,), jnp.int32).at[dest].set(
        (jnp.arange(m) // MOE_TOPK).astype(jnp.int32))
    starts = jnp.arange(n_blocks, dtype=jnp.int32) * MOE_ROWS
    block_e = jnp.minimum(jnp.sum((pad_end[None, :] <= starts[:, None]).astype(jnp.int32), axis=1),
                          N_EXPERTS - 1).astype(jnp.int32)
    n_used = (pad_end[-1] // MOE_ROWS).astype(jnp.int32).reshape(1)

    ys = _expert_mlp(x_slabs, src_row, block_e, n_used, w_gate, w_up, w_down, layer)
    return _combine_ln(ys, dest.astype(jnp.int32), gates, x, ln_g, ln_b)


def kernel(x, ev_w_in, ev_gate_bias, ev_conv_w, ev_conv_b, ev_hnorm_g, ev_w_out, od_mu, od_w_r, od_w_k, od_w_v, od_w_o, od_w0, od_w1, od_w2, od_a0, od_a1, od_a2, od_g1, od_g2, od_k_k, od_k_a, od_r_k, od_lnx_g, od_lnx_b, od_v0, od_v1, od_v2, ln_mix_g, ln_mix_b, ln_ffn_g, ln_ffn_b, moe_w_grp, moe_b_grp, moe_w_exp_r, moe_b_exp_r, moe_w_gate, moe_w_up, moe_w_down):
    bsz, s, d = x.shape
    depth = ln_mix_g.shape[0]
    xf = x.reshape(bsz * s, d)
    v_first = None
    for layer in range(depth):
        if layer % 2 == 0:
            e = layer // 2
            mix = _even_mixer(xf, ev_w_in[e], ev_gate_bias[e], ev_conv_w[e], ev_conv_b[e],
                              ev_hnorm_g[e], ev_w_out[e], bsz, s)
        else:
            o = layer // 2
            vres = None if o == 0 else (od_v0[o - 1], od_v1[o - 1], od_v2[o - 1])
            mix, v_first = _rwkv7_mixer(xf, v_first, od_mu[o], od_w_r[o], od_w_k[o], od_w_v[o],
                                        od_w_o[o], od_w0[o], od_w1[o], od_w2[o], od_a0[o],
                                        od_a1[o], od_a2[o], od_g1[o], od_g2[o], od_k_k[o],
                                        od_k_a[o], od_r_k[o], od_lnx_g[o], od_lnx_b[o], vres, bsz, s)
        xf, x_slabs = _resid_ln(xf, mix, ln_mix_g[layer], ln_mix_b[layer])
        xf = _moe_layer(xf, x_slabs, moe_w_grp[layer], moe_b_grp[layer], moe_w_exp_r[layer],
                        moe_b_exp_r[layer], moe_w_gate, moe_w_up, moe_w_down,
                        ln_ffn_g[layer], ln_ffn_b[layer], layer)
    return xf.reshape(bsz, s, d)
```

```python
import functools
import math

import jax
import jax.numpy as jnp
from jax import lax
from jax.experimental import pallas as pl
from jax.experimental.pallas import tpu as pltpu

F32 = jnp.float32
BF16 = jnp.bfloat16

DEPTH = 4
DN_ALPHA = (2 * DEPTH) ** 0.25
LN_EPS = 1e-5

A_HEADS = 8
A_HEAD_DIM = 128
A_WIDTH = A_HEADS * A_HEAD_DIM
MOBA_BLOCK = 256
MOBA_TOPK = 3
MOBA_GROUP = 4
ROPE_THETA = 500000.0
ROPE_DIM = A_HEAD_DIM // 4
NEG_INF = -1e30

B_HEADS = 4
B_HEAD_DIM = 256
B_WIDTH = B_HEADS * B_HEAD_DIM
MLSTM_CHUNK = 64
B_CONV = 4
HNORM_EPS = 1e-6
EVEN_MAIN = 3 * A_WIDTH + 4 * B_WIDTH

C_HEAD_DIM = 64
LNX_EPS = 64e-5
KK_EPS = 1e-12

N_GROUPS = 4
EXPERTS_PER_GROUP = 8
N_EXPERTS = N_GROUPS * EXPERTS_PER_GROUP
MOE_TOPK = 2
MOE_ROWS = 256

LANES = 128
SUBLANES = 8
VMEM_LIMIT = 56 * 1024 * 1024

_NT = (((1,), (1,)), ((), ()))
_TN = (((0,), (0,)), ((), ()))


def _params(*sem):
    return pltpu.CompilerParams(dimension_semantics=sem, vmem_limit_bytes=VMEM_LIMIT)


def _mm_kernel(x_ref, w_ref, o_ref):
    o_ref[...] = jnp.dot(x_ref[...].astype(BF16), w_ref[...],
                         preferred_element_type=F32).astype(o_ref.dtype)


def _matmul(x, w, out_dtype=F32, tm=1024, tn=512):
    m, k = x.shape
    n = w.shape[1]
    tm, tn = min(tm, m), min(tn, n)
    return pl.pallas_call(
        _mm_kernel,
        grid=(m // tm, n // tn),
        in_specs=[pl.BlockSpec((tm, k), lambda i, j: (i, 0)),
                  pl.BlockSpec((k, tn), lambda i, j: (0, j))],
        out_specs=pl.BlockSpec((tm, tn), lambda i, j: (i, j)),
        out_shape=jax.ShapeDtypeStruct((m, n), out_dtype),
        compiler_params=_params("parallel", "arbitrary"),
        name="matmul",
    )(x, w)


def _mm2_kernel(xa_ref, xb_ref, wa_ref, wb_ref, o_ref):
    acc = jnp.dot(xa_ref[...], wa_ref[...], preferred_element_type=F32)
    acc = acc + jnp.dot(xb_ref[...], wb_ref[...], preferred_element_type=F32)
    o_ref[...] = acc


def _matmul2(xa, xb, wa, wb, tm=1024, tn=512):
    m, ka = xa.shape
    kb = xb.shape[1]
    n = wa.shape[1]
    tm, tn = min(tm, m), min(tn, n)
    return pl.pallas_call(
        _mm2_kernel,
        grid=(m // tm, n // tn),
        in_specs=[pl.BlockSpec((tm, ka), lambda i, j: (i, 0)),
                  pl.BlockSpec((tm, kb), lambda i, j: (i, 0)),
                  pl.BlockSpec((ka, tn), lambda i, j: (0, j)),
                  pl.BlockSpec((kb, tn), lambda i, j: (0, j))],
        out_specs=pl.BlockSpec((tm, tn), lambda i, j: (i, j)),
        out_shape=jax.ShapeDtypeStruct((m, n), F32),
        compiler_params=_params("parallel", "arbitrary"),
        name="matmul2",
    )(xa, xb, wa, wb)


def _ln_rows(h, g, b):
    mu = jnp.mean(h, axis=-1, keepdims=True)
    hc = h - mu
    var = jnp.mean(hc * hc, axis=-1, keepdims=True)
    return hc * lax.rsqrt(var + LN_EPS) * g + b


def _resid_ln_kernel(x_ref, m_ref, g_ref, b_ref, o_ref, s_ref, *, tm):
    y = _ln_rows(DN_ALPHA * x_ref[...] + m_ref[...], g_ref[...], b_ref[...])
    o_ref[...] = y
    _store_slab_rows(s_ref, y, tm)


def _resid_ln(x, mix, g, b, tm=256):
    t, d = x.shape
    tm = min(tm, t)
    row = pl.BlockSpec((tm, d), lambda i: (i, 0))
    vec = pl.BlockSpec((1, d), lambda i: (0, 0))
    return pl.pallas_call(
        functools.partial(_resid_ln_kernel, tm=tm),
        grid=(t // tm,),
        in_specs=[row, row, vec, vec],
        out_specs=[row, pl.BlockSpec((tm * SLAB, LANES), lambda i: (i, 0))],
        out_shape=[jax.ShapeDtypeStruct((t, d), F32), jax.ShapeDtypeStruct((t * SLAB, LANES), F32)],
        compiler_params=_params("parallel"),
        name="resid_ln",
    )(x, mix, g.reshape(1, d), b.reshape(1, d))


def _rope_tables(s):
    half = ROPE_DIM // 2
    inv = jnp.power(jnp.float32(ROPE_THETA), -jnp.arange(half, dtype=F32) / half)
    ang = jnp.arange(s).astype(F32)[:, None] * inv[None, :]
    cos, sin = jnp.cos(ang), jnp.sin(ang)
    zero = jnp.zeros((s, A_HEAD_DIM - ROPE_DIM), F32)
    zh = jnp.zeros((s, half), F32)
    c = jnp.concatenate([cos, cos, zero + 1.0], axis=-1)
    s_up = jnp.concatenate([zh, sin, zero], axis=-1)
    s_dn = jnp.concatenate([-sin, zh, zero], axis=-1)
    return c, s_up, s_dn


def _rope(x, c, s_up, s_dn):
    half = ROPE_DIM // 2
    return (x * c + pltpu.roll(x, half, axis=1) * s_up
            + pltpu.roll(x, A_HEAD_DIM - half, axis=1) * s_dn)


def _moba_kernel(q_ref, k_ref, v_ref, c_ref, su_ref, sd_ref, o_ref, krot_ref, vb_ref, kmean_ref, *, nblk):
    qi = pl.program_id(2)
    blk = MOBA_BLOCK
    scale = A_HEAD_DIM ** -0.5

    @pl.when(qi == 0)
    def _prepare_keys():
        def body(j, carry):
            r = pl.multiple_of(j * blk, blk)
            rows = pl.ds(r, blk)
            kr = _rope(k_ref[rows, :].astype(F32), c_ref[rows, :], su_ref[rows, :], sd_ref[rows, :])
            block_tag = jnp.where(lax.broadcasted_iota(jnp.int32, (blk, LANES), 1) == j, 1.0, 0.0)
            krot_ref[rows, :] = jnp.concatenate([kr, block_tag], axis=1).astype(BF16)
            vb_ref[rows, :] = v_ref[rows, :].astype(BF16)
            kmean_ref[pl.ds(j, 1), :] = jnp.mean(kr, axis=0, keepdims=True)
            return carry
        lax.fori_loop(0, nblk, body, 0)

    r0 = pl.multiple_of(qi * blk, blk)
    qrows = pl.ds(r0, blk)
    qb = _rope(q_ref[...].astype(F32), c_ref[qrows, :], su_ref[qrows, :], sd_ref[qrows, :]).astype(BF16)

    g = lax.dot_general(kmean_ref[...].astype(BF16), qb, _NT, preferred_element_type=F32)
    brow = lax.broadcasted_iota(jnp.int32, (nblk, blk), 0)
    g = jnp.where(brow < qi, g, -jnp.inf)
    sel_t = jnp.zeros((nblk, blk), F32)
    for j in range(nblk):
        gj = g[j:j + 1, :]
        earlier = jnp.where(brow < j, 1.0, 0.0)
        beats = jnp.where(g > gj, 1.0, jnp.where(g == gj, earlier, 0.0))
        cnt = jnp.sum(beats, axis=0, keepdims=True)
        sel_t = jnp.where(brow == j, jnp.where(cnt < float(min(MOBA_TOPK, nblk)), 1.0, 0.0), sel_t)
    sel_t = jnp.where(brow < qi, sel_t, 0.0)
    eye = jnp.where(lax.broadcasted_iota(jnp.int32, (blk, blk), 0)
                    == lax.broadcasted_iota(jnp.int32, (blk, blk), 1), 1.0, 0.0).astype(BF16)
    sel_pad = jnp.concatenate([sel_t, jnp.zeros((LANES - nblk, blk), F32)], axis=0).astype(BF16)
    sel = lax.dot_general(eye, sel_pad, _NT, preferred_element_type=F32)
    lane = lax.broadcasted_iota(jnp.int32, (blk, LANES), 1)
    bias = jnp.where(sel > 0.5, 0.0, jnp.where(lane < nblk, NEG_INF / scale, 0.0))
    q_aug = jnp.concatenate([qb, bias.astype(BF16)], axis=1)

    kd = krot_ref[qrows, :A_HEAD_DIM]
    s = lax.dot_general(qb, kd, _NT, preferred_element_type=F32) * scale
    causal = (lax.broadcasted_iota(jnp.int32, (blk, blk), 1)
              <= lax.broadcasted_iota(jnp.int32, (blk, blk), 0))
    s = jnp.where(causal, s, NEG_INF)
    m0 = jnp.max(s, axis=1, keepdims=True)
    p = jnp.exp(s - m0)
    l0 = jnp.sum(p, axis=1, keepdims=True)
    acc0 = jnp.dot(p.astype(BF16), vb_ref[qrows, :], preferred_element_type=F32)

    span = MOBA_GROUP * blk

    def past_group(jg, carry):
        m, l, acc = carry
        rows = pl.ds(pl.multiple_of(jg * span, span), span)
        sj = lax.dot_general(q_aug, krot_ref[rows, :], _NT, preferred_element_type=F32) * scale
        m_new = jnp.maximum(m, jnp.max(sj, axis=1, keepdims=True))
        a = jnp.exp(m - m_new)
        pj = jnp.exp(sj - m_new)
        l = a * l + jnp.sum(pj, axis=1, keepdims=True)
        acc = a * acc + jnp.dot(pj.astype(BF16), vb_ref[rows, :], preferred_element_type=F32)
        return m_new, l, acc

    n_groups = (qi + MOBA_GROUP - 1) // MOBA_GROUP
    _, l, acc = lax.fori_loop(0, n_groups, past_group, (m0, l0, acc0))
    o_ref[...] = (acc / l).astype(o_ref.dtype)


def _moba(z, bsz, s):
    nblk = s // MOBA_BLOCK
    c, s_up, s_dn = _rope_tables(s)
    tab = pl.BlockSpec((s, A_HEAD_DIM), lambda b, h, qi: (0, 0))
    return pl.pallas_call(
        functools.partial(_moba_kernel, nblk=nblk),
        grid=(bsz, A_HEADS, nblk),
        in_specs=[pl.BlockSpec((MOBA_BLOCK, A_HEAD_DIM), lambda b, h, qi: (b * nblk + qi, h)),
                  pl.BlockSpec((s, A_HEAD_DIM), lambda b, h, qi: (b, A_HEADS + h)),
                  pl.BlockSpec((s, A_HEAD_DIM), lambda b, h, qi: (b, 2 * A_HEADS + h)),
                  tab, tab, tab],
        out_specs=pl.BlockSpec((MOBA_BLOCK, A_HEAD_DIM), lambda b, h, qi: (b * nblk + qi, h)),
        out_shape=jax.ShapeDtypeStruct((bsz * s, A_WIDTH), BF16),
        scratch_shapes=[pltpu.VMEM((s, A_HEAD_DIM + LANES), BF16), pltpu.VMEM((s, A_HEAD_DIM), BF16),
                        pltpu.VMEM((nblk, A_HEAD_DIM), F32)],
        compiler_params=_params("parallel", "parallel", "arbitrary"),
        name="moba",
    )(z, z, z, c, s_up, s_dn)


def _log_sigmoid(x):
    return jnp.minimum(x, 0.0) - jnp.log1p(jnp.exp(-jnp.abs(x)))


def _shift_rows(x, prev8, d):
    if d == 0:
        return x
    rolled = pltpu.roll(x, d, axis=0)
    top = jnp.where(lax.broadcasted_iota(jnp.int32, prev8.shape, 0) < d,
                    pltpu.roll(prev8, d, axis=0), rolled[:SUBLANES])
    return jnp.concatenate([top, rolled[SUBLANES:]], axis=0)


def _mlstm_kernel(q_ref, k_ref, v_ref, o_ref, gc_ref, gr_ref, bc_ref, br_ref,
                  cwq_ref, cwk_ref, cbq_ref, cbk_ref, hg_ref, y_ref,
                  c_ref, n_ref, m_ref, pq_ref, pk_ref, *, tt):
    L = MLSTM_CHUNK
    dk = B_HEAD_DIM
    heads = range(B_HEADS)
    hs = [slice(h * dk, (h + 1) * dk) for h in heads]

    @pl.when(pl.program_id(1) == 0)
    def _reset():
        c_ref[...] = jnp.zeros_like(c_ref)
        n_ref[...] = jnp.zeros_like(n_ref)
        m_ref[...] = jnp.zeros_like(m_ref)
        pq_ref[...] = jnp.zeros_like(pq_ref)
        pk_ref[...] = jnp.zeros_like(pk_ref)

    def conv_silu(x_ref, p_ref, w_ref, b_ref):
        x = x_ref[...].astype(F32)
        prev8 = p_ref[...]
        out = b_ref[...] + x * w_ref[B_CONV - 1:B_CONV, :]
        for d in range(1, B_CONV):
            out = out + _shift_rows(x, prev8, d) * w_ref[B_CONV - 1 - d:B_CONV - d, :]
        p_ref[...] = x[tt - SUBLANES:, :]
        return out * jax.nn.sigmoid(out)

    q_all = conv_silu(q_ref, pq_ref, cwq_ref, cbq_ref)
    k_all = conv_silu(k_ref, pk_ref, cwk_ref, cbk_ref) * (dk ** -0.5)

    gcol = gc_ref[...] + bc_ref[...]
    grow = gr_ref[...] + br_ref[...]
    i_col_all = [gcol[:, h:h + 1] for h in heads]
    f_col_all = [_log_sigmoid(gcol[:, B_HEADS + h:B_HEADS + h + 1]) for h in heads]
    i_row_all = [grow[h:h + 1, :] for h in heads]
    f_row_all = [_log_sigmoid(grow[B_HEADS + h:B_HEADS + h + 1, :]) for h in heads]

    rr = lax.broadcasted_iota(jnp.int32, (L, L), 0)
    cc = lax.broadcasted_iota(jnp.int32, (L, L), 1)
    tri = cc <= rr

    for c in range(tt // L):
        lo, hi = c * L, (c + 1) * L
        qc = [q_all[lo:hi, hs[h]] for h in heads]
        kc = [k_all[lo:hi, hs[h]] for h in heads]
        qcb = [x.astype(BF16) for x in qc]
        kcb = [x.astype(BF16) for x in kc]
        vcb = [v_ref[lo:hi, hs[h]].astype(BF16) for h in heads]
        i_col = [x[lo:hi] for x in i_col_all]
        f_col = [x[lo:hi] for x in f_col_all]
        i_row = [x[:, lo:hi] for x in i_row_all]
        f_row = [x[:, lo:hi] for x in f_row_all]
        m_prev = [m_ref[h] for h in heads]
        cst = [c_ref[h] for h in heads]
        nst = [n_ref[h] for h in heads]

        b_col = [jnp.sum(jnp.where(tri, f_row[h], 0.0), axis=1, keepdims=True) for h in heads]
        b_row = [jnp.sum(jnp.where(rr <= cc, f_col[h], 0.0), axis=0, keepdims=True) for h in heads]
        log_inter = [b_col[h] + m_prev[h] for h in heads]
        log_intra = [jnp.where(tri, b_col[h] - b_row[h] + i_row[h], -jnp.inf) for h in heads]
        m_t = [jnp.maximum(log_inter[h], jnp.max(log_intra[h], axis=1, keepdims=True)) for h in heads]
        w_inter = [jnp.exp(log_inter[h] - m_t[h]) for h in heads]
        qk = [lax.dot_general(qcb[h], kcb[h], _NT, preferred_element_type=F32)
              * jnp.exp(log_intra[h] - m_t[h]) for h in heads]
        q_c = [jnp.dot(qcb[h], cst[h].astype(BF16), preferred_element_type=F32) for h in heads]
        qk_v = [jnp.dot(qk[h].astype(BF16), vcb[h], preferred_element_type=F32) for h in heads]
        den = [w_inter[h] * jnp.sum(qc[h] * nst[h], axis=1, keepdims=True)
               + jnp.sum(qk[h], axis=1, keepdims=True) for h in heads]
        hc = [(w_inter[h] * q_c[h] + qk_v[h]) / jnp.maximum(jnp.abs(den[h]), jnp.exp(-m_t[h])) for h in heads]

        b_last = [jnp.sum(f_col[h], axis=0, keepdims=True) for h in heads]
        log_s = [b_last[h] - b_col[h] + i_col[h] for h in heads]
        m_new = [jnp.maximum(b_last[h] + m_prev[h], jnp.max(log_s[h], axis=0, keepdims=True)) for h in heads]
        decay = [jnp.exp(b_last[h] + m_prev[h] - m_new[h]) for h in heads]
        kw = [kc[h] * jnp.exp(log_s[h] - m_new[h]) for h in heads]
        kv = [lax.dot_general(kw[h].astype(BF16), vcb[h], _TN, preferred_element_type=F32) for h in heads]
        for h in heads:
            c_ref[h] = decay[h] * cst[h] + kv[h]
            n_ref[h] = decay[h] * nst[h] + jnp.sum(kw[h], axis=0, keepdims=True)
            m_ref[h] = m_new[h]
            hn = hc[h] * lax.rsqrt(jnp.mean(hc[h] * hc[h], axis=1, keepdims=True) + HNORM_EPS)
            y_ref[lo:hi, hs[h]] = (hn * hg_ref[:, hs[h]]
                                   * jax.nn.sigmoid(o_ref[lo:hi, hs[h]].astype(F32))).astype(y_ref.dtype)


def _mlstm(z, gates, gate_bias, conv_w, conv_b, hnorm_g, bsz, s, tt=256):
    nt = s // tt
    cb = 3 * A_WIDTH // B_WIDTH
    gates_t = gates.T
    bias = gate_bias.reshape(1, 2 * B_HEADS)

    def col(off):
        return pl.BlockSpec((tt, B_WIDTH), lambda b, t: (b * nt + t, cb + off))

    def par(rows, off):
        return pl.BlockSpec((rows, B_WIDTH), lambda b, t: (0, off))

    return pl.pallas_call(
        functools.partial(_mlstm_kernel, tt=tt),
        grid=(bsz, nt),
        in_specs=[col(0), col(1), col(2), col(3),
                  pl.BlockSpec((tt, 2 * B_HEADS), lambda b, t: (b * nt + t, 0)),
                  pl.BlockSpec((2 * B_HEADS, tt), lambda b, t: (0, b * nt + t)),
                  pl.BlockSpec((1, 2 * B_HEADS), lambda b, t: (0, 0)),
                  pl.BlockSpec((2 * B_HEADS, 1), lambda b, t: (0, 0)),
                  par(B_CONV, 0), par(B_CONV, 1), par(1, 0), par(1, 1), par(1, 0)],
        out_specs=pl.BlockSpec((tt, B_WIDTH), lambda b, t: (b * nt + t, 0)),
        out_shape=jax.ShapeDtypeStruct((bsz * s, B_WIDTH), BF16),
        scratch_shapes=[pltpu.VMEM((B_HEADS, B_HEAD_DIM, B_HEAD_DIM), F32),
                        pltpu.VMEM((B_HEADS, 1, B_HEAD_DIM), F32), pltpu.VMEM((B_HEADS, 1, 1), F32),
                        pltpu.VMEM((SUBLANES, B_WIDTH), F32), pltpu.VMEM((SUBLANES, B_WIDTH), F32)],
        compiler_params=_params("parallel", "arbitrary"),
        name="mlstm",
    )(z, z, z, z, gates, gates_t, bias, bias.T, conv_w, conv_w,
      conv_b.reshape(1, -1), conv_b.reshape(1, -1), hnorm_g.reshape(1, -1))


def _even_mixer(x, w_in, gate_bias, conv_w, conv_b, hnorm_g, w_out, bsz, s):
    d = x.shape[1]
    w_main = w_in[:, :EVEN_MAIN].astype(BF16)
    w_gate = jnp.pad(w_in[:, EVEN_MAIN:], ((0, 0), (0, LANES - 2 * B_HEADS))).astype(BF16)
    z = _matmul(x, w_main, out_dtype=BF16)
    gates = _matmul(x, w_gate, tn=LANES)[:, :2 * B_HEADS]
    y_a = _moba(z, bsz, s)
    y_b = _mlstm(z, gates, gate_bias, conv_w, conv_b, hnorm_g, bsz, s)
    w_o = w_out.astype(BF16)
    return _matmul2(y_a, y_b, w_o[:A_WIDTH], w_o[A_WIDTH:])


def _shift_mix_kernel(x_ref, p_ref, mu_ref, *o_refs, tiles_per_seq):
    x = x_ref[...]
    first = (pl.program_id(0) % tiles_per_seq) == 0
    prev_row = jnp.where(first, 0.0, p_ref[SUBLANES - 1:SUBLANES, :])
    row = lax.broadcasted_iota(jnp.int32, x.shape, 0)
    dx = jnp.where(row == 0, prev_row, pltpu.roll(x, 1, axis=0)) - x
    for j, o_ref in enumerate(o_refs):
        o_ref[...] = (x + dx * mu_ref[j:j + 1, :]).astype(o_ref.dtype)


def _shift_mix(x, mu, s, tm=256):
    t, d = x.shape
    n = mu.shape[0]
    row = pl.BlockSpec((tm, d), lambda i: (i, 0))
    prev = pl.BlockSpec((SUBLANES, d), lambda i: (jnp.maximum(i * (tm // SUBLANES) - 1, 0), 0))
    return pl.pallas_call(
        functools.partial(_shift_mix_kernel, tiles_per_seq=s // tm),
        grid=(t // tm,),
        in_specs=[row, prev, pl.BlockSpec((n, d), lambda i: (0, 0))],
        out_specs=[row] * n,
        out_shape=[jax.ShapeDtypeStruct((t, d), BF16)] * n,
        compiler_params=_params("parallel"),
        name="shift_mix",
    )(x, x, mu)


def _softplus(x):
    return jnp.maximum(x, 0.0) + jnp.log1p(jnp.exp(-jnp.abs(x)))


def _lora_kernel(x_ref, w1_ref, w2_ref, b_ref, o_ref, *, mid, out):
    hmid = jnp.dot(x_ref[...], w1_ref[...], preferred_element_type=F32)
    if mid == "tanh":
        hmid = jnp.tanh(hmid)
    elif mid == "sigmoid":
        hmid = jax.nn.sigmoid(hmid)
    y = jnp.dot(hmid.astype(BF16), w2_ref[...], preferred_element_type=F32)
    if out == "logdecay":
        y = -jnp.exp(-_softplus(-(b_ref[...] + y)) - 0.5)
    elif out == "sigmoid":
        y = jax.nn.sigmoid(b_ref[...] + y)
    o_ref[...] = y


def _lora(x, w1, w2, bias, mid, out, tm=512):
    t, d = x.shape
    tm = min(tm, t)
    r = w1.shape[1]
    rp = -(-r // LANES) * LANES
    w1p = jnp.pad(w1, ((0, 0), (0, rp - r))).astype(BF16)
    w2p = jnp.pad(w2, ((0, rp - r), (0, 0))).astype(BF16)
    dout = w2.shape[1]
    if bias is None:
        bias = jnp.zeros((dout,), F32)
    return pl.pallas_call(
        functools.partial(_lora_kernel, mid=mid, out=out),
        grid=(t // tm,),
        in_specs=[pl.BlockSpec((tm, d), lambda i: (i, 0)),
                  pl.BlockSpec((d, rp), lambda i: (0, 0)),
                  pl.BlockSpec((rp, dout), lambda i: (0, 0)),
                  pl.BlockSpec((1, dout), lambda i: (0, 0))],
        out_specs=pl.BlockSpec((tm, dout), lambda i: (i, 0)),
        out_shape=jax.ShapeDtypeStruct((t, dout), F32),
        compiler_params=_params("parallel"),
        name="lora_" + out,
    )(x, w1p, w2p, bias.reshape(1, dout))


def _head_sums(tiles, ones_b):
    c = tiles[0].shape[0]
    x = jnp.concatenate(tiles, axis=0)
    hi = x.astype(BF16)
    lo = (x - hi.astype(F32)).astype(BF16)
    tot = jnp.dot(hi, ones_b, preferred_element_type=F32) + jnp.dot(lo, ones_b, preferred_element_type=F32)
    return [tot[p * c:(p + 1) * c] for p in range(len(tiles))]


def _bdot(a, b):
    return jnp.dot(a.astype(BF16), b.astype(BF16), preferred_element_type=F32)


def _rwkv_chunk_pairs(r, ld, kh, v, kk, bb, g_cum, s0, c):
    pairs = range(len(r))
    lane = lax.broadcasted_iota(jnp.int32, (1, LANES), 1)
    m0 = jnp.where(lane < C_HEAD_DIM, 1.0, 0.0)
    m1 = 1.0 - m0
    n2 = 2 * c
    row = lax.broadcasted_iota(jnp.int32, (n2, n2), 0)
    col = lax.broadcasted_iota(jnp.int32, (n2, n2), 1)
    lower, lower_eq = col < row, col <= row

    def stack(x):
        return jnp.concatenate([x * m0, x * m1], axis=0)

    e_g = [jnp.exp(g_cum[p]) for p in pairs]
    e_ng = [jnp.exp(-g_cum[p]) for p in pairs]
    al2 = [stack(-(kk[p] * jnp.exp(g_cum[p] - ld[p]))) for p in pairs]
    be2 = [stack(bb[p] * e_ng[p]).astype(BF16) for p in pairs]
    kb2 = [stack(kh[p] * e_ng[p]).astype(BF16) for p in pairs]
    rb = [r[p] * e_g[p] for p in pairs]
    v2 = [stack(v[p]) for p in pairs]
    sc = [lax.dot_general(jnp.concatenate([al2[p], stack(rb[p])], axis=0).astype(BF16),
                          jnp.concatenate([be2[p], kb2[p]], axis=0), _NT,
                          preferred_element_type=F32) for p in pairs]
    l_ab = [jnp.where(lower, sc[p][:n2, :n2], 0.0) for p in pairs]
    l_ak = [jnp.where(lower, sc[p][:n2, n2:], 0.0) for p in pairs]
    l_r = [jnp.concatenate([jnp.where(lower_eq, sc[p][n2:, :n2], 0.0),
                            jnp.where(lower_eq, sc[p][n2:, n2:], 0.0)], axis=1).astype(BF16) for p in pairs]

    t_off = l_ab
    pw = [x.astype(BF16) for x in l_ab]
    for _ in range(max(c.bit_length() - 2, 0)):
        pw_f = [jnp.dot(pw[p], pw[p], preferred_element_type=F32) for p in pairs]
        pw = [x.astype(BF16) for x in pw_f]
        t_off = [t_off[p] + pw_f[p] + jnp.dot(t_off[p].astype(BF16), pw[p], preferred_element_type=F32)
                 for p in pairs]

    w0 = [jnp.concatenate([al2[p], _bdot(l_ak[p], v2[p])], axis=1) for p in pairs]
    x = [w0[p] + _bdot(t_off[p], w0[p]) for p in pairs]
    xb = [x[p].astype(BF16) for p in pairs]
    z = [jnp.dot(l_r[p],
                 jnp.concatenate([xb[p], jnp.concatenate([jnp.zeros_like(v2[p]), v2[p]], axis=1).astype(BF16)],
                                 axis=0), preferred_element_type=F32) for p in pairs]
    r_eff = [(rb[p] + z[p][:c, :LANES] + z[p][c:, :LANES]).astype(BF16) for p in pairs]
    y0 = [z[p][:c, LANES:] + z[p][c:, LANES:] for p in pairs]
    mn = [lax.dot_general(xb[p], be2[p], _TN, preferred_element_type=F32) for p in pairs]
    n_add = [mn[p][LANES:] + lax.dot_general(v2[p].astype(BF16), kb2[p], _TN, preferred_element_type=F32)
             for p in pairs]
    s0b = [s0[p].astype(BF16) for p in pairs]
    y = [y0[p] + lax.dot_general(r_eff[p], s0b[p], _NT, preferred_element_type=F32) for p in pairs]
    s_new = [(s0[p] + jnp.dot(s0b[p], mn[p][:LANES].astype(BF16), preferred_element_type=F32) + n_add[p])
             * e_g[p][c - 1:c, :] for p in pairs]
    return y, s_new


def _rwkv_scan_kernel(*refs, c, npair, vres):
    if vres:
        (r_ref, ld_ref, k_ref, a_ref, v_ref, g_ref, vf_ref, vg_ref,
         kkp_ref, kap_ref, rkp_ref, lg_ref, lb_ref, tri_ref, o_ref, s_ref) = refs
    else:
        (r_ref, ld_ref, k_ref, a_ref, v_ref, g_ref,
         kkp_ref, kap_ref, rkp_ref, lg_ref, lb_ref, tri_ref, o_ref, s_ref) = refs

    @pl.when(pl.program_id(2) == 0)
    def _reset():
        s_ref[...] = jnp.zeros_like(s_ref)

    pairs = range(npair)
    sl = [slice(p * LANES, (p + 1) * LANES) for p in pairs]
    take = lambda ref: [ref[:, q] for q in sl]
    shift = C_HEAD_DIM.bit_length() - 1
    ones_b = jnp.where(jnp.right_shift(lax.broadcasted_iota(jnp.int32, (LANES, LANES), 0), shift)
                       == jnp.right_shift(lax.broadcasted_iota(jnp.int32, (LANES, LANES), 1), shift),
                       1.0, 0.0).astype(BF16)
    g_all = jnp.dot(tri_ref[...], ld_ref[...], precision=lax.Precision.HIGHEST, preferred_element_type=F32)

    r, ld, k, a, v, gate = (take(x) for x in (r_ref, ld_ref, k_ref, a_ref, v_ref, g_ref))
    kk_raw = [k[p] * kkp_ref[:, sl[p]] for p in pairs]
    sq = _head_sums([x * x for x in kk_raw], ones_b)
    kk = [kk_raw[p] / jnp.maximum(jnp.sqrt(sq[p]), KK_EPS) for p in pairs]
    bb = [kk[p] * a[p] for p in pairs]
    kh = [k[p] * (1.0 + (a[p] - 1.0) * kap_ref[:, sl[p]]) for p in pairs]
    if vres:
        vf, vg = take(vf_ref), take(vg_ref)
        v = [v[p] + (vf[p] - v[p]) * vg[p] for p in pairs]

    y, s_new = _rwkv_chunk_pairs(r, ld, kh, v, kk, bb, [g_all[:, q] for q in sl],
                                 [s_ref[p] for p in pairs], c)

    inv_n = 1.0 / C_HEAD_DIM
    mean = _head_sums(y, ones_b)
    yc = [y[p] - mean[p] * inv_n for p in pairs]
    var = _head_sums([x * x for x in yc], ones_b)
    bonus = _head_sums([r[p] * kh[p] * rkp_ref[:, sl[p]] for p in pairs], ones_b)
    for p in pairs:
        yn = yc[p] * lax.rsqrt(var[p] * inv_n + LNX_EPS) * lg_ref[:, sl[p]] + lb_ref[:, sl[p]]
        o_ref[:, sl[p]] = ((yn + bonus[p] * v[p]) * gate[p]).astype(o_ref.dtype)
        s_ref[p] = s_new[p]


RWKV_CHUNK = 64
RWKV_PAIRS = 8


def _rwkv_scan(r, log_w, k, a, v, gate, v_first, vgate, k_k, k_a, r_k, lnx_g, lnx_b, bsz, s):
    t, d = r.shape
    c = min(RWKV_CHUNK, s)
    wid = RWKV_PAIRS * LANES
    nc = s // c
    vres = v_first is not None
    tri = (jnp.arange(c)[None, :] <= jnp.arange(c)[:, None]).astype(F32)
    blk = pl.BlockSpec((c, wid), lambda b, g, i: (b * nc + i, g))
    vec = pl.BlockSpec((1, wid), lambda b, g, i: (0, g))
    rows = [r, log_w, k, a, v, gate] + ([v_first, vgate] if vres else [])
    vecs = [x.reshape(1, d) for x in (k_k, k_a, r_k, lnx_g, lnx_b)]
    return pl.pallas_call(
        functools.partial(_rwkv_scan_kernel, c=c, npair=RWKV_PAIRS, vres=vres),
        grid=(bsz, d // wid, nc),
        in_specs=[blk] * len(rows) + [vec] * len(vecs) + [pl.BlockSpec((c, c), lambda b, g, i: (0, 0))],
        out_specs=blk,
        out_shape=jax.ShapeDtypeStruct((t, d), BF16),
        scratch_shapes=[pltpu.VMEM((RWKV_PAIRS, LANES, LANES), F32)],
        compiler_params=_params("parallel", "parallel", "arbitrary"),
        name="rwkv_scan",
    )(*rows, *vecs, tri)


def _rwkv7_mixer(x, v_first, mu, w_r, w_k, w_v, w_o, w0, w1, w2, a0, a1, a2, g1, g2,
                 k_k, k_a, r_k, lnx_g, lnx_b, vres, bsz, s):
    xr, xw, xk, xv, xa, xg = _shift_mix(x, mu, s)
    r = _matmul(xr, w_r.astype(BF16))
    k = _matmul(xk, w_k.astype(BF16))
    v = _matmul(xv, w_v.astype(BF16))
    log_decay = _lora(xw, w1, w2, w0, "tanh", "logdecay")
    a = _lora(xa, a1, a2, a0, "none", "sigmoid")
    g = _lora(xg, g1, g2, None, "sigmoid", "none")
    if vres is None:
        vgate = None
        v_keep = v
    else:
        v0, v1, v2 = vres
        vgate = _lora(xv, v1, v2, v0, "none", "sigmoid")
        v_keep = v_first
    yo = _rwkv_scan(r, log_decay, k, a, v, g, None if vres is None else v_first, vgate,
                    k_k, k_a, r_k, lnx_g, lnx_b, bsz, s)
    return _matmul(yo, w_o.astype(BF16)), v_keep


def _lane_pick(cond, lanef):
    return jnp.min(jnp.where(cond, lanef, float(LANES)), axis=1, keepdims=True)


def _router_kernel(x_ref, w_ref, b_ref, route_ref, cnt_ref, run_ref, *, tm):
    @pl.when(pl.program_id(0) == 0)
    def _reset():
        run_ref[...] = jnp.zeros_like(run_ref)

    logits = jnp.dot(x_ref[...].astype(BF16), w_ref[...], preferred_element_type=F32) + b_ref[...]
    lane = lax.broadcasted_iota(jnp.int32, (tm, LANES), 1)
    lanef = lane.astype(F32)

    def masked_softmax(keep):
        z = jnp.where(keep, logits, -jnp.inf)
        e = jnp.exp(z - jnp.max(z, axis=1, keepdims=True))
        return e / jnp.sum(e, axis=1, keepdims=True)

    in_groups = lane < N_GROUPS
    pg = masked_softmax(in_groups)
    g_p = jnp.max(pg, axis=1, keepdims=True)
    g_idx = _lane_pick(jnp.where(in_groups, pg, -1.0) == g_p, lanef)
    lo = float(N_GROUPS) + float(EXPERTS_PER_GROUP) * g_idx
    half = 0.5 * float(EXPERTS_PER_GROUP - 1)
    in_group = jnp.abs(lanef - lo - half) < half + 0.5
    pe = jnp.where(in_group, masked_softmax(in_group), -1.0)
    p1 = jnp.max(pe, axis=1, keepdims=True)
    i1 = _lane_pick(pe == p1, lanef)
    pe2 = jnp.where(lanef == i1, -1.0, pe)
    p2 = jnp.max(pe2, axis=1, keepdims=True)
    i2 = _lane_pick(pe2 == p2, lanef)
    den = p1 + p2
    gate = (g_p * (p1 / den), g_p * (p2 / den))
    expert = (i1 - float(N_GROUPS), i2 - float(N_GROUPS))

    earlier = jnp.where(lax.broadcasted_iota(jnp.int32, (tm, tm), 1)
                        < lax.broadcasted_iota(jnp.int32, (tm, tm), 0), 1.0, 0.0).astype(BF16)
    run = run_ref[...]
    rank = []
    for e in expert:
        onehot = jnp.where(lanef == e, 1.0, 0.0)
        before = jnp.dot(earlier, onehot.astype(BF16), preferred_element_type=F32) + run
        rank.append(jnp.sum(onehot * before, axis=1, keepdims=True))
        run = run + jnp.sum(onehot, axis=0, keepdims=True)
    run_ref[...] = run
    cnt_ref[...] = run
    cols = (expert[0], expert[1], rank[0], rank[1], gate[0], gate[1])
    route = jnp.zeros((tm, LANES), F32)
    for c, val in enumerate(cols):
        route = jnp.where(lane == c, val, route)
    route_ref[...] = route


def _router(x, w_router, b_router, tm=256):
    t, d = x.shape
    tm = min(tm, t)
    return pl.pallas_call(
        functools.partial(_router_kernel, tm=tm),
        grid=(t // tm,),
        in_specs=[pl.BlockSpec((tm, d), lambda i: (i, 0)),
                  pl.BlockSpec((d, LANES), lambda i: (0, 0)),
                  pl.BlockSpec((1, LANES), lambda i: (0, 0))],
        out_specs=[pl.BlockSpec((tm, LANES), lambda i: (i, 0)),
                   pl.BlockSpec((1, LANES), lambda i: (0, 0))],
        out_shape=[jax.ShapeDtypeStruct((t, LANES), F32), jax.ShapeDtypeStruct((1, LANES), F32)],
        scratch_shapes=[pltpu.VMEM((1, LANES), F32)],
        compiler_params=_params("arbitrary"),
        name="router",
    )(x, w_router, b_router)


SLAB = 2048 // LANES
DMA_UNROLL = 4


def _slab(ref, r):
    return ref.at[pl.ds(pl.multiple_of(r * SLAB, SLAB), SLAB)]


def _slab_rows(ref, rows):
    return jnp.concatenate([ref[pl.ds(c, rows, stride=SLAB), :] for c in range(SLAB)], axis=1)


def _store_slab_rows(ref, val, rows):
    for c in range(SLAB):
        ref[pl.ds(c, rows, stride=SLAB), :] = val[:, c * LANES:(c + 1) * LANES]


def _expert_kernel(be_ref, nu_ref, idx_ref, nxt_ref, x_hbm, wg_ref, wu_ref, wd_ref, o_ref,
                   xbuf, sems, wg_b, wu_b, wd_b, *, rows):
    i = pl.program_id(0)
    n_used = nu_ref[0]
    slot = i % 2

    def token_copy(tok, r, dst_slot):
        return pltpu.make_async_copy(_slab(x_hbm, tok), _slab(xbuf.at[dst_slot], r), sems.at[dst_slot])

    def gather(ids_ref, dst_slot):
        def start(r, carry):
            for rr in (r, r + rows // 2):
                token_copy(ids_ref[0, rr], rr, dst_slot).start()
            return carry
        lax.fori_loop(0, rows // 2, start, 0, unroll=DMA_UNROLL)

    @pl.when(i == 0)
    def _():
        gather(idx_ref, 0)

    @pl.when(i + 1 < n_used)
    def _():
        gather(nxt_ref, 1 - slot)

    @pl.when(jnp.logical_or(i == 0, be_ref[i] != be_ref[jnp.maximum(i - 1, 0)]))
    def _():
        wg_b[...] = wg_ref[...].astype(BF16)
        wu_b[...] = wu_ref[...].astype(BF16)
        wd_b[...] = wd_ref[...].astype(BF16)

    @pl.when(i < n_used)
    def _():
        for r in range(rows):
            token_copy(0, r, slot).wait()
        xb = _slab_rows(xbuf.at[slot], rows).astype(BF16)
        hg = jnp.dot(xb, wg_b[...], preferred_element_type=F32)
        hu = jnp.dot(xb, wu_b[...], preferred_element_type=F32)
        hid = (hg * jax.nn.sigmoid(hg) * hu).astype(BF16)
        _store_slab_rows(o_ref, jnp.dot(hid, wd_b[...], preferred_element_type=F32), rows)

    @pl.when(i >= n_used)
    def _():
        o_ref[...] = jnp.zeros_like(o_ref)


def _expert_mlp(x_slabs, src_row, block_e, n_used, w_gate, w_up, w_down, layer, rows=MOE_ROWS):
    d, ff = w_gate.shape[2], w_gate.shape[3]
    nb = src_row.shape[0] // rows
    ids = src_row.reshape(nb, 1, rows)
    grid_spec = pltpu.PrefetchScalarGridSpec(
        num_scalar_prefetch=2,
        grid=(nb,),
        in_specs=[pl.BlockSpec((None, 1, rows), lambda i, be, nu: (i, 0, 0), memory_space=pltpu.SMEM),
                  pl.BlockSpec((None, 1, rows), lambda i, be, nu: (jnp.minimum(i + 1, nb - 1), 0, 0),
                               memory_space=pltpu.SMEM),
                  pl.BlockSpec(memory_space=pl.ANY),
                  pl.BlockSpec((None, None, d, ff), lambda i, be, nu: (layer, be[i], 0, 0)),
                  pl.BlockSpec((None, None, d, ff), lambda i, be, nu: (layer, be[i], 0, 0)),
                  pl.BlockSpec((None, None, ff, d), lambda i, be, nu: (layer, be[i], 0, 0))],
        out_specs=pl.BlockSpec((rows * SLAB, LANES), lambda i, be, nu: (i, 0)),
        scratch_shapes=[pltpu.VMEM((2, rows * SLAB, LANES), F32), pltpu.SemaphoreType.DMA((2,)),
                        pltpu.VMEM((d, ff), BF16), pltpu.VMEM((d, ff), BF16), pltpu.VMEM((ff, d), BF16)],
    )
    return pl.pallas_call(
        functools.partial(_expert_kernel, rows=rows),
        grid_spec=grid_spec,
        out_shape=jax.ShapeDtypeStruct((nb * rows * SLAB, LANES), F32),
        compiler_params=_params("arbitrary"),
        name="expert_mlp",
    )(block_e, n_used, ids, ids, x_slabs, w_gate, w_up, w_down)


def _combine_ln_kernel(idx_ref, nxt_ref, y_ref, x_ref, gt_ref, g_ref, b_ref, o_ref, buf, sems, *, rows):
    i = pl.program_id(0)
    slot = i % 2

    def row_copy(src, r, kslot, dst_slot):
        return pltpu.make_async_copy(_slab(y_ref, src), _slab(buf.at[dst_slot, kslot], r), sems.at[dst_slot])

    def gather(ids_ref, dst_slot):
        def start(r, carry):
            for kslot in range(MOE_TOPK):
                for rr in (r, r + rows // 2):
                    row_copy(ids_ref[0, MOE_TOPK * rr + kslot], rr, kslot, dst_slot).start()
            return carry
        lax.fori_loop(0, rows // 2, start, 0, unroll=DMA_UNROLL)

    @pl.when(i == 0)
    def _():
        gather(idx_ref, 0)

    @pl.when(i + 1 < pl.num_programs(0))
    def _():
        gather(nxt_ref, 1 - slot)

    for r in range(rows):
        for kslot in range(MOE_TOPK):
            row_copy(0, r, kslot, slot).wait()
    gt = gt_ref[...]
    ffn = gt[:, 0:1] * _slab_rows(buf.at[slot, 0], rows)
    for kslot in range(1, MOE_TOPK):
        ffn = ffn + gt[:, kslot:kslot + 1] * _slab_rows(buf.at[slot, kslot], rows)
    o_ref[...] = _ln_rows(DN_ALPHA * x_ref[...] + ffn, g_ref[...], b_ref[...])


def _combine_ln(y_sorted, dest, gates, x, g, b, rows=256):
    t, d = x.shape
    nb = t // rows
    row = pl.BlockSpec((rows, d), lambda i: (i, 0))
    vec = pl.BlockSpec((1, d), lambda i: (0, 0))
    ids = dest.reshape(nb, 1, rows * MOE_TOPK)
    return pl.pallas_call(
        functools.partial(_combine_ln_kernel, rows=rows),
        grid=(nb,),
        in_specs=[pl.BlockSpec((None, 1, rows * MOE_TOPK), lambda i: (i, 0, 0), memory_space=pltpu.SMEM),
                  pl.BlockSpec((None, 1, rows * MOE_TOPK), lambda i: (jnp.minimum(i + 1, nb - 1), 0, 0),
                               memory_space=pltpu.SMEM),
                  pl.BlockSpec(memory_space=pl.ANY), row,
                  pl.BlockSpec((rows, MOE_TOPK), lambda i: (i, 0)), vec, vec],
        out_specs=row,
        out_shape=jax.ShapeDtypeStruct((t, d), F32),
        scratch_shapes=[pltpu.VMEM((2, MOE_TOPK, rows * SLAB, LANES), F32), pltpu.SemaphoreType.DMA((2,))],
        compiler_params=_params("arbitrary"),
        name="combine_ln",
    )(ids, ids, y_sorted, x, gates, g.reshape(1, d), b.reshape(1, d))


def _moe_layer(x, x_slabs, w_grp, b_grp, w_exp_r, b_exp_r, w_gate, w_up, w_down, ln_g, ln_b, layer):
    t, d = x.shape
    n_r = N_GROUPS + N_EXPERTS
    w_router = jnp.pad(jnp.concatenate([w_grp, w_exp_r], axis=1), ((0, 0), (0, LANES - n_r))).astype(BF16)
    b_router = jnp.pad(jnp.concatenate([b_grp, b_exp_r]), (0, LANES - n_r)).reshape(1, LANES)
    route, cnt = _router(x, w_router, b_router)
    flat_e = route[:, 0:MOE_TOPK].astype(jnp.int32).reshape(-1)
    rank = route[:, MOE_TOPK:2 * MOE_TOPK].astype(jnp.int32).reshape(-1)
    gates = route[:, 2 * MOE_TOPK:3 * MOE_TOPK]
    counts = cnt[0, :N_EXPERTS].astype(jnp.int32)
    m = t * MOE_TOPK
    padded = (counts + MOE_ROWS - 1) // MOE_ROWS * MOE_ROWS
    pad_end = jnp.cumsum(padded)
    dest = (pad_end - padded)[flat_e] + rank
    n_blocks = (m + N_EXPERTS * (MOE_ROWS - 1) + MOE_ROWS - 1) // MOE_ROWS
    src_row = jnp.zeros((n_blocks * MOE_ROWS,), jnp.int32).at[dest].set(
        (jnp.arange(m) // MOE_TOPK).astype(jnp.int32))
    starts = jnp.arange(n_blocks, dtype=jnp.int32) * MOE_ROWS
    block_e = jnp.minimum(jnp.sum((pad_end[None, :] <= starts[:, None]).astype(jnp.int32), axis=1),
                          N_EXPERTS - 1).astype(jnp.int32)
    n_used = (pad_end[-1] // MOE_ROWS).astype(jnp.int32).reshape(1)

    ys = _expert_mlp(x_slabs, src_row, block_e, n_used, w_gate, w_up, w_down, layer)
    return _combine_ln(ys, dest.astype(jnp.int32), gates, x, ln_g, ln_b)


def kernel(x, ev_w_in, ev_gate_bias, ev_conv_w, ev_conv_b, ev_hnorm_g, ev_w_out, od_mu, od_w_r, od_w_k, od_w_v, od_w_o, od_w0, od_w1, od_w2, od_a0, od_a1, od_a2, od_g1, od_g2, od_k_k, od_k_a, od_r_k, od_lnx_g, od_lnx_b, od_v0, od_v1, od_v2, ln_mix_g, ln_mix_b, ln_ffn_g, ln_ffn_b, moe_w_grp, moe_b_grp, moe_w_exp_r, moe_b_exp_r, moe_w_gate, moe_w_up, moe_w_down):
    bsz, s, d = x.shape
    depth = ln_mix_g.shape[0]
    xf = x.reshape(bsz * s, d)
    v_first = None
    for layer in range(depth):
        if layer % 2 == 0:
            e = layer // 2
            mix = _even_mixer(xf, ev_w_in[e], ev_gate_bias[e], ev_conv_w[e], ev_conv_b[e],
                              ev_hnorm_g[e], ev_w_out[e], bsz, s)
        else:
            o = layer // 2
            vres = None if o == 0 else (od_v0[o - 1], od_v1[o - 1], od_v2[o - 1])
            mix, v_first = _rwkv7_mixer(xf, v_first, od_mu[o], od_w_r[o], od_w_k[o], od_w_v[o],
                                        od_w_o[o], od_w0[o], od_w1[o], od_w2[o], od_a0[o],
                                        od_a1[o], od_a2[o], od_g1[o], od_g2[o], od_k_k[o],
                                        od_k_a[o], od_r_k[o], od_lnx_g[o], od_lnx_b[o], vres, bsz, s)
        xf, x_slabs = _resid_ln(xf, mix, ln_mix_g[layer], ln_mix_b[layer])
        xf = _moe_layer(xf, x_slabs, moe_w_grp[layer], moe_b_grp[layer], moe_w_exp_r[layer],
                        moe_b_exp_r[layer], moe_w_gate, moe_w_up, moe_w_down,
                        ln_ffn_g[layer], ln_ffn_b[layer], layer)
    return xf.reshape(bsz, s, d)
```

```python
import functools
import math

import jax
import jax.numpy as jnp
from jax import lax
from jax.experimental import pallas as pl
from jax.experimental.pallas import tpu as pltpu

F32 = jnp.float32
BF16 = jnp.bfloat16

DEPTH = 4
DN_ALPHA = (2 * DEPTH) ** 0.25
LN_EPS = 1e-5

A_HEADS = 8
A_HEAD_DIM = 128
A_WIDTH = A_HEADS * A_HEAD_DIM
MOBA_BLOCK = 256
MOBA_TOPK = 3
MOBA_GROUP = 4
ROPE_THETA = 500000.0
ROPE_DIM = A_HEAD_DIM // 4
NEG_INF = -1e30

B_HEADS = 4
B_HEAD_DIM = 256
B_WIDTH = B_HEADS * B_HEAD_DIM
MLSTM_CHUNK = 64
B_CONV = 4
HNORM_EPS = 1e-6
EVEN_MAIN = 3 * A_WIDTH + 4 * B_WIDTH

C_HEAD_DIM = 64
LNX_EPS = 64e-5
KK_EPS = 1e-12

N_GROUPS = 4
EXPERTS_PER_GROUP = 8
N_EXPERTS = N_GROUPS * EXPERTS_PER_GROUP
MOE_TOPK = 2
MOE_ROWS = 256

LANES = 128
SUBLANES = 8
VMEM_LIMIT = 56 * 1024 * 1024

_NT = (((1,), (1,)), ((), ()))
_TN = (((0,), (0,)), ((), ()))


def _params(*sem):
    return pltpu.CompilerParams(dimension_semantics=sem, vmem_limit_bytes=VMEM_LIMIT)


def _mm_kernel(x_ref, w_ref, o_ref):
    o_ref[...] = jnp.dot(x_ref[...].astype(BF16), w_ref[...],
                         preferred_element_type=F32).astype(o_ref.dtype)


def _matmul(x, w, out_dtype=F32, tm=1024, tn=512):
    m, k = x.shape
    n = w.shape[1]
    tm, tn = min(tm, m), min(tn, n)
    return pl.pallas_call(
        _mm_kernel,
        grid=(m // tm, n // tn),
        in_specs=[pl.BlockSpec((tm, k), lambda i, j: (i, 0)),
                  pl.BlockSpec((k, tn), lambda i, j: (0, j))],
        out_specs=pl.BlockSpec((tm, tn), lambda i, j: (i, j)),
        out_shape=jax.ShapeDtypeStruct((m, n), out_dtype),
        compiler_params=_params("parallel", "arbitrary"),
        name="matmul",
    )(x, w)


def _mm2_kernel(xa_ref, xb_ref, wa_ref, wb_ref, o_ref):
    acc = jnp.dot(xa_ref[...], wa_ref[...], preferred_element_type=F32)
    acc = acc + jnp.dot(xb_ref[...], wb_ref[...], preferred_element_type=F32)
    o_ref[...] = acc


def _matmul2(xa, xb, wa, wb, tm=1024, tn=512):
    m, ka = xa.shape
    kb = xb.shape[1]
    n = wa.shape[1]
    tm, tn = min(tm, m), min(tn, n)
    return pl.pallas_call(
        _mm2_kernel,
        grid=(m // tm, n // tn),
        in_specs=[pl.BlockSpec((tm, ka), lambda i, j: (i, 0)),
                  pl.BlockSpec((tm, kb), lambda i, j: (i, 0)),
                  pl.BlockSpec((ka, tn), lambda i, j: (0, j)),
                  pl.BlockSpec((kb, tn), lambda i, j: (0, j))],
        out_specs=pl.BlockSpec((tm, tn), lambda i, j: (i, j)),
        out_shape=jax.ShapeDtypeStruct((m, n), F32),
        compiler_params=_params("parallel", "arbitrary"),
        name="matmul2",
    )(xa, xb, wa, wb)


def _ln_rows(h, g, b):
    mu = jnp.mean(h, axis=-1, keepdims=True)
    hc = h - mu
    var = jnp.mean(hc * hc, axis=-1, keepdims=True)
    return hc * lax.rsqrt(var + LN_EPS) * g + b


def _resid_ln_kernel(x_ref, m_ref, g_ref, b_ref, w_ref, br_ref, o_ref, s_ref, route_ref, cnt_ref, run_ref,
                     *, tm):
    y = _ln_rows(DN_ALPHA * x_ref[...] + m_ref[...], g_ref[...], b_ref[...])
    o_ref[...] = y.astype(o_ref.dtype)
    _store_slab_rows(s_ref, y, tm)
    _route_tile(y, w_ref, br_ref, route_ref, cnt_ref, run_ref, tm)


def _resid_ln(x, mix, g, b, w_router, b_router, tm=256):
    t, d = x.shape
    tm = min(tm, t)
    row = pl.BlockSpec((tm, d), lambda i: (i, 0))
    vec = pl.BlockSpec((1, d), lambda i: (0, 0))
    one = pl.BlockSpec((1, LANES), lambda i: (0, 0))
    return pl.pallas_call(
        functools.partial(_resid_ln_kernel, tm=tm),
        grid=(t // tm,),
        in_specs=[row, row, vec, vec, pl.BlockSpec((d, LANES), lambda i: (0, 0)), one],
        out_specs=[row, pl.BlockSpec((tm * SLAB, LANES), lambda i: (i, 0)),
                   pl.BlockSpec((tm, LANES), lambda i: (i, 0)), one],
        out_shape=[jax.ShapeDtypeStruct((t, d), F32), jax.ShapeDtypeStruct((t * SLAB, LANES), F32),
                   jax.ShapeDtypeStruct((t, LANES), F32), jax.ShapeDtypeStruct((1, LANES), F32)],
        scratch_shapes=[pltpu.VMEM((1, LANES), F32)],
        compiler_params=_params("arbitrary"),
        name="resid_ln_router",
    )(x, mix, g.reshape(1, d), b.reshape(1, d), w_router, b_router)


def _rope_tables(s):
    half = ROPE_DIM // 2
    inv = jnp.power(jnp.float32(ROPE_THETA), -jnp.arange(half, dtype=F32) / half)
    ang = jnp.arange(s).astype(F32)[:, None] * inv[None, :]
    cos, sin = jnp.cos(ang), jnp.sin(ang)
    zero = jnp.zeros((s, A_HEAD_DIM - ROPE_DIM), F32)
    zh = jnp.zeros((s, half), F32)
    c = jnp.concatenate([cos, cos, zero + 1.0], axis=-1)
    s_up = jnp.concatenate([zh, sin, zero], axis=-1)
    s_dn = jnp.concatenate([-sin, zh, zero], axis=-1)
    return c, s_up, s_dn


def _rope(x, c, s_up, s_dn):
    half = ROPE_DIM // 2
    return (x * c + pltpu.roll(x, half, axis=1) * s_up
            + pltpu.roll(x, A_HEAD_DIM - half, axis=1) * s_dn)


def _moba_kernel(q_ref, k_ref, v_ref, c_ref, su_ref, sd_ref, o_ref, krot_ref, vb_ref, kmean_ref, *, nblk):
    qi = pl.program_id(2)
    blk = MOBA_BLOCK
    scale = A_HEAD_DIM ** -0.5

    @pl.when(qi == 0)
    def _prepare_keys():
        def body(j, carry):
            r = pl.multiple_of(j * blk, blk)
            rows = pl.ds(r, blk)
            kr = _rope(k_ref[rows, :].astype(F32), c_ref[rows, :], su_ref[rows, :], sd_ref[rows, :])
            block_tag = jnp.where(lax.broadcasted_iota(jnp.int32, (blk, LANES), 1) == j, 1.0, 0.0)
            krot_ref[rows, :] = jnp.concatenate([kr, block_tag], axis=1).astype(BF16)
            vb_ref[rows, :] = v_ref[rows, :].astype(BF16)
            kmean_ref[pl.ds(j, 1), :] = jnp.mean(kr, axis=0, keepdims=True)
            return carry
        lax.fori_loop(0, nblk, body, 0)

    r0 = pl.multiple_of(qi * blk, blk)
    qrows = pl.ds(r0, blk)
    qb = _rope(q_ref[...].astype(F32), c_ref[qrows, :], su_ref[qrows, :], sd_ref[qrows, :]).astype(BF16)

    g = lax.dot_general(kmean_ref[...].astype(BF16), qb, _NT, preferred_element_type=F32)
    brow = lax.broadcasted_iota(jnp.int32, (nblk, blk), 0)
    g = jnp.where(brow < qi, g, -jnp.inf)
    sel_t = jnp.zeros((nblk, blk), F32)
    for j in range(nblk):
        gj = g[j:j + 1, :]
        earlier = jnp.where(brow < j, 1.0, 0.0)
        beats = jnp.where(g > gj, 1.0, jnp.where(g == gj, earlier, 0.0))
        cnt = jnp.sum(beats, axis=0, keepdims=True)
        sel_t = jnp.where(brow == j, jnp.where(cnt < float(min(MOBA_TOPK, nblk)), 1.0, 0.0), sel_t)
    sel_t = jnp.where(brow < qi, sel_t, 0.0)
    eye = jnp.where(lax.broadcasted_iota(jnp.int32, (blk, blk), 0)
                    == lax.broadcasted_iota(jnp.int32, (blk, blk), 1), 1.0, 0.0).astype(BF16)
    sel_pad = jnp.concatenate([sel_t, jnp.zeros((LANES - nblk, blk), F32)], axis=0).astype(BF16)
    sel = lax.dot_general(eye, sel_pad, _NT, preferred_element_type=F32)
    lane = lax.broadcasted_iota(jnp.int32, (blk, LANES), 1)
    bias = jnp.where(sel > 0.5, 0.0, jnp.where(lane < nblk, NEG_INF / scale, 0.0))
    q_aug = jnp.concatenate([qb, bias.astype(BF16)], axis=1)

    kd = krot_ref[qrows, :A_HEAD_DIM]
    s = lax.dot_general(qb, kd, _NT, preferred_element_type=F32) * scale
    causal = (lax.broadcasted_iota(jnp.int32, (blk, blk), 1)
              <= lax.broadcasted_iota(jnp.int32, (blk, blk), 0))
    s = jnp.where(causal, s, NEG_INF)
    m0 = jnp.max(s, axis=1, keepdims=True)
    p = jnp.exp(s - m0)
    l0 = jnp.sum(p, axis=1, keepdims=True)
    acc0 = jnp.dot(p.astype(BF16), vb_ref[qrows, :], preferred_element_type=F32)

    span = MOBA_GROUP * blk

    def past_group(jg, carry):
        m, l, acc = carry
        rows = pl.ds(pl.multiple_of(jg * span, span), span)
        sj = lax.dot_general(q_aug, krot_ref[rows, :], _NT, preferred_element_type=F32) * scale
        m_new = jnp.maximum(m, jnp.max(sj, axis=1, keepdims=True))
        a = jnp.exp(m - m_new)
        pj = jnp.exp(sj - m_new)
        l = a * l + jnp.sum(pj, axis=1, keepdims=True)
        acc = a * acc + jnp.dot(pj.astype(BF16), vb_ref[rows, :], preferred_element_type=F32)
        return m_new, l, acc

    n_groups = (qi + MOBA_GROUP - 1) // MOBA_GROUP
    _, l, acc = lax.fori_loop(0, n_groups, past_group, (m0, l0, acc0))
    o_ref[...] = (acc / l).astype(o_ref.dtype)


def _moba(z, bsz, s):
    nblk = s // MOBA_BLOCK
    c, s_up, s_dn = _rope_tables(s)
    tab = pl.BlockSpec((s, A_HEAD_DIM), lambda b, h, qi: (0, 0))
    return pl.pallas_call(
        functools.partial(_moba_kernel, nblk=nblk),
        grid=(bsz, A_HEADS, nblk),
        in_specs=[pl.BlockSpec((MOBA_BLOCK, A_HEAD_DIM), lambda b, h, qi: (b * nblk + qi, h)),
                  pl.BlockSpec((s, A_HEAD_DIM), lambda b, h, qi: (b, A_HEADS + h)),
                  pl.BlockSpec((s, A_HEAD_DIM), lambda b, h, qi: (b, 2 * A_HEADS + h)),
                  tab, tab, tab],
        out_specs=pl.BlockSpec((MOBA_BLOCK, A_HEAD_DIM), lambda b, h, qi: (b * nblk + qi, h)),
        out_shape=jax.ShapeDtypeStruct((bsz * s, A_WIDTH), BF16),
        scratch_shapes=[pltpu.VMEM((s, A_HEAD_DIM + LANES), BF16), pltpu.VMEM((s, A_HEAD_DIM), BF16),
                        pltpu.VMEM((nblk, A_HEAD_DIM), F32)],
        compiler_params=_params("parallel", "parallel", "arbitrary"),
        name="moba",
    )(z, z, z, c, s_up, s_dn)


def _log_sigmoid(x):
    return jnp.minimum(x, 0.0) - jnp.log1p(jnp.exp(-jnp.abs(x)))


def _shift_rows(x, prev8, d):
    if d == 0:
        return x
    rolled = pltpu.roll(x, d, axis=0)
    top = jnp.where(lax.broadcasted_iota(jnp.int32, prev8.shape, 0) < d,
                    pltpu.roll(prev8, d, axis=0), rolled[:SUBLANES])
    return jnp.concatenate([top, rolled[SUBLANES:]], axis=0)


def _mlstm_kernel(q_ref, k_ref, v_ref, o_ref, gc_ref, gr_ref, bc_ref, br_ref,
                  cwq_ref, cwk_ref, cbq_ref, cbk_ref, hg_ref, y_ref,
                  c_ref, n_ref, m_ref, pq_ref, pk_ref, *, tt):
    L = MLSTM_CHUNK
    dk = B_HEAD_DIM
    heads = range(B_HEADS)
    hs = [slice(h * dk, (h + 1) * dk) for h in heads]

    @pl.when(pl.program_id(1) == 0)
    def _reset():
        c_ref[...] = jnp.zeros_like(c_ref)
        n_ref[...] = jnp.zeros_like(n_ref)
        m_ref[...] = jnp.zeros_like(m_ref)
        pq_ref[...] = jnp.zeros_like(pq_ref)
        pk_ref[...] = jnp.zeros_like(pk_ref)

    def conv_silu(x_ref, p_ref, w_ref, b_ref):
        x = x_ref[...].astype(F32)
        prev8 = p_ref[...]
        out = b_ref[...] + x * w_ref[B_CONV - 1:B_CONV, :]
        for d in range(1, B_CONV):
            out = out + _shift_rows(x, prev8, d) * w_ref[B_CONV - 1 - d:B_CONV - d, :]
        p_ref[...] = x[tt - SUBLANES:, :]
        return out * jax.nn.sigmoid(out)

    q_all = conv_silu(q_ref, pq_ref, cwq_ref, cbq_ref)
    k_all = conv_silu(k_ref, pk_ref, cwk_ref, cbk_ref) * (dk ** -0.5)

    gcol = gc_ref[...] + bc_ref[...]
    grow = gr_ref[...] + br_ref[...]
    i_col_all = [gcol[:, h:h + 1] for h in heads]
    f_col_all = [_log_sigmoid(gcol[:, B_HEADS + h:B_HEADS + h + 1]) for h in heads]
    i_row_all = [grow[h:h + 1, :] for h in heads]
    f_row_all = [_log_sigmoid(grow[B_HEADS + h:B_HEADS + h + 1, :]) for h in heads]

    rr = lax.broadcasted_iota(jnp.int32, (L, L), 0)
    cc = lax.broadcasted_iota(jnp.int32, (L, L), 1)
    tri = cc <= rr

    for c in range(tt // L):
        lo, hi = c * L, (c + 1) * L
        qc = [q_all[lo:hi, hs[h]] for h in heads]
        kc = [k_all[lo:hi, hs[h]] for h in heads]
        qcb = [x.astype(BF16) for x in qc]
        kcb = [x.astype(BF16) for x in kc]
        vcb = [v_ref[lo:hi, hs[h]].astype(BF16) for h in heads]
        i_col = [x[lo:hi] for x in i_col_all]
        f_col = [x[lo:hi] for x in f_col_all]
        i_row = [x[:, lo:hi] for x in i_row_all]
        f_row = [x[:, lo:hi] for x in f_row_all]
        m_prev = [m_ref[h] for h in heads]
        cst = [c_ref[h] for h in heads]
        nst = [n_ref[h] for h in heads]

        b_col = [jnp.sum(jnp.where(tri, f_row[h], 0.0), axis=1, keepdims=True) for h in heads]
        b_row = [jnp.sum(jnp.where(rr <= cc, f_col[h], 0.0), axis=0, keepdims=True) for h in heads]
        log_inter = [b_col[h] + m_prev[h] for h in heads]
        log_intra = [jnp.where(tri, b_col[h] - b_row[h] + i_row[h], -jnp.inf) for h in heads]
        m_t = [jnp.maximum(log_inter[h], jnp.max(log_intra[h], axis=1, keepdims=True)) for h in heads]
        w_inter = [jnp.exp(log_inter[h] - m_t[h]) for h in heads]
        qk = [lax.dot_general(qcb[h], kcb[h], _NT, preferred_element_type=F32)
              * jnp.exp(log_intra[h] - m_t[h]) for h in heads]
        q_c = [jnp.dot(qcb[h], cst[h].astype(BF16), preferred_element_type=F32) for h in heads]
        qk_v = [jnp.dot(qk[h].astype(BF16), vcb[h], preferred_element_type=F32) for h in heads]
        den = [w_inter[h] * jnp.sum(qc[h] * nst[h], axis=1, keepdims=True)
               + jnp.sum(qk[h], axis=1, keepdims=True) for h in heads]
        hc = [(w_inter[h] * q_c[h] + qk_v[h]) / jnp.maximum(jnp.abs(den[h]), jnp.exp(-m_t[h])) for h in heads]

        b_last = [jnp.sum(f_col[h], axis=0, keepdims=True) for h in heads]
        log_s = [b_last[h] - b_col[h] + i_col[h] for h in heads]
        m_new = [jnp.maximum(b_last[h] + m_prev[h], jnp.max(log_s[h], axis=0, keepdims=True)) for h in heads]
        decay = [jnp.exp(b_last[h] + m_prev[h] - m_new[h]) for h in heads]
        kw = [kc[h] * jnp.exp(log_s[h] - m_new[h]) for h in heads]
        kv = [lax.dot_general(kw[h].astype(BF16), vcb[h], _TN, preferred_element_type=F32) for h in heads]
        for h in heads:
            c_ref[h] = decay[h] * cst[h] + kv[h]
            n_ref[h] = decay[h] * nst[h] + jnp.sum(kw[h], axis=0, keepdims=True)
            m_ref[h] = m_new[h]
            hn = hc[h] * lax.rsqrt(jnp.mean(hc[h] * hc[h], axis=1, keepdims=True) + HNORM_EPS)
            y_ref[lo:hi, hs[h]] = (hn * hg_ref[:, hs[h]]
                                   * jax.nn.sigmoid(o_ref[lo:hi, hs[h]].astype(F32))).astype(y_ref.dtype)


def _mlstm(z, gates, gate_bias, conv_w, conv_b, hnorm_g, bsz, s, tt=256):
    nt = s // tt
    cb = 3 * A_WIDTH // B_WIDTH
    gates_t = gates.T
    bias = gate_bias.reshape(1, 2 * B_HEADS)

    def col(off):
        return pl.BlockSpec((tt, B_WIDTH), lambda b, t: (b * nt + t, cb + off))

    def par(rows, off):
        return pl.BlockSpec((rows, B_WIDTH), lambda b, t: (0, off))

    return pl.pallas_call(
        functools.partial(_mlstm_kernel, tt=tt),
        grid=(bsz, nt),
        in_specs=[col(0), col(1), col(2), col(3),
                  pl.BlockSpec((tt, 2 * B_HEADS), lambda b, t: (b * nt + t, 0)),
                  pl.BlockSpec((2 * B_HEADS, tt), lambda b, t: (0, b * nt + t)),
                  pl.BlockSpec((1, 2 * B_HEADS), lambda b, t: (0, 0)),
                  pl.BlockSpec((2 * B_HEADS, 1), lambda b, t: (0, 0)),
                  par(B_CONV, 0), par(B_CONV, 1), par(1, 0), par(1, 1), par(1, 0)],
        out_specs=pl.BlockSpec((tt, B_WIDTH), lambda b, t: (b * nt + t, 0)),
        out_shape=jax.ShapeDtypeStruct((bsz * s, B_WIDTH), BF16),
        scratch_shapes=[pltpu.VMEM((B_HEADS, B_HEAD_DIM, B_HEAD_DIM), F32),
                        pltpu.VMEM((B_HEADS, 1, B_HEAD_DIM), F32), pltpu.VMEM((B_HEADS, 1, 1), F32),
                        pltpu.VMEM((SUBLANES, B_WIDTH), F32), pltpu.VMEM((SUBLANES, B_WIDTH), F32)],
        compiler_params=_params("parallel", "arbitrary"),
        name="mlstm",
    )(z, z, z, z, gates, gates_t, bias, bias.T, conv_w, conv_w,
      conv_b.reshape(1, -1), conv_b.reshape(1, -1), hnorm_g.reshape(1, -1))


def _even_mixer(x, w_in, gate_bias, conv_w, conv_b, hnorm_g, w_out, bsz, s):
    d = x.shape[1]
    w_main = w_in[:, :EVEN_MAIN].astype(BF16)
    w_gate = jnp.pad(w_in[:, EVEN_MAIN:], ((0, 0), (0, LANES - 2 * B_HEADS))).astype(BF16)
    z = _matmul(x, w_main, out_dtype=BF16)
    gates = _matmul(x, w_gate, tn=LANES)[:, :2 * B_HEADS]
    y_a = _moba(z, bsz, s)
    y_b = _mlstm(z, gates, gate_bias, conv_w, conv_b, hnorm_g, bsz, s)
    w_o = w_out.astype(BF16)
    return _matmul2(y_a, y_b, w_o[:A_WIDTH], w_o[A_WIDTH:])


def _shift_mix_kernel(x_ref, p_ref, mu_ref, *o_refs, tiles_per_seq):
    x = x_ref[...]
    first = (pl.program_id(0) % tiles_per_seq) == 0
    prev_row = jnp.where(first, 0.0, p_ref[SUBLANES - 1:SUBLANES, :])
    row = lax.broadcasted_iota(jnp.int32, x.shape, 0)
    dx = jnp.where(row == 0, prev_row, pltpu.roll(x, 1, axis=0)) - x
    for j, o_ref in enumerate(o_refs):
        o_ref[...] = (x + dx * mu_ref[j:j + 1, :]).astype(o_ref.dtype)


def _shift_mix(x, mu, s, tm=256):
    t, d = x.shape
    n = mu.shape[0]
    row = pl.BlockSpec((tm, d), lambda i: (i, 0))
    prev = pl.BlockSpec((SUBLANES, d), lambda i: (jnp.maximum(i * (tm // SUBLANES) - 1, 0), 0))
    return pl.pallas_call(
        functools.partial(_shift_mix_kernel, tiles_per_seq=s // tm),
        grid=(t // tm,),
        in_specs=[row, prev, pl.BlockSpec((n, d), lambda i: (0, 0))],
        out_specs=[row] * n,
        out_shape=[jax.ShapeDtypeStruct((t, d), BF16)] * n,
        compiler_params=_params("parallel"),
        name="shift_mix",
    )(x, x, mu)


def _softplus(x):
    return jnp.maximum(x, 0.0) + jnp.log1p(jnp.exp(-jnp.abs(x)))


def _lora_kernel(x_ref, w1_ref, w2_ref, b_ref, o_ref, *, mid, out):
    hmid = jnp.dot(x_ref[...], w1_ref[...], preferred_element_type=F32)
    if mid == "tanh":
        hmid = jnp.tanh(hmid)
    elif mid == "sigmoid":
        hmid = jax.nn.sigmoid(hmid)
    y = jnp.dot(hmid.astype(BF16), w2_ref[...], preferred_element_type=F32)
    if out == "logdecay":
        y = -jnp.exp(-_softplus(-(b_ref[...] + y)) - 0.5)
    elif out == "sigmoid":
        y = jax.nn.sigmoid(b_ref[...] + y)
    o_ref[...] = y.astype(o_ref.dtype)


def _lora(x, w1, w2, bias, mid, out, out_dtype=F32, tm=512):
    t, d = x.shape
    tm = min(tm, t)
    r = w1.shape[1]
    rp = -(-r // LANES) * LANES
    w1p = jnp.pad(w1, ((0, 0), (0, rp - r))).astype(BF16)
    w2p = jnp.pad(w2, ((0, rp - r), (0, 0))).astype(BF16)
    dout = w2.shape[1]
    if bias is None:
        bias = jnp.zeros((dout,), F32)
    return pl.pallas_call(
        functools.partial(_lora_kernel, mid=mid, out=out),
        grid=(t // tm,),
        in_specs=[pl.BlockSpec((tm, d), lambda i: (i, 0)),
                  pl.BlockSpec((d, rp), lambda i: (0, 0)),
                  pl.BlockSpec((rp, dout), lambda i: (0, 0)),
                  pl.BlockSpec((1, dout), lambda i: (0, 0))],
        out_specs=pl.BlockSpec((tm, dout), lambda i: (i, 0)),
        out_shape=jax.ShapeDtypeStruct((t, dout), out_dtype),
        compiler_params=_params("parallel"),
        name="lora_" + out,
    )(x, w1p, w2p, bias.reshape(1, dout))


def _head_sums(tiles, ones_b, split=True):
    c = tiles[0].shape[0]
    x = jnp.concatenate(tiles, axis=0)
    hi = x.astype(BF16)
    tot = jnp.dot(hi, ones_b, preferred_element_type=F32)
    if split:
        lo = (x - hi.astype(F32)).astype(BF16)
        tot = tot + jnp.dot(lo, ones_b, preferred_element_type=F32)
    return [tot[p * c:(p + 1) * c] for p in range(len(tiles))]


def _bdot(a, b):
    return jnp.dot(a.astype(BF16), b.astype(BF16), preferred_element_type=F32)


def _rwkv_chunk_pairs(r, ld, kh, v, kk, bb, g_cum, s0, c):
    pairs = range(len(r))
    lane = lax.broadcasted_iota(jnp.int32, (1, LANES), 1)
    m0 = jnp.where(lane < C_HEAD_DIM, 1.0, 0.0)
    m1 = 1.0 - m0
    n2 = 2 * c
    row = lax.broadcasted_iota(jnp.int32, (n2, n2), 0)
    col = lax.broadcasted_iota(jnp.int32, (n2, n2), 1)
    lower, lower_eq = col < row, col <= row

    def stack(x):
        return jnp.concatenate([x * m0, x * m1], axis=0)

    e_g = [jnp.exp(g_cum[p]) for p in pairs]
    e_ng = [jnp.exp(-g_cum[p]) for p in pairs]
    al2 = [stack(-(kk[p] * jnp.exp(g_cum[p] - ld[p]))) for p in pairs]
    be2 = [stack(bb[p] * e_ng[p]).astype(BF16) for p in pairs]
    kb2 = [stack(kh[p] * e_ng[p]).astype(BF16) for p in pairs]
    rb = [r[p] * e_g[p] for p in pairs]
    v2 = [stack(v[p]) for p in pairs]
    sc = [lax.dot_general(jnp.concatenate([al2[p], stack(rb[p])], axis=0).astype(BF16),
                          jnp.concatenate([be2[p], kb2[p]], axis=0), _NT,
                          preferred_element_type=F32) for p in pairs]
    l_ab = [jnp.where(lower, sc[p][:n2, :n2], 0.0) for p in pairs]
    l_ak = [jnp.where(lower, sc[p][:n2, n2:], 0.0) for p in pairs]
    l_r = [jnp.concatenate([jnp.where(lower_eq, sc[p][n2:, :n2], 0.0),
                            jnp.where(lower_eq, sc[p][n2:, n2:], 0.0)], axis=1).astype(BF16) for p in pairs]

    t_off = l_ab
    pw = [x.astype(BF16) for x in l_ab]
    for _ in range(max(c.bit_length() - 2, 0)):
        pw_f = [jnp.dot(pw[p], pw[p], preferred_element_type=F32) for p in pairs]
        pw = [x.astype(BF16) for x in pw_f]
        t_off = [t_off[p] + pw_f[p] + jnp.dot(t_off[p].astype(BF16), pw[p], preferred_element_type=F32)
                 for p in pairs]

    w0 = [jnp.concatenate([al2[p], _bdot(l_ak[p], v2[p])], axis=1) for p in pairs]
    x = [w0[p] + _bdot(t_off[p], w0[p]) for p in pairs]
    xb = [x[p].astype(BF16) for p in pairs]
    z = [jnp.dot(l_r[p],
                 jnp.concatenate([xb[p], jnp.concatenate([jnp.zeros_like(v2[p]), v2[p]], axis=1).astype(BF16)],
                                 axis=0), preferred_element_type=F32) for p in pairs]
    r_eff = [(rb[p] + z[p][:c, :LANES] + z[p][c:, :LANES]).astype(BF16) for p in pairs]
    y0 = [z[p][:c, LANES:] + z[p][c:, LANES:] for p in pairs]
    mn = [lax.dot_general(xb[p], be2[p], _TN, preferred_element_type=F32) for p in pairs]
    n_add = [mn[p][LANES:] + lax.dot_general(v2[p].astype(BF16), kb2[p], _TN, preferred_element_type=F32)
             for p in pairs]
    s0b = [s0[p].astype(BF16) for p in pairs]
    y = [y0[p] + lax.dot_general(r_eff[p], s0b[p], _NT, preferred_element_type=F32) for p in pairs]
    s_new = [(s0[p] + jnp.dot(s0b[p], mn[p][:LANES].astype(BF16), preferred_element_type=F32) + n_add[p])
             * e_g[p][c - 1:c, :] for p in pairs]
    return y, s_new


def _rwkv_scan_kernel(*refs, c, npair, vres):
    if vres:
        (r_ref, ld_ref, k_ref, a_ref, v_ref, g_ref, vf_ref, vg_ref,
         kkp_ref, kap_ref, rkp_ref, lg_ref, lb_ref, tri_ref, o_ref, s_ref) = refs
    else:
        (r_ref, ld_ref, k_ref, a_ref, v_ref, g_ref,
         kkp_ref, kap_ref, rkp_ref, lg_ref, lb_ref, tri_ref, o_ref, s_ref) = refs

    @pl.when(pl.program_id(2) == 0)
    def _reset():
        s_ref[...] = jnp.zeros_like(s_ref)

    pairs = range(npair)
    sl = [slice(p * LANES, (p + 1) * LANES) for p in pairs]
    take = lambda ref: [ref[:, q].astype(F32) for q in sl]
    shift = C_HEAD_DIM.bit_length() - 1
    ones_b = jnp.where(jnp.right_shift(lax.broadcasted_iota(jnp.int32, (LANES, LANES), 0), shift)
                       == jnp.right_shift(lax.broadcasted_iota(jnp.int32, (LANES, LANES), 1), shift),
                       1.0, 0.0).astype(BF16)
    g_all = jnp.dot(tri_ref[...], ld_ref[...], precision=lax.Precision.HIGHEST, preferred_element_type=F32)

    r, ld, k, a, v, gate = (take(x) for x in (r_ref, ld_ref, k_ref, a_ref, v_ref, g_ref))
    kk_raw = [k[p] * kkp_ref[:, sl[p]] for p in pairs]
    sq = _head_sums([x * x for x in kk_raw], ones_b)
    kk = [kk_raw[p] / jnp.maximum(jnp.sqrt(sq[p]), KK_EPS) for p in pairs]
    bb = [kk[p] * a[p] for p in pairs]
    kh = [k[p] * (1.0 + (a[p] - 1.0) * kap_ref[:, sl[p]]) for p in pairs]
    if vres:
        vf, vg = take(vf_ref), take(vg_ref)
        v = [v[p] + (vf[p] - v[p]) * vg[p] for p in pairs]

    y, s_new = _rwkv_chunk_pairs(r, ld, kh, v, kk, bb, [g_all[:, q] for q in sl],
                                 [s_ref[p] for p in pairs], c)

    inv_n = 1.0 / C_HEAD_DIM
    mean = _head_sums(y, ones_b)
    yc = [y[p] - mean[p] * inv_n for p in pairs]
    var = _head_sums([x * x for x in yc], ones_b)
    bonus = _head_sums([r[p] * kh[p] * rkp_ref[:, sl[p]] for p in pairs], ones_b, split=False)
    for p in pairs:
        yn = yc[p] * lax.rsqrt(var[p] * inv_n + LNX_EPS) * lg_ref[:, sl[p]] + lb_ref[:, sl[p]]
        o_ref[:, sl[p]] = ((yn + bonus[p] * v[p]) * gate[p]).astype(o_ref.dtype)
        s_ref[p] = s_new[p]


RWKV_CHUNK = 64
RWKV_PAIRS = 8


def _rwkv_scan(r, log_w, k, a, v, gate, v_first, vgate, k_k, k_a, r_k, lnx_g, lnx_b, bsz, s):
    t, d = r.shape
    c = min(RWKV_CHUNK, s)
    wid = RWKV_PAIRS * LANES
    nc = s // c
    vres = v_first is not None
    tri = (jnp.arange(c)[None, :] <= jnp.arange(c)[:, None]).astype(F32)
    blk = pl.BlockSpec((c, wid), lambda b, g, i: (b * nc + i, g))
    vec = pl.BlockSpec((1, wid), lambda b, g, i: (0, g))
    rows = [r, log_w, k, a, v, gate] + ([v_first, vgate] if vres else [])
    vecs = [x.reshape(1, d) for x in (k_k, k_a, r_k, lnx_g, lnx_b)]
    return pl.pallas_call(
        functools.partial(_rwkv_scan_kernel, c=c, npair=RWKV_PAIRS, vres=vres),
        grid=(bsz, d // wid, nc),
        in_specs=[blk] * len(rows) + [vec] * len(vecs) + [pl.BlockSpec((c, c), lambda b, g, i: (0, 0))],
        out_specs=blk,
        out_shape=jax.ShapeDtypeStruct((t, d), BF16),
        scratch_shapes=[pltpu.VMEM((RWKV_PAIRS, LANES, LANES), F32)],
        compiler_params=_params("parallel", "parallel", "arbitrary"),
        name="rwkv_scan",
    )(*rows, *vecs, tri)


def _rwkv7_mixer(x, v_first, mu, w_r, w_k, w_v, w_o, w0, w1, w2, a0, a1, a2, g1, g2,
                 k_k, k_a, r_k, lnx_g, lnx_b, vres, bsz, s):
    xr, xw, xk, xv, xa, xg = _shift_mix(x, mu, s)
    r = _matmul(xr, w_r.astype(BF16), out_dtype=BF16)
    k = _matmul(xk, w_k.astype(BF16), out_dtype=BF16)
    v = _matmul(xv, w_v.astype(BF16), out_dtype=BF16)
    log_decay = _lora(xw, w1, w2, w0, "tanh", "logdecay")
    a = _lora(xa, a1, a2, a0, "none", "sigmoid", out_dtype=BF16)
    g = _lora(xg, g1, g2, None, "sigmoid", "none", out_dtype=BF16)
    if vres is None:
        vgate = None
        v_keep = v
    else:
        v0, v1, v2 = vres
        vgate = _lora(xv, v1, v2, v0, "none", "sigmoid", out_dtype=BF16)
        v_keep = v_first
    yo = _rwkv_scan(r, log_decay, k, a, v, g, None if vres is None else v_first, vgate,
                    k_k, k_a, r_k, lnx_g, lnx_b, bsz, s)
    return _matmul(yo, w_o.astype(BF16)), v_keep


def _lane_pick(cond, lanef):
    return jnp.min(jnp.where(cond, lanef, float(LANES)), axis=1, keepdims=True)


def _route_tile(xt, w_ref, b_ref, route_ref, cnt_ref, run_ref, tm):
    @pl.when(pl.program_id(0) == 0)
    def _reset():
        run_ref[...] = jnp.zeros_like(run_ref)

    logits = jnp.dot(xt.astype(BF16), w_ref[...], preferred_element_type=F32) + b_ref[...]
    lane = lax.broadcasted_iota(jnp.int32, (tm, LANES), 1)
    lanef = lane.astype(F32)

    def masked_softmax(keep):
        z = jnp.where(keep, logits, -jnp.inf)
        e = jnp.exp(z - jnp.max(z, axis=1, keepdims=True))
        return e / jnp.sum(e, axis=1, keepdims=True)

    in_groups = lane < N_GROUPS
    pg = masked_softmax(in_groups)
    g_p = jnp.max(pg, axis=1, keepdims=True)
    g_idx = _lane_pick(jnp.where(in_groups, pg, -1.0) == g_p, lanef)
    lo = float(N_GROUPS) + float(EXPERTS_PER_GROUP) * g_idx
    half = 0.5 * float(EXPERTS_PER_GROUP - 1)
    in_group = jnp.abs(lanef - lo - half) < half + 0.5
    pe = jnp.where(in_group, masked_softmax(in_group), -1.0)
    p1 = jnp.max(pe, axis=1, keepdims=True)
    i1 = _lane_pick(pe == p1, lanef)
    pe2 = jnp.where(lanef == i1, -1.0, pe)
    p2 = jnp.max(pe2, axis=1, keepdims=True)
    i2 = _lane_pick(pe2 == p2, lanef)
    den = p1 + p2
    gate = (g_p * (p1 / den), g_p * (p2 / den))
    expert = (i1 - float(N_GROUPS), i2 - float(N_GROUPS))

    earlier = jnp.where(lax.broadcasted_iota(jnp.int32, (tm, tm), 1)
                        < lax.broadcasted_iota(jnp.int32, (tm, tm), 0), 1.0, 0.0).astype(BF16)
    run = run_ref[...]
    rank = []
    for e in expert:
        onehot = jnp.where(lanef == e, 1.0, 0.0)
        before = jnp.dot(earlier, onehot.astype(BF16), preferred_element_type=F32) + run
        rank.append(jnp.sum(onehot * before, axis=1, keepdims=True))
        run = run + jnp.sum(onehot, axis=0, keepdims=True)
    run_ref[...] = run
    cnt_ref[...] = run
    cols = (expert[0], expert[1], rank[0], rank[1], gate[0], gate[1])
    route = jnp.zeros((tm, LANES), F32)
    for c, val in enumerate(cols):
        route = jnp.where(lane == c, val, route)
    route_ref[...] = route


SLAB = 2048 // LANES
DMA_UNROLL = 4


def _slab(ref, r):
    return ref.at[pl.ds(pl.multiple_of(r * SLAB, SLAB), SLAB)]


def _slab_rows(ref, rows):
    return jnp.concatenate([ref[pl.ds(c, rows, stride=SLAB), :] for c in range(SLAB)], axis=1)


def _store_slab_rows(ref, val, rows):
    for c in range(SLAB):
        ref[pl.ds(c, rows, stride=SLAB), :] = val[:, c * LANES:(c + 1) * LANES]


def _expert_kernel(be_ref, nu_ref, idx_ref, nxt_ref, x_hbm, wg_ref, wu_ref, wd_ref, o_ref,
                   xbuf, sems, wg_b, wu_b, wd_b, *, rows):
    i = pl.program_id(0)
    n_used = nu_ref[0]
    slot = i % 2

    def token_copy(tok, r, dst_slot):
        return pltpu.make_async_copy(_slab(x_hbm, tok), _slab(xbuf.at[dst_slot], r), sems.at[dst_slot])

    def gather(ids_ref, dst_slot):
        def start(r, carry):
            for rr in (r, r + rows // 2):
                token_copy(ids_ref[0, rr], rr, dst_slot).start()
            return carry
        lax.fori_loop(0, rows // 2, start, 0, unroll=DMA_UNROLL)

    @pl.when(i == 0)
    def _():
        gather(idx_ref, 0)

    @pl.when(i + 1 < n_used)
    def _():
        gather(nxt_ref, 1 - slot)

    @pl.when(jnp.logical_or(i == 0, be_ref[i] != be_ref[jnp.maximum(i - 1, 0)]))
    def _():
        wg_b[...] = wg_ref[...].astype(BF16)
        wu_b[...] = wu_ref[...].astype(BF16)
        wd_b[...] = wd_ref[...].astype(BF16)

    @pl.when(i < n_used)
    def _():
        for r in range(rows):
            token_copy(0, r, slot).wait()
        xb = _slab_rows(xbuf.at[slot], rows).astype(BF16)
        hg = jnp.dot(xb, wg_b[...], preferred_element_type=F32)
        hu = jnp.dot(xb, wu_b[...], preferred_element_type=F32)
        hid = (hg * jax.nn.sigmoid(hg) * hu).astype(BF16)
        _store_slab_rows(o_ref, jnp.dot(hid, wd_b[...], preferred_element_type=F32), rows)

    @pl.when(i >= n_used)
    def _():
        o_ref[...] = jnp.zeros_like(o_ref)


def _expert_mlp(x_slabs, src_row, block_e, n_used, w_gate, w_up, w_down, layer, rows=MOE_ROWS):
    d, ff = w_gate.shape[2], w_gate.shape[3]
    nb = src_row.shape[0] // rows
    ids = src_row.reshape(nb, 1, rows)
    grid_spec = pltpu.PrefetchScalarGridSpec(
        num_scalar_prefetch=2,
        grid=(nb,),
        in_specs=[pl.BlockSpec((None, 1, rows), lambda i, be, nu: (i, 0, 0), memory_space=pltpu.SMEM),
                  pl.BlockSpec((None, 1, rows), lambda i, be, nu: (jnp.minimum(i + 1, nb - 1), 0, 0),
                               memory_space=pltpu.SMEM),
                  pl.BlockSpec(memory_space=pl.ANY),
                  pl.BlockSpec((None, None, d, ff), lambda i, be, nu: (layer, be[i], 0, 0)),
                  pl.BlockSpec((None, None, d, ff), lambda i, be, nu: (layer, be[i], 0, 0)),
                  pl.BlockSpec((None, None, ff, d), lambda i, be, nu: (layer, be[i], 0, 0))],
        out_specs=pl.BlockSpec((rows * SLAB, LANES), lambda i, be, nu: (i, 0)),
        scratch_shapes=[pltpu.VMEM((2, rows * SLAB, LANES), F32), pltpu.SemaphoreType.DMA((2,)),
                        pltpu.VMEM((d, ff), BF16), pltpu.VMEM((d, ff), BF16), pltpu.VMEM((ff, d), BF16)],
    )
    return pl.pallas_call(
        functools.partial(_expert_kernel, rows=rows),
        grid_spec=grid_spec,
        out_shape=jax.ShapeDtypeStruct((nb * rows * SLAB, LANES), F32),
        compiler_params=_params("arbitrary"),
        name="expert_mlp",
    )(block_e, n_used, ids, ids, x_slabs, w_gate, w_up, w_down)


def _combine_ln_kernel(idx_ref, nxt_ref, y_ref, x_ref, gt_ref, g_ref, b_ref, o_ref, buf, sems, *, rows):
    i = pl.program_id(0)
    slot = i % 2

    def row_copy(src, r, kslot, dst_slot):
        return pltpu.make_async_copy(_slab(y_ref, src), _slab(buf.at[dst_slot, kslot], r), sems.at[dst_slot])

    def gather(ids_ref, dst_slot):
        def start(r, carry):
            for kslot in range(MOE_TOPK):
                for rr in (r, r + rows // 2):
                    row_copy(ids_ref[0, MOE_TOPK * rr + kslot], rr, kslot, dst_slot).start()
            return carry
        lax.fori_loop(0, rows // 2, start, 0, unroll=DMA_UNROLL)

    @pl.when(i == 0)
    def _():
        gather(idx_ref, 0)

    @pl.when(i + 1 < pl.num_programs(0))
    def _():
        gather(nxt_ref, 1 - slot)

    for r in range(rows):
        for kslot in range(MOE_TOPK):
            row_copy(0, r, kslot, slot).wait()
    gt = gt_ref[...]
    ffn = gt[:, 0:1] * _slab_rows(buf.at[slot, 0], rows)
    for kslot in range(1, MOE_TOPK):
        ffn = ffn + gt[:, kslot:kslot + 1] * _slab_rows(buf.at[slot, kslot], rows)
    o_ref[...] = _ln_rows(DN_ALPHA * x_ref[...] + ffn, g_ref[...], b_ref[...])


def _combine_ln(y_sorted, dest, gates, x, g, b, rows=256):
    t, d = x.shape
    nb = t // rows
    row = pl.BlockSpec((rows, d), lambda i: (i, 0))
    vec = pl.BlockSpec((1, d), lambda i: (0, 0))
    ids = dest.reshape(nb, 1, rows * MOE_TOPK)
    return pl.pallas_call(
        functools.partial(_combine_ln_kernel, rows=rows),
        grid=(nb,),
        in_specs=[pl.BlockSpec((None, 1, rows * MOE_TOPK), lambda i: (i, 0, 0), memory_space=pltpu.SMEM),
                  pl.BlockSpec((None, 1, rows * MOE_TOPK), lambda i: (jnp.minimum(i + 1, nb - 1), 0, 0),
                               memory_space=pltpu.SMEM),
                  pl.BlockSpec(memory_space=pl.ANY), row,
                  pl.BlockSpec((rows, MOE_TOPK), lambda i: (i, 0)), vec, vec],
        out_specs=row,
        out_shape=jax.ShapeDtypeStruct((t, d), F32),
        scratch_shapes=[pltpu.VMEM((2, MOE_TOPK, rows * SLAB, LANES), F32), pltpu.SemaphoreType.DMA((2,))],
        compiler_params=_params("arbitrary"),
        name="combine_ln",
    )(ids, ids, y_sorted, x, gates, g.reshape(1, d), b.reshape(1, d))


def _router_params(w_grp, b_grp, w_exp_r, b_exp_r):
    n_r = N_GROUPS + N_EXPERTS
    w_router = jnp.pad(jnp.concatenate([w_grp, w_exp_r], axis=1), ((0, 0), (0, LANES - n_r))).astype(BF16)
    b_router = jnp.pad(jnp.concatenate([b_grp, b_exp_r]), (0, LANES - n_r)).reshape(1, LANES)
    return w_router, b_router


def _moe_layer(x, x_slabs, route, cnt, w_gate, w_up, w_down, ln_g, ln_b, layer):
    t, d = x.shape
    flat_e = route[:, 0:MOE_TOPK].astype(jnp.int32).reshape(-1)
    rank = route[:, MOE_TOPK:2 * MOE_TOPK].astype(jnp.int32).reshape(-1)
    gates = route[:, 2 * MOE_TOPK:3 * MOE_TOPK]
    counts = cnt[0, :N_EXPERTS].astype(jnp.int32)
    m = t * MOE_TOPK
    padded = (counts + MOE_ROWS - 1) // MOE_ROWS * MOE_ROWS
    pad_end = jnp.cumsum(padded)
    dest = (pad_end - padded)[flat_e] + rank
    n_blocks = (m + N_EXPERTS * (MOE_ROWS - 1) + MOE_ROWS - 1) // MOE_ROWS
    src_row = jnp.zeros((n_blocks * MOE_ROWS,), jnp.int32).at[dest].set(
        (jnp.arange(m) // MOE_TOPK).astype(jnp.int32))
    starts = jnp.arange(n_blocks, dtype=jnp.int32) * MOE_ROWS
    block_e = jnp.minimum(jnp.sum((pad_end[None, :] <= starts[:, None]).astype(jnp.int32), axis=1),
                          N_EXPERTS - 1).astype(jnp.int32)
    n_used = (pad_end[-1] // MOE_ROWS).astype(jnp.int32).reshape(1)

    ys = _expert_mlp(x_slabs, src_row, block_e, n_used, w_gate, w_up, w_down, layer)
    return _combine_ln(ys, dest.astype(jnp.int32), gates, x, ln_g, ln_b)


def kernel(x, ev_w_in, ev_gate_bias, ev_conv_w, ev_conv_b, ev_hnorm_g, ev_w_out, od_mu, od_w_r, od_w_k, od_w_v, od_w_o, od_w0, od_w1, od_w2, od_a0, od_a1, od_a2, od_g1, od_g2, od_k_k, od_k_a, od_r_k, od_lnx_g, od_lnx_b, od_v0, od_v1, od_v2, ln_mix_g, ln_mix_b, ln_ffn_g, ln_ffn_b, moe_w_grp, moe_b_grp, moe_w_exp_r, moe_b_exp_r, moe_w_gate, moe_w_up, moe_w_down):
    bsz, s, d = x.shape
    depth = ln_mix_g.shape[0]
    xf = x.reshape(bsz * s, d)
    v_first = None
    for layer in range(depth):
        if layer % 2 == 0:
            e = layer // 2
            mix = _even_mixer(xf, ev_w_in[e], ev_gate_bias[e], ev_conv_w[e], ev_conv_b[e],
                              ev_hnorm_g[e], ev_w_out[e], bsz, s)
        else:
            o = layer // 2
            vres = None if o == 0 else (od_v0[o - 1], od_v1[o - 1], od_v2[o - 1])
            mix, v_first = _rwkv7_mixer(xf, v_first, od_mu[o], od_w_r[o], od_w_k[o], od_w_v[o],
                                        od_w_o[o], od_w0[o], od_w1[o], od_w2[o], od_a0[o],
                                        od_a1[o], od_a2[o], od_g1[o], od_g2[o], od_k_k[o],
                                        od_k_a[o], od_r_k[o], od_lnx_g[o], od_lnx_b[o], vres, bsz, s)
        xf, x_slabs, route, cnt = _resid_ln(xf, mix, ln_mix_g[layer], ln_mix_b[layer],
                                            *_router_params(moe_w_grp[layer], moe_b_grp[layer],
                                                            moe_w_exp_r[layer], moe_b_exp_r[layer]))
        xf = _moe_layer(xf, x_slabs, route, cnt, moe_w_gate, moe_w_up, moe_w_down,
                        ln_ffn_g[layer], ln_ffn_b[layer], layer)
    return xf.reshape(bsz, s, d)
```

```python
import functools
import math

import jax
import jax.numpy as jnp
from jax import lax
from jax.experimental import pallas as pl
from jax.experimental.pallas import tpu as pltpu

F32 = jnp.float32
BF16 = jnp.bfloat16

DEPTH = 4
DN_ALPHA = (2 * DEPTH) ** 0.25
LN_EPS = 1e-5

A_HEADS = 8
A_HEAD_DIM = 128
A_WIDTH = A_HEADS * A_HEAD_DIM
MOBA_BLOCK = 256
MOBA_TOPK = 3
MOBA_GROUP = 4
MOBA_HEADS = 2
ROPE_THETA = 500000.0
ROPE_DIM = A_HEAD_DIM // 4
NEG_INF = -1e30

B_HEADS = 4
B_HEAD_DIM = 256
B_WIDTH = B_HEADS * B_HEAD_DIM
MLSTM_CHUNK = 64
B_CONV = 4
HNORM_EPS = 1e-6
EVEN_MAIN = 3 * A_WIDTH + 4 * B_WIDTH

C_HEAD_DIM = 64
LNX_EPS = 64e-5
KK_EPS = 1e-12

N_GROUPS = 4
EXPERTS_PER_GROUP = 8
N_EXPERTS = N_GROUPS * EXPERTS_PER_GROUP
MOE_TOPK = 2
MOE_ROWS = 256

LANES = 128
SUBLANES = 8
VMEM_LIMIT = 56 * 1024 * 1024

_NT = (((1,), (1,)), ((), ()))
_TN = (((0,), (0,)), ((), ()))


def _params(*sem):
    return pltpu.CompilerParams(dimension_semantics=sem, vmem_limit_bytes=VMEM_LIMIT)


def _mm_kernel(x_ref, w_ref, o_ref):
    o_ref[...] = jnp.dot(x_ref[...].astype(BF16), w_ref[...],
                         preferred_element_type=F32).astype(o_ref.dtype)


def _matmul(x, w, out_dtype=F32, tm=1024, tn=512):
    m, k = x.shape
    n = w.shape[1]
    tm, tn = min(tm, m), min(tn, n)
    return pl.pallas_call(
        _mm_kernel,
        grid=(m // tm, n // tn),
        in_specs=[pl.BlockSpec((tm, k), lambda i, j: (i, 0)),
                  pl.BlockSpec((k, tn), lambda i, j: (0, j))],
        out_specs=pl.BlockSpec((tm, tn), lambda i, j: (i, j)),
        out_shape=jax.ShapeDtypeStruct((m, n), out_dtype),
        compiler_params=_params("parallel", "arbitrary"),
        name="matmul",
    )(x, w)


def _mm2_kernel(xa_ref, xb_ref, wa_ref, wb_ref, o_ref):
    acc = jnp.dot(xa_ref[...], wa_ref[...], preferred_element_type=F32)
    acc = acc + jnp.dot(xb_ref[...], wb_ref[...], preferred_element_type=F32)
    o_ref[...] = acc


def _matmul2(xa, xb, wa, wb, tm=1024, tn=512):
    m, ka = xa.shape
    kb = xb.shape[1]
    n = wa.shape[1]
    tm, tn = min(tm, m), min(tn, n)
    return pl.pallas_call(
        _mm2_kernel,
        grid=(m // tm, n // tn),
        in_specs=[pl.BlockSpec((tm, ka), lambda i, j: (i, 0)),
                  pl.BlockSpec((tm, kb), lambda i, j: (i, 0)),
                  pl.BlockSpec((ka, tn), lambda i, j: (0, j)),
                  pl.BlockSpec((kb, tn), lambda i, j: (0, j))],
        out_specs=pl.BlockSpec((tm, tn), lambda i, j: (i, j)),
        out_shape=jax.ShapeDtypeStruct((m, n), F32),
        compiler_params=_params("parallel", "arbitrary"),
        name="matmul2",
    )(xa, xb, wa, wb)


def _ln_rows(h, g, b):
    mu = jnp.mean(h, axis=-1, keepdims=True)
    hc = h - mu
    var = jnp.mean(hc * hc, axis=-1, keepdims=True)
    return hc * lax.rsqrt(var + LN_EPS) * g + b


def _resid_ln_kernel(x_ref, m_ref, g_ref, b_ref, w_ref, br_ref, o_ref, s_ref, route_ref, cnt_ref, run_ref,
                     *, tm):
    y = _ln_rows(DN_ALPHA * x_ref[...] + m_ref[...], g_ref[...], b_ref[...])
    o_ref[...] = y.astype(o_ref.dtype)
    _store_slab_rows(s_ref, y, tm)
    _route_tile(y, w_ref, br_ref, route_ref, cnt_ref, run_ref, tm)


def _resid_ln(x, mix, g, b, w_router, b_router, tm=256):
    t, d = x.shape
    tm = min(tm, t)
    row = pl.BlockSpec((tm, d), lambda i: (i, 0))
    vec = pl.BlockSpec((1, d), lambda i: (0, 0))
    one = pl.BlockSpec((1, LANES), lambda i: (0, 0))
    return pl.pallas_call(
        functools.partial(_resid_ln_kernel, tm=tm),
        grid=(t // tm,),
        in_specs=[row, row, vec, vec, pl.BlockSpec((d, LANES), lambda i: (0, 0)), one],
        out_specs=[row, pl.BlockSpec((tm * SLAB, LANES), lambda i: (i, 0)),
                   pl.BlockSpec((tm, LANES), lambda i: (i, 0)), one],
        out_shape=[jax.ShapeDtypeStruct((t, d), F32), jax.ShapeDtypeStruct((t * SLAB, LANES), F32),
                   jax.ShapeDtypeStruct((t, LANES), F32), jax.ShapeDtypeStruct((1, LANES), F32)],
        scratch_shapes=[pltpu.VMEM((1, LANES), F32)],
        compiler_params=_params("arbitrary"),
        name="resid_ln_router",
    )(x, mix, g.reshape(1, d), b.reshape(1, d), w_router, b_router)


def _rope_tables(s):
    half = ROPE_DIM // 2
    inv = jnp.power(jnp.float32(ROPE_THETA), -jnp.arange(half, dtype=F32) / half)
    ang = jnp.arange(s).astype(F32)[:, None] * inv[None, :]
    cos, sin = jnp.cos(ang), jnp.sin(ang)
    zero = jnp.zeros((s, A_HEAD_DIM - ROPE_DIM), F32)
    zh = jnp.zeros((s, half), F32)
    c = jnp.concatenate([cos, cos, zero + 1.0], axis=-1)
    s_up = jnp.concatenate([zh, sin, zero], axis=-1)
    s_dn = jnp.concatenate([-sin, zh, zero], axis=-1)
    return c, s_up, s_dn


def _rope(x, c, s_up, s_dn):
    half = ROPE_DIM // 2
    return (x * c + pltpu.roll(x, half, axis=1) * s_up
            + pltpu.roll(x, A_HEAD_DIM - half, axis=1) * s_dn)


def _moba_kernel(q_ref, k_ref, v_ref, c_ref, su_ref, sd_ref, o_ref, krot_ref, vb_ref, kmean_ref, *, nblk):
    qi = pl.program_id(2)
    blk = MOBA_BLOCK
    dh = A_HEAD_DIM
    scale = dh ** -0.5
    heads = range(MOBA_HEADS)
    hs = [slice(h * dh, (h + 1) * dh) for h in heads]

    @pl.when(qi == 0)
    def _prepare_keys():
        def body(j, carry):
            rows = pl.ds(pl.multiple_of(j * blk, blk), blk)
            block_tag = jnp.where(lax.broadcasted_iota(jnp.int32, (blk, LANES), 1) == j, 1.0, 0.0)
            for h in heads:
                kr = _rope(k_ref[rows, hs[h]].astype(F32), c_ref[rows, :], su_ref[rows, :], sd_ref[rows, :])
                krot_ref[h, rows, :] = jnp.concatenate([kr, block_tag], axis=1).astype(BF16)
                vb_ref[h, rows, :] = v_ref[rows, hs[h]].astype(BF16)
                kmean_ref[h, pl.ds(j, 1), :] = jnp.mean(kr, axis=0, keepdims=True)
            return carry
        lax.fori_loop(0, nblk, body, 0)

    qrows = pl.ds(pl.multiple_of(qi * blk, blk), blk)
    qb = [_rope(q_ref[:, hs[h]].astype(F32), c_ref[qrows, :], su_ref[qrows, :], sd_ref[qrows, :]).astype(BF16)
          for h in heads]

    brow = lax.broadcasted_iota(jnp.int32, (nblk, blk), 0)
    g = [jnp.where(brow < qi, lax.dot_general(kmean_ref[h].astype(BF16), qb[h], _NT,
                                              preferred_element_type=F32), -jnp.inf) for h in heads]
    sel_t = [jnp.zeros((nblk, blk), F32) for _ in heads]
    for j in range(nblk):
        earlier = jnp.where(brow < j, 1.0, 0.0)
        for h in heads:
            gj = g[h][j:j + 1, :]
            beats = jnp.where(g[h] > gj, 1.0, jnp.where(g[h] == gj, earlier, 0.0))
            cnt = jnp.sum(beats, axis=0, keepdims=True)
            sel_t[h] = jnp.where(brow == j, jnp.where(cnt < float(min(MOBA_TOPK, nblk)), 1.0, 0.0), sel_t[h])
    eye = jnp.where(lax.broadcasted_iota(jnp.int32, (blk, blk), 0)
                    == lax.broadcasted_iota(jnp.int32, (blk, blk), 1), 1.0, 0.0).astype(BF16)
    pad = jnp.zeros((LANES - nblk, blk), F32)
    sel = [lax.dot_general(eye, jnp.concatenate([jnp.where(brow < qi, sel_t[h], 0.0), pad], axis=0).astype(BF16),
                           _NT, preferred_element_type=F32) for h in heads]
    lane = lax.broadcasted_iota(jnp.int32, (blk, LANES), 1)
    off = jnp.where(lane < nblk, NEG_INF / scale, 0.0)
    q_aug = [jnp.concatenate([qb[h], jnp.where(sel[h] > 0.5, 0.0, off).astype(BF16)], axis=1) for h in heads]

    causal = (lax.broadcasted_iota(jnp.int32, (blk, blk), 1)
              <= lax.broadcasted_iota(jnp.int32, (blk, blk), 0))
    s = [jnp.where(causal, lax.dot_general(qb[h], krot_ref[h, qrows, :dh], _NT,
                                           preferred_element_type=F32) * scale, NEG_INF) for h in heads]
    m0 = [jnp.max(s[h], axis=1, keepdims=True) for h in heads]
    p = [jnp.exp(s[h] - m0[h]) for h in heads]
    l0 = [jnp.sum(p[h], axis=1, keepdims=True) for h in heads]
    acc0 = [jnp.dot(p[h].astype(BF16), vb_ref[h, qrows, :], preferred_element_type=F32) for h in heads]

    span = MOBA_GROUP * blk

    def past_group(jg, carry):
        m, l, acc = carry
        rows = pl.ds(pl.multiple_of(jg * span, span), span)
        sj = [lax.dot_general(q_aug[h], krot_ref[h, rows, :], _NT, preferred_element_type=F32) * scale
              for h in heads]
        m_new = [jnp.maximum(m[h], jnp.max(sj[h], axis=1, keepdims=True)) for h in heads]
        a = [jnp.exp(m[h] - m_new[h]) for h in heads]
        pj = [jnp.exp(sj[h] - m_new[h]) for h in heads]
        l = [a[h] * l[h] + jnp.sum(pj[h], axis=1, keepdims=True) for h in heads]
        acc = [a[h] * acc[h] + jnp.dot(pj[h].astype(BF16), vb_ref[h, rows, :], preferred_element_type=F32)
               for h in heads]
        return m_new, l, acc

    n_groups = (qi + MOBA_GROUP - 1) // MOBA_GROUP
    _, l, acc = lax.fori_loop(0, n_groups, past_group, (m0, l0, acc0))
    for h in heads:
        o_ref[:, hs[h]] = (acc[h] / l[h]).astype(o_ref.dtype)


def _moba(z, bsz, s):
    nblk = s // MOBA_BLOCK
    c, s_up, s_dn = _rope_tables(s)
    wid = MOBA_HEADS * A_HEAD_DIM
    ncol = A_WIDTH // wid
    tab = pl.BlockSpec((s, A_HEAD_DIM), lambda b, h, qi: (0, 0))
    return pl.pallas_call(
        functools.partial(_moba_kernel, nblk=nblk),
        grid=(bsz, ncol, nblk),
        in_specs=[pl.BlockSpec((MOBA_BLOCK, wid), lambda b, h, qi: (b * nblk + qi, h)),
                  pl.BlockSpec((s, wid), lambda b, h, qi: (b, ncol + h)),
                  pl.BlockSpec((s, wid), lambda b, h, qi: (b, 2 * ncol + h)),
                  tab, tab, tab],
        out_specs=pl.BlockSpec((MOBA_BLOCK, wid), lambda b, h, qi: (b * nblk + qi, h)),
        out_shape=jax.ShapeDtypeStruct((bsz * s, A_WIDTH), BF16),
        scratch_shapes=[pltpu.VMEM((MOBA_HEADS, s, A_HEAD_DIM + LANES), BF16),
                        pltpu.VMEM((MOBA_HEADS, s, A_HEAD_DIM), BF16),
                        pltpu.VMEM((MOBA_HEADS, nblk, A_HEAD_DIM), F32)],
        compiler_params=_params("parallel", "parallel", "arbitrary"),
        name="moba",
    )(z, z, z, c, s_up, s_dn)


def _log_sigmoid(x):
    return jnp.minimum(x, 0.0) - jnp.log1p(jnp.exp(-jnp.abs(x)))


def _shift_rows(x, prev8, d):
    if d == 0:
        return x
    rolled = pltpu.roll(x, d, axis=0)
    top = jnp.where(lax.broadcasted_iota(jnp.int32, prev8.shape, 0) < d,
                    pltpu.roll(prev8, d, axis=0), rolled[:SUBLANES])
    return jnp.concatenate([top, rolled[SUBLANES:]], axis=0)


def _mlstm_kernel(q_ref, k_ref, v_ref, o_ref, gc_ref, gr_ref, bc_ref, br_ref,
                  cwq_ref, cwk_ref, cbq_ref, cbk_ref, hg_ref, y_ref,
                  c_ref, n_ref, m_ref, pq_ref, pk_ref, *, tt):
    L = MLSTM_CHUNK
    dk = B_HEAD_DIM
    heads = range(B_HEADS)
    hs = [slice(h * dk, (h + 1) * dk) for h in heads]

    @pl.when(pl.program_id(1) == 0)
    def _reset():
        c_ref[...] = jnp.zeros_like(c_ref)
        n_ref[...] = jnp.zeros_like(n_ref)
        m_ref[...] = jnp.zeros_like(m_ref)
        pq_ref[...] = jnp.zeros_like(pq_ref)
        pk_ref[...] = jnp.zeros_like(pk_ref)

    def conv_silu(x_ref, p_ref, w_ref, b_ref):
        x = x_ref[...].astype(F32)
        prev8 = p_ref[...]
        out = b_ref[...] + x * w_ref[B_CONV - 1:B_CONV, :]
        for d in range(1, B_CONV):
            out = out + _shift_rows(x, prev8, d) * w_ref[B_CONV - 1 - d:B_CONV - d, :]
        p_ref[...] = x[tt - SUBLANES:, :]
        return out * jax.nn.sigmoid(out)

    q_all = conv_silu(q_ref, pq_ref, cwq_ref, cbq_ref)
    k_all = conv_silu(k_ref, pk_ref, cwk_ref, cbk_ref) * (dk ** -0.5)

    gcol = gc_ref[...] + bc_ref[...]
    grow = gr_ref[...] + br_ref[...]
    i_col_all = [gcol[:, h:h + 1] for h in heads]
    f_col_all = [_log_sigmoid(gcol[:, B_HEADS + h:B_HEADS + h + 1]) for h in heads]
    i_row_all = [grow[h:h + 1, :] for h in heads]
    f_row_all = [_log_sigmoid(grow[B_HEADS + h:B_HEADS + h + 1, :]) for h in heads]

    rr = lax.broadcasted_iota(jnp.int32, (L, L), 0)
    cc = lax.broadcasted_iota(jnp.int32, (L, L), 1)
    tri = cc <= rr

    for c in range(tt // L):
        lo, hi = c * L, (c + 1) * L
        qc = [q_all[lo:hi, hs[h]] for h in heads]
        kc = [k_all[lo:hi, hs[h]] for h in heads]
        qcb = [x.astype(BF16) for x in qc]
        kcb = [x.astype(BF16) for x in kc]
        vcb = [v_ref[lo:hi, hs[h]].astype(BF16) for h in heads]
        i_col = [x[lo:hi] for x in i_col_all]
        f_col = [x[lo:hi] for x in f_col_all]
        i_row = [x[:, lo:hi] for x in i_row_all]
        f_row = [x[:, lo:hi] for x in f_row_all]
        m_prev = [m_ref[h] for h in heads]
        cst = [c_ref[h] for h in heads]
        nst = [n_ref[h] for h in heads]

        b_col = [jnp.sum(jnp.where(tri, f_row[h], 0.0), axis=1, keepdims=True) for h in heads]
        b_row = [jnp.sum(jnp.where(rr <= cc, f_col[h], 0.0), axis=0, keepdims=True) for h in heads]
        log_inter = [b_col[h] + m_prev[h] for h in heads]
        log_intra = [jnp.where(tri, b_col[h] - b_row[h] + i_row[h], -jnp.inf) for h in heads]
        m_t = [jnp.maximum(log_inter[h], jnp.max(log_intra[h], axis=1, keepdims=True)) for h in heads]
        w_inter = [jnp.exp(log_inter[h] - m_t[h]) for h in heads]
        qk = [lax.dot_general(qcb[h], kcb[h], _NT, preferred_element_type=F32)
              * jnp.exp(log_intra[h] - m_t[h]) for h in heads]
        q_c = [jnp.dot(qcb[h], cst[h].astype(BF16), preferred_element_type=F32) for h in heads]
        qk_v = [jnp.dot(qk[h].astype(BF16), vcb[h], preferred_element_type=F32) for h in heads]
        den = [w_inter[h] * jnp.sum(qc[h] * nst[h], axis=1, keepdims=True)
               + jnp.sum(qk[h], axis=1, keepdims=True) for h in heads]
        hc = [(w_inter[h] * q_c[h] + qk_v[h]) / jnp.maximum(jnp.abs(den[h]), jnp.exp(-m_t[h])) for h in heads]

        b_last = [jnp.sum(f_col[h], axis=0, keepdims=True) for h in heads]
        log_s = [b_last[h] - b_col[h] + i_col[h] for h in heads]
        m_new = [jnp.maximum(b_last[h] + m_prev[h], jnp.max(log_s[h], axis=0, keepdims=True)) for h in heads]
        decay = [jnp.exp(b_last[h] + m_prev[h] - m_new[h]) for h in heads]
        kw = [kc[h] * jnp.exp(log_s[h] - m_new[h]) for h in heads]
        kv = [lax.dot_general(kw[h].astype(BF16), vcb[h], _TN, preferred_element_type=F32) for h in heads]
        for h in heads:
            c_ref[h] = decay[h] * cst[h] + kv[h]
            n_ref[h] = decay[h] * nst[h] + jnp.sum(kw[h], axis=0, keepdims=True)
            m_ref[h] = m_new[h]
            hn = hc[h] * lax.rsqrt(jnp.mean(hc[h] * hc[h], axis=1, keepdims=True) + HNORM_EPS)
            y_ref[lo:hi, hs[h]] = (hn * hg_ref[:, hs[h]]
                                   * jax.nn.sigmoid(o_ref[lo:hi, hs[h]].astype(F32))).astype(y_ref.dtype)


def _mlstm(z, gates, gate_bias, conv_w, conv_b, hnorm_g, bsz, s, tt=256):
    nt = s // tt
    cb = 3 * A_WIDTH // B_WIDTH
    gates_t = gates.T
    bias = gate_bias.reshape(1, 2 * B_HEADS)

    def col(off):
        return pl.BlockSpec((tt, B_WIDTH), lambda b, t: (b * nt + t, cb + off))

    def par(rows, off):
        return pl.BlockSpec((rows, B_WIDTH), lambda b, t: (0, off))

    return pl.pallas_call(
        functools.partial(_mlstm_kernel, tt=tt),
        grid=(bsz, nt),
        in_specs=[col(0), col(1), col(2), col(3),
                  pl.BlockSpec((tt, 2 * B_HEADS), lambda b, t: (b * nt + t, 0)),
                  pl.BlockSpec((2 * B_HEADS, tt), lambda b, t: (0, b * nt + t)),
                  pl.BlockSpec((1, 2 * B_HEADS), lambda b, t: (0, 0)),
                  pl.BlockSpec((2 * B_HEADS, 1), lambda b, t: (0, 0)),
                  par(B_CONV, 0), par(B_CONV, 1), par(1, 0), par(1, 1), par(1, 0)],
        out_specs=pl.BlockSpec((tt, B_WIDTH), lambda b, t: (b * nt + t, 0)),
        out_shape=jax.ShapeDtypeStruct((bsz * s, B_WIDTH), BF16),
        scratch_shapes=[pltpu.VMEM((B_HEADS, B_HEAD_DIM, B_HEAD_DIM), F32),
                        pltpu.VMEM((B_HEADS, 1, B_HEAD_DIM), F32), pltpu.VMEM((B_HEADS, 1, 1), F32),
                        pltpu.VMEM((SUBLANES, B_WIDTH), F32), pltpu.VMEM((SUBLANES, B_WIDTH), F32)],
        compiler_params=_params("parallel", "arbitrary"),
        name="mlstm",
    )(z, z, z, z, gates, gates_t, bias, bias.T, conv_w, conv_w,
      conv_b.reshape(1, -1), conv_b.reshape(1, -1), hnorm_g.reshape(1, -1))


def _even_mixer(x, w_in, gate_bias, conv_w, conv_b, hnorm_g, w_out, bsz, s):
    d = x.shape[1]
    w_main = w_in[:, :EVEN_MAIN].astype(BF16)
    w_gate = jnp.pad(w_in[:, EVEN_MAIN:], ((0, 0), (0, LANES - 2 * B_HEADS))).astype(BF16)
    z = _matmul(x, w_main, out_dtype=BF16)
    gates = _matmul(x, w_gate, tn=LANES)[:, :2 * B_HEADS]
    y_a = _moba(z, bsz, s)
    y_b = _mlstm(z, gates, gate_bias, conv_w, conv_b, hnorm_g, bsz, s)
    w_o = w_out.astype(BF16)
    return _matmul2(y_a, y_b, w_o[:A_WIDTH], w_o[A_WIDTH:])


def _shift_mix_kernel(x_ref, p_ref, mu_ref, *o_refs, tiles_per_seq):
    x = x_ref[...]
    first = (pl.program_id(0) % tiles_per_seq) == 0
    prev_row = jnp.where(first, 0.0, p_ref[SUBLANES - 1:SUBLANES, :])
    row = lax.broadcasted_iota(jnp.int32, x.shape, 0)
    dx = jnp.where(row == 0, prev_row, pltpu.roll(x, 1, axis=0)) - x
    for j, o_ref in enumerate(o_refs):
        o_ref[...] = (x + dx * mu_ref[j:j + 1, :]).astype(o_ref.dtype)


def _shift_mix(x, mu, s, tm=256):
    t, d = x.shape
    n = mu.shape[0]
    row = pl.BlockSpec((tm, d), lambda i: (i, 0))
    prev = pl.BlockSpec((SUBLANES, d), lambda i: (jnp.maximum(i * (tm // SUBLANES) - 1, 0), 0))
    return pl.pallas_call(
        functools.partial(_shift_mix_kernel, tiles_per_seq=s // tm),
        grid=(t // tm,),
        in_specs=[row, prev, pl.BlockSpec((n, d), lambda i: (0, 0))],
        out_specs=[row] * n,
        out_shape=[jax.ShapeDtypeStruct((t, d), BF16)] * n,
        compiler_params=_params("parallel"),
        name="shift_mix",
    )(x, x, mu)


def _softplus(x):
    return jnp.maximum(x, 0.0) + jnp.log1p(jnp.exp(-jnp.abs(x)))


def _lora_kernel(x_ref, w1_ref, w2_ref, b_ref, o_ref, *, mid, out):
    hmid = jnp.dot(x_ref[...], w1_ref[...], preferred_element_type=F32)
    if mid == "tanh":
        hmid = jnp.tanh(hmid)
    elif mid == "sigmoid":
        hmid = jax.nn.sigmoid(hmid)
    y = jnp.dot(hmid.astype(BF16), w2_ref[...], preferred_element_type=F32)
    if out == "logdecay":
        y = -jnp.exp(-_softplus(-(b_ref[...] + y)) - 0.5)
    elif out == "sigmoid":
        y = jax.nn.sigmoid(b_ref[...] + y)
    o_ref[...] = y.astype(o_ref.dtype)


def _lora(x, w1, w2, bias, mid, out, out_dtype=F32, tm=512):
    t, d = x.shape
    tm = min(tm, t)
    r = w1.shape[1]
    rp = -(-r // LANES) * LANES
    w1p = jnp.pad(w1, ((0, 0), (0, rp - r))).astype(BF16)
    w2p = jnp.pad(w2, ((0, rp - r), (0, 0))).astype(BF16)
    dout = w2.shape[1]
    if bias is None:
        bias = jnp.zeros((dout,), F32)
    return pl.pallas_call(
        functools.partial(_lora_kernel, mid=mid, out=out),
        grid=(t // tm,),
        in_specs=[pl.BlockSpec((tm, d), lambda i: (i, 0)),
                  pl.BlockSpec((d, rp), lambda i: (0, 0)),
                  pl.BlockSpec((rp, dout), lambda i: (0, 0)),
                  pl.BlockSpec((1, dout), lambda i: (0, 0))],
        out_specs=pl.BlockSpec((tm, dout), lambda i: (i, 0)),
        out_shape=jax.ShapeDtypeStruct((t, dout), out_dtype),
        compiler_params=_params("parallel"),
        name="lora_" + out,
    )(x, w1p, w2p, bias.reshape(1, dout))


def _head_sums(tiles, ones_b, split=True):
    c = tiles[0].shape[0]
    x = jnp.concatenate(tiles, axis=0)
    hi = x.astype(BF16)
    tot = jnp.dot(hi, ones_b, preferred_element_type=F32)
    if split:
        lo = (x - hi.astype(F32)).astype(BF16)
        tot = tot + jnp.dot(lo, ones_b, preferred_element_type=F32)
    return [tot[p * c:(p + 1) * c] for p in range(len(tiles))]


def _bdot(a, b):
    return jnp.dot(a.astype(BF16), b.astype(BF16), preferred_element_type=F32)


def _rwkv_chunk_pairs(r, ld, kh, v, kk, bb, g_cum, s0, c):
    pairs = range(len(r))
    lane = lax.broadcasted_iota(jnp.int32, (1, LANES), 1)
    m0 = jnp.where(lane < C_HEAD_DIM, 1.0, 0.0)
    m1 = 1.0 - m0
    n2 = 2 * c
    row = lax.broadcasted_iota(jnp.int32, (n2, n2), 0)
    col = lax.broadcasted_iota(jnp.int32, (n2, n2), 1)
    lower, lower_eq = col < row, col <= row

    def stack(x):
        return jnp.concatenate([x * m0, x * m1], axis=0)

    e_g = [jnp.exp(g_cum[p]) for p in pairs]
    e_ng = [jnp.exp(-g_cum[p]) for p in pairs]
    al2 = [stack(-(kk[p] * jnp.exp(g_cum[p] - ld[p]))) for p in pairs]
    be2 = [stack(bb[p] * e_ng[p]).astype(BF16) for p in pairs]
    kb2 = [stack(kh[p] * e_ng[p]).astype(BF16) for p in pairs]
    rb = [r[p] * e_g[p] for p in pairs]
    v2 = [stack(v[p]) for p in pairs]
    sc = [lax.dot_general(jnp.concatenate([al2[p], stack(rb[p])], axis=0).astype(BF16),
                          jnp.concatenate([be2[p], kb2[p]], axis=0), _NT,
                          preferred_element_type=F32) for p in pairs]
    l_ab = [jnp.where(lower, sc[p][:n2, :n2], 0.0) for p in pairs]
    l_ak = [jnp.where(lower, sc[p][:n2, n2:], 0.0) for p in pairs]
    l_r = [jnp.concatenate([jnp.where(lower_eq, sc[p][n2:, :n2], 0.0),
                            jnp.where(lower_eq, sc[p][n2:, n2:], 0.0)], axis=1).astype(BF16) for p in pairs]

    t_off = l_ab
    pw = [x.astype(BF16) for x in l_ab]
    for _ in range(max(c.bit_length() - 2, 0)):
        pw_f = [jnp.dot(pw[p], pw[p], preferred_element_type=F32) for p in pairs]
        pw = [x.astype(BF16) for x in pw_f]
        t_off = [t_off[p] + pw_f[p] + jnp.dot(t_off[p].astype(BF16), pw[p], preferred_element_type=F32)
                 for p in pairs]

    w0 = [jnp.concatenate([al2[p], _bdot(l_ak[p], v2[p])], axis=1) for p in pairs]
    x = [w0[p] + _bdot(t_off[p], w0[p]) for p in pairs]
    xb = [x[p].astype(BF16) for p in pairs]
    z = [jnp.dot(l_r[p],
                 jnp.concatenate([xb[p], jnp.concatenate([jnp.zeros_like(v2[p]), v2[p]], axis=1).astype(BF16)],
                                 axis=0), preferred_element_type=F32) for p in pairs]
    r_eff = [(rb[p] + z[p][:c, :LANES] + z[p][c:, :LANES]).astype(BF16) for p in pairs]
    y0 = [z[p][:c, LANES:] + z[p][c:, LANES:] for p in pairs]
    mn = [lax.dot_general(xb[p], be2[p], _TN, preferred_element_type=F32) for p in pairs]
    n_add = [mn[p][LANES:] + lax.dot_general(v2[p].astype(BF16), kb2[p], _TN, preferred_element_type=F32)
             for p in pairs]
    s0b = [s0[p].astype(BF16) for p in pairs]
    y = [y0[p] + lax.dot_general(r_eff[p], s0b[p], _NT, preferred_element_type=F32) for p in pairs]
    s_new = [(s0[p] + jnp.dot(s0b[p], mn[p][:LANES].astype(BF16), preferred_element_type=F32) + n_add[p])
             * e_g[p][c - 1:c, :] for p in pairs]
    return y, s_new


def _rwkv_scan_kernel(*refs, c, npair, vres):
    if vres:
        (r_ref, ld_ref, k_ref, a_ref, v_ref, g_ref, vf_ref, vg_ref,
         kkp_ref, kap_ref, rkp_ref, lg_ref, lb_ref, tri_ref, o_ref, s_ref) = refs
    else:
        (r_ref, ld_ref, k_ref, a_ref, v_ref, g_ref,
         kkp_ref, kap_ref, rkp_ref, lg_ref, lb_ref, tri_ref, o_ref, s_ref) = refs

    @pl.when(pl.program_id(2) == 0)
    def _reset():
        s_ref[...] = jnp.zeros_like(s_ref)

    pairs = range(npair)
    sl = [slice(p * LANES, (p + 1) * LANES) for p in pairs]
    take = lambda ref: [ref[:, q].astype(F32) for q in sl]
    shift = C_HEAD_DIM.bit_length() - 1
    ones_b = jnp.where(jnp.right_shift(lax.broadcasted_iota(jnp.int32, (LANES, LANES), 0), shift)
                       == jnp.right_shift(lax.broadcasted_iota(jnp.int32, (LANES, LANES), 1), shift),
                       1.0, 0.0).astype(BF16)
    g_all = jnp.dot(tri_ref[...], ld_ref[...], precision=lax.Precision.HIGHEST, preferred_element_type=F32)

    r, ld, k, a, v, gate = (take(x) for x in (r_ref, ld_ref, k_ref, a_ref, v_ref, g_ref))
    kk_raw = [k[p] * kkp_ref[:, sl[p]] for p in pairs]
    sq = _head_sums([x * x for x in kk_raw], ones_b)
    kk = [kk_raw[p] / jnp.maximum(jnp.sqrt(sq[p]), KK_EPS) for p in pairs]
    bb = [kk[p] * a[p] for p in pairs]
    kh = [k[p] * (1.0 + (a[p] - 1.0) * kap_ref[:, sl[p]]) for p in pairs]
    if vres:
        vf, vg = take(vf_ref), take(vg_ref)
        v = [v[p] + (vf[p] - v[p]) * vg[p] for p in pairs]

    y, s_new = _rwkv_chunk_pairs(r, ld, kh, v, kk, bb, [g_all[:, q] for q in sl],
                                 [s_ref[p] for p in pairs], c)

    inv_n = 1.0 / C_HEAD_DIM
    mean = _head_sums(y, ones_b)
    yc = [y[p] - mean[p] * inv_n for p in pairs]
    var = _head_sums([x * x for x in yc], ones_b)
    bonus = _head_sums([r[p] * kh[p] * rkp_ref[:, sl[p]] for p in pairs], ones_b, split=False)
    for p in pairs:
        yn = yc[p] * lax.rsqrt(var[p] * inv_n + LNX_EPS) * lg_ref[:, sl[p]] + lb_ref[:, sl[p]]
        o_ref[:, sl[p]] = ((yn + bonus[p] * v[p]) * gate[p]).astype(o_ref.dtype)
        s_ref[p] = s_new[p]


RWKV_CHUNK = 64
RWKV_PAIRS = 8


def _rwkv_scan(r, log_w, k, a, v, gate, v_first, vgate, k_k, k_a, r_k, lnx_g, lnx_b, bsz, s):
    t, d = r.shape
    c = min(RWKV_CHUNK, s)
    wid = RWKV_PAIRS * LANES
    nc = s // c
    vres = v_first is not None
    tri = (jnp.arange(c)[None, :] <= jnp.arange(c)[:, None]).astype(F32)
    blk = pl.BlockSpec((c, wid), lambda b, g, i: (b * nc + i, g))
    vec = pl.BlockSpec((1, wid), lambda b, g, i: (0, g))
    rows = [r, log_w, k, a, v, gate] + ([v_first, vgate] if vres else [])
    vecs = [x.reshape(1, d) for x in (k_k, k_a, r_k, lnx_g, lnx_b)]
    return pl.pallas_call(
        functools.partial(_rwkv_scan_kernel, c=c, npair=RWKV_PAIRS, vres=vres),
        grid=(bsz, d // wid, nc),
        in_specs=[blk] * len(rows) + [vec] * len(vecs) + [pl.BlockSpec((c, c), lambda b, g, i: (0, 0))],
        out_specs=blk,
        out_shape=jax.ShapeDtypeStruct((t, d), BF16),
        scratch_shapes=[pltpu.VMEM((RWKV_PAIRS, LANES, LANES), F32)],
        compiler_params=_params("parallel", "parallel", "arbitrary"),
        name="rwkv_scan",
    )(*rows, *vecs, tri)


def _rwkv7_mixer(x, v_first, mu, w_r, w_k, w_v, w_o, w0, w1, w2, a0, a1, a2, g1, g2,
                 k_k, k_a, r_k, lnx_g, lnx_b, vres, bsz, s):
    xr, xw, xk, xv, xa, xg = _shift_mix(x, mu, s)
    r = _matmul(xr, w_r.astype(BF16), out_dtype=BF16)
    k = _matmul(xk, w_k.astype(BF16), out_dtype=BF16)
    v = _matmul(xv, w_v.astype(BF16), out_dtype=BF16)
    log_decay = _lora(xw, w1, w2, w0, "tanh", "logdecay")
    a = _lora(xa, a1, a2, a0, "none", "sigmoid", out_dtype=BF16)
    g = _lora(xg, g1, g2, None, "sigmoid", "none", out_dtype=BF16)
    if vres is None:
        vgate = None
        v_keep = v
    else:
        v0, v1, v2 = vres
        vgate = _lora(xv, v1, v2, v0, "none", "sigmoid", out_dtype=BF16)
        v_keep = v_first
    yo = _rwkv_scan(r, log_decay, k, a, v, g, None if vres is None else v_first, vgate,
                    k_k, k_a, r_k, lnx_g, lnx_b, bsz, s)
    return _matmul(yo, w_o.astype(BF16)), v_keep


def _lane_pick(cond, lanef):
    return jnp.min(jnp.where(cond, lanef, float(LANES)), axis=1, keepdims=True)


def _route_tile(xt, w_ref, b_ref, route_ref, cnt_ref, run_ref, tm):
    @pl.when(pl.program_id(0) == 0)
    def _reset():
        run_ref[...] = jnp.zeros_like(run_ref)

    logits = jnp.dot(xt.astype(BF16), w_ref[...], preferred_element_type=F32) + b_ref[...]
    lane = lax.broadcasted_iota(jnp.int32, (tm, LANES), 1)
    lanef = lane.astype(F32)

    def masked_softmax(keep):
        z = jnp.where(keep, logits, -jnp.inf)
        e = jnp.exp(z - jnp.max(z, axis=1, keepdims=True))
        return e / jnp.sum(e, axis=1, keepdims=True)

    in_groups = lane < N_GROUPS
    pg = masked_softmax(in_groups)
    g_p = jnp.max(pg, axis=1, keepdims=True)
    g_idx = _lane_pick(jnp.where(in_groups, pg, -1.0) == g_p, lanef)
    lo = float(N_GROUPS) + float(EXPERTS_PER_GROUP) * g_idx
    half = 0.5 * float(EXPERTS_PER_GROUP - 1)
    in_group = jnp.abs(lanef - lo - half) < half + 0.5
    pe = jnp.where(in_group, masked_softmax(in_group), -1.0)
    p1 = jnp.max(pe, axis=1, keepdims=True)
    i1 = _lane_pick(pe == p1, lanef)
    pe2 = jnp.where(lanef == i1, -1.0, pe)
    p2 = jnp.max(pe2, axis=1, keepdims=True)
    i2 = _lane_pick(pe2 == p2, lanef)
    den = p1 + p2
    gate = (g_p * (p1 / den), g_p * (p2 / den))
    expert = (i1 - float(N_GROUPS), i2 - float(N_GROUPS))

    earlier = jnp.where(lax.broadcasted_iota(jnp.int32, (tm, tm), 1)
                        < lax.broadcasted_iota(jnp.int32, (tm, tm), 0), 1.0, 0.0).astype(BF16)
    run = run_ref[...]
    rank = []
    for e in expert:
        onehot = jnp.where(lanef == e, 1.0, 0.0)
        before = jnp.dot(earlier, onehot.astype(BF16), preferred_element_type=F32) + run
        rank.append(jnp.sum(onehot * before, axis=1, keepdims=True))
        run = run + jnp.sum(onehot, axis=0, keepdims=True)
    run_ref[...] = run
    cnt_ref[...] = run
    cols = (expert[0], expert[1], rank[0], rank[1], gate[0], gate[1])
    route = jnp.zeros((tm, LANES), F32)
    for c, val in enumerate(cols):
        route = jnp.where(lane == c, val, route)
    route_ref[...] = route


SLAB = 2048 // LANES
DMA_UNROLL = 4


def _slab(ref, r):
    return ref.at[pl.ds(pl.multiple_of(r * SLAB, SLAB), SLAB)]


def _slab_rows(ref, rows):
    return jnp.concatenate([ref[pl.ds(c, rows, stride=SLAB), :] for c in range(SLAB)], axis=1)


def _store_slab_rows(ref, val, rows):
    for c in range(SLAB):
        ref[pl.ds(c, rows, stride=SLAB), :] = val[:, c * LANES:(c + 1) * LANES]


def _expert_kernel(be_ref, nu_ref, idx_ref, nxt_ref, x_hbm, wg_ref, wu_ref, wd_ref, o_ref,
                   xbuf, sems, wg_b, wu_b, wd_b, *, rows):
    i = pl.program_id(0)
    n_used = nu_ref[0]
    slot = i % 2

    def token_copy(tok, r, dst_slot):
        return pltpu.make_async_copy(_slab(x_hbm, tok), _slab(xbuf.at[dst_slot], r), sems.at[dst_slot])

    def gather(ids_ref, dst_slot):
        def start(r, carry):
            for rr in (r, r + rows // 2):
                token_copy(ids_ref[0, rr], rr, dst_slot).start()
            return carry
        lax.fori_loop(0, rows // 2, start, 0, unroll=DMA_UNROLL)

    @pl.when(i == 0)
    def _():
        gather(idx_ref, 0)

    @pl.when(i + 1 < n_used)
    def _():
        gather(nxt_ref, 1 - slot)

    @pl.when(jnp.logical_or(i == 0, be_ref[i] != be_ref[jnp.maximum(i - 1, 0)]))
    def _():
        wg_b[...] = wg_ref[...].astype(BF16)
        wu_b[...] = wu_ref[...].astype(BF16)
        wd_b[...] = wd_ref[...].astype(BF16)

    @pl.when(i < n_used)
    def _():
        for r in range(rows):
            token_copy(0, r, slot).wait()
        xb = _slab_rows(xbuf.at[slot], rows).astype(BF16)
        hg = jnp.dot(xb, wg_b[...], preferred_element_type=F32)
        hu = jnp.dot(xb, wu_b[...], preferred_element_type=F32)
        hid = (hg * jax.nn.sigmoid(hg) * hu).astype(BF16)
        _store_slab_rows(o_ref, jnp.dot(hid, wd_b[...], preferred_element_type=F32), rows)

    @pl.when(i >= n_used)
    def _():
        o_ref[...] = jnp.zeros_like(o_ref)


def _expert_mlp(x_slabs, src_row, block_e, n_used, w_gate, w_up, w_down, layer, rows=MOE_ROWS):
    d, ff = w_gate.shape[2], w_gate.shape[3]
    nb = src_row.shape[0] // rows
    ids = src_row.reshape(nb, 1, rows)
    grid_spec = pltpu.PrefetchScalarGridSpec(
        num_scalar_prefetch=2,
        grid=(nb,),
        in_specs=[pl.BlockSpec((None, 1, rows), lambda i, be, nu: (i, 0, 0), memory_space=pltpu.SMEM),
                  pl.BlockSpec((None, 1, rows), lambda i, be, nu: (jnp.minimum(i + 1, nb - 1), 0, 0),
                               memory_space=pltpu.SMEM),
                  pl.BlockSpec(memory_space=pl.ANY),
                  pl.BlockSpec((None, None, d, ff), lambda i, be, nu: (layer, be[i], 0, 0)),
                  pl.BlockSpec((None, None, d, ff), lambda i, be, nu: (layer, be[i], 0, 0)),
                  pl.BlockSpec((None, None, ff, d), lambda i, be, nu: (layer, be[i], 0, 0))],
        out_specs=pl.BlockSpec((rows * SLAB, LANES), lambda i, be, nu: (i, 0)),
        scratch_shapes=[pltpu.VMEM((2, rows * SLAB, LANES), F32), pltpu.SemaphoreType.DMA((2,)),
                        pltpu.VMEM((d, ff), BF16), pltpu.VMEM((d, ff), BF16), pltpu.VMEM((ff, d), BF16)],
    )
    return pl.pallas_call(
        functools.partial(_expert_kernel, rows=rows),
        grid_spec=grid_spec,
        out_shape=jax.ShapeDtypeStruct((nb * rows * SLAB, LANES), F32),
        compiler_params=_params("arbitrary"),
        name="expert_mlp",
    )(block_e, n_used, ids, ids, x_slabs, w_gate, w_up, w_down)


def _combine_ln_kernel(idx_ref, nxt_ref, y_ref, x_ref, gt_ref, g_ref, b_ref, o_ref, buf, sems, *, rows):
    i = pl.program_id(0)
    slot = i % 2

    def row_copy(src, r, kslot, dst_slot):
        return pltpu.make_async_copy(_slab(y_ref, src), _slab(buf.at[dst_slot, kslot], r), sems.at[dst_slot])

    def gather(ids_ref, dst_slot):
        def start(r, carry):
            for kslot in range(MOE_TOPK):
                for rr in (r, r + rows // 2):
                    row_copy(ids_ref[0, MOE_TOPK * rr + kslot], rr, kslot, dst_slot).start()
            return carry
        lax.fori_loop(0, rows // 2, start, 0, unroll=DMA_UNROLL)

    @pl.when(i == 0)
    def _():
        gather(idx_ref, 0)

    @pl.when(i + 1 < pl.num_programs(0))
    def _():
        gather(nxt_ref, 1 - slot)

    for r in range(rows):
        for kslot in range(MOE_TOPK):
            row_copy(0, r, kslot, slot).wait()
    gt = gt_ref[...]
    ffn = gt[:, 0:1] * _slab_rows(buf.at[slot, 0], rows)
    for kslot in range(1, MOE_TOPK):
        ffn = ffn + gt[:, kslot:kslot + 1] * _slab_rows(buf.at[slot, kslot], rows)
    o_ref[...] = _ln_rows(DN_ALPHA * x_ref[...] + ffn, g_ref[...], b_ref[...])


def _combine_ln(y_sorted, dest, gates, x, g, b, rows=256):
    t, d = x.shape
    nb = t // rows
    row = pl.BlockSpec((rows, d), lambda i: (i, 0))
    vec = pl.BlockSpec((1, d), lambda i: (0, 0))
    ids = dest.reshape(nb, 1, rows * MOE_TOPK)
    return pl.pallas_call(
        functools.partial(_combine_ln_kernel, rows=rows),
        grid=(nb,),
        in_specs=[pl.BlockSpec((None, 1, rows * MOE_TOPK), lambda i: (i, 0, 0), memory_space=pltpu.SMEM),
                  pl.BlockSpec((None, 1, rows * MOE_TOPK), lambda i: (jnp.minimum(i + 1, nb - 1), 0, 0),
                               memory_space=pltpu.SMEM),
                  pl.BlockSpec(memory_space=pl.ANY), row,
                  pl.BlockSpec((rows, MOE_TOPK), lambda i: (i, 0)), vec, vec],
        out_specs=row,
        out_shape=jax.ShapeDtypeStruct((t, d), F32),
        scratch_shapes=[pltpu.VMEM((2, MOE_TOPK, rows * SLAB, LANES), F32), pltpu.SemaphoreType.DMA((2,))],
        compiler_params=_params("arbitrary"),
        name="combine_ln",
    )(ids, ids, y_sorted, x, gates, g.reshape(1, d), b.reshape(1, d))


def _router_params(w_grp, b_grp, w_exp_r, b_exp_r):
    n_r = N_GROUPS + N_EXPERTS
    w_router = jnp.pad(jnp.concatenate([w_grp, w_exp_r], axis=1), ((0, 0), (0, LANES - n_r))).astype(BF16)
    b_router = jnp.pad(jnp.concatenate([b_grp, b_exp_r]), (0, LANES - n_r)).reshape(1, LANES)
    return w_router, b_router


def _moe_layer(x, x_slabs, route, cnt, w_gate, w_up, w_down, ln_g, ln_b, layer):
    t, d = x.shape
    flat_e = route[:, 0:MOE_TOPK].astype(jnp.int32).reshape(-1)
    rank = route[:, MOE_TOPK:2 * MOE_TOPK].astype(jnp.int32).reshape(-1)
    gates = route[:, 2 * MOE_TOPK:3 * MOE_TOPK]
    counts = cnt[0, :N_EXPERTS].astype(jnp.int32)
    m = t * MOE_TOPK
    padded = (counts + MOE_ROWS - 1) // MOE_ROWS * MOE_ROWS
    pad_end = jnp.cumsum(padded)
    dest = (pad_end - padded)[flat_e] + rank
    n_blocks = (m + N_EXPERTS * (MOE_ROWS - 1) + MOE_ROWS - 1) // MOE_ROWS
    src_row = jnp.zeros((n_blocks * MOE_ROWS,), jnp.int32).at[dest].set(
        (jnp.arange(m) // MOE_TOPK).astype(jnp.int32))
    starts = jnp.arange(n_blocks, dtype=jnp.int32) * MOE_ROWS
    block_e = jnp.minimum(jnp.sum((pad_end[None, :] <= starts[:, None]).astype(jnp.int32), axis=1),
                          N_EXPERTS - 1).astype(jnp.int32)
    n_used = (pad_end[-1] // MOE_ROWS).astype(jnp.int32).reshape(1)

    ys = _expert_mlp(x_slabs, src_row, block_e, n_used, w_gate, w_up, w_down, layer)
    return _combine_ln(ys, dest.astype(jnp.int32), gates, x, ln_g, ln_b)


def kernel(x, ev_w_in, ev_gate_bias, ev_conv_w, ev_conv_b, ev_hnorm_g, ev_w_out, od_mu, od_w_r, od_w_k, od_w_v, od_w_o, od_w0, od_w1, od_w2, od_a0, od_a1, od_a2, od_g1, od_g2, od_k_k, od_k_a, od_r_k, od_lnx_g, od_lnx_b, od_v0, od_v1, od_v2, ln_mix_g, ln_mix_b, ln_ffn_g, ln_ffn_b, moe_w_grp, moe_b_grp, moe_w_exp_r, moe_b_exp_r, moe_w_gate, moe_w_up, moe_w_down):
    bsz, s, d = x.shape
    depth = ln_mix_g.shape[0]
    xf = x.reshape(bsz * s, d)
    v_first = None
    for layer in range(depth):
        if layer % 2 == 0:
            e = layer // 2
            mix = _even_mixer(xf, ev_w_in[e], ev_gate_bias[e], ev_conv_w[e], ev_conv_b[e],
                              ev_hnorm_g[e], ev_w_out[e], bsz, s)
        else:
            o = layer // 2
            vres = None if o == 0 else (od_v0[o - 1], od_v1[o - 1], od_v2[o - 1])
            mix, v_first = _rwkv7_mixer(xf, v_first, od_mu[o], od_w_r[o], od_w_k[o], od_w_v[o],
                                        od_w_o[o], od_w0[o], od_w1[o], od_w2[o], od_a0[o],
                                        od_a1[o], od_a2[o], od_g1[o], od_g2[o], od_k_k[o],
                                        od_k_a[o], od_r_k[o], od_lnx_g[o], od_lnx_b[o], vres, bsz, s)
        xf, x_slabs, route, cnt = _resid_ln(xf, mix, ln_mix_g[layer], ln_mix_b[layer],
                                            *_router_params(moe_w_grp[layer], moe_b_grp[layer],
                                                            moe_w_exp_r[layer], moe_b_exp_r[layer]))
        xf = _moe_layer(xf, x_slabs, route, cnt, moe_w_gate, moe_w_up, moe_w_down,
                        ln_ffn_g[layer], ln_ffn_b[layer], layer)
    return xf.reshape(bsz, s, d)
```

```python
import functools
import math

import jax
import jax.numpy as jnp
from jax import lax
from jax.experimental import pallas as pl
from jax.experimental.pallas import tpu as pltpu

F32 = jnp.float32
BF16 = jnp.bfloat16

DEPTH = 4
DN_ALPHA = (2 * DEPTH) ** 0.25
LN_EPS = 1e-5

A_HEADS = 8
A_HEAD_DIM = 128
A_WIDTH = A_HEADS * A_HEAD_DIM
MOBA_BLOCK = 256
MOBA_TOPK = 3
MOBA_GROUP = 4
MOBA_HEADS = 4
ROPE_THETA = 500000.0
ROPE_DIM = A_HEAD_DIM // 4
NEG_INF = -1e30

B_HEADS = 4
B_HEAD_DIM = 256
B_WIDTH = B_HEADS * B_HEAD_DIM
MLSTM_CHUNK = 64
B_CONV = 4
HNORM_EPS = 1e-6
EVEN_MAIN = 3 * A_WIDTH + 4 * B_WIDTH

C_HEAD_DIM = 64
LNX_EPS = 64e-5
KK_EPS = 1e-12

N_GROUPS = 4
EXPERTS_PER_GROUP = 8
N_EXPERTS = N_GROUPS * EXPERTS_PER_GROUP
MOE_TOPK = 2
MOE_ROWS = 256

LANES = 128
SUBLANES = 8
VMEM_LIMIT = 56 * 1024 * 1024

_NT = (((1,), (1,)), ((), ()))
_TN = (((0,), (0,)), ((), ()))


def _params(*sem):
    return pltpu.CompilerParams(dimension_semantics=sem, vmem_limit_bytes=VMEM_LIMIT)


def _mm_kernel(x_ref, w_ref, o_ref):
    o_ref[...] = jnp.dot(x_ref[...].astype(BF16), w_ref[...],
                         preferred_element_type=F32).astype(o_ref.dtype)


def _matmul(x, w, out_dtype=F32, tm=1024, tn=512):
    m, k = x.shape
    n = w.shape[1]
    tm, tn = min(tm, m), min(tn, n)
    return pl.pallas_call(
        _mm_kernel,
        grid=(m // tm, n // tn),
        in_specs=[pl.BlockSpec((tm, k), lambda i, j: (i, 0)),
                  pl.BlockSpec((k, tn), lambda i, j: (0, j))],
        out_specs=pl.BlockSpec((tm, tn), lambda i, j: (i, j)),
        out_shape=jax.ShapeDtypeStruct((m, n), out_dtype),
        compiler_params=_params("parallel", "arbitrary"),
        name="matmul",
    )(x, w)


def _mm2_kernel(xa_ref, xb_ref, wa_ref, wb_ref, o_ref):
    acc = jnp.dot(xa_ref[...], wa_ref[...], preferred_element_type=F32)
    acc = acc + jnp.dot(xb_ref[...], wb_ref[...], preferred_element_type=F32)
    o_ref[...] = acc


def _matmul2(xa, xb, wa, wb, tm=1024, tn=512):
    m, ka = xa.shape
    kb = xb.shape[1]
    n = wa.shape[1]
    tm, tn = min(tm, m), min(tn, n)
    return pl.pallas_call(
        _mm2_kernel,
        grid=(m // tm, n // tn),
        in_specs=[pl.BlockSpec((tm, ka), lambda i, j: (i, 0)),
                  pl.BlockSpec((tm, kb), lambda i, j: (i, 0)),
                  pl.BlockSpec((ka, tn), lambda i, j: (0, j)),
                  pl.BlockSpec((kb, tn), lambda i, j: (0, j))],
        out_specs=pl.BlockSpec((tm, tn), lambda i, j: (i, j)),
        out_shape=jax.ShapeDtypeStruct((m, n), F32),
        compiler_params=_params("parallel", "arbitrary"),
        name="matmul2",
    )(xa, xb, wa, wb)


def _ln_rows(h, g, b):
    mu = jnp.mean(h, axis=-1, keepdims=True)
    hc = h - mu
    var = jnp.mean(hc * hc, axis=-1, keepdims=True)
    return hc * lax.rsqrt(var + LN_EPS) * g + b


def _resid_ln_kernel(x_ref, m_ref, g_ref, b_ref, w_ref, br_ref, o_ref, s_ref, route_ref, cnt_ref, run_ref,
                     *, tm):
    y = _ln_rows(DN_ALPHA * x_ref[...] + m_ref[...], g_ref[...], b_ref[...])
    o_ref[...] = y.astype(o_ref.dtype)
    _store_slab_rows(s_ref, y, tm)
    _route_tile(y, w_ref, br_ref, route_ref, cnt_ref, run_ref, tm)


def _resid_ln(x, mix, g, b, w_router, b_router, tm=256):
    t, d = x.shape
    tm = min(tm, t)
    row = pl.BlockSpec((tm, d), lambda i: (i, 0))
    vec = pl.BlockSpec((1, d), lambda i: (0, 0))
    one = pl.BlockSpec((1, LANES), lambda i: (0, 0))
    return pl.pallas_call(
        functools.partial(_resid_ln_kernel, tm=tm),
        grid=(t // tm,),
        in_specs=[row, row, vec, vec, pl.BlockSpec((d, LANES), lambda i: (0, 0)), one],
        out_specs=[row, pl.BlockSpec((tm * SLAB, LANES), lambda i: (i, 0)),
                   pl.BlockSpec((tm, LANES), lambda i: (i, 0)), one],
        out_shape=[jax.ShapeDtypeStruct((t, d), F32), jax.ShapeDtypeStruct((t * SLAB, LANES), F32),
                   jax.ShapeDtypeStruct((t, LANES), F32), jax.ShapeDtypeStruct((1, LANES), F32)],
        scratch_shapes=[pltpu.VMEM((1, LANES), F32)],
        compiler_params=_params("arbitrary"),
        name="resid_ln_router",
    )(x, mix, g.reshape(1, d), b.reshape(1, d), w_router, b_router)


def _rope_tables(s):
    half = ROPE_DIM // 2
    inv = jnp.power(jnp.float32(ROPE_THETA), -jnp.arange(half, dtype=F32) / half)
    ang = jnp.arange(s).astype(F32)[:, None] * inv[None, :]
    cos, sin = jnp.cos(ang), jnp.sin(ang)
    zero = jnp.zeros((s, A_HEAD_DIM - ROPE_DIM), F32)
    zh = jnp.zeros((s, half), F32)
    c = jnp.concatenate([cos, cos, zero + 1.0], axis=-1)
    s_up = jnp.concatenate([zh, sin, zero], axis=-1)
    s_dn = jnp.concatenate([-sin, zh, zero], axis=-1)
    return c, s_up, s_dn


def _rope(x, c, s_up, s_dn):
    half = ROPE_DIM // 2
    return (x * c + pltpu.roll(x, half, axis=1) * s_up
            + pltpu.roll(x, A_HEAD_DIM - half, axis=1) * s_dn)


def _moba_kernel(q_ref, k_ref, v_ref, c_ref, su_ref, sd_ref, o_ref, krot_ref, vb_ref, kmean_ref, *, nblk):
    qi = pl.program_id(2)
    blk = MOBA_BLOCK
    dh = A_HEAD_DIM
    scale = dh ** -0.5
    heads = range(MOBA_HEADS)
    hs = [slice(h * dh, (h + 1) * dh) for h in heads]

    @pl.when(qi == 0)
    def _prepare_keys():
        def body(j, carry):
            rows = pl.ds(pl.multiple_of(j * blk, blk), blk)
            block_tag = jnp.where(lax.broadcasted_iota(jnp.int32, (blk, LANES), 1) == j, 1.0, 0.0)
            for h in heads:
                kr = _rope(k_ref[rows, hs[h]].astype(F32), c_ref[rows, :], su_ref[rows, :], sd_ref[rows, :])
                krot_ref[h, rows, :] = jnp.concatenate([kr, block_tag], axis=1).astype(BF16)
                vb_ref[h, rows, :] = v_ref[rows, hs[h]].astype(BF16)
                kmean_ref[h, pl.ds(j, 1), :] = jnp.mean(kr, axis=0, keepdims=True)
            return carry
        lax.fori_loop(0, nblk, body, 0)

    qrows = pl.ds(pl.multiple_of(qi * blk, blk), blk)
    qb = [_rope(q_ref[:, hs[h]].astype(F32), c_ref[qrows, :], su_ref[qrows, :], sd_ref[qrows, :]).astype(BF16)
          for h in heads]

    brow = lax.broadcasted_iota(jnp.int32, (nblk, blk), 0)
    g = [jnp.where(brow < qi, lax.dot_general(kmean_ref[h].astype(BF16), qb[h], _NT,
                                              preferred_element_type=F32), -jnp.inf) for h in heads]
    sel_t = [jnp.zeros((nblk, blk), F32) for _ in heads]
    for j in range(nblk):
        earlier = jnp.where(brow < j, 1.0, 0.0)
        for h in heads:
            gj = g[h][j:j + 1, :]
            beats = jnp.where(g[h] > gj, 1.0, jnp.where(g[h] == gj, earlier, 0.0))
            cnt = jnp.sum(beats, axis=0, keepdims=True)
            sel_t[h] = jnp.where(brow == j, jnp.where(cnt < float(min(MOBA_TOPK, nblk)), 1.0, 0.0), sel_t[h])
    eye = jnp.where(lax.broadcasted_iota(jnp.int32, (blk, blk), 0)
                    == lax.broadcasted_iota(jnp.int32, (blk, blk), 1), 1.0, 0.0).astype(BF16)
    pad = jnp.zeros((LANES - nblk, blk), F32)
    sel = [lax.dot_general(eye, jnp.concatenate([jnp.where(brow < qi, sel_t[h], 0.0), pad], axis=0).astype(BF16),
                           _NT, preferred_element_type=F32) for h in heads]
    lane = lax.broadcasted_iota(jnp.int32, (blk, LANES), 1)
    off = jnp.where(lane < nblk, NEG_INF / scale, 0.0)
    q_aug = [jnp.concatenate([qb[h], jnp.where(sel[h] > 0.5, 0.0, off).astype(BF16)], axis=1) for h in heads]

    causal = (lax.broadcasted_iota(jnp.int32, (blk, blk), 1)
              <= lax.broadcasted_iota(jnp.int32, (blk, blk), 0))
    s = [jnp.where(causal, lax.dot_general(qb[h], krot_ref[h, qrows, :dh], _NT,
                                           preferred_element_type=F32) * scale, NEG_INF) for h in heads]
    m0 = [jnp.max(s[h], axis=1, keepdims=True) for h in heads]
    p = [jnp.exp(s[h] - m0[h]) for h in heads]
    l0 = [jnp.sum(p[h], axis=1, keepdims=True) for h in heads]
    acc0 = [jnp.dot(p[h].astype(BF16), vb_ref[h, qrows, :], preferred_element_type=F32) for h in heads]

    span = MOBA_GROUP * blk

    def past_group(jg, carry):
        m, l, acc = carry
        rows = pl.ds(pl.multiple_of(jg * span, span), span)
        sj = [lax.dot_general(q_aug[h], krot_ref[h, rows, :], _NT, preferred_element_type=F32) * scale
              for h in heads]
        m_new = [jnp.maximum(m[h], jnp.max(sj[h], axis=1, keepdims=True)) for h in heads]
        a = [jnp.exp(m[h] - m_new[h]) for h in heads]
        pj = [jnp.exp(sj[h] - m_new[h]) for h in heads]
        l = [a[h] * l[h] + jnp.sum(pj[h], axis=1, keepdims=True) for h in heads]
        acc = [a[h] * acc[h] + jnp.dot(pj[h].astype(BF16), vb_ref[h, rows, :], preferred_element_type=F32)
               for h in heads]
        return m_new, l, acc

    n_groups = (qi + MOBA_GROUP - 1) // MOBA_GROUP
    _, l, acc = lax.fori_loop(0, n_groups, past_group, (m0, l0, acc0))
    for h in heads:
        o_ref[:, hs[h]] = (acc[h] / l[h]).astype(o_ref.dtype)


def _moba(z, bsz, s):
    nblk = s // MOBA_BLOCK
    c, s_up, s_dn = _rope_tables(s)
    wid = MOBA_HEADS * A_HEAD_DIM
    ncol = A_WIDTH // wid
    tab = pl.BlockSpec((s, A_HEAD_DIM), lambda b, h, qi: (0, 0))
    return pl.pallas_call(
        functools.partial(_moba_kernel, nblk=nblk),
        grid=(bsz, ncol, nblk),
        in_specs=[pl.BlockSpec((MOBA_BLOCK, wid), lambda b, h, qi: (b * nblk + qi, h)),
                  pl.BlockSpec((s, wid), lambda b, h, qi: (b, ncol + h)),
                  pl.BlockSpec((s, wid), lambda b, h, qi: (b, 2 * ncol + h)),
                  tab, tab, tab],
        out_specs=pl.BlockSpec((MOBA_BLOCK, wid), lambda b, h, qi: (b * nblk + qi, h)),
        out_shape=jax.ShapeDtypeStruct((bsz * s, A_WIDTH), BF16),
        scratch_shapes=[pltpu.VMEM((MOBA_HEADS, s, A_HEAD_DIM + LANES), BF16),
                        pltpu.VMEM((MOBA_HEADS, s, A_HEAD_DIM), BF16),
                        pltpu.VMEM((MOBA_HEADS, nblk, A_HEAD_DIM), F32)],
        compiler_params=_params("parallel", "parallel", "arbitrary"),
        name="moba",
    )(z, z, z, c, s_up, s_dn)


def _log_sigmoid(x):
    return jnp.minimum(x, 0.0) - jnp.log1p(jnp.exp(-jnp.abs(x)))


def _shift_rows(x, prev8, d):
    if d == 0:
        return x
    rolled = pltpu.roll(x, d, axis=0)
    top = jnp.where(lax.broadcasted_iota(jnp.int32, prev8.shape, 0) < d,
                    pltpu.roll(prev8, d, axis=0), rolled[:SUBLANES])
    return jnp.concatenate([top, rolled[SUBLANES:]], axis=0)


def _mlstm_kernel(q_ref, k_ref, v_ref, o_ref, gc_ref, gr_ref, bc_ref, br_ref,
                  cwq_ref, cwk_ref, cbq_ref, cbk_ref, hg_ref, y_ref,
                  c_ref, n_ref, m_ref, pq_ref, pk_ref, *, tt):
    L = MLSTM_CHUNK
    dk = B_HEAD_DIM
    heads = range(B_HEADS)
    hs = [slice(h * dk, (h + 1) * dk) for h in heads]

    @pl.when(pl.program_id(1) == 0)
    def _reset():
        c_ref[...] = jnp.zeros_like(c_ref)
        n_ref[...] = jnp.zeros_like(n_ref)
        m_ref[...] = jnp.zeros_like(m_ref)
        pq_ref[...] = jnp.zeros_like(pq_ref)
        pk_ref[...] = jnp.zeros_like(pk_ref)

    def conv_silu(x_ref, p_ref, w_ref, b_ref):
        x = x_ref[...].astype(F32)
        prev8 = p_ref[...]
        out = b_ref[...] + x * w_ref[B_CONV - 1:B_CONV, :]
        for d in range(1, B_CONV):
            out = out + _shift_rows(x, prev8, d) * w_ref[B_CONV - 1 - d:B_CONV - d, :]
        p_ref[...] = x[tt - SUBLANES:, :]
        return out * jax.nn.sigmoid(out)

    q_all = conv_silu(q_ref, pq_ref, cwq_ref, cbq_ref)
    k_all = conv_silu(k_ref, pk_ref, cwk_ref, cbk_ref) * (dk ** -0.5)

    gcol = gc_ref[...] + bc_ref[...]
    grow = gr_ref[...] + br_ref[...]
    i_col_all = [gcol[:, h:h + 1] for h in heads]
    f_col_all = [_log_sigmoid(gcol[:, B_HEADS + h:B_HEADS + h + 1]) for h in heads]
    i_row_all = [grow[h:h + 1, :] for h in heads]
    f_row_all = [_log_sigmoid(grow[B_HEADS + h:B_HEADS + h + 1, :]) for h in heads]

    rr = lax.broadcasted_iota(jnp.int32, (L, L), 0)
    cc = lax.broadcasted_iota(jnp.int32, (L, L), 1)
    tri = cc <= rr

    for c in range(tt // L):
        lo, hi = c * L, (c + 1) * L
        qc = [q_all[lo:hi, hs[h]] for h in heads]
        kc = [k_all[lo:hi, hs[h]] for h in heads]
        qcb = [x.astype(BF16) for x in qc]
        kcb = [x.astype(BF16) for x in kc]
        vcb = [v_ref[lo:hi, hs[h]].astype(BF16) for h in heads]
        i_col = [x[lo:hi] for x in i_col_all]
        f_col = [x[lo:hi] for x in f_col_all]
        i_row = [x[:, lo:hi] for x in i_row_all]
        f_row = [x[:, lo:hi] for x in f_row_all]
        m_prev = [m_ref[h] for h in heads]
        cst = [c_ref[h] for h in heads]
        nst = [n_ref[h] for h in heads]

        b_col = [jnp.sum(jnp.where(tri, f_row[h], 0.0), axis=1, keepdims=True) for h in heads]
        b_row = [jnp.sum(jnp.where(rr <= cc, f_col[h], 0.0), axis=0, keepdims=True) for h in heads]
        log_inter = [b_col[h] + m_prev[h] for h in heads]
        log_intra = [jnp.where(tri, b_col[h] - b_row[h] + i_row[h], -jnp.inf) for h in heads]
        m_t = [jnp.maximum(log_inter[h], jnp.max(log_intra[h], axis=1, keepdims=True)) for h in heads]
        w_inter = [jnp.exp(log_inter[h] - m_t[h]) for h in heads]
        qk = [lax.dot_general(qcb[h], kcb[h], _NT, preferred_element_type=F32)
              * jnp.exp(log_intra[h] - m_t[h]) for h in heads]
        q_c = [jnp.dot(qcb[h], cst[h].astype(BF16), preferred_element_type=F32) for h in heads]
        qk_v = [jnp.dot(qk[h].astype(BF16), vcb[h], preferred_element_type=F32) for h in heads]
        den = [w_inter[h] * jnp.sum(qc[h] * nst[h], axis=1, keepdims=True)
               + jnp.sum(qk[h], axis=1, keepdims=True) for h in heads]
        hc = [(w_inter[h] * q_c[h] + qk_v[h]) / jnp.maximum(jnp.abs(den[h]), jnp.exp(-m_t[h])) for h in heads]

        b_last = [jnp.sum(f_col[h], axis=0, keepdims=True) for h in heads]
        log_s = [b_last[h] - b_col[h] + i_col[h] for h in heads]
        m_new = [jnp.maximum(b_last[h] + m_prev[h], jnp.max(log_s[h], axis=0, keepdims=True)) for h in heads]
        decay = [jnp.exp(b_last[h] + m_prev[h] - m_new[h]) for h in heads]
        kw = [kc[h] * jnp.exp(log_s[h] - m_new[h]) for h in heads]
        kv = [lax.dot_general(kw[h].astype(BF16), vcb[h], _TN, preferred_element_type=F32) for h in heads]
        for h in heads:
            c_ref[h] = decay[h] * cst[h] + kv[h]
            n_ref[h] = decay[h] * nst[h] + jnp.sum(kw[h], axis=0, keepdims=True)
            m_ref[h] = m_new[h]
            hn = hc[h] * lax.rsqrt(jnp.mean(hc[h] * hc[h], axis=1, keepdims=True) + HNORM_EPS)
            y_ref[lo:hi, hs[h]] = (hn * hg_ref[:, hs[h]]
                                   * jax.nn.sigmoid(o_ref[lo:hi, hs[h]].astype(F32))).astype(y_ref.dtype)


def _mlstm(z, gates, gate_bias, conv_w, conv_b, hnorm_g, bsz, s, tt=256):
    nt = s // tt
    cb = 3 * A_WIDTH // B_WIDTH
    gates_t = gates.T
    bias = gate_bias.reshape(1, 2 * B_HEADS)

    def col(off):
        return pl.BlockSpec((tt, B_WIDTH), lambda b, t: (b * nt + t, cb + off))

    def par(rows, off):
        return pl.BlockSpec((rows, B_WIDTH), lambda b, t: (0, off))

    return pl.pallas_call(
        functools.partial(_mlstm_kernel, tt=tt),
        grid=(bsz, nt),
        in_specs=[col(0), col(1), col(2), col(3),
                  pl.BlockSpec((tt, 2 * B_HEADS), lambda b, t: (b * nt + t, 0)),
                  pl.BlockSpec((2 * B_HEADS, tt), lambda b, t: (0, b * nt + t)),
                  pl.BlockSpec((1, 2 * B_HEADS), lambda b, t: (0, 0)),
                  pl.BlockSpec((2 * B_HEADS, 1), lambda b, t: (0, 0)),
                  par(B_CONV, 0), par(B_CONV, 1), par(1, 0), par(1, 1), par(1, 0)],
        out_specs=pl.BlockSpec((tt, B_WIDTH), lambda b, t: (b * nt + t, 0)),
        out_shape=jax.ShapeDtypeStruct((bsz * s, B_WIDTH), BF16),
        scratch_shapes=[pltpu.VMEM((B_HEADS, B_HEAD_DIM, B_HEAD_DIM), F32),
                        pltpu.VMEM((B_HEADS, 1, B_HEAD_DIM), F32), pltpu.VMEM((B_HEADS, 1, 1), F32),
                        pltpu.VMEM((SUBLANES, B_WIDTH), F32), pltpu.VMEM((SUBLANES, B_WIDTH), F32)],
        compiler_params=_params("parallel", "arbitrary"),
        name="mlstm",
    )(z, z, z, z, gates, gates_t, bias, bias.T, conv_w, conv_w,
      conv_b.reshape(1, -1), conv_b.reshape(1, -1), hnorm_g.reshape(1, -1))


def _even_mixer(x, w_in, gate_bias, conv_w, conv_b, hnorm_g, w_out, bsz, s):
    d = x.shape[1]
    w_main = w_in[:, :EVEN_MAIN].astype(BF16)
    w_gate = jnp.pad(w_in[:, EVEN_MAIN:], ((0, 0), (0, LANES - 2 * B_HEADS))).astype(BF16)
    z = _matmul(x, w_main, out_dtype=BF16)
    gates = _matmul(x, w_gate, tn=LANES)[:, :2 * B_HEADS]
    y_a = _moba(z, bsz, s)
    y_b = _mlstm(z, gates, gate_bias, conv_w, conv_b, hnorm_g, bsz, s)
    w_o = w_out.astype(BF16)
    return _matmul2(y_a, y_b, w_o[:A_WIDTH], w_o[A_WIDTH:])


def _shift_mix_kernel(x_ref, p_ref, mu_ref, *o_refs, tiles_per_seq):
    x = x_ref[...]
    first = (pl.program_id(0) % tiles_per_seq) == 0
    prev_row = jnp.where(first, 0.0, p_ref[SUBLANES - 1:SUBLANES, :])
    row = lax.broadcasted_iota(jnp.int32, x.shape, 0)
    dx = jnp.where(row == 0, prev_row, pltpu.roll(x, 1, axis=0)) - x
    for j, o_ref in enumerate(o_refs):
        o_ref[...] = (x + dx * mu_ref[j:j + 1, :]).astype(o_ref.dtype)


def _shift_mix(x, mu, s, tm=256):
    t, d = x.shape
    n = mu.shape[0]
    row = pl.BlockSpec((tm, d), lambda i: (i, 0))
    prev = pl.BlockSpec((SUBLANES, d), lambda i: (jnp.maximum(i * (tm // SUBLANES) - 1, 0), 0))
    return pl.pallas_call(
        functools.partial(_shift_mix_kernel, tiles_per_seq=s // tm),
        grid=(t // tm,),
        in_specs=[row, prev, pl.BlockSpec((n, d), lambda i: (0, 0))],
        out_specs=[row] * n,
        out_shape=[jax.ShapeDtypeStruct((t, d), BF16)] * n,
        compiler_params=_params("parallel"),
        name="shift_mix",
    )(x, x, mu)


def _softplus(x):
    return jnp.maximum(x, 0.0) + jnp.log1p(jnp.exp(-jnp.abs(x)))


def _lora_kernel(x_ref, w1_ref, w2_ref, b_ref, o_ref, *, mid, out):
    hmid = jnp.dot(x_ref[...], w1_ref[...], preferred_element_type=F32)
    if mid == "tanh":
        hmid = jnp.tanh(hmid)
    elif mid == "sigmoid":
        hmid = jax.nn.sigmoid(hmid)
    y = jnp.dot(hmid.astype(BF16), w2_ref[...], preferred_element_type=F32)
    if out == "logdecay":
        y = -jnp.exp(-_softplus(-(b_ref[...] + y)) - 0.5)
    elif out == "sigmoid":
        y = jax.nn.sigmoid(b_ref[...] + y)
    o_ref[...] = y.astype(o_ref.dtype)


def _lora(x, w1, w2, bias, mid, out, out_dtype=F32, tm=512):
    t, d = x.shape
    tm = min(tm, t)
    r = w1.shape[1]
    rp = -(-r // LANES) * LANES
    w1p = jnp.pad(w1, ((0, 0), (0, rp - r))).astype(BF16)
    w2p = jnp.pad(w2, ((0, rp - r), (0, 0))).astype(BF16)
    dout = w2.shape[1]
    if bias is None:
        bias = jnp.zeros((dout,), F32)
    return pl.pallas_call(
        functools.partial(_lora_kernel, mid=mid, out=out),
        grid=(t // tm,),
        in_specs=[pl.BlockSpec((tm, d), lambda i: (i, 0)),
                  pl.BlockSpec((d, rp), lambda i: (0, 0)),
                  pl.BlockSpec((rp, dout), lambda i: (0, 0)),
                  pl.BlockSpec((1, dout), lambda i: (0, 0))],
        out_specs=pl.BlockSpec((tm, dout), lambda i: (i, 0)),
        out_shape=jax.ShapeDtypeStruct((t, dout), out_dtype),
        compiler_params=_params("parallel"),
        name="lora_" + out,
    )(x, w1p, w2p, bias.reshape(1, dout))


def _head_sums(tiles, ones_b, split=True):
    c = tiles[0].shape[0]
    x = jnp.concatenate(tiles, axis=0)
    hi = x.astype(BF16)
    tot = jnp.dot(hi, ones_b, preferred_element_type=F32)
    if split:
        lo = (x - hi.astype(F32)).astype(BF16)
        tot = tot + jnp.dot(lo, ones_b, preferred_element_type=F32)
    return [tot[p * c:(p + 1) * c] for p in range(len(tiles))]


def _bdot(a, b):
    return jnp.dot(a.astype(BF16), b.astype(BF16), preferred_element_type=F32)


def _rwkv_chunk_pairs(r, ld, kh, v, kk, bb, g_cum, s0, c):
    pairs = range(len(r))
    lane = lax.broadcasted_iota(jnp.int32, (1, LANES), 1)
    m0 = jnp.where(lane < C_HEAD_DIM, 1.0, 0.0)
    m1 = 1.0 - m0
    n2 = 2 * c
    row = lax.broadcasted_iota(jnp.int32, (n2, n2), 0)
    col = lax.broadcasted_iota(jnp.int32, (n2, n2), 1)
    lower, lower_eq = col < row, col <= row

    def stack(x):
        return jnp.concatenate([x * m0, x * m1], axis=0)

    e_g = [jnp.exp(g_cum[p]) for p in pairs]
    e_ng = [jnp.exp(-g_cum[p]) for p in pairs]
    al2 = [stack(-(kk[p] * jnp.exp(g_cum[p] - ld[p]))) for p in pairs]
    be2 = [stack(bb[p] * e_ng[p]).astype(BF16) for p in pairs]
    kb2 = [stack(kh[p] * e_ng[p]).astype(BF16) for p in pairs]
    rb = [r[p] * e_g[p] for p in pairs]
    v2 = [stack(v[p]) for p in pairs]
    sc = [lax.dot_general(jnp.concatenate([al2[p], stack(rb[p])], axis=0).astype(BF16),
                          jnp.concatenate([be2[p], kb2[p]], axis=0), _NT,
                          preferred_element_type=F32) for p in pairs]
    l_ab = [jnp.where(lower, sc[p][:n2, :n2], 0.0) for p in pairs]
    l_ak = [jnp.where(lower, sc[p][:n2, n2:], 0.0) for p in pairs]
    l_r = [jnp.concatenate([jnp.where(lower_eq, sc[p][n2:, :n2], 0.0),
                            jnp.where(lower_eq, sc[p][n2:, n2:], 0.0)], axis=1).astype(BF16) for p in pairs]

    t_off = l_ab
    pw = [x.astype(BF16) for x in l_ab]
    for _ in range(max(c.bit_length() - 2, 0)):
        pw_f = [jnp.dot(pw[p], pw[p], preferred_element_type=F32) for p in pairs]
        pw = [x.astype(BF16) for x in pw_f]
        t_off = [t_off[p] + pw_f[p] + jnp.dot(t_off[p].astype(BF16), pw[p], preferred_element_type=F32)
                 for p in pairs]

    w0 = [jnp.concatenate([al2[p], _bdot(l_ak[p], v2[p])], axis=1) for p in pairs]
    x = [w0[p] + _bdot(t_off[p], w0[p]) for p in pairs]
    xb = [x[p].astype(BF16) for p in pairs]
    z = [jnp.dot(l_r[p],
                 jnp.concatenate([xb[p], jnp.concatenate([jnp.zeros_like(v2[p]), v2[p]], axis=1).astype(BF16)],
                                 axis=0), preferred_element_type=F32) for p in pairs]
    r_eff = [(rb[p] + z[p][:c, :LANES] + z[p][c:, :LANES]).astype(BF16) for p in pairs]
    y0 = [z[p][:c, LANES:] + z[p][c:, LANES:] for p in pairs]
    mn = [lax.dot_general(xb[p], be2[p], _TN, preferred_element_type=F32) for p in pairs]
    n_add = [mn[p][LANES:] + lax.dot_general(v2[p].astype(BF16), kb2[p], _TN, preferred_element_type=F32)
             for p in pairs]
    s0b = [s0[p].astype(BF16) for p in pairs]
    y = [y0[p] + lax.dot_general(r_eff[p], s0b[p], _NT, preferred_element_type=F32) for p in pairs]
    s_new = [(s0[p] + jnp.dot(s0b[p], mn[p][:LANES].astype(BF16), preferred_element_type=F32) + n_add[p])
             * e_g[p][c - 1:c, :] for p in pairs]
    return y, s_new


def _rwkv_scan_kernel(*refs, c, npair, vres):
    if vres:
        (r_ref, ld_ref, k_ref, a_ref, v_ref, g_ref, vf_ref, vg_ref,
         kkp_ref, kap_ref, rkp_ref, lg_ref, lb_ref, tri_ref, o_ref, s_ref) = refs
    else:
        (r_ref, ld_ref, k_ref, a_ref, v_ref, g_ref,
         kkp_ref, kap_ref, rkp_ref, lg_ref, lb_ref, tri_ref, o_ref, s_ref) = refs

    @pl.when(pl.program_id(2) == 0)
    def _reset():
        s_ref[...] = jnp.zeros_like(s_ref)

    pairs = range(npair)
    sl = [slice(p * LANES, (p + 1) * LANES) for p in pairs]
    take = lambda ref: [ref[:, q].astype(F32) for q in sl]
    shift = C_HEAD_DIM.bit_length() - 1
    ones_b = jnp.where(jnp.right_shift(lax.broadcasted_iota(jnp.int32, (LANES, LANES), 0), shift)
                       == jnp.right_shift(lax.broadcasted_iota(jnp.int32, (LANES, LANES), 1), shift),
                       1.0, 0.0).astype(BF16)
    g_all = jnp.dot(tri_ref[...], ld_ref[...], precision=lax.Precision.HIGHEST, preferred_element_type=F32)

    r, ld, k, a, v, gate = (take(x) for x in (r_ref, ld_ref, k_ref, a_ref, v_ref, g_ref))
    kk_raw = [k[p] * kkp_ref[:, sl[p]] for p in pairs]
    sq = _head_sums([x * x for x in kk_raw], ones_b)
    kk = [kk_raw[p] / jnp.maximum(jnp.sqrt(sq[p]), KK_EPS) for p in pairs]
    bb = [kk[p] * a[p] for p in pairs]
    kh = [k[p] * (1.0 + (a[p] - 1.0) * kap_ref[:, sl[p]]) for p in pairs]
    if vres:
        vf, vg = take(vf_ref), take(vg_ref)
        v = [v[p] + (vf[p] - v[p]) * vg[p] for p in pairs]

    y, s_new = _rwkv_chunk_pairs(r, ld, kh, v, kk, bb, [g_all[:, q] for q in sl],
                                 [s_ref[p] for p in pairs], c)

    inv_n = 1.0 / C_HEAD_DIM
    mean = _head_sums(y, ones_b)
    yc = [y[p] - mean[p] * inv_n for p in pairs]
    var = _head_sums([x * x for x in yc], ones_b)
    bonus = _head_sums([r[p] * kh[p] * rkp_ref[:, sl[p]] for p in pairs], ones_b, split=False)
    for p in pairs:
        yn = yc[p] * lax.rsqrt(var[p] * inv_n + LNX_EPS) * lg_ref[:, sl[p]] + lb_ref[:, sl[p]]
        o_ref[:, sl[p]] = ((yn + bonus[p] * v[p]) * gate[p]).astype(o_ref.dtype)
        s_ref[p] = s_new[p]


RWKV_CHUNK = 64
RWKV_PAIRS = 16


def _rwkv_scan(r, log_w, k, a, v, gate, v_first, vgate, k_k, k_a, r_k, lnx_g, lnx_b, bsz, s):
    t, d = r.shape
    c = min(RWKV_CHUNK, s)
    wid = RWKV_PAIRS * LANES
    nc = s // c
    vres = v_first is not None
    tri = (jnp.arange(c)[None, :] <= jnp.arange(c)[:, None]).astype(F32)
    blk = pl.BlockSpec((c, wid), lambda b, g, i: (b * nc + i, g))
    vec = pl.BlockSpec((1, wid), lambda b, g, i: (0, g))
    rows = [r, log_w, k, a, v, gate] + ([v_first, vgate] if vres else [])
    vecs = [x.reshape(1, d) for x in (k_k, k_a, r_k, lnx_g, lnx_b)]
    return pl.pallas_call(
        functools.partial(_rwkv_scan_kernel, c=c, npair=RWKV_PAIRS, vres=vres),
        grid=(bsz, d // wid, nc),
        in_specs=[blk] * len(rows) + [vec] * len(vecs) + [pl.BlockSpec((c, c), lambda b, g, i: (0, 0))],
        out_specs=blk,
        out_shape=jax.ShapeDtypeStruct((t, d), BF16),
        scratch_shapes=[pltpu.VMEM((RWKV_PAIRS, LANES, LANES), F32)],
        compiler_params=_params("parallel", "parallel", "arbitrary"),
        name="rwkv_scan",
    )(*rows, *vecs, tri)


def _rwkv7_mixer(x, v_first, mu, w_r, w_k, w_v, w_o, w0, w1, w2, a0, a1, a2, g1, g2,
                 k_k, k_a, r_k, lnx_g, lnx_b, vres, bsz, s):
    xr, xw, xk, xv, xa, xg = _shift_mix(x, mu, s)
    r = _matmul(xr, w_r.astype(BF16), out_dtype=BF16)
    k = _matmul(xk, w_k.astype(BF16), out_dtype=BF16)
    v = _matmul(xv, w_v.astype(BF16), out_dtype=BF16)
    log_decay = _lora(xw, w1, w2, w0, "tanh", "logdecay")
    a = _lora(xa, a1, a2, a0, "none", "sigmoid", out_dtype=BF16)
    g = _lora(xg, g1, g2, None, "sigmoid", "none", out_dtype=BF16)
    if vres is None:
        vgate = None
        v_keep = v
    else:
        v0, v1, v2 = vres
        vgate = _lora(xv, v1, v2, v0, "none", "sigmoid", out_dtype=BF16)
        v_keep = v_first
    yo = _rwkv_scan(r, log_decay, k, a, v, g, None if vres is None else v_first, vgate,
                    k_k, k_a, r_k, lnx_g, lnx_b, bsz, s)
    return _matmul(yo, w_o.astype(BF16)), v_keep


def _lane_pick(cond, lanef):
    return jnp.min(jnp.where(cond, lanef, float(LANES)), axis=1, keepdims=True)


def _route_tile(xt, w_ref, b_ref, route_ref, cnt_ref, run_ref, tm):
    @pl.when(pl.program_id(0) == 0)
    def _reset():
        run_ref[...] = jnp.zeros_like(run_ref)

    logits = jnp.dot(xt.astype(BF16), w_ref[...], preferred_element_type=F32) + b_ref[...]
    lane = lax.broadcasted_iota(jnp.int32, (tm, LANES), 1)
    lanef = lane.astype(F32)

    def masked_softmax(keep):
        z = jnp.where(keep, logits, -jnp.inf)
        e = jnp.exp(z - jnp.max(z, axis=1, keepdims=True))
        return e / jnp.sum(e, axis=1, keepdims=True)

    in_groups = lane < N_GROUPS
    pg = masked_softmax(in_groups)
    g_p = jnp.max(pg, axis=1, keepdims=True)
    g_idx = _lane_pick(jnp.where(in_groups, pg, -1.0) == g_p, lanef)
    lo = float(N_GROUPS) + float(EXPERTS_PER_GROUP) * g_idx
    half = 0.5 * float(EXPERTS_PER_GROUP - 1)
    in_group = jnp.abs(lanef - lo - half) < half + 0.5
    pe = jnp.where(in_group, masked_softmax(in_group), -1.0)
    p1 = jnp.max(pe, axis=1, keepdims=True)
    i1 = _lane_pick(pe == p1, lanef)
    pe2 = jnp.where(lanef == i1, -1.0, pe)
    p2 = jnp.max(pe2, axis=1, keepdims=True)
    i2 = _lane_pick(pe2 == p2, lanef)
    den = p1 + p2
    gate = (g_p * (p1 / den), g_p * (p2 / den))
    expert = (i1 - float(N_GROUPS), i2 - float(N_GROUPS))

    earlier = jnp.where(lax.broadcasted_iota(jnp.int32, (tm, tm), 1)
                        < lax.broadcasted_iota(jnp.int32, (tm, tm), 0), 1.0, 0.0).astype(BF16)
    run = run_ref[...]
    rank = []
    for e in expert:
        onehot = jnp.where(lanef == e, 1.0, 0.0)
        before = jnp.dot(earlier, onehot.astype(BF16), preferred_element_type=F32) + run
        rank.append(jnp.sum(onehot * before, axis=1, keepdims=True))
        run = run + jnp.sum(onehot, axis=0, keepdims=True)
    run_ref[...] = run
    cnt_ref[...] = run
    cols = (expert[0], expert[1], rank[0], rank[1], gate[0], gate[1])
    route = jnp.zeros((tm, LANES), F32)
    for c, val in enumerate(cols):
        route = jnp.where(lane == c, val, route)
    route_ref[...] = route


SLAB = 2048 // LANES
DMA_UNROLL = 4


def _slab(ref, r):
    return ref.at[pl.ds(pl.multiple_of(r * SLAB, SLAB), SLAB)]


def _slab_rows(ref, rows):
    return jnp.concatenate([ref[pl.ds(c, rows, stride=SLAB), :] for c in range(SLAB)], axis=1)


def _store_slab_rows(ref, val, rows):
    for c in range(SLAB):
        ref[pl.ds(c, rows, stride=SLAB), :] = val[:, c * LANES:(c + 1) * LANES]


def _expert_kernel(be_ref, nu_ref, idx_ref, nxt_ref, x_hbm, wg_ref, wu_ref, wd_ref, o_ref,
                   xbuf, sems, wg_b, wu_b, wd_b, *, rows):
    i = pl.program_id(0)
    n_used = nu_ref[0]
    slot = i % 2

    def token_copy(tok, r, dst_slot):
        return pltpu.make_async_copy(_slab(x_hbm, tok), _slab(xbuf.at[dst_slot], r), sems.at[dst_slot])

    def gather(ids_ref, dst_slot):
        def start(r, carry):
            for rr in (r, r + rows // 2):
                token_copy(ids_ref[0, rr], rr, dst_slot).start()
            return carry
        lax.fori_loop(0, rows // 2, start, 0, unroll=DMA_UNROLL)

    @pl.when(i == 0)
    def _():
        gather(idx_ref, 0)

    @pl.when(i + 1 < n_used)
    def _():
        gather(nxt_ref, 1 - slot)

    @pl.when(jnp.logical_or(i == 0, be_ref[i] != be_ref[jnp.maximum(i - 1, 0)]))
    def _():
        wg_b[...] = wg_ref[...].astype(BF16)
        wu_b[...] = wu_ref[...].astype(BF16)
        wd_b[...] = wd_ref[...].astype(BF16)

    @pl.when(i < n_used)
    def _():
        for r in range(rows):
            token_copy(0, r, slot).wait()
        xb = _slab_rows(xbuf.at[slot], rows).astype(BF16)
        hg = jnp.dot(xb, wg_b[...], preferred_element_type=F32)
        hu = jnp.dot(xb, wu_b[...], preferred_element_type=F32)
        hid = (hg * jax.nn.sigmoid(hg) * hu).astype(BF16)
        _store_slab_rows(o_ref, jnp.dot(hid, wd_b[...], preferred_element_type=F32), rows)

    @pl.when(i >= n_used)
    def _():
        o_ref[...] = jnp.zeros_like(o_ref)


def _expert_mlp(x_slabs, src_row, block_e, n_used, w_gate, w_up, w_down, layer, rows=MOE_ROWS):
    d, ff = w_gate.shape[2], w_gate.shape[3]
    nb = src_row.shape[0] // rows
    ids = src_row.reshape(nb, 1, rows)
    grid_spec = pltpu.PrefetchScalarGridSpec(
        num_scalar_prefetch=2,
        grid=(nb,),
        in_specs=[pl.BlockSpec((None, 1, rows), lambda i, be, nu: (i, 0, 0), memory_space=pltpu.SMEM),
                  pl.BlockSpec((None, 1, rows), lambda i, be, nu: (jnp.minimum(i + 1, nb - 1), 0, 0),
                               memory_space=pltpu.SMEM),
                  pl.BlockSpec(memory_space=pl.ANY),
                  pl.BlockSpec((None, None, d, ff), lambda i, be, nu: (layer, be[i], 0, 0)),
                  pl.BlockSpec((None, None, d, ff), lambda i, be, nu: (layer, be[i], 0, 0)),
                  pl.BlockSpec((None, None, ff, d), lambda i, be, nu: (layer, be[i], 0, 0))],
        out_specs=pl.BlockSpec((rows * SLAB, LANES), lambda i, be, nu: (i, 0)),
        scratch_shapes=[pltpu.VMEM((2, rows * SLAB, LANES), F32), pltpu.SemaphoreType.DMA((2,)),
                        pltpu.VMEM((d, ff), BF16), pltpu.VMEM((d, ff), BF16), pltpu.VMEM((ff, d), BF16)],
    )
    return pl.pallas_call(
        functools.partial(_expert_kernel, rows=rows),
        grid_spec=grid_spec,
        out_shape=jax.ShapeDtypeStruct((nb * rows * SLAB, LANES), F32),
        compiler_params=_params("arbitrary"),
        name="expert_mlp",
    )(block_e, n_used, ids, ids, x_slabs, w_gate, w_up, w_down)


def _combine_ln_kernel(idx_ref, nxt_ref, y_ref, x_ref, gt_ref, g_ref, b_ref, o_ref, buf, sems, *, rows):
    i = pl.program_id(0)
    slot = i % 2

    def row_copy(src, r, kslot, dst_slot):
        return pltpu.make_async_copy(_slab(y_ref, src), _slab(buf.at[dst_slot, kslot], r), sems.at[dst_slot])

    def gather(ids_ref, dst_slot):
        def start(r, carry):
            for kslot in range(MOE_TOPK):
                for rr in (r, r + rows // 2):
                    row_copy(ids_ref[0, MOE_TOPK * rr + kslot], rr, kslot, dst_slot).start()
            return carry
        lax.fori_loop(0, rows // 2, start, 0, unroll=DMA_UNROLL)

    @pl.when(i == 0)
    def _():
        gather(idx_ref, 0)

    @pl.when(i + 1 < pl.num_programs(0))
    def _():
        gather(nxt_ref, 1 - slot)

    for r in range(rows):
        for kslot in range(MOE_TOPK):
            row_copy(0, r, kslot, slot).wait()
    gt = gt_ref[...]
    ffn = gt[:, 0:1] * _slab_rows(buf.at[slot, 0], rows)
    for kslot in range(1, MOE_TOPK):
        ffn = ffn + gt[:, kslot:kslot + 1] * _slab_rows(buf.at[slot, kslot], rows)
    o_ref[...] = _ln_rows(DN_ALPHA * x_ref[...] + ffn, g_ref[...], b_ref[...])


def _combine_ln(y_sorted, dest, gates, x, g, b, rows=256):
    t, d = x.shape
    nb = t // rows
    row = pl.BlockSpec((rows, d), lambda i: (i, 0))
    vec = pl.BlockSpec((1, d), lambda i: (0, 0))
    ids = dest.reshape(nb, 1, rows * MOE_TOPK)
    return pl.pallas_call(
        functools.partial(_combine_ln_kernel, rows=rows),
        grid=(nb,),
        in_specs=[pl.BlockSpec((None, 1, rows * MOE_TOPK), lambda i: (i, 0, 0), memory_space=pltpu.SMEM),
                  pl.BlockSpec((None, 1, rows * MOE_TOPK), lambda i: (jnp.minimum(i + 1, nb - 1), 0, 0),
                               memory_space=pltpu.SMEM),
                  pl.BlockSpec(memory_space=pl.ANY), row,
                  pl.BlockSpec((rows, MOE_TOPK), lambda i: (i, 0)), vec, vec],
        out_specs=row,
        out_shape=jax.ShapeDtypeStruct((t, d), F32),
        scratch_shapes=[pltpu.VMEM((2, MOE_TOPK, rows * SLAB, LANES), F32), pltpu.SemaphoreType.DMA((2,))],
        compiler_params=_params("arbitrary"),
        name="combine_ln",
    )(ids, ids, y_sorted, x, gates, g.reshape(1, d), b.reshape(1, d))


def _router_params(w_grp, b_grp, w_exp_r, b_exp_r):
    n_r = N_GROUPS + N_EXPERTS
    w_router = jnp.pad(jnp.concatenate([w_grp, w_exp_r], axis=1), ((0, 0), (0, LANES - n_r))).astype(BF16)
    b_router = jnp.pad(jnp.concatenate([b_grp, b_exp_r]), (0, LANES - n_r)).reshape(1, LANES)
    return w_router, b_router


def _moe_layer(x, x_slabs, route, cnt, w_gate, w_up, w_down, ln_g, ln_b, layer):
    t, d = x.shape
    flat_e = route[:, 0:MOE_TOPK].astype(jnp.int32).reshape(-1)
    rank = route[:, MOE_TOPK:2 * MOE_TOPK].astype(jnp.int32).reshape(-1)
    gates = route[:, 2 * MOE_TOPK:3 * MOE_TOPK]
    counts = cnt[0, :N_EXPERTS].astype(jnp.int32)
    m = t * MOE_TOPK
    padded = (counts + MOE_ROWS - 1) // MOE_ROWS * MOE_ROWS
    pad_end = jnp.cumsum(padded)
    dest = (pad_end - padded)[flat_e] + rank
    n_blocks = (m + N_EXPERTS * (MOE_ROWS - 1) + MOE_ROWS - 1) // MOE_ROWS
    src_row = jnp.zeros((n_blocks * MOE_ROWS,), jnp.int32).at[dest].set(
        (jnp.arange(m) // MOE_TOPK).astype(jnp.int32))
    starts = jnp.arange(n_blocks, dtype=jnp.int32) * MOE_ROWS
    block_e = jnp.minimum(jnp.sum((pad_end[None, :] <= starts[:, None]).astype(jnp.int32), axis=1),
                          N_EXPERTS - 1).astype(jnp.int32)
    n_used = (pad_end[-1] // MOE_ROWS).astype(jnp.int32).reshape(1)

    ys = _expert_mlp(x_slabs, src_row, block_e, n_used, w_gate, w_up, w_down, layer)
    return _combine_ln(ys, dest.astype(jnp.int32), gates, x, ln_g, ln_b)


def kernel(x, ev_w_in, ev_gate_bias, ev_conv_w, ev_conv_b, ev_hnorm_g, ev_w_out, od_mu, od_w_r, od_w_k, od_w_v, od_w_o, od_w0, od_w1, od_w2, od_a0, od_a1, od_a2, od_g1, od_g2, od_k_k, od_k_a, od_r_k, od_lnx_g, od_lnx_b, od_v0, od_v1, od_v2, ln_mix_g, ln_mix_b, ln_ffn_g, ln_ffn_b, moe_w_grp, moe_b_grp, moe_w_exp_r, moe_b_exp_r, moe_w_gate, moe_w_up, moe_w_down):
    bsz, s, d = x.shape
    depth = ln_mix_g.shape[0]
    xf = x.reshape(bsz * s, d)
    v_first = None
    for layer in range(depth):
        if layer % 2 == 0:
            e = layer // 2
            mix = _even_mixer(xf, ev_w_in[e], ev_gate_bias[e], ev_conv_w[e], ev_conv_b[e],
                              ev_hnorm_g[e], ev_w_out[e], bsz, s)
        else:
            o = layer // 2
            vres = None if o == 0 else (od_v0[o - 1], od_v1[o - 1], od_v2[o - 1])
            mix, v_first = _rwkv7_mixer(xf, v_first, od_mu[o], od_w_r[o], od_w_k[o], od_w_v[o],
                                        od_w_o[o], od_w0[o], od_w1[o], od_w2[o], od_a0[o],
                                        od_a1[o], od_a2[o], od_g1[o], od_g2[o], od_k_k[o],
                                        od_k_a[o], od_r_k[o], od_lnx_g[o], od_lnx_b[o], vres, bsz, s)
        xf, x_slabs, route, cnt = _resid_ln(xf, mix, ln_mix_g[layer], ln_mix_b[layer],
                                            *_router_params(moe_w_grp[layer], moe_b_grp[layer],
                                                            moe_w_exp_r[layer], moe_b_exp_r[layer]))
        xf = _moe_layer(xf, x_slabs, route, cnt, moe_w_gate, moe_w_up, moe_w_down,
                        ln_ffn_g[layer], ln_ffn_b[layer], layer)
    return xf.reshape(bsz, s, d)
```

```python
import functools
import math

import jax
import jax.numpy as jnp
from jax import lax
from jax.experimental import pallas as pl
from jax.experimental.pallas import tpu as pltpu

F32 = jnp.float32
BF16 = jnp.bfloat16

DEPTH = 4
DN_ALPHA = (2 * DEPTH) ** 0.25
LN_EPS = 1e-5

A_HEADS = 8
A_HEAD_DIM = 128
A_WIDTH = A_HEADS * A_HEAD_DIM
MOBA_BLOCK = 256
MOBA_TOPK = 3
MOBA_GROUP = 4
MOBA_HEADS = 4
ROPE_THETA = 500000.0
ROPE_DIM = A_HEAD_DIM // 4
NEG_INF = -1e30

B_HEADS = 4
B_HEAD_DIM = 256
B_WIDTH = B_HEADS * B_HEAD_DIM
MLSTM_CHUNK = 64
B_CONV = 4
HNORM_EPS = 1e-6
EVEN_MAIN = 3 * A_WIDTH + 4 * B_WIDTH

C_HEAD_DIM = 64
LNX_EPS = 64e-5
KK_EPS = 1e-12

N_GROUPS = 4
EXPERTS_PER_GROUP = 8
N_EXPERTS = N_GROUPS * EXPERTS_PER_GROUP
MOE_TOPK = 2
MOE_ROWS = 256

LANES = 128
SUBLANES = 8
VMEM_LIMIT = 56 * 1024 * 1024

_NT = (((1,), (1,)), ((), ()))
_TN = (((0,), (0,)), ((), ()))


def _params(*sem):
    return pltpu.CompilerParams(dimension_semantics=sem, vmem_limit_bytes=VMEM_LIMIT)


def _mm_kernel(x_ref, w_ref, o_ref):
    o_ref[...] = jnp.dot(x_ref[...].astype(BF16), w_ref[...],
                         preferred_element_type=F32).astype(o_ref.dtype)


LHS_TILE_BYTES = 8 * 1024 * 1024


def _row_tile(m, row_bytes):
    tm = 1 << ((LHS_TILE_BYTES // row_bytes).bit_length() - 1)
    return min(tm, m)


def _matmul(x, w, out_dtype=F32, tn=512):
    m, k = x.shape
    n = w.shape[1]
    tm, tn = _row_tile(m, k * x.dtype.itemsize), min(tn, n)
    return pl.pallas_call(
        _mm_kernel,
        grid=(m // tm, n // tn),
        in_specs=[pl.BlockSpec((tm, k), lambda i, j: (i, 0)),
                  pl.BlockSpec((k, tn), lambda i, j: (0, j))],
        out_specs=pl.BlockSpec((tm, tn), lambda i, j: (i, j)),
        out_shape=jax.ShapeDtypeStruct((m, n), out_dtype),
        compiler_params=_params("parallel", "arbitrary"),
        name="matmul",
    )(x, w)


def _mm2_kernel(xa_ref, xb_ref, wa_ref, wb_ref, o_ref):
    acc = jnp.dot(xa_ref[...], wa_ref[...], preferred_element_type=F32)
    acc = acc + jnp.dot(xb_ref[...], wb_ref[...], preferred_element_type=F32)
    o_ref[...] = acc


def _matmul2(xa, xb, wa, wb, tn=512):
    m, ka = xa.shape
    kb = xb.shape[1]
    n = wa.shape[1]
    tm, tn = _row_tile(m, ka * xa.dtype.itemsize + kb * xb.dtype.itemsize), min(tn, n)
    return pl.pallas_call(
        _mm2_kernel,
        grid=(m // tm, n // tn),
        in_specs=[pl.BlockSpec((tm, ka), lambda i, j: (i, 0)),
                  pl.BlockSpec((tm, kb), lambda i, j: (i, 0)),
                  pl.BlockSpec((ka, tn), lambda i, j: (0, j)),
                  pl.BlockSpec((kb, tn), lambda i, j: (0, j))],
        out_specs=pl.BlockSpec((tm, tn), lambda i, j: (i, j)),
        out_shape=jax.ShapeDtypeStruct((m, n), F32),
        compiler_params=_params("parallel", "arbitrary"),
        name="matmul2",
    )(xa, xb, wa, wb)


def _ln_rows(h, g, b):
    mu = jnp.mean(h, axis=-1, keepdims=True)
    hc = h - mu
    var = jnp.mean(hc * hc, axis=-1, keepdims=True)
    return hc * lax.rsqrt(var + LN_EPS) * g + b


def _resid_ln_kernel(x_ref, m_ref, g_ref, b_ref, w_ref, br_ref, o_ref, s_ref, route_ref, cnt_ref, run_ref,
                     *, tm):
    y = _ln_rows(DN_ALPHA * x_ref[...] + m_ref[...], g_ref[...], b_ref[...])
    o_ref[...] = y.astype(o_ref.dtype)
    _store_slab_rows(s_ref, y, tm)
    _route_tile(y, w_ref, br_ref, route_ref, cnt_ref, run_ref, tm)


def _resid_ln(x, mix, g, b, w_router, b_router, tm=256):
    t, d = x.shape
    tm = min(tm, t)
    row = pl.BlockSpec((tm, d), lambda i: (i, 0))
    vec = pl.BlockSpec((1, d), lambda i: (0, 0))
    one = pl.BlockSpec((1, LANES), lambda i: (0, 0))
    return pl.pallas_call(
        functools.partial(_resid_ln_kernel, tm=tm),
        grid=(t // tm,),
        in_specs=[row, row, vec, vec, pl.BlockSpec((d, LANES), lambda i: (0, 0)), one],
        out_specs=[row, pl.BlockSpec((tm * SLAB, LANES), lambda i: (i, 0)),
                   pl.BlockSpec((tm, LANES), lambda i: (i, 0)), one],
        out_shape=[jax.ShapeDtypeStruct((t, d), F32), jax.ShapeDtypeStruct((t * SLAB, LANES), F32),
                   jax.ShapeDtypeStruct((t, LANES), F32), jax.ShapeDtypeStruct((1, LANES), F32)],
        scratch_shapes=[pltpu.VMEM((1, LANES), F32)],
        compiler_params=_params("arbitrary"),
        name="resid_ln_router",
    )(x, mix, g.reshape(1, d), b.reshape(1, d), w_router, b_router)


def _rope_tables(s):
    half = ROPE_DIM // 2
    inv = jnp.power(jnp.float32(ROPE_THETA), -jnp.arange(half, dtype=F32) / half)
    ang = jnp.arange(s).astype(F32)[:, None] * inv[None, :]
    cos, sin = jnp.cos(ang), jnp.sin(ang)
    zero = jnp.zeros((s, A_HEAD_DIM - ROPE_DIM), F32)
    zh = jnp.zeros((s, half), F32)
    c = jnp.concatenate([cos, cos, zero + 1.0], axis=-1)
    s_up = jnp.concatenate([zh, sin, zero], axis=-1)
    s_dn = jnp.concatenate([-sin, zh, zero], axis=-1)
    return c, s_up, s_dn


def _rope(x, c, s_up, s_dn):
    half = ROPE_DIM // 2
    return (x * c + pltpu.roll(x, half, axis=1) * s_up
            + pltpu.roll(x, A_HEAD_DIM - half, axis=1) * s_dn)


def _moba_kernel(q_ref, k_ref, v_ref, c_ref, su_ref, sd_ref, o_ref, krot_ref, vb_ref, kmean_ref, *, nblk):
    qi = pl.program_id(2)
    blk = MOBA_BLOCK
    dh = A_HEAD_DIM
    scale = dh ** -0.5
    heads = range(MOBA_HEADS)
    hs = [slice(h * dh, (h + 1) * dh) for h in heads]

    @pl.when(qi == 0)
    def _prepare_keys():
        def body(j, carry):
            rows = pl.ds(pl.multiple_of(j * blk, blk), blk)
            block_tag = jnp.where(lax.broadcasted_iota(jnp.int32, (blk, LANES), 1) == j, 1.0, 0.0)
            for h in heads:
                kr = _rope(k_ref[rows, hs[h]].astype(F32), c_ref[rows, :], su_ref[rows, :], sd_ref[rows, :])
                krot_ref[h, rows, :] = jnp.concatenate([kr, block_tag], axis=1).astype(BF16)
                vb_ref[h, rows, :] = v_ref[rows, hs[h]].astype(BF16)
                kmean_ref[h, pl.ds(j, 1), :] = jnp.mean(kr, axis=0, keepdims=True)
            return carry
        lax.fori_loop(0, nblk, body, 0)

    qrows = pl.ds(pl.multiple_of(qi * blk, blk), blk)
    qb = [_rope(q_ref[:, hs[h]].astype(F32), c_ref[qrows, :], su_ref[qrows, :], sd_ref[qrows, :]).astype(BF16)
          for h in heads]

    brow = lax.broadcasted_iota(jnp.int32, (nblk, blk), 0)
    g = [jnp.where(brow < qi, lax.dot_general(kmean_ref[h].astype(BF16), qb[h], _NT,
                                              preferred_element_type=F32), -jnp.inf) for h in heads]
    sel_t = [jnp.zeros((nblk, blk), F32) for _ in heads]
    for j in range(nblk):
        earlier = jnp.where(brow < j, 1.0, 0.0)
        for h in heads:
            gj = g[h][j:j + 1, :]
            beats = jnp.where(g[h] > gj, 1.0, jnp.where(g[h] == gj, earlier, 0.0))
            cnt = jnp.sum(beats, axis=0, keepdims=True)
            sel_t[h] = jnp.where(brow == j, jnp.where(cnt < float(min(MOBA_TOPK, nblk)), 1.0, 0.0), sel_t[h])
    eye = jnp.where(lax.broadcasted_iota(jnp.int32, (blk, blk), 0)
                    == lax.broadcasted_iota(jnp.int32, (blk, blk), 1), 1.0, 0.0).astype(BF16)
    pad = jnp.zeros((LANES - nblk, blk), F32)
    sel = [lax.dot_general(eye, jnp.concatenate([jnp.where(brow < qi, sel_t[h], 0.0), pad], axis=0).astype(BF16),
                           _NT, preferred_element_type=F32) for h in heads]
    lane = lax.broadcasted_iota(jnp.int32, (blk, LANES), 1)
    off = jnp.where(lane < nblk, NEG_INF / scale, 0.0)
    q_aug = [jnp.concatenate([qb[h], jnp.where(sel[h] > 0.5, 0.0, off).astype(BF16)], axis=1) for h in heads]

    causal = (lax.broadcasted_iota(jnp.int32, (blk, blk), 1)
              <= lax.broadcasted_iota(jnp.int32, (blk, blk), 0))
    s = [jnp.where(causal, lax.dot_general(qb[h], krot_ref[h, qrows, :dh], _NT,
                                           preferred_element_type=F32) * scale, NEG_INF) for h in heads]
    m0 = [jnp.max(s[h], axis=1, keepdims=True) for h in heads]
    p = [jnp.exp(s[h] - m0[h]) for h in heads]
    l0 = [jnp.sum(p[h], axis=1, keepdims=True) for h in heads]
    acc0 = [jnp.dot(p[h].astype(BF16), vb_ref[h, qrows, :], preferred_element_type=F32) for h in heads]

    span = MOBA_GROUP * blk

    def past_group(jg, carry):
        m, l, acc = carry
        rows = pl.ds(pl.multiple_of(jg * span, span), span)
        sj = [lax.dot_general(q_aug[h], krot_ref[h, rows, :], _NT, preferred_element_type=F32) * scale
              for h in heads]
        m_new = [jnp.maximum(m[h], jnp.max(sj[h], axis=1, keepdims=True)) for h in heads]
        a = [jnp.exp(m[h] - m_new[h]) for h in heads]
        pj = [jnp.exp(sj[h] - m_new[h]) for h in heads]
        l = [a[h] * l[h] + jnp.sum(pj[h], axis=1, keepdims=True) for h in heads]
        acc = [a[h] * acc[h] + jnp.dot(pj[h].astype(BF16), vb_ref[h, rows, :], preferred_element_type=F32)
               for h in heads]
        return m_new, l, acc

    n_groups = (qi + MOBA_GROUP - 1) // MOBA_GROUP
    _, l, acc = lax.fori_loop(0, n_groups, past_group, (m0, l0, acc0))
    for h in heads:
        o_ref[:, hs[h]] = (acc[h] / l[h]).astype(o_ref.dtype)


def _moba(z, bsz, s):
    nblk = s // MOBA_BLOCK
    c, s_up, s_dn = _rope_tables(s)
    wid = MOBA_HEADS * A_HEAD_DIM
    ncol = A_WIDTH // wid
    tab = pl.BlockSpec((s, A_HEAD_DIM), lambda b, h, qi: (0, 0))
    return pl.pallas_call(
        functools.partial(_moba_kernel, nblk=nblk),
        grid=(bsz, ncol, nblk),
        in_specs=[pl.BlockSpec((MOBA_BLOCK, wid), lambda b, h, qi: (b * nblk + qi, h)),
                  pl.BlockSpec((s, wid), lambda b, h, qi: (b, ncol + h)),
                  pl.BlockSpec((s, wid), lambda b, h, qi: (b, 2 * ncol + h)),
                  tab, tab, tab],
        out_specs=pl.BlockSpec((MOBA_BLOCK, wid), lambda b, h, qi: (b * nblk + qi, h)),
        out_shape=jax.ShapeDtypeStruct((bsz * s, A_WIDTH), BF16),
        scratch_shapes=[pltpu.VMEM((MOBA_HEADS, s, A_HEAD_DIM + LANES), BF16),
                        pltpu.VMEM((MOBA_HEADS, s, A_HEAD_DIM), BF16),
                        pltpu.VMEM((MOBA_HEADS, nblk, A_HEAD_DIM), F32)],
        compiler_params=_params("parallel", "parallel", "arbitrary"),
        name="moba",
    )(z, z, z, c, s_up, s_dn)


def _log_sigmoid(x):
    return jnp.minimum(x, 0.0) - jnp.log1p(jnp.exp(-jnp.abs(x)))


def _shift_rows(x, prev8, d):
    if d == 0:
        return x
    rolled = pltpu.roll(x, d, axis=0)
    top = jnp.where(lax.broadcasted_iota(jnp.int32, prev8.shape, 0) < d,
                    pltpu.roll(prev8, d, axis=0), rolled[:SUBLANES])
    return jnp.concatenate([top, rolled[SUBLANES:]], axis=0)


def _mlstm_kernel(q_ref, k_ref, v_ref, o_ref, gc_ref, gr_ref, bc_ref, br_ref,
                  cwq_ref, cwk_ref, cbq_ref, cbk_ref, hg_ref, y_ref,
                  c_ref, n_ref, m_ref, pq_ref, pk_ref, *, tt):
    L = MLSTM_CHUNK
    dk = B_HEAD_DIM
    heads = range(B_HEADS)
    hs = [slice(h * dk, (h + 1) * dk) for h in heads]

    @pl.when(pl.program_id(1) == 0)
    def _reset():
        c_ref[...] = jnp.zeros_like(c_ref)
        n_ref[...] = jnp.zeros_like(n_ref)
        m_ref[...] = jnp.zeros_like(m_ref)
        pq_ref[...] = jnp.zeros_like(pq_ref)
        pk_ref[...] = jnp.zeros_like(pk_ref)

    def conv_silu(x_ref, p_ref, w_ref, b_ref):
        x = x_ref[...].astype(F32)
        prev8 = p_ref[...]
        out = b_ref[...] + x * w_ref[B_CONV - 1:B_CONV, :]
        for d in range(1, B_CONV):
            out = out + _shift_rows(x, prev8, d) * w_ref[B_CONV - 1 - d:B_CONV - d, :]
        p_ref[...] = x[tt - SUBLANES:, :]
        return out * jax.nn.sigmoid(out)

    q_all = conv_silu(q_ref, pq_ref, cwq_ref, cbq_ref)
    k_all = conv_silu(k_ref, pk_ref, cwk_ref, cbk_ref) * (dk ** -0.5)

    gcol = gc_ref[...] + bc_ref[...]
    grow = gr_ref[...] + br_ref[...]
    i_col_all = [gcol[:, h:h + 1] for h in heads]
    f_col_all = [_log_sigmoid(gcol[:, B_HEADS + h:B_HEADS + h + 1]) for h in heads]
    i_row_all = [grow[h:h + 1, :] for h in heads]
    f_row_all = [_log_sigmoid(grow[B_HEADS + h:B_HEADS + h + 1, :]) for h in heads]

    rr = lax.broadcasted_iota(jnp.int32, (L, L), 0)
    cc = lax.broadcasted_iota(jnp.int32, (L, L), 1)
    tri = cc <= rr

    for c in range(tt // L):
        lo, hi = c * L, (c + 1) * L
        qc = [q_all[lo:hi, hs[h]] for h in heads]
        kc = [k_all[lo:hi, hs[h]] for h in heads]
        qcb = [x.astype(BF16) for x in qc]
        kcb = [x.astype(BF16) for x in kc]
        vcb = [v_ref[lo:hi, hs[h]].astype(BF16) for h in heads]
        i_col = [x[lo:hi] for x in i_col_all]
        f_col = [x[lo:hi] for x in f_col_all]
        i_row = [x[:, lo:hi] for x in i_row_all]
        f_row = [x[:, lo:hi] for x in f_row_all]
        m_prev = [m_ref[h] for h in heads]
        cst = [c_ref[h] for h in heads]
        nst = [n_ref[h] for h in heads]

        b_col = [jnp.sum(jnp.where(tri, f_row[h], 0.0), axis=1, keepdims=True) for h in heads]
        b_row = [jnp.sum(jnp.where(rr <= cc, f_col[h], 0.0), axis=0, keepdims=True) for h in heads]
        log_inter = [b_col[h] + m_prev[h] for h in heads]
        log_intra = [jnp.where(tri, b_col[h] - b_row[h] + i_row[h], -jnp.inf) for h in heads]
        m_t = [jnp.maximum(log_inter[h], jnp.max(log_intra[h], axis=1, keepdims=True)) for h in heads]
        w_inter = [jnp.exp(log_inter[h] - m_t[h]) for h in heads]
        qk = [lax.dot_general(qcb[h], kcb[h], _NT, preferred_element_type=F32)
              * jnp.exp(log_intra[h] - m_t[h]) for h in heads]
        q_c = [jnp.dot(qcb[h], cst[h].astype(BF16), preferred_element_type=F32) for h in heads]
        qk_v = [jnp.dot(qk[h].astype(BF16), vcb[h], preferred_element_type=F32) for h in heads]
        den = [w_inter[h] * jnp.sum(qc[h] * nst[h], axis=1, keepdims=True)
               + jnp.sum(qk[h], axis=1, keepdims=True) for h in heads]
        hc = [(w_inter[h] * q_c[h] + qk_v[h]) / jnp.maximum(jnp.abs(den[h]), jnp.exp(-m_t[h])) for h in heads]

        b_last = [jnp.sum(f_col[h], axis=0, keepdims=True) for h in heads]
        log_s = [b_last[h] - b_col[h] + i_col[h] for h in heads]
        m_new = [jnp.maximum(b_last[h] + m_prev[h], jnp.max(log_s[h], axis=0, keepdims=True)) for h in heads]
        decay = [jnp.exp(b_last[h] + m_prev[h] - m_new[h]) for h in heads]
        kw = [kc[h] * jnp.exp(log_s[h] - m_new[h]) for h in heads]
        kv = [lax.dot_general(kw[h].astype(BF16), vcb[h], _TN, preferred_element_type=F32) for h in heads]
        for h in heads:
            c_ref[h] = decay[h] * cst[h] + kv[h]
            n_ref[h] = decay[h] * nst[h] + jnp.sum(kw[h], axis=0, keepdims=True)
            m_ref[h] = m_new[h]
            hn = hc[h] * lax.rsqrt(jnp.mean(hc[h] * hc[h], axis=1, keepdims=True) + HNORM_EPS)
            y_ref[lo:hi, hs[h]] = (hn * hg_ref[:, hs[h]]
                                   * jax.nn.sigmoid(o_ref[lo:hi, hs[h]].astype(F32))).astype(y_ref.dtype)


def _mlstm(z, gates, gate_bias, conv_w, conv_b, hnorm_g, bsz, s, tt=256):
    nt = s // tt
    cb = 3 * A_WIDTH // B_WIDTH
    gates_t = gates.T
    bias = gate_bias.reshape(1, 2 * B_HEADS)

    def col(off):
        return pl.BlockSpec((tt, B_WIDTH), lambda b, t: (b * nt + t, cb + off))

    def par(rows, off):
        return pl.BlockSpec((rows, B_WIDTH), lambda b, t: (0, off))

    return pl.pallas_call(
        functools.partial(_mlstm_kernel, tt=tt),
        grid=(bsz, nt),
        in_specs=[col(0), col(1), col(2), col(3),
                  pl.BlockSpec((tt, 2 * B_HEADS), lambda b, t: (b * nt + t, 0)),
                  pl.BlockSpec((2 * B_HEADS, tt), lambda b, t: (0, b * nt + t)),
                  pl.BlockSpec((1, 2 * B_HEADS), lambda b, t: (0, 0)),
                  pl.BlockSpec((2 * B_HEADS, 1), lambda b, t: (0, 0)),
                  par(B_CONV, 0), par(B_CONV, 1), par(1, 0), par(1, 1), par(1, 0)],
        out_specs=pl.BlockSpec((tt, B_WIDTH), lambda b, t: (b * nt + t, 0)),
        out_shape=jax.ShapeDtypeStruct((bsz * s, B_WIDTH), BF16),
        scratch_shapes=[pltpu.VMEM((B_HEADS, B_HEAD_DIM, B_HEAD_DIM), F32),
                        pltpu.VMEM((B_HEADS, 1, B_HEAD_DIM), F32), pltpu.VMEM((B_HEADS, 1, 1), F32),
                        pltpu.VMEM((SUBLANES, B_WIDTH), F32), pltpu.VMEM((SUBLANES, B_WIDTH), F32)],
        compiler_params=_params("parallel", "arbitrary"),
        name="mlstm",
    )(z, z, z, z, gates, gates_t, bias, bias.T, conv_w, conv_w,
      conv_b.reshape(1, -1), conv_b.reshape(1, -1), hnorm_g.reshape(1, -1))


def _even_mixer(x, w_in, gate_bias, conv_w, conv_b, hnorm_g, w_out, bsz, s):
    d = x.shape[1]
    w_main = w_in[:, :EVEN_MAIN].astype(BF16)
    w_gate = jnp.pad(w_in[:, EVEN_MAIN:], ((0, 0), (0, LANES - 2 * B_HEADS))).astype(BF16)
    z = _matmul(x, w_main, out_dtype=BF16)
    gates = _matmul(x, w_gate, tn=LANES)[:, :2 * B_HEADS]
    y_a = _moba(z, bsz, s)
    y_b = _mlstm(z, gates, gate_bias, conv_w, conv_b, hnorm_g, bsz, s)
    w_o = w_out.astype(BF16)
    return _matmul2(y_a, y_b, w_o[:A_WIDTH], w_o[A_WIDTH:])


def _shift_mix_kernel(x_ref, p_ref, mu_ref, *o_refs, tiles_per_seq):
    x = x_ref[...]
    first = (pl.program_id(0) % tiles_per_seq) == 0
    prev_row = jnp.where(first, 0.0, p_ref[SUBLANES - 1:SUBLANES, :])
    row = lax.broadcasted_iota(jnp.int32, x.shape, 0)
    dx = jnp.where(row == 0, prev_row, pltpu.roll(x, 1, axis=0)) - x
    for j, o_ref in enumerate(o_refs):
        o_ref[...] = (x + dx * mu_ref[j:j + 1, :]).astype(o_ref.dtype)


def _shift_mix(x, mu, s, tm=256):
    t, d = x.shape
    n = mu.shape[0]
    row = pl.BlockSpec((tm, d), lambda i: (i, 0))
    prev = pl.BlockSpec((SUBLANES, d), lambda i: (jnp.maximum(i * (tm // SUBLANES) - 1, 0), 0))
    return pl.pallas_call(
        functools.partial(_shift_mix_kernel, tiles_per_seq=s // tm),
        grid=(t // tm,),
        in_specs=[row, prev, pl.BlockSpec((n, d), lambda i: (0, 0))],
        out_specs=[row] * n,
        out_shape=[jax.ShapeDtypeStruct((t, d), BF16)] * n,
        compiler_params=_params("parallel"),
        name="shift_mix",
    )(x, x, mu)


def _softplus(x):
    return jnp.maximum(x, 0.0) + jnp.log1p(jnp.exp(-jnp.abs(x)))


def _lora_kernel(x_ref, w1_ref, w2_ref, b_ref, o_ref, *, mid, out):
    hmid = jnp.dot(x_ref[...], w1_ref[...], preferred_element_type=F32)
    if mid == "tanh":
        hmid = jnp.tanh(hmid)
    elif mid == "sigmoid":
        hmid = jax.nn.sigmoid(hmid)
    y = jnp.dot(hmid.astype(BF16), w2_ref[...], preferred_element_type=F32)
    if out == "logdecay":
        y = -jnp.exp(-_softplus(-(b_ref[...] + y)) - 0.5)
    elif out == "sigmoid":
        y = jax.nn.sigmoid(b_ref[...] + y)
    o_ref[...] = y.astype(o_ref.dtype)


def _lora(x, w1, w2, bias, mid, out, out_dtype=F32, tm=512):
    t, d = x.shape
    tm = min(tm, t)
    r = w1.shape[1]
    rp = -(-r // LANES) * LANES
    w1p = jnp.pad(w1, ((0, 0), (0, rp - r))).astype(BF16)
    w2p = jnp.pad(w2, ((0, rp - r), (0, 0))).astype(BF16)
    dout = w2.shape[1]
    if bias is None:
        bias = jnp.zeros((dout,), F32)
    return pl.pallas_call(
        functools.partial(_lora_kernel, mid=mid, out=out),
        grid=(t // tm,),
        in_specs=[pl.BlockSpec((tm, d), lambda i: (i, 0)),
                  pl.BlockSpec((d, rp), lambda i: (0, 0)),
                  pl.BlockSpec((rp, dout), lambda i: (0, 0)),
                  pl.BlockSpec((1, dout), lambda i: (0, 0))],
        out_specs=pl.BlockSpec((tm, dout), lambda i: (i, 0)),
        out_shape=jax.ShapeDtypeStruct((t, dout), out_dtype),
        compiler_params=_params("parallel"),
        name="lora_" + out,
    )(x, w1p, w2p, bias.reshape(1, dout))


def _head_sums(tiles, ones_b, split=True):
    c = tiles[0].shape[0]
    x = jnp.concatenate(tiles, axis=0)
    hi = x.astype(BF16)
    tot = jnp.dot(hi, ones_b, preferred_element_type=F32)
    if split:
        lo = (x - hi.astype(F32)).astype(BF16)
        tot = tot + jnp.dot(lo, ones_b, preferred_element_type=F32)
    return [tot[p * c:(p + 1) * c] for p in range(len(tiles))]


def _bdot(a, b):
    return jnp.dot(a.astype(BF16), b.astype(BF16), preferred_element_type=F32)


def _rwkv_chunk_pairs(r, ld, kh, v, kk, bb, g_cum, s0, c):
    pairs = range(len(r))
    lane = lax.broadcasted_iota(jnp.int32, (1, LANES), 1)
    m0 = jnp.where(lane < C_HEAD_DIM, 1.0, 0.0)
    m1 = 1.0 - m0
    n2 = 2 * c
    row = lax.broadcasted_iota(jnp.int32, (n2, n2), 0)
    col = lax.broadcasted_iota(jnp.int32, (n2, n2), 1)
    lower, lower_eq = col < row, col <= row

    def stack(x):
        return jnp.concatenate([x * m0, x * m1], axis=0)

    e_g = [jnp.exp(g_cum[p]) for p in pairs]
    e_ng = [jnp.exp(-g_cum[p]) for p in pairs]
    al2 = [stack(-(kk[p] * jnp.exp(g_cum[p] - ld[p]))) for p in pairs]
    be2 = [stack(bb[p] * e_ng[p]).astype(BF16) for p in pairs]
    kb2 = [stack(kh[p] * e_ng[p]).astype(BF16) for p in pairs]
    rb = [r[p] * e_g[p] for p in pairs]
    v2 = [stack(v[p]) for p in pairs]
    sc = [lax.dot_general(jnp.concatenate([al2[p], stack(rb[p])], axis=0).astype(BF16),
                          jnp.concatenate([be2[p], kb2[p]], axis=0), _NT,
                          preferred_element_type=F32) for p in pairs]
    l_ab = [jnp.where(lower, sc[p][:n2, :n2], 0.0) for p in pairs]
    l_ak = [jnp.where(lower, sc[p][:n2, n2:], 0.0) for p in pairs]
    l_r = [jnp.concatenate([jnp.where(lower_eq, sc[p][n2:, :n2], 0.0),
                            jnp.where(lower_eq, sc[p][n2:, n2:], 0.0)], axis=1).astype(BF16) for p in pairs]

    t_off = l_ab
    pw = [x.astype(BF16) for x in l_ab]
    for _ in range(max(c.bit_length() - 2, 0)):
        pw_f = [jnp.dot(pw[p], pw[p], preferred_element_type=F32) for p in pairs]
        pw = [x.astype(BF16) for x in pw_f]
        t_off = [t_off[p] + pw_f[p] + jnp.dot(t_off[p].astype(BF16), pw[p], preferred_element_type=F32)
                 for p in pairs]

    w0 = [jnp.concatenate([al2[p], _bdot(l_ak[p], v2[p])], axis=1) for p in pairs]
    x = [w0[p] + _bdot(t_off[p], w0[p]) for p in pairs]
    xb = [x[p].astype(BF16) for p in pairs]
    z = [jnp.dot(l_r[p],
                 jnp.concatenate([xb[p], jnp.concatenate([jnp.zeros_like(v2[p]), v2[p]], axis=1).astype(BF16)],
                                 axis=0), preferred_element_type=F32) for p in pairs]
    r_eff = [(rb[p] + z[p][:c, :LANES] + z[p][c:, :LANES]).astype(BF16) for p in pairs]
    y0 = [z[p][:c, LANES:] + z[p][c:, LANES:] for p in pairs]
    mn = [lax.dot_general(xb[p], be2[p], _TN, preferred_element_type=F32) for p in pairs]
    n_add = [mn[p][LANES:] + lax.dot_general(v2[p].astype(BF16), kb2[p], _TN, preferred_element_type=F32)
             for p in pairs]
    s0b = [s0[p].astype(BF16) for p in pairs]
    y = [y0[p] + lax.dot_general(r_eff[p], s0b[p], _NT, preferred_element_type=F32) for p in pairs]
    s_new = [(s0[p] + jnp.dot(s0b[p], mn[p][:LANES].astype(BF16), preferred_element_type=F32) + n_add[p])
             * e_g[p][c - 1:c, :] for p in pairs]
    return y, s_new


def _rwkv_scan_kernel(*refs, c, npair, vres):
    if vres:
        (r_ref, ld_ref, k_ref, a_ref, v_ref, g_ref, vf_ref, vg_ref,
         kkp_ref, kap_ref, rkp_ref, lg_ref, lb_ref, tri_ref, o_ref, s_ref) = refs
    else:
        (r_ref, ld_ref, k_ref, a_ref, v_ref, g_ref,
         kkp_ref, kap_ref, rkp_ref, lg_ref, lb_ref, tri_ref, o_ref, s_ref) = refs

    @pl.when(pl.program_id(2) == 0)
    def _reset():
        s_ref[...] = jnp.zeros_like(s_ref)

    pairs = range(npair)
    sl = [slice(p * LANES, (p + 1) * LANES) for p in pairs]
    take = lambda ref: [ref[:, q].astype(F32) for q in sl]
    shift = C_HEAD_DIM.bit_length() - 1
    ones_b = jnp.where(jnp.right_shift(lax.broadcasted_iota(jnp.int32, (LANES, LANES), 0), shift)
                       == jnp.right_shift(lax.broadcasted_iota(jnp.int32, (LANES, LANES), 1), shift),
                       1.0, 0.0).astype(BF16)
    g_all = jnp.dot(tri_ref[...], ld_ref[...], precision=lax.Precision.HIGHEST, preferred_element_type=F32)

    r, ld, k, a, v, gate = (take(x) for x in (r_ref, ld_ref, k_ref, a_ref, v_ref, g_ref))
    kk_raw = [k[p] * kkp_ref[:, sl[p]] for p in pairs]
    sq = _head_sums([x * x for x in kk_raw], ones_b, split=False)
    kk = [kk_raw[p] / jnp.maximum(jnp.sqrt(sq[p]), KK_EPS) for p in pairs]
    bb = [kk[p] * a[p] for p in pairs]
    kh = [k[p] * (1.0 + (a[p] - 1.0) * kap_ref[:, sl[p]]) for p in pairs]
    if vres:
        vf, vg = take(vf_ref), take(vg_ref)
        v = [v[p] + (vf[p] - v[p]) * vg[p] for p in pairs]

    y, s_new = _rwkv_chunk_pairs(r, ld, kh, v, kk, bb, [g_all[:, q] for q in sl],
                                 [s_ref[p] for p in pairs], c)

    inv_n = 1.0 / C_HEAD_DIM
    mean = _head_sums(y, ones_b)
    yc = [y[p] - mean[p] * inv_n for p in pairs]
    var = _head_sums([x * x for x in yc], ones_b)
    bonus = _head_sums([r[p] * kh[p] * rkp_ref[:, sl[p]] for p in pairs], ones_b, split=False)
    for p in pairs:
        yn = yc[p] * lax.rsqrt(var[p] * inv_n + LNX_EPS) * lg_ref[:, sl[p]] + lb_ref[:, sl[p]]
        o_ref[:, sl[p]] = ((yn + bonus[p] * v[p]) * gate[p]).astype(o_ref.dtype)
        s_ref[p] = s_new[p]


RWKV_CHUNK = 64
RWKV_PAIRS = 16


def _rwkv_scan(r, log_w, k, a, v, gate, v_first, vgate, k_k, k_a, r_k, lnx_g, lnx_b, bsz, s):
    t, d = r.shape
    c = min(RWKV_CHUNK, s)
    wid = RWKV_PAIRS * LANES
    nc = s // c
    vres = v_first is not None
    tri = (jnp.arange(c)[None, :] <= jnp.arange(c)[:, None]).astype(F32)
    blk = pl.BlockSpec((c, wid), lambda b, g, i: (b * nc + i, g))
    vec = pl.BlockSpec((1, wid), lambda b, g, i: (0, g))
    rows = [r, log_w, k, a, v, gate] + ([v_first, vgate] if vres else [])
    vecs = [x.reshape(1, d) for x in (k_k, k_a, r_k, lnx_g, lnx_b)]
    return pl.pallas_call(
        functools.partial(_rwkv_scan_kernel, c=c, npair=RWKV_PAIRS, vres=vres),
        grid=(bsz, d // wid, nc),
        in_specs=[blk] * len(rows) + [vec] * len(vecs) + [pl.BlockSpec((c, c), lambda b, g, i: (0, 0))],
        out_specs=blk,
        out_shape=jax.ShapeDtypeStruct((t, d), BF16),
        scratch_shapes=[pltpu.VMEM((RWKV_PAIRS, LANES, LANES), F32)],
        compiler_params=_params("parallel", "parallel", "arbitrary"),
        name="rwkv_scan",
    )(*rows, *vecs, tri)


def _rwkv7_mixer(x, v_first, mu, w_r, w_k, w_v, w_o, w0, w1, w2, a0, a1, a2, g1, g2,
                 k_k, k_a, r_k, lnx_g, lnx_b, vres, bsz, s):
    xr, xw, xk, xv, xa, xg = _shift_mix(x, mu, s)
    r = _matmul(xr, w_r.astype(BF16), out_dtype=BF16)
    k = _matmul(xk, w_k.astype(BF16), out_dtype=BF16)
    v = _matmul(xv, w_v.astype(BF16), out_dtype=BF16)
    log_decay = _lora(xw, w1, w2, w0, "tanh", "logdecay")
    a = _lora(xa, a1, a2, a0, "none", "sigmoid", out_dtype=BF16)
    g = _lora(xg, g1, g2, None, "sigmoid", "none", out_dtype=BF16)
    if vres is None:
        vgate = None
        v_keep = v
    else:
        v0, v1, v2 = vres
        vgate = _lora(xv, v1, v2, v0, "none", "sigmoid", out_dtype=BF16)
        v_keep = v_first
    yo = _rwkv_scan(r, log_decay, k, a, v, g, None if vres is None else v_first, vgate,
                    k_k, k_a, r_k, lnx_g, lnx_b, bsz, s)
    return _matmul(yo, w_o.astype(BF16)), v_keep


def _lane_pick(cond, lanef):
    return jnp.min(jnp.where(cond, lanef, float(LANES)), axis=1, keepdims=True)


def _route_tile(xt, w_ref, b_ref, route_ref, cnt_ref, run_ref, tm):
    @pl.when(pl.program_id(0) == 0)
    def _reset():
        run_ref[...] = jnp.zeros_like(run_ref)

    logits = jnp.dot(xt.astype(BF16), w_ref[...], preferred_element_type=F32) + b_ref[...]
    lane = lax.broadcasted_iota(jnp.int32, (tm, LANES), 1)
    lanef = lane.astype(F32)

    def masked_softmax(keep):
        z = jnp.where(keep, logits, -jnp.inf)
        e = jnp.exp(z - jnp.max(z, axis=1, keepdims=True))
        return e / jnp.sum(e, axis=1, keepdims=True)

    in_groups = lane < N_GROUPS
    pg = masked_softmax(in_groups)
    g_p = jnp.max(pg, axis=1, keepdims=True)
    g_idx = _lane_pick(jnp.where(in_groups, pg, -1.0) == g_p, lanef)
    lo = float(N_GROUPS) + float(EXPERTS_PER_GROUP) * g_idx
    half = 0.5 * float(EXPERTS_PER_GROUP - 1)
    in_group = jnp.abs(lanef - lo - half) < half + 0.5
    pe = jnp.where(in_group, masked_softmax(in_group), -1.0)
    p1 = jnp.max(pe, axis=1, keepdims=True)
    i1 = _lane_pick(pe == p1, lanef)
    pe2 = jnp.where(lanef == i1, -1.0, pe)
    p2 = jnp.max(pe2, axis=1, keepdims=True)
    i2 = _lane_pick(pe2 == p2, lanef)
    den = p1 + p2
    gate = (g_p * (p1 / den), g_p * (p2 / den))
    expert = (i1 - float(N_GROUPS), i2 - float(N_GROUPS))

    earlier = jnp.where(lax.broadcasted_iota(jnp.int32, (tm, tm), 1)
                        < lax.broadcasted_iota(jnp.int32, (tm, tm), 0), 1.0, 0.0).astype(BF16)
    run = run_ref[...]
    rank = []
    for e in expert:
        onehot = jnp.where(lanef == e, 1.0, 0.0)
        before = jnp.dot(earlier, onehot.astype(BF16), preferred_element_type=F32) + run
        rank.append(jnp.sum(onehot * before, axis=1, keepdims=True))
        run = run + jnp.sum(onehot, axis=0, keepdims=True)
    run_ref[...] = run
    cnt_ref[...] = run
    cols = (expert[0], expert[1], rank[0], rank[1], gate[0], gate[1])
    route = jnp.zeros((tm, LANES), F32)
    for c, val in enumerate(cols):
        route = jnp.where(lane == c, val, route)
    route_ref[...] = route


SLAB = 2048 // LANES
DMA_UNROLL = 4


def _slab(ref, r):
    return ref.at[pl.ds(pl.multiple_of(r * SLAB, SLAB), SLAB)]


def _slab_rows(ref, rows):
    return jnp.concatenate([ref[pl.ds(c, rows, stride=SLAB), :] for c in range(SLAB)], axis=1)


def _store_slab_rows(ref, val, rows):
    for c in range(SLAB):
        ref[pl.ds(c, rows, stride=SLAB), :] = val[:, c * LANES:(c + 1) * LANES]


def _expert_kernel(be_ref, nu_ref, idx_ref, nxt_ref, x_hbm, wg_ref, wu_ref, wd_ref, o_ref,
                   xbuf, sems, wg_b, wu_b, wd_b, *, rows):
    i = pl.program_id(0)
    n_used = nu_ref[0]
    slot = i % 2

    def token_copy(tok, r, dst_slot):
        return pltpu.make_async_copy(_slab(x_hbm, tok), _slab(xbuf.at[dst_slot], r), sems.at[dst_slot])

    def gather(ids_ref, dst_slot):
        def start(r, carry):
            for rr in (r, r + rows // 2):
                token_copy(ids_ref[0, rr], rr, dst_slot).start()
            return carry
        lax.fori_loop(0, rows // 2, start, 0, unroll=DMA_UNROLL)

    @pl.when(i == 0)
    def _():
        gather(idx_ref, 0)

    @pl.when(i + 1 < n_used)
    def _():
        gather(nxt_ref, 1 - slot)

    @pl.when(jnp.logical_or(i == 0, be_ref[i] != be_ref[jnp.maximum(i - 1, 0)]))
    def _():
        wg_b[...] = wg_ref[...].astype(BF16)
        wu_b[...] = wu_ref[...].astype(BF16)
        wd_b[...] = wd_ref[...].astype(BF16)

    @pl.when(i < n_used)
    def _():
        for r in range(rows):
            token_copy(0, r, slot).wait()
        xb = _slab_rows(xbuf.at[slot], rows).astype(BF16)
        hg = jnp.dot(xb, wg_b[...], preferred_element_type=F32)
        hu = jnp.dot(xb, wu_b[...], preferred_element_type=F32)
        hid = (hg * jax.nn.sigmoid(hg) * hu).astype(BF16)
        _store_slab_rows(o_ref, jnp.dot(hid, wd_b[...], preferred_element_type=F32), rows)

    @pl.when(i >= n_used)
    def _():
        o_ref[...] = jnp.zeros_like(o_ref)


def _expert_mlp(x_slabs, src_row, block_e, n_used, w_gate, w_up, w_down, layer, rows=MOE_ROWS):
    d, ff = w_gate.shape[2], w_gate.shape[3]
    nb = src_row.shape[0] // rows
    ids = src_row.reshape(nb, 1, rows)
    grid_spec = pltpu.PrefetchScalarGridSpec(
        num_scalar_prefetch=2,
        grid=(nb,),
        in_specs=[pl.BlockSpec((None, 1, rows), lambda i, be, nu: (i, 0, 0), memory_space=pltpu.SMEM),
                  pl.BlockSpec((None, 1, rows), lambda i, be, nu: (jnp.minimum(i + 1, nb - 1), 0, 0),
                               memory_space=pltpu.SMEM),
                  pl.BlockSpec(memory_space=pl.ANY),
                  pl.BlockSpec((None, None, d, ff), lambda i, be, nu: (layer, be[i], 0, 0)),
                  pl.BlockSpec((None, None, d, ff), lambda i, be, nu: (layer, be[i], 0, 0)),
                  pl.BlockSpec((None, None, ff, d), lambda i, be, nu: (layer, be[i], 0, 0))],
        out_specs=pl.BlockSpec((rows * SLAB, LANES), lambda i, be, nu: (i, 0)),
        scratch_shapes=[pltpu.VMEM((2, rows * SLAB, LANES), F32), pltpu.SemaphoreType.DMA((2,)),
                        pltpu.VMEM((d, ff), BF16), pltpu.VMEM((d, ff), BF16), pltpu.VMEM((ff, d), BF16)],
    )
    return pl.pallas_call(
        functools.partial(_expert_kernel, rows=rows),
        grid_spec=grid_spec,
        out_shape=jax.ShapeDtypeStruct((nb * rows * SLAB, LANES), F32),
        compiler_params=_params("arbitrary"),
        name="expert_mlp",
    )(block_e, n_used, ids, ids, x_slabs, w_gate, w_up, w_down)


def _combine_ln_kernel(idx_ref, nxt_ref, y_ref, x_ref, gt_ref, g_ref, b_ref, o_ref, buf, sems, *, rows):
    i = pl.program_id(0)
    slot = i % 2

    def row_copy(src, r, kslot, dst_slot):
        return pltpu.make_async_copy(_slab(y_ref, src), _slab(buf.at[dst_slot, kslot], r), sems.at[dst_slot])

    def gather(ids_ref, dst_slot):
        def start(r, carry):
            for kslot in range(MOE_TOPK):
                for rr in (r, r + rows // 2):
                    row_copy(ids_ref[0, MOE_TOPK * rr + kslot], rr, kslot, dst_slot).start()
            return carry
        lax.fori_loop(0, rows // 2, start, 0, unroll=DMA_UNROLL)

    @pl.when(i == 0)
    def _():
        gather(idx_ref, 0)

    @pl.when(i + 1 < pl.num_programs(0))
    def _():
        gather(nxt_ref, 1 - slot)

    for r in range(rows):
        for kslot in range(MOE_TOPK):
            row_copy(0, r, kslot, slot).wait()
    gt = gt_ref[...]
    ffn = gt[:, 0:1] * _slab_rows(buf.at[slot, 0], rows)
    for kslot in range(1, MOE_TOPK):
        ffn = ffn + gt[:, kslot:kslot + 1] * _slab_rows(buf.at[slot, kslot], rows)
    o_ref[...] = _ln_rows(DN_ALPHA * x_ref[...] + ffn, g_ref[...], b_ref[...])


def _combine_ln(y_sorted, dest, gates, x, g, b, rows=256):
    t, d = x.shape
    nb = t // rows
    row = pl.BlockSpec((rows, d), lambda i: (i, 0))
    vec = pl.BlockSpec((1, d), lambda i: (0, 0))
    ids = dest.reshape(nb, 1, rows * MOE_TOPK)
    return pl.pallas_call(
        functools.partial(_combine_ln_kernel, rows=rows),
        grid=(nb,),
        in_specs=[pl.BlockSpec((None, 1, rows * MOE_TOPK), lambda i: (i, 0, 0), memory_space=pltpu.SMEM),
                  pl.BlockSpec((None, 1, rows * MOE_TOPK), lambda i: (jnp.minimum(i + 1, nb - 1), 0, 0),
                               memory_space=pltpu.SMEM),
                  pl.BlockSpec(memory_space=pl.ANY), row,
                  pl.BlockSpec((rows, MOE_TOPK), lambda i: (i, 0)), vec, vec],
        out_specs=row,
        out_shape=jax.ShapeDtypeStruct((t, d), F32),
        scratch_shapes=[pltpu.VMEM((2, MOE_TOPK, rows * SLAB, LANES), F32), pltpu.SemaphoreType.DMA((2,))],
        compiler_params=_params("arbitrary"),
        name="combine_ln",
    )(ids, ids, y_sorted, x, gates, g.reshape(1, d), b.reshape(1, d))


def _router_params(w_grp, b_grp, w_exp_r, b_exp_r):
    n_r = N_GROUPS + N_EXPERTS
    w_router = jnp.pad(jnp.concatenate([w_grp, w_exp_r], axis=1), ((0, 0), (0, LANES - n_r))).astype(BF16)
    b_router = jnp.pad(jnp.concatenate([b_grp, b_exp_r]), (0, LANES - n_r)).reshape(1, LANES)
    return w_router, b_router


def _moe_layer(x, x_slabs, route, cnt, w_gate, w_up, w_down, ln_g, ln_b, layer):
    t, d = x.shape
    flat_e = route[:, 0:MOE_TOPK].astype(jnp.int32).reshape(-1)
    rank = route[:, MOE_TOPK:2 * MOE_TOPK].astype(jnp.int32).reshape(-1)
    gates = route[:, 2 * MOE_TOPK:3 * MOE_TOPK]
    counts = cnt[0, :N_EXPERTS].astype(jnp.int32)
    m = t * MOE_TOPK
    padded = (counts + MOE_ROWS - 1) // MOE_ROWS * MOE_ROWS
    pad_end = jnp.cumsum(padded)
    dest = (pad_end - padded)[flat_e] + rank
    n_blocks = (m + N_EXPERTS * (MOE_ROWS - 1) + MOE_ROWS - 1) // MOE_ROWS
    src_row = jnp.zeros((n_blocks * MOE_ROWS,), jnp.int32).at[dest].set(
        (jnp.arange(m) // MOE_TOPK).astype(jnp.int32))
    starts = jnp.arange(n_blocks, dtype=jnp.int32) * MOE_ROWS
    block_e = jnp.minimum(jnp.sum((pad_end[None, :] <= starts[:, None]).astype(jnp.int32), axis=1),
                          N_EXPERTS - 1).astype(jnp.int32)
    n_used = (pad_end[-1] // MOE_ROWS).astype(jnp.int32).reshape(1)

    ys = _expert_mlp(x_slabs, src_row, block_e, n_used, w_gate, w_up, w_down, layer)
    return _combine_ln(ys, dest.astype(jnp.int32), gates, x, ln_g, ln_b)


def kernel(x, ev_w_in, ev_gate_bias, ev_conv_w, ev_conv_b, ev_hnorm_g, ev_w_out, od_mu, od_w_r, od_w_k, od_w_v, od_w_o, od_w0, od_w1, od_w2, od_a0, od_a1, od_a2, od_g1, od_g2, od_k_k, od_k_a, od_r_k, od_lnx_g, od_lnx_b, od_v0, od_v1, od_v2, ln_mix_g, ln_mix_b, ln_ffn_g, ln_ffn_b, moe_w_grp, moe_b_grp, moe_w_exp_r, moe_b_exp_r, moe_w_gate, moe_w_up, moe_w_down):
    bsz, s, d = x.shape
    depth = ln_mix_g.shape[0]
    xf = x.reshape(bsz * s, d)
    v_first = None
    for layer in range(depth):
        if layer % 2 == 0:
            e = layer // 2
            mix = _even_mixer(xf, ev_w_in[e], ev_gate_bias[e], ev_conv_w[e], ev_conv_b[e],
                              ev_hnorm_g[e], ev_w_out[e], bsz, s)
        else:
            o = layer // 2
            vres = None if o == 0 else (od_v0[o - 1], od_v1[o - 1], od_v2[o - 1])
            mix, v_first = _rwkv7_mixer(xf, v_first, od_mu[o], od_w_r[o], od_w_k[o], od_w_v[o],
                                        od_w_o[o], od_w0[o], od_w1[o], od_w2[o], od_a0[o],
                                        od_a1[o], od_a2[o], od_g1[o], od_g2[o], od_k_k[o],
                                        od_k_a[o], od_r_k[o], od_lnx_g[o], od_lnx_b[o], vres, bsz, s)
        xf, x_slabs, route, cnt = _resid_ln(xf, mix, ln_mix_g[layer], ln_mix_b[layer],
                                            *_router_params(moe_w_grp[layer], moe_b_grp[layer],
                                                            moe_w_exp_r[layer], moe_b_exp_r[layer]))
        xf = _moe_layer(xf, x_slabs, route, cnt, moe_w_gate, moe_w_up, moe_w_down,
                        ln_ffn_g[layer], ln_ffn_b[layer], layer)
    return xf.reshape(bsz, s, d)
```

```python
import functools
import math

import jax
import jax.numpy as jnp
from jax import lax
from jax.experimental import pallas as pl
from jax.experimental.pallas import tpu as pltpu

F32 = jnp.float32
BF16 = jnp.bfloat16

DEPTH = 4
DN_ALPHA = (2 * DEPTH) ** 0.25
LN_EPS = 1e-5

A_HEADS = 8
A_HEAD_DIM = 128
A_WIDTH = A_HEADS * A_HEAD_DIM
MOBA_BLOCK = 256
MOBA_TOPK = 3
MOBA_GROUP = 4
MOBA_HEADS = 4
ROPE_THETA = 500000.0
ROPE_DIM = A_HEAD_DIM // 4
NEG_INF = -1e30

B_HEADS = 4
B_HEAD_DIM = 256
B_WIDTH = B_HEADS * B_HEAD_DIM
MLSTM_CHUNK = 64
B_CONV = 4
HNORM_EPS = 1e-6
EVEN_MAIN = 3 * A_WIDTH + 4 * B_WIDTH

C_HEAD_DIM = 64
LNX_EPS = 64e-5
KK_EPS = 1e-12

N_GROUPS = 4
EXPERTS_PER_GROUP = 8
N_EXPERTS = N_GROUPS * EXPERTS_PER_GROUP
MOE_TOPK = 2
MOE_ROWS = 256

LANES = 128
SUBLANES = 8
VMEM_LIMIT = 56 * 1024 * 1024

_NT = (((1,), (1,)), ((), ()))
_TN = (((0,), (0,)), ((), ()))


def _params(*sem):
    return pltpu.CompilerParams(dimension_semantics=sem, vmem_limit_bytes=VMEM_LIMIT)


def _mm_kernel(x_ref, w_ref, o_ref):
    o_ref[...] = jnp.dot(x_ref[...].astype(BF16), w_ref[...],
                         preferred_element_type=F32).astype(o_ref.dtype)


LHS_TILE_BYTES = 8 * 1024 * 1024


def _row_tile(m, row_bytes):
    tm = 1 << ((LHS_TILE_BYTES // row_bytes).bit_length() - 1)
    return min(tm, m)


def _matmul(x, w, out_dtype=F32, tn=512):
    m, k = x.shape
    n = w.shape[1]
    tm, tn = _row_tile(m, k * x.dtype.itemsize), min(tn, n)
    return pl.pallas_call(
        _mm_kernel,
        grid=(m // tm, n // tn),
        in_specs=[pl.BlockSpec((tm, k), lambda i, j: (i, 0)),
                  pl.BlockSpec((k, tn), lambda i, j: (0, j))],
        out_specs=pl.BlockSpec((tm, tn), lambda i, j: (i, j)),
        out_shape=jax.ShapeDtypeStruct((m, n), out_dtype),
        compiler_params=_params("parallel", "arbitrary"),
        name="matmul",
    )(x, w)


def _mm2_kernel(xa_ref, xb_ref, wa_ref, wb_ref, o_ref):
    acc = jnp.dot(xa_ref[...], wa_ref[...], preferred_element_type=F32)
    acc = acc + jnp.dot(xb_ref[...], wb_ref[...], preferred_element_type=F32)
    o_ref[...] = acc


def _matmul2(xa, xb, wa, wb, tn=512):
    m, ka = xa.shape
    kb = xb.shape[1]
    n = wa.shape[1]
    tm, tn = _row_tile(m, ka * xa.dtype.itemsize + kb * xb.dtype.itemsize), min(tn, n)
    return pl.pallas_call(
        _mm2_kernel,
        grid=(m // tm, n // tn),
        in_specs=[pl.BlockSpec((tm, ka), lambda i, j: (i, 0)),
                  pl.BlockSpec((tm, kb), lambda i, j: (i, 0)),
                  pl.BlockSpec((ka, tn), lambda i, j: (0, j)),
                  pl.BlockSpec((kb, tn), lambda i, j: (0, j))],
        out_specs=pl.BlockSpec((tm, tn), lambda i, j: (i, j)),
        out_shape=jax.ShapeDtypeStruct((m, n), F32),
        compiler_params=_params("parallel", "arbitrary"),
        name="matmul2",
    )(xa, xb, wa, wb)


def _ln_rows(h, g, b):
    mu = jnp.mean(h, axis=-1, keepdims=True)
    hc = h - mu
    var = jnp.mean(hc * hc, axis=-1, keepdims=True)
    return hc * lax.rsqrt(var + LN_EPS) * g + b


def _resid_ln_kernel(x_ref, m_ref, g_ref, b_ref, w_ref, br_ref, o_ref, s_ref, route_ref, cnt_ref, run_ref,
                     *, tm):
    y = _ln_rows(DN_ALPHA * x_ref[...] + m_ref[...], g_ref[...], b_ref[...])
    o_ref[...] = y.astype(o_ref.dtype)
    _store_slab_rows(s_ref, y, tm)
    _route_tile(y, w_ref, br_ref, route_ref, cnt_ref, run_ref, tm)


def _resid_ln(x, mix, g, b, w_router, b_router, tm=256):
    t, d = x.shape
    tm = min(tm, t)
    row = pl.BlockSpec((tm, d), lambda i: (i, 0))
    vec = pl.BlockSpec((1, d), lambda i: (0, 0))
    one = pl.BlockSpec((1, LANES), lambda i: (0, 0))
    return pl.pallas_call(
        functools.partial(_resid_ln_kernel, tm=tm),
        grid=(t // tm,),
        in_specs=[row, row, vec, vec, pl.BlockSpec((d, LANES), lambda i: (0, 0)), one],
        out_specs=[row, pl.BlockSpec((tm * SLAB, LANES), lambda i: (i, 0)),
                   pl.BlockSpec((tm, LANES), lambda i: (i, 0)), one],
        out_shape=[jax.ShapeDtypeStruct((t, d), F32), jax.ShapeDtypeStruct((t * SLAB, LANES), F32),
                   jax.ShapeDtypeStruct((t, LANES), F32), jax.ShapeDtypeStruct((1, LANES), F32)],
        scratch_shapes=[pltpu.VMEM((1, LANES), F32)],
        compiler_params=_params("arbitrary"),
        name="resid_ln_router",
    )(x, mix, g.reshape(1, d), b.reshape(1, d), w_router, b_router)


def _rope_tables(s):
    half = ROPE_DIM // 2
    inv = jnp.power(jnp.float32(ROPE_THETA), -jnp.arange(half, dtype=F32) / half)
    ang = jnp.arange(s).astype(F32)[:, None] * inv[None, :]
    cos, sin = jnp.cos(ang), jnp.sin(ang)
    zero = jnp.zeros((s, A_HEAD_DIM - ROPE_DIM), F32)
    zh = jnp.zeros((s, half), F32)
    c = jnp.concatenate([cos, cos, zero + 1.0], axis=-1)
    s_up = jnp.concatenate([zh, sin, zero], axis=-1)
    s_dn = jnp.concatenate([-sin, zh, zero], axis=-1)
    return c, s_up, s_dn


def _rope(x, c, s_up, s_dn):
    half = ROPE_DIM // 2
    return (x * c + pltpu.roll(x, half, axis=1) * s_up
            + pltpu.roll(x, A_HEAD_DIM - half, axis=1) * s_dn)


def _moba_kernel(q_ref, k_ref, v_ref, c_ref, su_ref, sd_ref, o_ref, krot_ref, vb_ref, kmean_ref, *, nblk):
    qi = pl.program_id(2)
    blk = MOBA_BLOCK
    dh = A_HEAD_DIM
    scale = dh ** -0.5
    heads = range(MOBA_HEADS)
    hs = [slice(h * dh, (h + 1) * dh) for h in heads]

    @pl.when(qi == 0)
    def _prepare_keys():
        def body(j, carry):
            rows = pl.ds(pl.multiple_of(j * blk, blk), blk)
            block_tag = jnp.where(lax.broadcasted_iota(jnp.int32, (blk, LANES), 1) == j, 1.0, 0.0)
            for h in heads:
                kr = _rope(k_ref[rows, hs[h]].astype(F32), c_ref[rows, :], su_ref[rows, :], sd_ref[rows, :])
                krot_ref[h, rows, :] = jnp.concatenate([kr, block_tag], axis=1).astype(BF16)
                vb_ref[h, rows, :] = v_ref[rows, hs[h]].astype(BF16)
                kmean_ref[h, pl.ds(j, 1), :] = jnp.mean(kr, axis=0, keepdims=True)
            return carry
        lax.fori_loop(0, nblk, body, 0)

    qrows = pl.ds(pl.multiple_of(qi * blk, blk), blk)
    qb = [_rope(q_ref[:, hs[h]].astype(F32), c_ref[qrows, :], su_ref[qrows, :], sd_ref[qrows, :]).astype(BF16)
          for h in heads]

    brow = lax.broadcasted_iota(jnp.int32, (nblk, blk), 0)
    g = [jnp.where(brow < qi, lax.dot_general(kmean_ref[h].astype(BF16), qb[h], _NT,
                                              preferred_element_type=F32), -jnp.inf) for h in heads]
    sel_t = [jnp.zeros((nblk, blk), F32) for _ in heads]
    for j in range(nblk):
        earlier = jnp.where(brow < j, 1.0, 0.0)
        for h in heads:
            gj = g[h][j:j + 1, :]
            beats = jnp.where(g[h] > gj, 1.0, jnp.where(g[h] == gj, earlier, 0.0))
            cnt = jnp.sum(beats, axis=0, keepdims=True)
            sel_t[h] = jnp.where(brow == j, jnp.where(cnt < float(min(MOBA_TOPK, nblk)), 1.0, 0.0), sel_t[h])
    eye = jnp.where(lax.broadcasted_iota(jnp.int32, (blk, blk), 0)
                    == lax.broadcasted_iota(jnp.int32, (blk, blk), 1), 1.0, 0.0).astype(BF16)
    pad = jnp.zeros((LANES - nblk, blk), F32)
    sel = [lax.dot_general(eye, jnp.concatenate([jnp.where(brow < qi, sel_t[h], 0.0), pad], axis=0).astype(BF16),
                           _NT, preferred_element_type=F32) for h in heads]
    lane = lax.broadcasted_iota(jnp.int32, (blk, LANES), 1)
    off = jnp.where(lane < nblk, NEG_INF / scale, 0.0)
    q_aug = [jnp.concatenate([qb[h], jnp.where(sel[h] > 0.5, 0.0, off).astype(BF16)], axis=1) for h in heads]

    causal = (lax.broadcasted_iota(jnp.int32, (blk, blk), 1)
              <= lax.broadcasted_iota(jnp.int32, (blk, blk), 0))
    s = [jnp.where(causal, lax.dot_general(qb[h], krot_ref[h, qrows, :dh], _NT,
                                           preferred_element_type=F32) * scale, NEG_INF) for h in heads]
    m0 = [jnp.max(s[h], axis=1, keepdims=True) for h in heads]
    p = [jnp.exp(s[h] - m0[h]) for h in heads]
    l0 = [jnp.sum(p[h], axis=1, keepdims=True) for h in heads]
    acc0 = [jnp.dot(p[h].astype(BF16), vb_ref[h, qrows, :], preferred_element_type=F32) for h in heads]

    span = MOBA_GROUP * blk

    def past_group(jg, carry):
        m, l, acc = carry
        rows = pl.ds(pl.multiple_of(jg * span, span), span)
        sj = [lax.dot_general(q_aug[h], krot_ref[h, rows, :], _NT, preferred_element_type=F32) * scale
              for h in heads]
        m_new = [jnp.maximum(m[h], jnp.max(sj[h], axis=1, keepdims=True)) for h in heads]
        a = [jnp.exp(m[h] - m_new[h]) for h in heads]
        pj = [jnp.exp(sj[h] - m_new[h]) for h in heads]
        l = [a[h] * l[h] + jnp.sum(pj[h], axis=1, keepdims=True) for h in heads]
        acc = [a[h] * acc[h] + jnp.dot(pj[h].astype(BF16), vb_ref[h, rows, :], preferred_element_type=F32)
               for h in heads]
        return m_new, l, acc

    n_groups = (qi + MOBA_GROUP - 1) // MOBA_GROUP
    _, l, acc = lax.fori_loop(0, n_groups, past_group, (m0, l0, acc0))
    for h in heads:
        o_ref[:, hs[h]] = (acc[h] / l[h]).astype(o_ref.dtype)


def _moba(z, bsz, s):
    nblk = s // MOBA_BLOCK
    c, s_up, s_dn = _rope_tables(s)
    wid = MOBA_HEADS * A_HEAD_DIM
    ncol = A_WIDTH // wid
    tab = pl.BlockSpec((s, A_HEAD_DIM), lambda b, h, qi: (0, 0))
    return pl.pallas_call(
        functools.partial(_moba_kernel, nblk=nblk),
        grid=(bsz, ncol, nblk),
        in_specs=[pl.BlockSpec((MOBA_BLOCK, wid), lambda b, h, qi: (b * nblk + qi, h)),
                  pl.BlockSpec((s, wid), lambda b, h, qi: (b, ncol + h)),
                  pl.BlockSpec((s, wid), lambda b, h, qi: (b, 2 * ncol + h)),
                  tab, tab, tab],
        out_specs=pl.BlockSpec((MOBA_BLOCK, wid), lambda b, h, qi: (b * nblk + qi, h)),
        out_shape=jax.ShapeDtypeStruct((bsz * s, A_WIDTH), BF16),
        scratch_shapes=[pltpu.VMEM((MOBA_HEADS, s, A_HEAD_DIM + LANES), BF16),
                        pltpu.VMEM((MOBA_HEADS, s, A_HEAD_DIM), BF16),
                        pltpu.VMEM((MOBA_HEADS, nblk, A_HEAD_DIM), F32)],
        compiler_params=_params("parallel", "parallel", "arbitrary"),
        name="moba",
    )(z, z, z, c, s_up, s_dn)


def _log_sigmoid(x):
    return jnp.minimum(x, 0.0) - jnp.log1p(jnp.exp(-jnp.abs(x)))


def _shift_rows(x, prev8, d):
    if d == 0:
        return x
    rolled = pltpu.roll(x, d, axis=0)
    top = jnp.where(lax.broadcasted_iota(jnp.int32, prev8.shape, 0) < d,
                    pltpu.roll(prev8, d, axis=0), rolled[:SUBLANES])
    return jnp.concatenate([top, rolled[SUBLANES:]], axis=0)


def _mlstm_kernel(q_ref, k_ref, v_ref, o_ref, gc_ref, gr_ref, bc_ref, br_ref,
                  cwq_ref, cwk_ref, cbq_ref, cbk_ref, hg_ref, y_ref,
                  c_ref, n_ref, m_ref, pq_ref, pk_ref, *, tt):
    L = MLSTM_CHUNK
    dk = B_HEAD_DIM
    heads = range(B_HEADS)
    hs = [slice(h * dk, (h + 1) * dk) for h in heads]

    @pl.when(pl.program_id(1) == 0)
    def _reset():
        c_ref[...] = jnp.zeros_like(c_ref)
        n_ref[...] = jnp.zeros_like(n_ref)
        m_ref[...] = jnp.zeros_like(m_ref)
        pq_ref[...] = jnp.zeros_like(pq_ref)
        pk_ref[...] = jnp.zeros_like(pk_ref)

    def conv_silu(x_ref, p_ref, w_ref, b_ref):
        x = x_ref[...].astype(F32)
        prev8 = p_ref[...]
        out = b_ref[...] + x * w_ref[B_CONV - 1:B_CONV, :]
        for d in range(1, B_CONV):
            out = out + _shift_rows(x, prev8, d) * w_ref[B_CONV - 1 - d:B_CONV - d, :]
        p_ref[...] = x[tt - SUBLANES:, :]
        return out * jax.nn.sigmoid(out)

    q_all = conv_silu(q_ref, pq_ref, cwq_ref, cbq_ref)
    k_all = conv_silu(k_ref, pk_ref, cwk_ref, cbk_ref) * (dk ** -0.5)

    gcol = gc_ref[...] + bc_ref[...]
    grow = gr_ref[...] + br_ref[...]
    i_col_all = [gcol[:, h:h + 1] for h in heads]
    f_col_all = [_log_sigmoid(gcol[:, B_HEADS + h:B_HEADS + h + 1]) for h in heads]
    i_row_all = [grow[h:h + 1, :] for h in heads]
    f_row_all = [_log_sigmoid(grow[B_HEADS + h:B_HEADS + h + 1, :]) for h in heads]

    rr = lax.broadcasted_iota(jnp.int32, (L, L), 0)
    cc = lax.broadcasted_iota(jnp.int32, (L, L), 1)
    tri = cc <= rr

    for c in range(tt // L):
        lo, hi = c * L, (c + 1) * L
        qc = [q_all[lo:hi, hs[h]] for h in heads]
        kc = [k_all[lo:hi, hs[h]] for h in heads]
        qcb = [x.astype(BF16) for x in qc]
        kcb = [x.astype(BF16) for x in kc]
        vcb = [v_ref[lo:hi, hs[h]].astype(BF16) for h in heads]
        i_col = [x[lo:hi] for x in i_col_all]
        f_col = [x[lo:hi] for x in f_col_all]
        i_row = [x[:, lo:hi] for x in i_row_all]
        f_row = [x[:, lo:hi] for x in f_row_all]
        m_prev = [m_ref[h] for h in heads]
        cst = [c_ref[h] for h in heads]
        nst = [n_ref[h] for h in heads]

        b_col = [jnp.sum(jnp.where(tri, f_row[h], 0.0), axis=1, keepdims=True) for h in heads]
        b_row = [jnp.sum(jnp.where(rr <= cc, f_col[h], 0.0), axis=0, keepdims=True) for h in heads]
        log_inter = [b_col[h] + m_prev[h] for h in heads]
        log_intra = [jnp.where(tri, b_col[h] - b_row[h] + i_row[h], -jnp.inf) for h in heads]
        m_t = [jnp.maximum(log_inter[h], jnp.max(log_intra[h], axis=1, keepdims=True)) for h in heads]
        w_inter = [jnp.exp(log_inter[h] - m_t[h]) for h in heads]
        qk = [lax.dot_general(qcb[h], kcb[h], _NT, preferred_element_type=F32)
              * jnp.exp(log_intra[h] - m_t[h]) for h in heads]
        q_c = [jnp.dot(qcb[h], cst[h].astype(BF16), preferred_element_type=F32) for h in heads]
        qk_v = [jnp.dot(qk[h].astype(BF16), vcb[h], preferred_element_type=F32) for h in heads]
        den = [w_inter[h] * jnp.sum(qc[h] * nst[h], axis=1, keepdims=True)
               + jnp.sum(qk[h], axis=1, keepdims=True) for h in heads]
        hc = [(w_inter[h] * q_c[h] + qk_v[h]) / jnp.maximum(jnp.abs(den[h]), jnp.exp(-m_t[h])) for h in heads]

        b_last = [jnp.sum(f_col[h], axis=0, keepdims=True) for h in heads]
        log_s = [b_last[h] - b_col[h] + i_col[h] for h in heads]
        m_new = [jnp.maximum(b_last[h] + m_prev[h], jnp.max(log_s[h], axis=0, keepdims=True)) for h in heads]
        decay = [jnp.exp(b_last[h] + m_prev[h] - m_new[h]) for h in heads]
        kw = [kc[h] * jnp.exp(log_s[h] - m_new[h]) for h in heads]
        kv = [lax.dot_general(kw[h].astype(BF16), vcb[h], _TN, preferred_element_type=F32) for h in heads]
        for h in heads:
            c_ref[h] = decay[h] * cst[h] + kv[h]
            n_ref[h] = decay[h] * nst[h] + jnp.sum(kw[h], axis=0, keepdims=True)
            m_ref[h] = m_new[h]
            hn = hc[h] * lax.rsqrt(jnp.mean(hc[h] * hc[h], axis=1, keepdims=True) + HNORM_EPS)
            y_ref[lo:hi, hs[h]] = (hn * hg_ref[:, hs[h]]
                                   * jax.nn.sigmoid(o_ref[lo:hi, hs[h]].astype(F32))).astype(y_ref.dtype)


def _mlstm(z, gates, gate_bias, conv_w, conv_b, hnorm_g, bsz, s, tt=256):
    nt = s // tt
    cb = 3 * A_WIDTH // B_WIDTH
    gates_t = gates.T
    bias = gate_bias.reshape(1, 2 * B_HEADS)

    def col(off):
        return pl.BlockSpec((tt, B_WIDTH), lambda b, t: (b * nt + t, cb + off))

    def par(rows, off):
        return pl.BlockSpec((rows, B_WIDTH), lambda b, t: (0, off))

    return pl.pallas_call(
        functools.partial(_mlstm_kernel, tt=tt),
        grid=(bsz, nt),
        in_specs=[col(0), col(1), col(2), col(3),
                  pl.BlockSpec((tt, 2 * B_HEADS), lambda b, t: (b * nt + t, 0)),
                  pl.BlockSpec((2 * B_HEADS, tt), lambda b, t: (0, b * nt + t)),
                  pl.BlockSpec((1, 2 * B_HEADS), lambda b, t: (0, 0)),
                  pl.BlockSpec((2 * B_HEADS, 1), lambda b, t: (0, 0)),
                  par(B_CONV, 0), par(B_CONV, 1), par(1, 0), par(1, 1), par(1, 0)],
        out_specs=pl.BlockSpec((tt, B_WIDTH), lambda b, t: (b * nt + t, 0)),
        out_shape=jax.ShapeDtypeStruct((bsz * s, B_WIDTH), BF16),
        scratch_shapes=[pltpu.VMEM((B_HEADS, B_HEAD_DIM, B_HEAD_DIM), F32),
                        pltpu.VMEM((B_HEADS, 1, B_HEAD_DIM), F32), pltpu.VMEM((B_HEADS, 1, 1), F32),
                        pltpu.VMEM((SUBLANES, B_WIDTH), F32), pltpu.VMEM((SUBLANES, B_WIDTH), F32)],
        compiler_params=_params("parallel", "arbitrary"),
        name="mlstm",
    )(z, z, z, z, gates, gates_t, bias, bias.T, conv_w, conv_w,
      conv_b.reshape(1, -1), conv_b.reshape(1, -1), hnorm_g.reshape(1, -1))


def _even_mixer(x, w_in, gate_bias, conv_w, conv_b, hnorm_g, w_out, bsz, s):
    d = x.shape[1]
    w_main = w_in[:, :EVEN_MAIN].astype(BF16)
    w_gate = jnp.pad(w_in[:, EVEN_MAIN:], ((0, 0), (0, LANES - 2 * B_HEADS))).astype(BF16)
    z = _matmul(x, w_main, out_dtype=BF16)
    gates = _matmul(x, w_gate, tn=LANES)[:, :2 * B_HEADS]
    y_a = _moba(z, bsz, s)
    y_b = _mlstm(z, gates, gate_bias, conv_w, conv_b, hnorm_g, bsz, s)
    w_o = w_out.astype(BF16)
    return _matmul2(y_a, y_b, w_o[:A_WIDTH], w_o[A_WIDTH:])


def _softplus(x):
    return jnp.maximum(x, 0.0) + jnp.log1p(jnp.exp(-jnp.abs(x)))


def _lora_apply(xb, w1_ref, w2_ref, b_ref, mid, out):
    hmid = jnp.dot(xb, w1_ref[...], preferred_element_type=F32)
    if mid == "tanh":
        hmid = jnp.tanh(hmid)
    elif mid == "sigmoid":
        hmid = jax.nn.sigmoid(hmid)
    y = jnp.dot(hmid.astype(BF16), w2_ref[...], preferred_element_type=F32)
    if out == "logdecay":
        y = -jnp.exp(-_softplus(-(b_ref[...] + y)) - 0.5)
    elif out == "sigmoid":
        y = jax.nn.sigmoid(b_ref[...] + y)
    return y


MIX_R, MIX_W, MIX_K, MIX_V, MIX_A, MIX_G = range(6)


def _shift_mix_kernel(*refs, tiles_per_seq, vres):
    x_ref, p_ref, mu_ref = refs[:3]
    n_lora = 4 if vres else 3
    lora = [refs[3 + 3 * j:6 + 3 * j] for j in range(n_lora)]
    outs = refs[3 + 3 * n_lora:]
    xr_o, xk_o, xv_o, ld_o, a_o, g_o = outs[:6]
    x = x_ref[...]
    first = (pl.program_id(0) % tiles_per_seq) == 0
    prev_row = jnp.where(first, 0.0, p_ref[SUBLANES - 1:SUBLANES, :])
    row = lax.broadcasted_iota(jnp.int32, x.shape, 0)
    dx = jnp.where(row == 0, prev_row, pltpu.roll(x, 1, axis=0)) - x

    def mixed(j):
        return (x + dx * mu_ref[j:j + 1, :]).astype(BF16)

    xr_o[...] = mixed(MIX_R)
    xk_o[...] = mixed(MIX_K)
    xv = mixed(MIX_V)
    xv_o[...] = xv
    ld_o[...] = _lora_apply(mixed(MIX_W), *lora[0], "tanh", "logdecay")
    a_o[...] = _lora_apply(mixed(MIX_A), *lora[1], "none", "sigmoid").astype(a_o.dtype)
    g_o[...] = _lora_apply(mixed(MIX_G), *lora[2], "sigmoid", "none").astype(g_o.dtype)
    if vres:
        outs[6][...] = _lora_apply(xv, *lora[3], "none", "sigmoid").astype(outs[6].dtype)


def _shift_mix(x, mu, loras, s, tm=256):
    t, d = x.shape
    tm = min(tm, t)
    n = mu.shape[0]
    vres = len(loras) == 4
    row = pl.BlockSpec((tm, d), lambda i: (i, 0))
    prev = pl.BlockSpec((SUBLANES, d), lambda i: (jnp.maximum(i * (tm // SUBLANES) - 1, 0), 0))
    args, specs = [], []
    for w1, w2, bias in loras:
        r = w1.shape[1]
        rp = -(-r // LANES) * LANES
        dout = w2.shape[1]
        args += [jnp.pad(w1, ((0, 0), (0, rp - r))).astype(BF16), jnp.pad(w2, ((0, rp - r), (0, 0))).astype(BF16),
                 (jnp.zeros((dout,), F32) if bias is None else bias).reshape(1, dout)]
        specs += [pl.BlockSpec((d, rp), lambda i: (0, 0)), pl.BlockSpec((rp, dout), lambda i: (0, 0)),
                  pl.BlockSpec((1, dout), lambda i: (0, 0))]
    dtypes = [BF16, BF16, BF16, F32, BF16, BF16] + ([BF16] if vres else [])
    return pl.pallas_call(
        functools.partial(_shift_mix_kernel, tiles_per_seq=s // tm, vres=vres),
        grid=(t // tm,),
        in_specs=[row, prev, pl.BlockSpec((n, d), lambda i: (0, 0))] + specs,
        out_specs=[row] * len(dtypes),
        out_shape=[jax.ShapeDtypeStruct((t, d), dt) for dt in dtypes],
        compiler_params=_params("parallel"),
        name="shift_mix_lora",
    )(x, x, mu, *args)


def _head_sums(tiles, ones_b, split=True):
    c = tiles[0].shape[0]
    x = jnp.concatenate(tiles, axis=0)
    hi = x.astype(BF16)
    tot = jnp.dot(hi, ones_b, preferred_element_type=F32)
    if split:
        lo = (x - hi.astype(F32)).astype(BF16)
        tot = tot + jnp.dot(lo, ones_b, preferred_element_type=F32)
    return [tot[p * c:(p + 1) * c] for p in range(len(tiles))]


def _bdot(a, b):
    return jnp.dot(a.astype(BF16), b.astype(BF16), preferred_element_type=F32)


def _rwkv_chunk_pairs(r, ld, kh, v, kk, bb, g_cum, s0, c):
    pairs = range(len(r))
    lane = lax.broadcasted_iota(jnp.int32, (1, LANES), 1)
    m0 = jnp.where(lane < C_HEAD_DIM, 1.0, 0.0)
    m1 = 1.0 - m0
    n2 = 2 * c
    row = lax.broadcasted_iota(jnp.int32, (n2, n2), 0)
    col = lax.broadcasted_iota(jnp.int32, (n2, n2), 1)
    lower, lower_eq = col < row, col <= row

    def stack(x):
        return jnp.concatenate([x * m0, x * m1], axis=0)

    e_g = [jnp.exp(g_cum[p]) for p in pairs]
    e_ng = [jnp.exp(-g_cum[p]) for p in pairs]
    al2 = [stack(-(kk[p] * jnp.exp(g_cum[p] - ld[p]))) for p in pairs]
    be2 = [stack(bb[p] * e_ng[p]).astype(BF16) for p in pairs]
    kb2 = [stack(kh[p] * e_ng[p]).astype(BF16) for p in pairs]
    rb = [r[p] * e_g[p] for p in pairs]
    v2 = [stack(v[p]) for p in pairs]
    sc = [lax.dot_general(jnp.concatenate([al2[p], stack(rb[p])], axis=0).astype(BF16),
                          jnp.concatenate([be2[p], kb2[p]], axis=0), _NT,
                          preferred_element_type=F32) for p in pairs]
    l_ab = [jnp.where(lower, sc[p][:n2, :n2], 0.0) for p in pairs]
    l_ak = [jnp.where(lower, sc[p][:n2, n2:], 0.0) for p in pairs]
    l_r = [jnp.concatenate([jnp.where(lower_eq, sc[p][n2:, :n2], 0.0),
                            jnp.where(lower_eq, sc[p][n2:, n2:], 0.0)], axis=1).astype(BF16) for p in pairs]

    t_off = l_ab
    pw = [x.astype(BF16) for x in l_ab]
    for _ in range(max(c.bit_length() - 2, 0)):
        pw_f = [jnp.dot(pw[p], pw[p], preferred_element_type=F32) for p in pairs]
        pw = [x.astype(BF16) for x in pw_f]
        t_off = [t_off[p] + pw_f[p] + jnp.dot(t_off[p].astype(BF16), pw[p], preferred_element_type=F32)
                 for p in pairs]

    w0 = [jnp.concatenate([al2[p], _bdot(l_ak[p], v2[p])], axis=1) for p in pairs]
    x = [w0[p] + _bdot(t_off[p], w0[p]) for p in pairs]
    xb = [x[p].astype(BF16) for p in pairs]
    z = [jnp.dot(l_r[p],
                 jnp.concatenate([xb[p], jnp.concatenate([jnp.zeros_like(v2[p]), v2[p]], axis=1).astype(BF16)],
                                 axis=0), preferred_element_type=F32) for p in pairs]
    r_eff = [(rb[p] + z[p][:c, :LANES] + z[p][c:, :LANES]).astype(BF16) for p in pairs]
    y0 = [z[p][:c, LANES:] + z[p][c:, LANES:] for p in pairs]
    mn = [lax.dot_general(xb[p], be2[p], _TN, preferred_element_type=F32) for p in pairs]
    n_add = [mn[p][LANES:] + lax.dot_general(v2[p].astype(BF16), kb2[p], _TN, preferred_element_type=F32)
             for p in pairs]
    s0b = [s0[p].astype(BF16) for p in pairs]
    y = [y0[p] + lax.dot_general(r_eff[p], s0b[p], _NT, preferred_element_type=F32) for p in pairs]
    s_new = [(s0[p] + jnp.dot(s0b[p], mn[p][:LANES].astype(BF16), preferred_element_type=F32) + n_add[p])
             * e_g[p][c - 1:c, :] for p in pairs]
    return y, s_new


def _rwkv_scan_kernel(*refs, c, npair, vres):
    if vres:
        (r_ref, ld_ref, k_ref, a_ref, v_ref, g_ref, vf_ref, vg_ref,
         kkp_ref, kap_ref, rkp_ref, lg_ref, lb_ref, tri_ref, o_ref, s_ref) = refs
    else:
        (r_ref, ld_ref, k_ref, a_ref, v_ref, g_ref,
         kkp_ref, kap_ref, rkp_ref, lg_ref, lb_ref, tri_ref, o_ref, s_ref) = refs

    @pl.when(pl.program_id(2) == 0)
    def _reset():
        s_ref[...] = jnp.zeros_like(s_ref)

    pairs = range(npair)
    sl = [slice(p * LANES, (p + 1) * LANES) for p in pairs]
    take = lambda ref: [ref[:, q].astype(F32) for q in sl]
    shift = C_HEAD_DIM.bit_length() - 1
    ones_b = jnp.where(jnp.right_shift(lax.broadcasted_iota(jnp.int32, (LANES, LANES), 0), shift)
                       == jnp.right_shift(lax.broadcasted_iota(jnp.int32, (LANES, LANES), 1), shift),
                       1.0, 0.0).astype(BF16)
    g_all = jnp.dot(tri_ref[...], ld_ref[...], precision=lax.Precision.HIGHEST, preferred_element_type=F32)

    r, ld, k, a, v, gate = (take(x) for x in (r_ref, ld_ref, k_ref, a_ref, v_ref, g_ref))
    kk_raw = [k[p] * kkp_ref[:, sl[p]] for p in pairs]
    sq = _head_sums([x * x for x in kk_raw], ones_b, split=False)
    kk = [kk_raw[p] / jnp.maximum(jnp.sqrt(sq[p]), KK_EPS) for p in pairs]
    bb = [kk[p] * a[p] for p in pairs]
    kh = [k[p] * (1.0 + (a[p] - 1.0) * kap_ref[:, sl[p]]) for p in pairs]
    if vres:
        vf, vg = take(vf_ref), take(vg_ref)
        v = [v[p] + (vf[p] - v[p]) * vg[p] for p in pairs]

    y, s_new = _rwkv_chunk_pairs(r, ld, kh, v, kk, bb, [g_all[:, q] for q in sl],
                                 [s_ref[p] for p in pairs], c)

    inv_n = 1.0 / C_HEAD_DIM
    mean = _head_sums(y, ones_b)
    yc = [y[p] - mean[p] * inv_n for p in pairs]
    var = _head_sums([x * x for x in yc], ones_b)
    bonus = _head_sums([r[p] * kh[p] * rkp_ref[:, sl[p]] for p in pairs], ones_b, split=False)
    for p in pairs:
        yn = yc[p] * lax.rsqrt(var[p] * inv_n + LNX_EPS) * lg_ref[:, sl[p]] + lb_ref[:, sl[p]]
        o_ref[:, sl[p]] = ((yn + bonus[p] * v[p]) * gate[p]).astype(o_ref.dtype)
        s_ref[p] = s_new[p]


RWKV_CHUNK = 64
RWKV_PAIRS = 16


def _rwkv_scan(r, log_w, k, a, v, gate, v_first, vgate, k_k, k_a, r_k, lnx_g, lnx_b, bsz, s):
    t, d = r.shape
    c = min(RWKV_CHUNK, s)
    wid = RWKV_PAIRS * LANES
    nc = s // c
    vres = v_first is not None
    tri = (jnp.arange(c)[None, :] <= jnp.arange(c)[:, None]).astype(F32)
    blk = pl.BlockSpec((c, wid), lambda b, g, i: (b * nc + i, g))
    vec = pl.BlockSpec((1, wid), lambda b, g, i: (0, g))
    rows = [r, log_w, k, a, v, gate] + ([v_first, vgate] if vres else [])
    vecs = [x.reshape(1, d) for x in (k_k, k_a, r_k, lnx_g, lnx_b)]
    return pl.pallas_call(
        functools.partial(_rwkv_scan_kernel, c=c, npair=RWKV_PAIRS, vres=vres),
        grid=(bsz, d // wid, nc),
        in_specs=[blk] * len(rows) + [vec] * len(vecs) + [pl.BlockSpec((c, c), lambda b, g, i: (0, 0))],
        out_specs=blk,
        out_shape=jax.ShapeDtypeStruct((t, d), BF16),
        scratch_shapes=[pltpu.VMEM((RWKV_PAIRS, LANES, LANES), F32)],
        compiler_params=_params("parallel", "parallel", "arbitrary"),
        name="rwkv_scan",
    )(*rows, *vecs, tri)


def _rwkv7_mixer(x, v_first, mu, w_r, w_k, w_v, w_o, w0, w1, w2, a0, a1, a2, g1, g2,
                 k_k, k_a, r_k, lnx_g, lnx_b, vres, bsz, s):
    loras = [(w1, w2, w0), (a1, a2, a0), (g1, g2, None)]
    if vres is not None:
        v0, v1, v2 = vres
        loras.append((v1, v2, v0))
    xr, xk, xv, log_decay, a, g, *rest = _shift_mix(x, mu, loras, s)
    r = _matmul(xr, w_r.astype(BF16), out_dtype=BF16)
    k = _matmul(xk, w_k.astype(BF16), out_dtype=BF16)
    v = _matmul(xv, w_v.astype(BF16), out_dtype=BF16)
    vgate = rest[0] if rest else None
    v_keep = v if vres is None else v_first
    yo = _rwkv_scan(r, log_decay, k, a, v, g, None if vres is None else v_first, vgate,
                    k_k, k_a, r_k, lnx_g, lnx_b, bsz, s)
    return _matmul(yo, w_o.astype(BF16)), v_keep


def _lane_pick(cond, lanef):
    return jnp.min(jnp.where(cond, lanef, float(LANES)), axis=1, keepdims=True)


def _route_tile(xt, w_ref, b_ref, route_ref, cnt_ref, run_ref, tm):
    @pl.when(pl.program_id(0) == 0)
    def _reset():
        run_ref[...] = jnp.zeros_like(run_ref)

    logits = jnp.dot(xt.astype(BF16), w_ref[...], preferred_element_type=F32) + b_ref[...]
    lane = lax.broadcasted_iota(jnp.int32, (tm, LANES), 1)
    lanef = lane.astype(F32)

    def masked_softmax(keep):
        z = jnp.where(keep, logits, -jnp.inf)
        e = jnp.exp(z - jnp.max(z, axis=1, keepdims=True))
        return e / jnp.sum(e, axis=1, keepdims=True)

    in_groups = lane < N_GROUPS
    pg = masked_softmax(in_groups)
    g_p = jnp.max(pg, axis=1, keepdims=True)
    g_idx = _lane_pick(jnp.where(in_groups, pg, -1.0) == g_p, lanef)
    lo = float(N_GROUPS) + float(EXPERTS_PER_GROUP) * g_idx
    half = 0.5 * float(EXPERTS_PER_GROUP - 1)
    in_group = jnp.abs(lanef - lo - half) < half + 0.5
    pe = jnp.where(in_group, masked_softmax(in_group), -1.0)
    p1 = jnp.max(pe, axis=1, keepdims=True)
    i1 = _lane_pick(pe == p1, lanef)
    pe2 = jnp.where(lanef == i1, -1.0, pe)
    p2 = jnp.max(pe2, axis=1, keepdims=True)
    i2 = _lane_pick(pe2 == p2, lanef)
    den = p1 + p2
    gate = (g_p * (p1 / den), g_p * (p2 / den))
    expert = (i1 - float(N_GROUPS), i2 - float(N_GROUPS))

    earlier = jnp.where(lax.broadcasted_iota(jnp.int32, (tm, tm), 1)
                        < lax.broadcasted_iota(jnp.int32, (tm, tm), 0), 1.0, 0.0).astype(BF16)
    run = run_ref[...]
    rank = []
    for e in expert:
        onehot = jnp.where(lanef == e, 1.0, 0.0)
        before = jnp.dot(earlier, onehot.astype(BF16), preferred_element_type=F32) + run
        rank.append(jnp.sum(onehot * before, axis=1, keepdims=True))
        run = run + jnp.sum(onehot, axis=0, keepdims=True)
    run_ref[...] = run
    cnt_ref[...] = run
    cols = (expert[0], expert[1], rank[0], rank[1], gate[0], gate[1])
    route = jnp.zeros((tm, LANES), F32)
    for c, val in enumerate(cols):
        route = jnp.where(lane == c, val, route)
    route_ref[...] = route


SLAB = 2048 // LANES
DMA_UNROLL = 4


def _slab(ref, r):
    return ref.at[pl.ds(pl.multiple_of(r * SLAB, SLAB), SLAB)]


def _slab_rows(ref, rows):
    return jnp.concatenate([ref[pl.ds(c, rows, stride=SLAB), :] for c in range(SLAB)], axis=1)


def _store_slab_rows(ref, val, rows):
    for c in range(SLAB):
        ref[pl.ds(c, rows, stride=SLAB), :] = val[:, c * LANES:(c + 1) * LANES]


def _expert_kernel(be_ref, nu_ref, idx_ref, nxt_ref, x_hbm, wg_ref, wu_ref, wd_ref, o_ref,
                   xbuf, sems, wg_b, wu_b, wd_b, *, rows):
    i = pl.program_id(0)
    n_used = nu_ref[0]
    slot = i % 2

    def token_copy(tok, r, dst_slot):
        return pltpu.make_async_copy(_slab(x_hbm, tok), _slab(xbuf.at[dst_slot], r), sems.at[dst_slot])

    def gather(ids_ref, dst_slot):
        def start(r, carry):
            for rr in (r, r + rows // 2):
                token_copy(ids_ref[0, rr], rr, dst_slot).start()
            return carry
        lax.fori_loop(0, rows // 2, start, 0, unroll=DMA_UNROLL)

    @pl.when(i == 0)
    def _():
        gather(idx_ref, 0)

    @pl.when(i + 1 < n_used)
    def _():
        gather(nxt_ref, 1 - slot)

    @pl.when(jnp.logical_or(i == 0, be_ref[i] != be_ref[jnp.maximum(i - 1, 0)]))
    def _():
        wg_b[...] = wg_ref[...].astype(BF16)
        wu_b[...] = wu_ref[...].astype(BF16)
        wd_b[...] = wd_ref[...].astype(BF16)

    @pl.when(i < n_used)
    def _():
        for r in range(rows):
            token_copy(0, r, slot).wait()
        xb = _slab_rows(xbuf.at[slot], rows).astype(BF16)
        hg = jnp.dot(xb, wg_b[...], preferred_element_type=F32)
        hu = jnp.dot(xb, wu_b[...], preferred_element_type=F32)
        hid = (hg * jax.nn.sigmoid(hg) * hu).astype(BF16)
        _store_slab_rows(o_ref, jnp.dot(hid, wd_b[...], preferred_element_type=F32), rows)

    @pl.when(i >= n_used)
    def _():
        o_ref[...] = jnp.zeros_like(o_ref)


def _expert_mlp(x_slabs, src_row, block_e, n_used, w_gate, w_up, w_down, layer, rows=MOE_ROWS):
    d, ff = w_gate.shape[2], w_gate.shape[3]
    nb = src_row.shape[0] // rows
    ids = src_row.reshape(nb, 1, rows)
    grid_spec = pltpu.PrefetchScalarGridSpec(
        num_scalar_prefetch=2,
        grid=(nb,),
        in_specs=[pl.BlockSpec((None, 1, rows), lambda i, be, nu: (i, 0, 0), memory_space=pltpu.SMEM),
                  pl.BlockSpec((None, 1, rows), lambda i, be, nu: (jnp.minimum(i + 1, nb - 1), 0, 0),
                               memory_space=pltpu.SMEM),
                  pl.BlockSpec(memory_space=pl.ANY),
                  pl.BlockSpec((None, None, d, ff), lambda i, be, nu: (layer, be[i], 0, 0)),
                  pl.BlockSpec((None, None, d, ff), lambda i, be, nu: (layer, be[i], 0, 0)),
                  pl.BlockSpec((None, None, ff, d), lambda i, be, nu: (layer, be[i], 0, 0))],
        out_specs=pl.BlockSpec((rows * SLAB, LANES), lambda i, be, nu: (i, 0)),
        scratch_shapes=[pltpu.VMEM((2, rows * SLAB, LANES), F32), pltpu.SemaphoreType.DMA((2,)),
                        pltpu.VMEM((d, ff), BF16), pltpu.VMEM((d, ff), BF16), pltpu.VMEM((ff, d), BF16)],
    )
    return pl.pallas_call(
        functools.partial(_expert_kernel, rows=rows),
        grid_spec=grid_spec,
        out_shape=jax.ShapeDtypeStruct((nb * rows * SLAB, LANES), F32),
        compiler_params=_params("arbitrary"),
        name="expert_mlp",
    )(block_e, n_used, ids, ids, x_slabs, w_gate, w_up, w_down)


def _combine_ln_kernel(idx_ref, nxt_ref, y_ref, x_ref, gt_ref, g_ref, b_ref, o_ref, buf, sems, *, rows):
    i = pl.program_id(0)
    slot = i % 2

    def row_copy(src, r, kslot, dst_slot):
        return pltpu.make_async_copy(_slab(y_ref, src), _slab(buf.at[dst_slot, kslot], r), sems.at[dst_slot])

    def gather(ids_ref, dst_slot):
        def start(r, carry):
            for kslot in range(MOE_TOPK):
                for rr in (r, r + rows // 2):
                    row_copy(ids_ref[0, MOE_TOPK * rr + kslot], rr, kslot, dst_slot).start()
            return carry
        lax.fori_loop(0, rows // 2, start, 0, unroll=DMA_UNROLL)

    @pl.when(i == 0)
    def _():
        gather(idx_ref, 0)

    @pl.when(i + 1 < pl.num_programs(0))
    def _():
        gather(nxt_ref, 1 - slot)

    for r in range(rows):
        for kslot in range(MOE_TOPK):
            row_copy(0, r, kslot, slot).wait()
    gt = gt_ref[...]
    ffn = gt[:, 0:1] * _slab_rows(buf.at[slot, 0], rows)
    for kslot in range(1, MOE_TOPK):
        ffn = ffn + gt[:, kslot:kslot + 1] * _slab_rows(buf.at[slot, kslot], rows)
    o_ref[...] = _ln_rows(DN_ALPHA * x_ref[...] + ffn, g_ref[...], b_ref[...])


def _combine_ln(y_sorted, dest, gates, x, g, b, rows=256):
    t, d = x.shape
    nb = t // rows
    row = pl.BlockSpec((rows, d), lambda i: (i, 0))
    vec = pl.BlockSpec((1, d), lambda i: (0, 0))
    ids = dest.reshape(nb, 1, rows * MOE_TOPK)
    return pl.pallas_call(
        functools.partial(_combine_ln_kernel, rows=rows),
        grid=(nb,),
        in_specs=[pl.BlockSpec((None, 1, rows * MOE_TOPK), lambda i: (i, 0, 0), memory_space=pltpu.SMEM),
                  pl.BlockSpec((None, 1, rows * MOE_TOPK), lambda i: (jnp.minimum(i + 1, nb - 1), 0, 0),
                               memory_space=pltpu.SMEM),
                  pl.BlockSpec(memory_space=pl.ANY), row,
                  pl.BlockSpec((rows, MOE_TOPK), lambda i: (i, 0)), vec, vec],
        out_specs=row,
        out_shape=jax.ShapeDtypeStruct((t, d), F32),
        scratch_shapes=[pltpu.VMEM((2, MOE_TOPK, rows * SLAB, LANES), F32), pltpu.SemaphoreType.DMA((2,))],
        compiler_params=_params("arbitrary"),
        name="combine_ln",
    )(ids, ids, y_sorted, x, gates, g.reshape(1, d), b.reshape(1, d))


def _router_params(w_grp, b_grp, w_exp_r, b_exp_r):
    n_r = N_GROUPS + N_EXPERTS
    w_router = jnp.pad(jnp.concatenate([w_grp, w_exp_r], axis=1), ((0, 0), (0, LANES - n_r))).astype(BF16)
    b_router = jnp.pad(jnp.concatenate([b_grp, b_exp_r]), (0, LANES - n_r)).reshape(1, LANES)
    return w_router, b_router


def _moe_layer(x, x_slabs, route, cnt, w_gate, w_up, w_down, ln_g, ln_b, layer):
    t, d = x.shape
    flat_e = route[:, 0:MOE_TOPK].astype(jnp.int32).reshape(-1)
    rank = route[:, MOE_TOPK:2 * MOE_TOPK].astype(jnp.int32).reshape(-1)
    gates = route[:, 2 * MOE_TOPK:3 * MOE_TOPK]
    counts = cnt[0, :N_EXPERTS].astype(jnp.int32)
    m = t * MOE_TOPK
    padded = (counts + MOE_ROWS - 1) // MOE_ROWS * MOE_ROWS
    pad_end = jnp.cumsum(padded)
    dest = (pad_end - padded)[flat_e] + rank
    n_blocks = (m + N_EXPERTS * (MOE_ROWS - 1) + MOE_ROWS - 1) // MOE_ROWS
    src_row = jnp.zeros((n_blocks * MOE_ROWS,), jnp.int32).at[dest].set(
        (jnp.arange(m) // MOE_TOPK).astype(jnp.int32))
    starts = jnp.arange(n_blocks, dtype=jnp.int32) * MOE_ROWS
    block_e = jnp.minimum(jnp.sum((pad_end[None, :] <= starts[:, None]).astype(jnp.int32), axis=1),
                          N_EXPERTS - 1).astype(jnp.int32)
    n_used = (pad_end[-1] // MOE_ROWS).astype(jnp.int32).reshape(1)

    ys = _expert_mlp(x_slabs, src_row, block_e, n_used, w_gate, w_up, w_down, layer)
    return _combine_ln(ys, dest.astype(jnp.int32), gates, x, ln_g, ln_b)


def kernel(x, ev_w_in, ev_gate_bias, ev_conv_w, ev_conv_b, ev_hnorm_g, ev_w_out, od_mu, od_w_r, od_w_k, od_w_v, od_w_o, od_w0, od_w1, od_w2, od_a0, od_a1, od_a2, od_g1, od_g2, od_k_k, od_k_a, od_r_k, od_lnx_g, od_lnx_b, od_v0, od_v1, od_v2, ln_mix_g, ln_mix_b, ln_ffn_g, ln_ffn_b, moe_w_grp, moe_b_grp, moe_w_exp_r, moe_b_exp_r, moe_w_gate, moe_w_up, moe_w_down):
    bsz, s, d = x.shape
    depth = ln_mix_g.shape[0]
    xf = x.reshape(bsz * s, d)
    v_first = None
    for layer in range(depth):
        if layer % 2 == 0:
            e = layer // 2
            mix = _even_mixer(xf, ev_w_in[e], ev_gate_bias[e], ev_conv_w[e], ev_conv_b[e],
                              ev_hnorm_g[e], ev_w_out[e], bsz, s)
        else:
            o = layer // 2
            vres = None if o == 0 else (od_v0[o - 1], od_v1[o - 1], od_v2[o - 1])
            mix, v_first = _rwkv7_mixer(xf, v_first, od_mu[o], od_w_r[o], od_w_k[o], od_w_v[o],
                                        od_w_o[o], od_w0[o], od_w1[o], od_w2[o], od_a0[o],
                                        od_a1[o], od_a2[o], od_g1[o], od_g2[o], od_k_k[o],
                                        od_k_a[o], od_r_k[o], od_lnx_g[o], od_lnx_b[o], vres, bsz, s)
        xf, x_slabs, route, cnt = _resid_ln(xf, mix, ln_mix_g[layer], ln_mix_b[layer],
                                            *_router_params(moe_w_grp[layer], moe_b_grp[layer],
                                                            moe_w_exp_r[layer], moe_b_exp_r[layer]))
        xf = _moe_layer(xf, x_slabs, route, cnt, moe_w_gate, moe_w_up, moe_w_down,
                        ln_ffn_g[layer], ln_ffn_b[layer], layer)
    return xf.reshape(bsz, s, d)
```

```python
import functools
import math

import jax
import jax.numpy as jnp
from jax import lax
from jax.experimental import pallas as pl
from jax.experimental.pallas import tpu as pltpu

F32 = jnp.float32
BF16 = jnp.bfloat16

DEPTH = 4
DN_ALPHA = (2 * DEPTH) ** 0.25
LN_EPS = 1e-5

A_HEADS = 8
A_HEAD_DIM = 128
A_WIDTH = A_HEADS * A_HEAD_DIM
MOBA_BLOCK = 256
MOBA_TOPK = 3
MOBA_GROUP = 4
MOBA_HEADS = 4
ROPE_THETA = 500000.0
ROPE_DIM = A_HEAD_DIM // 4
NEG_INF = -1e30

B_HEADS = 4
B_HEAD_DIM = 256
B_WIDTH = B_HEADS * B_HEAD_DIM
MLSTM_CHUNK = 64
B_CONV = 4
HNORM_EPS = 1e-6
EVEN_MAIN = 3 * A_WIDTH + 4 * B_WIDTH

C_HEAD_DIM = 64
LNX_EPS = 64e-5
KK_EPS = 1e-12

N_GROUPS = 4
EXPERTS_PER_GROUP = 8
N_EXPERTS = N_GROUPS * EXPERTS_PER_GROUP
MOE_TOPK = 2
MOE_ROWS = 256

LANES = 128
SUBLANES = 8
VMEM_LIMIT = 56 * 1024 * 1024

_NT = (((1,), (1,)), ((), ()))
_TN = (((0,), (0,)), ((), ()))


def _params(*sem):
    return pltpu.CompilerParams(dimension_semantics=sem, vmem_limit_bytes=VMEM_LIMIT)


def _mm_kernel(x_ref, w_ref, o_ref):
    o_ref[...] = jnp.dot(x_ref[...].astype(BF16), w_ref[...],
                         preferred_element_type=F32).astype(o_ref.dtype)


LHS_TILE_BYTES = 8 * 1024 * 1024


def _row_tile(m, row_bytes):
    tm = 1 << ((LHS_TILE_BYTES // row_bytes).bit_length() - 1)
    return min(tm, m)


def _matmul(x, w, out_dtype=F32, tn=512):
    m, k = x.shape
    n = w.shape[1]
    tm, tn = _row_tile(m, k * x.dtype.itemsize), min(tn, n)
    return pl.pallas_call(
        _mm_kernel,
        grid=(m // tm, n // tn),
        in_specs=[pl.BlockSpec((tm, k), lambda i, j: (i, 0)),
                  pl.BlockSpec((k, tn), lambda i, j: (0, j))],
        out_specs=pl.BlockSpec((tm, tn), lambda i, j: (i, j)),
        out_shape=jax.ShapeDtypeStruct((m, n), out_dtype),
        compiler_params=_params("parallel", "arbitrary"),
        name="matmul",
    )(x, w)


def _ln_rows(h, g, b):
    mu = jnp.mean(h, axis=-1, keepdims=True)
    hc = h - mu
    var = jnp.mean(hc * hc, axis=-1, keepdims=True)
    return hc * lax.rsqrt(var + LN_EPS) * g + b


def _resid_ln_kernel(*refs, n_in, tm):
    x_ref, ys, ws = refs[0], refs[1:1 + n_in], refs[1 + n_in:1 + 2 * n_in]
    g_ref, b_ref, w_ref, br_ref, o_ref, s_ref, route_ref, cnt_ref, run_ref = refs[1 + 2 * n_in:]
    mix = jnp.dot(ys[0][...], ws[0][...], preferred_element_type=F32)
    for y_ref, wo_ref in zip(ys[1:], ws[1:]):
        mix = mix + jnp.dot(y_ref[...], wo_ref[...], preferred_element_type=F32)
    y = _ln_rows(DN_ALPHA * x_ref[...] + mix, g_ref[...], b_ref[...])
    o_ref[...] = y.astype(o_ref.dtype)
    _store_slab_rows(s_ref, y, tm)
    _route_tile(y, w_ref, br_ref, route_ref, cnt_ref, run_ref, tm)


def _resid_ln(x, ys, ws, g, b, w_router, b_router, tm=256):
    t, d = x.shape
    tm = min(tm, t)
    row = pl.BlockSpec((tm, d), lambda i: (i, 0))
    vec = pl.BlockSpec((1, d), lambda i: (0, 0))
    one = pl.BlockSpec((1, LANES), lambda i: (0, 0))
    return pl.pallas_call(
        functools.partial(_resid_ln_kernel, n_in=len(ys), tm=tm),
        grid=(t // tm,),
        in_specs=[row] + [pl.BlockSpec((tm, y.shape[1]), lambda i: (i, 0)) for y in ys]
                 + [pl.BlockSpec(w.shape, lambda i: (0, 0)) for w in ws]
                 + [vec, vec, pl.BlockSpec((d, LANES), lambda i: (0, 0)), one],
        out_specs=[row, pl.BlockSpec((tm * SLAB, LANES), lambda i: (i, 0)),
                   pl.BlockSpec((tm, LANES), lambda i: (i, 0)), one],
        out_shape=[jax.ShapeDtypeStruct((t, d), F32), jax.ShapeDtypeStruct((t * SLAB, LANES), F32),
                   jax.ShapeDtypeStruct((t, LANES), F32), jax.ShapeDtypeStruct((1, LANES), F32)],
        scratch_shapes=[pltpu.VMEM((1, LANES), F32)],
        compiler_params=_params("arbitrary"),
        name="resid_ln_router",
    )(x, *ys, *ws, g.reshape(1, d), b.reshape(1, d), w_router, b_router)


def _rope_tables(s):
    half = ROPE_DIM // 2
    inv = jnp.power(jnp.float32(ROPE_THETA), -jnp.arange(half, dtype=F32) / half)
    ang = jnp.arange(s).astype(F32)[:, None] * inv[None, :]
    cos, sin = jnp.cos(ang), jnp.sin(ang)
    zero = jnp.zeros((s, A_HEAD_DIM - ROPE_DIM), F32)
    zh = jnp.zeros((s, half), F32)
    c = jnp.concatenate([cos, cos, zero + 1.0], axis=-1)
    s_up = jnp.concatenate([zh, sin, zero], axis=-1)
    s_dn = jnp.concatenate([-sin, zh, zero], axis=-1)
    return c, s_up, s_dn


def _rope(x, c, s_up, s_dn):
    half = ROPE_DIM // 2
    return (x * c + pltpu.roll(x, half, axis=1) * s_up
            + pltpu.roll(x, A_HEAD_DIM - half, axis=1) * s_dn)


def _moba_kernel(q_ref, k_ref, v_ref, c_ref, su_ref, sd_ref, o_ref, krot_ref, vb_ref, kmean_ref, *, nblk):
    qi = pl.program_id(2)
    blk = MOBA_BLOCK
    dh = A_HEAD_DIM
    scale = dh ** -0.5
    heads = range(MOBA_HEADS)
    hs = [slice(h * dh, (h + 1) * dh) for h in heads]

    @pl.when(qi == 0)
    def _prepare_keys():
        def body(j, carry):
            rows = pl.ds(pl.multiple_of(j * blk, blk), blk)
            block_tag = jnp.where(lax.broadcasted_iota(jnp.int32, (blk, LANES), 1) == j, 1.0, 0.0)
            for h in heads:
                kr = _rope(k_ref[rows, hs[h]].astype(F32), c_ref[rows, :], su_ref[rows, :], sd_ref[rows, :])
                krot_ref[h, rows, :] = jnp.concatenate([kr, block_tag], axis=1).astype(BF16)
                vb_ref[h, rows, :] = v_ref[rows, hs[h]].astype(BF16)
                kmean_ref[h, pl.ds(j, 1), :] = jnp.mean(kr, axis=0, keepdims=True)
            return carry
        lax.fori_loop(0, nblk, body, 0)

    qrows = pl.ds(pl.multiple_of(qi * blk, blk), blk)
    qb = [_rope(q_ref[:, hs[h]].astype(F32), c_ref[qrows, :], su_ref[qrows, :], sd_ref[qrows, :]).astype(BF16)
          for h in heads]

    brow = lax.broadcasted_iota(jnp.int32, (nblk, blk), 0)
    g = [jnp.where(brow < qi, lax.dot_general(kmean_ref[h].astype(BF16), qb[h], _NT,
                                              preferred_element_type=F32), -jnp.inf) for h in heads]
    sel_t = [jnp.zeros((nblk, blk), F32) for _ in heads]
    for j in range(nblk):
        earlier = jnp.where(brow < j, 1.0, 0.0)
        for h in heads:
            gj = g[h][j:j + 1, :]
            beats = jnp.where(g[h] > gj, 1.0, jnp.where(g[h] == gj, earlier, 0.0))
            cnt = jnp.sum(beats, axis=0, keepdims=True)
            sel_t[h] = jnp.where(brow == j, jnp.where(cnt < float(min(MOBA_TOPK, nblk)), 1.0, 0.0), sel_t[h])
    eye = jnp.where(lax.broadcasted_iota(jnp.int32, (blk, blk), 0)
                    == lax.broadcasted_iota(jnp.int32, (blk, blk), 1), 1.0, 0.0).astype(BF16)
    pad = jnp.zeros((LANES - nblk, blk), F32)
    sel = [lax.dot_general(eye, jnp.concatenate([jnp.where(brow < qi, sel_t[h], 0.0), pad], axis=0).astype(BF16),
                           _NT, preferred_element_type=F32) for h in heads]
    lane = lax.broadcasted_iota(jnp.int32, (blk, LANES), 1)
    off = jnp.where(lane < nblk, NEG_INF / scale, 0.0)
    q_aug = [jnp.concatenate([qb[h], jnp.where(sel[h] > 0.5, 0.0, off).astype(BF16)], axis=1) for h in heads]

    causal = (lax.broadcasted_iota(jnp.int32, (blk, blk), 1)
              <= lax.broadcasted_iota(jnp.int32, (blk, blk), 0))
    s = [jnp.where(causal, lax.dot_general(qb[h], krot_ref[h, qrows, :dh], _NT,
                                           preferred_element_type=F32) * scale, NEG_INF) for h in heads]
    m0 = [jnp.max(s[h], axis=1, keepdims=True) for h in heads]
    p = [jnp.exp(s[h] - m0[h]) for h in heads]
    l0 = [jnp.sum(p[h], axis=1, keepdims=True) for h in heads]
    acc0 = [jnp.dot(p[h].astype(BF16), vb_ref[h, qrows, :], preferred_element_type=F32) for h in heads]

    span = MOBA_GROUP * blk

    def past_group(jg, carry):
        m, l, acc = carry
        rows = pl.ds(pl.multiple_of(jg * span, span), span)
        sj = [lax.dot_general(q_aug[h], krot_ref[h, rows, :], _NT, preferred_element_type=F32) * scale
              for h in heads]
        m_new = [jnp.maximum(m[h], jnp.max(sj[h], axis=1, keepdims=True)) for h in heads]
        a = [jnp.exp(m[h] - m_new[h]) for h in heads]
        pj = [jnp.exp(sj[h] - m_new[h]) for h in heads]
        l = [a[h] * l[h] + jnp.sum(pj[h], axis=1, keepdims=True) for h in heads]
        acc = [a[h] * acc[h] + jnp.dot(pj[h].astype(BF16), vb_ref[h, rows, :], preferred_element_type=F32)
               for h in heads]
        return m_new, l, acc

    n_groups = (qi + MOBA_GROUP - 1) // MOBA_GROUP
    _, l, acc = lax.fori_loop(0, n_groups, past_group, (m0, l0, acc0))
    for h in heads:
        o_ref[:, hs[h]] = (acc[h] / l[h]).astype(o_ref.dtype)


def _moba(z, bsz, s):
    nblk = s // MOBA_BLOCK
    c, s_up, s_dn = _rope_tables(s)
    wid = MOBA_HEADS * A_HEAD_DIM
    ncol = A_WIDTH // wid
    tab = pl.BlockSpec((s, A_HEAD_DIM), lambda b, h, qi: (0, 0))
    return pl.pallas_call(
        functools.partial(_moba_kernel, nblk=nblk),
        grid=(bsz, ncol, nblk),
        in_specs=[pl.BlockSpec((MOBA_BLOCK, wid), lambda b, h, qi: (b * nblk + qi, h)),
                  pl.BlockSpec((s, wid), lambda b, h, qi: (b, ncol + h)),
                  pl.BlockSpec((s, wid), lambda b, h, qi: (b, 2 * ncol + h)),
                  tab, tab, tab],
        out_specs=pl.BlockSpec((MOBA_BLOCK, wid), lambda b, h, qi: (b * nblk + qi, h)),
        out_shape=jax.ShapeDtypeStruct((bsz * s, A_WIDTH), BF16),
        scratch_shapes=[pltpu.VMEM((MOBA_HEADS, s, A_HEAD_DIM + LANES), BF16),
                        pltpu.VMEM((MOBA_HEADS, s, A_HEAD_DIM), BF16),
                        pltpu.VMEM((MOBA_HEADS, nblk, A_HEAD_DIM), F32)],
        compiler_params=_params("parallel", "parallel", "arbitrary"),
        name="moba",
    )(z, z, z, c, s_up, s_dn)


def _log_sigmoid(x):
    return jnp.minimum(x, 0.0) - jnp.log1p(jnp.exp(-jnp.abs(x)))


def _shift_rows(x, prev8, d):
    if d == 0:
        return x
    rolled = pltpu.roll(x, d, axis=0)
    top = jnp.where(lax.broadcasted_iota(jnp.int32, prev8.shape, 0) < d,
                    pltpu.roll(prev8, d, axis=0), rolled[:SUBLANES])
    return jnp.concatenate([top, rolled[SUBLANES:]], axis=0)


def _mlstm_kernel(q_ref, k_ref, v_ref, o_ref, gc_ref, gr_ref, bc_ref, br_ref,
                  cwq_ref, cwk_ref, cbq_ref, cbk_ref, hg_ref, y_ref,
                  c_ref, n_ref, m_ref, pq_ref, pk_ref, *, tt):
    L = MLSTM_CHUNK
    dk = B_HEAD_DIM
    heads = range(B_HEADS)
    hs = [slice(h * dk, (h + 1) * dk) for h in heads]

    @pl.when(pl.program_id(1) == 0)
    def _reset():
        c_ref[...] = jnp.zeros_like(c_ref)
        n_ref[...] = jnp.zeros_like(n_ref)
        m_ref[...] = jnp.zeros_like(m_ref)
        pq_ref[...] = jnp.zeros_like(pq_ref)
        pk_ref[...] = jnp.zeros_like(pk_ref)

    def conv_silu(x_ref, p_ref, w_ref, b_ref):
        x = x_ref[...].astype(F32)
        prev8 = p_ref[...]
        out = b_ref[...] + x * w_ref[B_CONV - 1:B_CONV, :]
        for d in range(1, B_CONV):
            out = out + _shift_rows(x, prev8, d) * w_ref[B_CONV - 1 - d:B_CONV - d, :]
        p_ref[...] = x[tt - SUBLANES:, :]
        return out * jax.nn.sigmoid(out)

    q_all = conv_silu(q_ref, pq_ref, cwq_ref, cbq_ref)
    k_all = conv_silu(k_ref, pk_ref, cwk_ref, cbk_ref) * (dk ** -0.5)

    gcol = gc_ref[...] + bc_ref[...]
    grow = gr_ref[...] + br_ref[...]
    i_col_all = [gcol[:, h:h + 1] for h in heads]
    f_col_all = [_log_sigmoid(gcol[:, B_HEADS + h:B_HEADS + h + 1]) for h in heads]
    i_row_all = [grow[h:h + 1, :] for h in heads]
    f_row_all = [_log_sigmoid(grow[B_HEADS + h:B_HEADS + h + 1, :]) for h in heads]

    rr = lax.broadcasted_iota(jnp.int32, (L, L), 0)
    cc = lax.broadcasted_iota(jnp.int32, (L, L), 1)
    tri = cc <= rr

    for c in range(tt // L):
        lo, hi = c * L, (c + 1) * L
        qc = [q_all[lo:hi, hs[h]] for h in heads]
        kc = [k_all[lo:hi, hs[h]] for h in heads]
        qcb = [x.astype(BF16) for x in qc]
        kcb = [x.astype(BF16) for x in kc]
        vcb = [v_ref[lo:hi, hs[h]].astype(BF16) for h in heads]
        i_col = [x[lo:hi] for x in i_col_all]
        f_col = [x[lo:hi] for x in f_col_all]
        i_row = [x[:, lo:hi] for x in i_row_all]
        f_row = [x[:, lo:hi] for x in f_row_all]
        m_prev = [m_ref[h] for h in heads]
        cst = [c_ref[h] for h in heads]
        nst = [n_ref[h] for h in heads]

        b_col = [jnp.sum(jnp.where(tri, f_row[h], 0.0), axis=1, keepdims=True) for h in heads]
        b_row = [jnp.sum(jnp.where(rr <= cc, f_col[h], 0.0), axis=0, keepdims=True) for h in heads]
        log_inter = [b_col[h] + m_prev[h] for h in heads]
        log_intra = [jnp.where(tri, b_col[h] - b_row[h] + i_row[h], -jnp.inf) for h in heads]
        m_t = [jnp.maximum(log_inter[h], jnp.max(log_intra[h], axis=1, keepdims=True)) for h in heads]
        w_inter = [jnp.exp(log_inter[h] - m_t[h]) for h in heads]
        qk = [lax.dot_general(qcb[h], kcb[h], _NT, preferred_element_type=F32)
              * jnp.exp(log_intra[h] - m_t[h]) for h in heads]
        q_c = [jnp.dot(qcb[h], cst[h].astype(BF16), preferred_element_type=F32) for h in heads]
        qk_v = [jnp.dot(qk[h].astype(BF16), vcb[h], preferred_element_type=F32) for h in heads]
        den = [w_inter[h] * jnp.sum(qc[h] * nst[h], axis=1, keepdims=True)
               + jnp.sum(qk[h], axis=1, keepdims=True) for h in heads]
        hc = [(w_inter[h] * q_c[h] + qk_v[h]) / jnp.maximum(jnp.abs(den[h]), jnp.exp(-m_t[h])) for h in heads]

        b_last = [jnp.sum(f_col[h], axis=0, keepdims=True) for h in heads]
        log_s = [b_last[h] - b_col[h] + i_col[h] for h in heads]
        m_new = [jnp.maximum(b_last[h] + m_prev[h], jnp.max(log_s[h], axis=0, keepdims=True)) for h in heads]
        decay = [jnp.exp(b_last[h] + m_prev[h] - m_new[h]) for h in heads]
        kw = [kc[h] * jnp.exp(log_s[h] - m_new[h]) for h in heads]
        kv = [lax.dot_general(kw[h].astype(BF16), vcb[h], _TN, preferred_element_type=F32) for h in heads]
        for h in heads:
            c_ref[h] = decay[h] * cst[h] + kv[h]
            n_ref[h] = decay[h] * nst[h] + jnp.sum(kw[h], axis=0, keepdims=True)
            m_ref[h] = m_new[h]
            hn = hc[h] * lax.rsqrt(jnp.mean(hc[h] * hc[h], axis=1, keepdims=True) + HNORM_EPS)
            y_ref[lo:hi, hs[h]] = (hn * hg_ref[:, hs[h]]
                                   * jax.nn.sigmoid(o_ref[lo:hi, hs[h]].astype(F32))).astype(y_ref.dtype)


def _mlstm(z, gates, gate_bias, conv_w, conv_b, hnorm_g, bsz, s, tt=256):
    nt = s // tt
    cb = 3 * A_WIDTH // B_WIDTH
    gates_t = gates.T
    bias = gate_bias.reshape(1, 2 * B_HEADS)

    def col(off):
        return pl.BlockSpec((tt, B_WIDTH), lambda b, t: (b * nt + t, cb + off))

    def par(rows, off):
        return pl.BlockSpec((rows, B_WIDTH), lambda b, t: (0, off))

    return pl.pallas_call(
        functools.partial(_mlstm_kernel, tt=tt),
        grid=(bsz, nt),
        in_specs=[col(0), col(1), col(2), col(3),
                  pl.BlockSpec((tt, 2 * B_HEADS), lambda b, t: (b * nt + t, 0)),
                  pl.BlockSpec((2 * B_HEADS, tt), lambda b, t: (0, b * nt + t)),
                  pl.BlockSpec((1, 2 * B_HEADS), lambda b, t: (0, 0)),
                  pl.BlockSpec((2 * B_HEADS, 1), lambda b, t: (0, 0)),
                  par(B_CONV, 0), par(B_CONV, 1), par(1, 0), par(1, 1), par(1, 0)],
        out_specs=pl.BlockSpec((tt, B_WIDTH), lambda b, t: (b * nt + t, 0)),
        out_shape=jax.ShapeDtypeStruct((bsz * s, B_WIDTH), BF16),
        scratch_shapes=[pltpu.VMEM((B_HEADS, B_HEAD_DIM, B_HEAD_DIM), F32),
                        pltpu.VMEM((B_HEADS, 1, B_HEAD_DIM), F32), pltpu.VMEM((B_HEADS, 1, 1), F32),
                        pltpu.VMEM((SUBLANES, B_WIDTH), F32), pltpu.VMEM((SUBLANES, B_WIDTH), F32)],
        compiler_params=_params("parallel", "arbitrary"),
        name="mlstm",
    )(z, z, z, z, gates, gates_t, bias, bias.T, conv_w, conv_w,
      conv_b.reshape(1, -1), conv_b.reshape(1, -1), hnorm_g.reshape(1, -1))


def _even_mixer(x, w_in, gate_bias, conv_w, conv_b, hnorm_g, w_out, bsz, s):
    d = x.shape[1]
    w_main = w_in[:, :EVEN_MAIN].astype(BF16)
    w_gate = jnp.pad(w_in[:, EVEN_MAIN:], ((0, 0), (0, LANES - 2 * B_HEADS))).astype(BF16)
    z = _matmul(x, w_main, out_dtype=BF16)
    gates = _matmul(x, w_gate, tn=LANES)[:, :2 * B_HEADS]
    y_a = _moba(z, bsz, s)
    y_b = _mlstm(z, gates, gate_bias, conv_w, conv_b, hnorm_g, bsz, s)
    w_o = w_out.astype(BF16)
    return [y_a, y_b], [w_o[:A_WIDTH], w_o[A_WIDTH:]]


def _softplus(x):
    return jnp.maximum(x, 0.0) + jnp.log1p(jnp.exp(-jnp.abs(x)))


def _lora_apply(xb, w1_ref, w2_ref, b_ref, mid, out):
    hmid = jnp.dot(xb, w1_ref[...], preferred_element_type=F32)
    if mid == "tanh":
        hmid = jnp.tanh(hmid)
    elif mid == "sigmoid":
        hmid = jax.nn.sigmoid(hmid)
    y = jnp.dot(hmid.astype(BF16), w2_ref[...], preferred_element_type=F32)
    if out == "logdecay":
        y = -jnp.exp(-_softplus(-(b_ref[...] + y)) - 0.5)
    elif out == "sigmoid":
        y = jax.nn.sigmoid(b_ref[...] + y)
    return y


MIX_R, MIX_W, MIX_K, MIX_V, MIX_A, MIX_G = range(6)


def _shift_mix_kernel(*refs, tiles_per_seq, vres):
    x_ref, p_ref, mu_ref = refs[:3]
    n_lora = 4 if vres else 3
    lora = [refs[3 + 3 * j:6 + 3 * j] for j in range(n_lora)]
    outs = refs[3 + 3 * n_lora:]
    xr_o, xk_o, xv_o, ld_o, a_o, g_o = outs[:6]
    x = x_ref[...]
    first = (pl.program_id(0) % tiles_per_seq) == 0
    prev_row = jnp.where(first, 0.0, p_ref[SUBLANES - 1:SUBLANES, :])
    row = lax.broadcasted_iota(jnp.int32, x.shape, 0)
    dx = jnp.where(row == 0, prev_row, pltpu.roll(x, 1, axis=0)) - x

    def mixed(j):
        return (x + dx * mu_ref[j:j + 1, :]).astype(BF16)

    xr_o[...] = mixed(MIX_R)
    xk_o[...] = mixed(MIX_K)
    xv = mixed(MIX_V)
    xv_o[...] = xv
    ld_o[...] = _lora_apply(mixed(MIX_W), *lora[0], "tanh", "logdecay")
    a_o[...] = _lora_apply(mixed(MIX_A), *lora[1], "none", "sigmoid").astype(a_o.dtype)
    g_o[...] = _lora_apply(mixed(MIX_G), *lora[2], "sigmoid", "none").astype(g_o.dtype)
    if vres:
        outs[6][...] = _lora_apply(xv, *lora[3], "none", "sigmoid").astype(outs[6].dtype)


def _shift_mix(x, mu, loras, s, tm=256):
    t, d = x.shape
    tm = min(tm, t)
    n = mu.shape[0]
    vres = len(loras) == 4
    row = pl.BlockSpec((tm, d), lambda i: (i, 0))
    prev = pl.BlockSpec((SUBLANES, d), lambda i: (jnp.maximum(i * (tm // SUBLANES) - 1, 0), 0))
    args, specs = [], []
    for w1, w2, bias in loras:
        r = w1.shape[1]
        rp = -(-r // LANES) * LANES
        dout = w2.shape[1]
        args += [jnp.pad(w1, ((0, 0), (0, rp - r))).astype(BF16), jnp.pad(w2, ((0, rp - r), (0, 0))).astype(BF16),
                 (jnp.zeros((dout,), F32) if bias is None else bias).reshape(1, dout)]
        specs += [pl.BlockSpec((d, rp), lambda i: (0, 0)), pl.BlockSpec((rp, dout), lambda i: (0, 0)),
                  pl.BlockSpec((1, dout), lambda i: (0, 0))]
    dtypes = [BF16, BF16, BF16, F32, BF16, BF16] + ([BF16] if vres else [])
    return pl.pallas_call(
        functools.partial(_shift_mix_kernel, tiles_per_seq=s // tm, vres=vres),
        grid=(t // tm,),
        in_specs=[row, prev, pl.BlockSpec((n, d), lambda i: (0, 0))] + specs,
        out_specs=[row] * len(dtypes),
        out_shape=[jax.ShapeDtypeStruct((t, d), dt) for dt in dtypes],
        compiler_params=_params("parallel"),
        name="shift_mix_lora",
    )(x, x, mu, *args)


def _head_sums(tiles, ones_b, split=True):
    c = tiles[0].shape[0]
    x = jnp.concatenate(tiles, axis=0)
    hi = x.astype(BF16)
    tot = jnp.dot(hi, ones_b, preferred_element_type=F32)
    if split:
        lo = (x - hi.astype(F32)).astype(BF16)
        tot = tot + jnp.dot(lo, ones_b, preferred_element_type=F32)
    return [tot[p * c:(p + 1) * c] for p in range(len(tiles))]


def _bdot(a, b):
    return jnp.dot(a.astype(BF16), b.astype(BF16), preferred_element_type=F32)


def _rwkv_chunk_pairs(r, ld, kh, v, kk, bb, g_cum, s0, c):
    pairs = range(len(r))
    lane = lax.broadcasted_iota(jnp.int32, (1, LANES), 1)
    m0 = jnp.where(lane < C_HEAD_DIM, 1.0, 0.0)
    m1 = 1.0 - m0
    n2 = 2 * c
    row = lax.broadcasted_iota(jnp.int32, (n2, n2), 0)
    col = lax.broadcasted_iota(jnp.int32, (n2, n2), 1)
    lower, lower_eq = col < row, col <= row

    def stack(x):
        return jnp.concatenate([x * m0, x * m1], axis=0)

    e_g = [jnp.exp(g_cum[p]) for p in pairs]
    e_ng = [jnp.exp(-g_cum[p]) for p in pairs]
    al2 = [stack(-(kk[p] * jnp.exp(g_cum[p] - ld[p]))) for p in pairs]
    be2 = [stack(bb[p] * e_ng[p]).astype(BF16) for p in pairs]
    kb2 = [stack(kh[p] * e_ng[p]).astype(BF16) for p in pairs]
    rb = [r[p] * e_g[p] for p in pairs]
    v2 = [stack(v[p]) for p in pairs]
    sc = [lax.dot_general(jnp.concatenate([al2[p], stack(rb[p])], axis=0).astype(BF16),
                          jnp.concatenate([be2[p], kb2[p]], axis=0), _NT,
                          preferred_element_type=F32) for p in pairs]
    l_ab = [jnp.where(lower, sc[p][:n2, :n2], 0.0) for p in pairs]
    l_ak = [jnp.where(lower, sc[p][:n2, n2:], 0.0) for p in pairs]
    l_r = [jnp.concatenate([jnp.where(lower_eq, sc[p][n2:, :n2], 0.0),
                            jnp.where(lower_eq, sc[p][n2:, n2:], 0.0)], axis=1).astype(BF16) for p in pairs]

    t_off = l_ab
    pw = [x.astype(BF16) for x in l_ab]
    for _ in range(max(c.bit_length() - 2, 0)):
        pw_f = [jnp.dot(pw[p], pw[p], preferred_element_type=F32) for p in pairs]
        pw = [x.astype(BF16) for x in pw_f]
        t_off = [t_off[p] + pw_f[p] + jnp.dot(t_off[p].astype(BF16), pw[p], preferred_element_type=F32)
                 for p in pairs]

    w0 = [jnp.concatenate([al2[p], _bdot(l_ak[p], v2[p])], axis=1) for p in pairs]
    x = [w0[p] + _bdot(t_off[p], w0[p]) for p in pairs]
    xb = [x[p].astype(BF16) for p in pairs]
    z = [jnp.dot(l_r[p],
                 jnp.concatenate([xb[p], jnp.concatenate([jnp.zeros_like(v2[p]), v2[p]], axis=1).astype(BF16)],
                                 axis=0), preferred_element_type=F32) for p in pairs]
    r_eff = [(rb[p] + z[p][:c, :LANES] + z[p][c:, :LANES]).astype(BF16) for p in pairs]
    y0 = [z[p][:c, LANES:] + z[p][c:, LANES:] for p in pairs]
    mn = [lax.dot_general(xb[p], be2[p], _TN, preferred_element_type=F32) for p in pairs]
    n_add = [mn[p][LANES:] + lax.dot_general(v2[p].astype(BF16), kb2[p], _TN, preferred_element_type=F32)
             for p in pairs]
    s0b = [s0[p].astype(BF16) for p in pairs]
    y = [y0[p] + lax.dot_general(r_eff[p], s0b[p], _NT, preferred_element_type=F32) for p in pairs]
    s_new = [(s0[p] + jnp.dot(s0b[p], mn[p][:LANES].astype(BF16), preferred_element_type=F32) + n_add[p])
             * e_g[p][c - 1:c, :] for p in pairs]
    return y, s_new


def _rwkv_scan_kernel(*refs, c, npair, vres):
    if vres:
        (r_ref, ld_ref, k_ref, a_ref, v_ref, g_ref, vf_ref, vg_ref,
         kkp_ref, kap_ref, rkp_ref, lg_ref, lb_ref, tri_ref, o_ref, s_ref) = refs
    else:
        (r_ref, ld_ref, k_ref, a_ref, v_ref, g_ref,
         kkp_ref, kap_ref, rkp_ref, lg_ref, lb_ref, tri_ref, o_ref, s_ref) = refs

    @pl.when(pl.program_id(2) == 0)
    def _reset():
        s_ref[...] = jnp.zeros_like(s_ref)

    pairs = range(npair)
    sl = [slice(p * LANES, (p + 1) * LANES) for p in pairs]
    take = lambda ref: [ref[:, q].astype(F32) for q in sl]
    shift = C_HEAD_DIM.bit_length() - 1
    ones_b = jnp.where(jnp.right_shift(lax.broadcasted_iota(jnp.int32, (LANES, LANES), 0), shift)
                       == jnp.right_shift(lax.broadcasted_iota(jnp.int32, (LANES, LANES), 1), shift),
                       1.0, 0.0).astype(BF16)
    g_all = jnp.dot(tri_ref[...], ld_ref[...], precision=lax.Precision.HIGHEST, preferred_element_type=F32)

    r, ld, k, a, v, gate = (take(x) for x in (r_ref, ld_ref, k_ref, a_ref, v_ref, g_ref))
    kk_raw = [k[p] * kkp_ref[:, sl[p]] for p in pairs]
    sq = _head_sums([x * x for x in kk_raw], ones_b, split=False)
    kk = [kk_raw[p] / jnp.maximum(jnp.sqrt(sq[p]), KK_EPS) for p in pairs]
    bb = [kk[p] * a[p] for p in pairs]
    kh = [k[p] * (1.0 + (a[p] - 1.0) * kap_ref[:, sl[p]]) for p in pairs]
    if vres:
        vf, vg = take(vf_ref), take(vg_ref)
        v = [v[p] + (vf[p] - v[p]) * vg[p] for p in pairs]

    y, s_new = _rwkv_chunk_pairs(r, ld, kh, v, kk, bb, [g_all[:, q] for q in sl],
                                 [s_ref[p] for p in pairs], c)

    inv_n = 1.0 / C_HEAD_DIM
    mean = _head_sums(y, ones_b)
    yc = [y[p] - mean[p] * inv_n for p in pairs]
    var = _head_sums([x * x for x in yc], ones_b)
    bonus = _head_sums([r[p] * kh[p] * rkp_ref[:, sl[p]] for p in pairs], ones_b, split=False)
    for p in pairs:
        yn = yc[p] * lax.rsqrt(var[p] * inv_n + LNX_EPS) * lg_ref[:, sl[p]] + lb_ref[:, sl[p]]
        o_ref[:, sl[p]] = ((yn + bonus[p] * v[p]) * gate[p]).astype(o_ref.dtype)
        s_ref[p] = s_new[p]


RWKV_CHUNK = 64
RWKV_PAIRS = 16


def _rwkv_scan(r, log_w, k, a, v, gate, v_first, vgate, k_k, k_a, r_k, lnx_g, lnx_b, bsz, s):
    t, d = r.shape
    c = min(RWKV_CHUNK, s)
    wid = RWKV_PAIRS * LANES
    nc = s // c
    vres = v_first is not None
    tri = (jnp.arange(c)[None, :] <= jnp.arange(c)[:, None]).astype(F32)
    blk = pl.BlockSpec((c, wid), lambda b, g, i: (b * nc + i, g))
    vec = pl.BlockSpec((1, wid), lambda b, g, i: (0, g))
    rows = [r, log_w, k, a, v, gate] + ([v_first, vgate] if vres else [])
    vecs = [x.reshape(1, d) for x in (k_k, k_a, r_k, lnx_g, lnx_b)]
    return pl.pallas_call(
        functools.partial(_rwkv_scan_kernel, c=c, npair=RWKV_PAIRS, vres=vres),
        grid=(bsz, d // wid, nc),
        in_specs=[blk] * len(rows) + [vec] * len(vecs) + [pl.BlockSpec((c, c), lambda b, g, i: (0, 0))],
        out_specs=blk,
        out_shape=jax.ShapeDtypeStruct((t, d), BF16),
        scratch_shapes=[pltpu.VMEM((RWKV_PAIRS, LANES, LANES), F32)],
        compiler_params=_params("parallel", "parallel", "arbitrary"),
        name="rwkv_scan",
    )(*rows, *vecs, tri)


def _rwkv7_mixer(x, v_first, mu, w_r, w_k, w_v, w_o, w0, w1, w2, a0, a1, a2, g1, g2,
                 k_k, k_a, r_k, lnx_g, lnx_b, vres, bsz, s):
    loras = [(w1, w2, w0), (a1, a2, a0), (g1, g2, None)]
    if vres is not None:
        v0, v1, v2 = vres
        loras.append((v1, v2, v0))
    xr, xk, xv, log_decay, a, g, *rest = _shift_mix(x, mu, loras, s)
    r = _matmul(xr, w_r.astype(BF16), out_dtype=BF16)
    k = _matmul(xk, w_k.astype(BF16), out_dtype=BF16)
    v = _matmul(xv, w_v.astype(BF16), out_dtype=BF16)
    vgate = rest[0] if rest else None
    v_keep = v if vres is None else v_first
    yo = _rwkv_scan(r, log_decay, k, a, v, g, None if vres is None else v_first, vgate,
                    k_k, k_a, r_k, lnx_g, lnx_b, bsz, s)
    return ([yo], [w_o.astype(BF16)]), v_keep


def _lane_pick(cond, lanef):
    return jnp.min(jnp.where(cond, lanef, float(LANES)), axis=1, keepdims=True)


def _route_tile(xt, w_ref, b_ref, route_ref, cnt_ref, run_ref, tm):
    @pl.when(pl.program_id(0) == 0)
    def _reset():
        run_ref[...] = jnp.zeros_like(run_ref)

    logits = jnp.dot(xt.astype(BF16), w_ref[...], preferred_element_type=F32) + b_ref[...]
    lane = lax.broadcasted_iota(jnp.int32, (tm, LANES), 1)
    lanef = lane.astype(F32)

    def masked_softmax(keep):
        z = jnp.where(keep, logits, -jnp.inf)
        e = jnp.exp(z - jnp.max(z, axis=1, keepdims=True))
        return e / jnp.sum(e, axis=1, keepdims=True)

    in_groups = lane < N_GROUPS
    pg = masked_softmax(in_groups)
    g_p = jnp.max(pg, axis=1, keepdims=True)
    g_idx = _lane_pick(jnp.where(in_groups, pg, -1.0) == g_p, lanef)
    lo = float(N_GROUPS) + float(EXPERTS_PER_GROUP) * g_idx
    half = 0.5 * float(EXPERTS_PER_GROUP - 1)
    in_group = jnp.abs(lanef - lo - half) < half + 0.5
    pe = jnp.where(in_group, masked_softmax(in_group), -1.0)
    p1 = jnp.max(pe, axis=1, keepdims=True)
    i1 = _lane_pick(pe == p1, lanef)
    pe2 = jnp.where(lanef == i1, -1.0, pe)
    p2 = jnp.max(pe2, axis=1, keepdims=True)
    i2 = _lane_pick(pe2 == p2, lanef)
    den = p1 + p2
    gate = (g_p * (p1 / den), g_p * (p2 / den))
    expert = (i1 - float(N_GROUPS), i2 - float(N_GROUPS))

    earlier = jnp.where(lax.broadcasted_iota(jnp.int32, (tm, tm), 1)
                        < lax.broadcasted_iota(jnp.int32, (tm, tm), 0), 1.0, 0.0).astype(BF16)
    run = run_ref[...]
    rank = []
    for e in expert:
        onehot = jnp.where(lanef == e, 1.0, 0.0)
        before = jnp.dot(earlier, onehot.astype(BF16), preferred_element_type=F32) + run
        rank.append(jnp.sum(onehot * before, axis=1, keepdims=True))
        run = run + jnp.sum(onehot, axis=0, keepdims=True)
    run_ref[...] = run
    cnt_ref[...] = run
    cols = (expert[0], expert[1], rank[0], rank[1], gate[0], gate[1])
    route = jnp.zeros((tm, LANES), F32)
    for c, val in enumerate(cols):
        route = jnp.where(lane == c, val, route)
    route_ref[...] = route


SLAB = 2048 // LANES
DMA_UNROLL = 4


def _slab(ref, r):
    return ref.at[pl.ds(pl.multiple_of(r * SLAB, SLAB), SLAB)]


def _slab_rows(ref, rows):
    return jnp.concatenate([ref[pl.ds(c, rows, stride=SLAB), :] for c in range(SLAB)], axis=1)


def _store_slab_rows(ref, val, rows):
    for c in range(SLAB):
        ref[pl.ds(c, rows, stride=SLAB), :] = val[:, c * LANES:(c + 1) * LANES]


def _expert_kernel(be_ref, nu_ref, idx_ref, nxt_ref, x_hbm, wg_ref, wu_ref, wd_ref, o_ref,
                   xbuf, sems, wg_b, wu_b, wd_b, *, rows):
    i = pl.program_id(0)
    n_used = nu_ref[0]
    slot = i % 2

    def token_copy(tok, r, dst_slot):
        return pltpu.make_async_copy(_slab(x_hbm, tok), _slab(xbuf.at[dst_slot], r), sems.at[dst_slot])

    def gather(ids_ref, dst_slot):
        def start(r, carry):
            for rr in (r, r + rows // 2):
                token_copy(ids_ref[0, rr], rr, dst_slot).start()
            return carry
        lax.fori_loop(0, rows // 2, start, 0, unroll=DMA_UNROLL)

    @pl.when(i == 0)
    def _():
        gather(idx_ref, 0)

    @pl.when(i + 1 < n_used)
    def _():
        gather(nxt_ref, 1 - slot)

    @pl.when(jnp.logical_or(i == 0, be_ref[i] != be_ref[jnp.maximum(i - 1, 0)]))
    def _():
        wg_b[...] = wg_ref[...].astype(BF16)
        wu_b[...] = wu_ref[...].astype(BF16)
        wd_b[...] = wd_ref[...].astype(BF16)

    @pl.when(i < n_used)
    def _():
        for r in range(rows):
            token_copy(0, r, slot).wait()
        xb = _slab_rows(xbuf.at[slot], rows).astype(BF16)
        hg = jnp.dot(xb, wg_b[...], preferred_element_type=F32)
        hu = jnp.dot(xb, wu_b[...], preferred_element_type=F32)
        hid = (hg * jax.nn.sigmoid(hg) * hu).astype(BF16)
        _store_slab_rows(o_ref, jnp.dot(hid, wd_b[...], preferred_element_type=F32), rows)

    @pl.when(i >= n_used)
    def _():
        o_ref[...] = jnp.zeros_like(o_ref)


def _expert_mlp(x_slabs, src_row, block_e, n_used, w_gate, w_up, w_down, layer, rows=MOE_ROWS):
    d, ff = w_gate.shape[2], w_gate.shape[3]
    nb = src_row.shape[0] // rows
    ids = src_row.reshape(nb, 1, rows)
    grid_spec = pltpu.PrefetchScalarGridSpec(
        num_scalar_prefetch=2,
        grid=(nb,),
        in_specs=[pl.BlockSpec((None, 1, rows), lambda i, be, nu: (i, 0, 0), memory_space=pltpu.SMEM),
                  pl.BlockSpec((None, 1, rows), lambda i, be, nu: (jnp.minimum(i + 1, nb - 1), 0, 0),
                               memory_space=pltpu.SMEM),
                  pl.BlockSpec(memory_space=pl.ANY),
                  pl.BlockSpec((None, None, d, ff), lambda i, be, nu: (layer, be[i], 0, 0)),
                  pl.BlockSpec((None, None, d, ff), lambda i, be, nu: (layer, be[i], 0, 0)),
                  pl.BlockSpec((None, None, ff, d), lambda i, be, nu: (layer, be[i], 0, 0))],
        out_specs=pl.BlockSpec((rows * SLAB, LANES), lambda i, be, nu: (i, 0)),
        scratch_shapes=[pltpu.VMEM((2, rows * SLAB, LANES), F32), pltpu.SemaphoreType.DMA((2,)),
                        pltpu.VMEM((d, ff), BF16), pltpu.VMEM((d, ff), BF16), pltpu.VMEM((ff, d), BF16)],
    )
    return pl.pallas_call(
        functools.partial(_expert_kernel, rows=rows),
        grid_spec=grid_spec,
        out_shape=jax.ShapeDtypeStruct((nb * rows * SLAB, LANES), F32),
        compiler_params=_params("arbitrary"),
        name="expert_mlp",
    )(block_e, n_used, ids, ids, x_slabs, w_gate, w_up, w_down)


def _combine_ln_kernel(idx_ref, nxt_ref, y_ref, x_ref, gt_ref, g_ref, b_ref, o_ref, buf, sems, *, rows):
    i = pl.program_id(0)
    slot = i % 2

    def row_copy(src, r, kslot, dst_slot):
        return pltpu.make_async_copy(_slab(y_ref, src), _slab(buf.at[dst_slot, kslot], r), sems.at[dst_slot])

    def gather(ids_ref, dst_slot):
        def start(r, carry):
            for kslot in range(MOE_TOPK):
                for rr in (r, r + rows // 2):
                    row_copy(ids_ref[0, MOE_TOPK * rr + kslot], rr, kslot, dst_slot).start()
            return carry
        lax.fori_loop(0, rows // 2, start, 0, unroll=DMA_UNROLL)

    @pl.when(i == 0)
    def _():
        gather(idx_ref, 0)

    @pl.when(i + 1 < pl.num_programs(0))
    def _():
        gather(nxt_ref, 1 - slot)

    for r in range(rows):
        for kslot in range(MOE_TOPK):
            row_copy(0, r, kslot, slot).wait()
    gt = gt_ref[...]
    ffn = gt[:, 0:1] * _slab_rows(buf.at[slot, 0], rows)
    for kslot in range(1, MOE_TOPK):
        ffn = ffn + gt[:, kslot:kslot + 1] * _slab_rows(buf.at[slot, kslot], rows)
    o_ref[...] = _ln_rows(DN_ALPHA * x_ref[...] + ffn, g_ref[...], b_ref[...])


def _combine_ln(y_sorted, dest, gates, x, g, b, rows=256):
    t, d = x.shape
    nb = t // rows
    row = pl.BlockSpec((rows, d), lambda i: (i, 0))
    vec = pl.BlockSpec((1, d), lambda i: (0, 0))
    ids = dest.reshape(nb, 1, rows * MOE_TOPK)
    return pl.pallas_call(
        functools.partial(_combine_ln_kernel, rows=rows),
        grid=(nb,),
        in_specs=[pl.BlockSpec((None, 1, rows * MOE_TOPK), lambda i: (i, 0, 0), memory_space=pltpu.SMEM),
                  pl.BlockSpec((None, 1, rows * MOE_TOPK), lambda i: (jnp.minimum(i + 1, nb - 1), 0, 0),
                               memory_space=pltpu.SMEM),
                  pl.BlockSpec(memory_space=pl.ANY), row,
                  pl.BlockSpec((rows, MOE_TOPK), lambda i: (i, 0)), vec, vec],
        out_specs=row,
        out_shape=jax.ShapeDtypeStruct((t, d), F32),
        scratch_shapes=[pltpu.VMEM((2, MOE_TOPK, rows * SLAB, LANES), F32), pltpu.SemaphoreType.DMA((2,))],
        compiler_params=_params("arbitrary"),
        name="combine_ln",
    )(ids, ids, y_sorted, x, gates, g.reshape(1, d), b.reshape(1, d))


def _router_params(w_grp, b_grp, w_exp_r, b_exp_r):
    n_r = N_GROUPS + N_EXPERTS
    w_router = jnp.pad(jnp.concatenate([w_grp, w_exp_r], axis=1), ((0, 0), (0, LANES - n_r))).astype(BF16)
    b_router = jnp.pad(jnp.concatenate([b_grp, b_exp_r]), (0, LANES - n_r)).reshape(1, LANES)
    return w_router, b_router


def _moe_layer(x, x_slabs, route, cnt, w_gate, w_up, w_down, ln_g, ln_b, layer):
    t, d = x.shape
    flat_e = route[:, 0:MOE_TOPK].astype(jnp.int32).reshape(-1)
    rank = route[:, MOE_TOPK:2 * MOE_TOPK].astype(jnp.int32).reshape(-1)
    gates = route[:, 2 * MOE_TOPK:3 * MOE_TOPK]
    counts = cnt[0, :N_EXPERTS].astype(jnp.int32)
    m = t * MOE_TOPK
    padded = (counts + MOE_ROWS - 1) // MOE_ROWS * MOE_ROWS
    pad_end = jnp.cumsum(padded)
    dest = (pad_end - padded)[flat_e] + rank
    n_blocks = (m + N_EXPERTS * (MOE_ROWS - 1) + MOE_ROWS - 1) // MOE_ROWS
    src_row = jnp.zeros((n_blocks * MOE_ROWS,), jnp.int32).at[dest].set(
        (jnp.arange(m) // MOE_TOPK).astype(jnp.int32))
    starts = jnp.arange(n_blocks, dtype=jnp.int32) * MOE_ROWS
    block_e = jnp.minimum(jnp.sum((pad_end[None, :] <= starts[:, None]).astype(jnp.int32), axis=1),
                          N_EXPERTS - 1).astype(jnp.int32)
    n_used = (pad_end[-1] // MOE_ROWS).astype(jnp.int32).reshape(1)

    ys = _expert_mlp(x_slabs, src_row, block_e, n_used, w_gate, w_up, w_down, layer)
    return _combine_ln(ys, dest.astype(jnp.int32), gates, x, ln_g, ln_b)


def kernel(x, ev_w_in, ev_gate_bias, ev_conv_w, ev_conv_b, ev_hnorm_g, ev_w_out, od_mu, od_w_r, od_w_k, od_w_v, od_w_o, od_w0, od_w1, od_w2, od_a0, od_a1, od_a2, od_g1, od_g2, od_k_k, od_k_a, od_r_k, od_lnx_g, od_lnx_b, od_v0, od_v1, od_v2, ln_mix_g, ln_mix_b, ln_ffn_g, ln_ffn_b, moe_w_grp, moe_b_grp, moe_w_exp_r, moe_b_exp_r, moe_w_gate, moe_w_up, moe_w_down):
    bsz, s, d = x.shape
    depth = ln_mix_g.shape[0]
    xf = x.reshape(bsz * s, d)
    v_first = None
    for layer in range(depth):
        if layer % 2 == 0:
            e = layer // 2
            ys, ws = _even_mixer(xf, ev_w_in[e], ev_gate_bias[e], ev_conv_w[e], ev_conv_b[e],
                              ev_hnorm_g[e], ev_w_out[e], bsz, s)
        else:
            o = layer // 2
            vres = None if o == 0 else (od_v0[o - 1], od_v1[o - 1], od_v2[o - 1])
            (ys, ws), v_first = _rwkv7_mixer(xf, v_first, od_mu[o], od_w_r[o], od_w_k[o], od_w_v[o],
                                        od_w_o[o], od_w0[o], od_w1[o], od_w2[o], od_a0[o],
                                        od_a1[o], od_a2[o], od_g1[o], od_g2[o], od_k_k[o],
                                        od_k_a[o], od_r_k[o], od_lnx_g[o], od_lnx_b[o], vres, bsz, s)
        xf, x_slabs, route, cnt = _resid_ln(xf, ys, ws, ln_mix_g[layer], ln_mix_b[layer],
                                            *_router_params(moe_w_grp[layer], moe_b_grp[layer],
                                                            moe_w_exp_r[layer], moe_b_exp_r[layer]))
        xf = _moe_layer(xf, x_slabs, route, cnt, moe_w_gate, moe_w_up, moe_w_down,
                        ln_ffn_g[layer], ln_ffn_b[layer], layer)
    return xf.reshape(bsz, s, d)
```
